```python
import math
import jax, jax.numpy as jnp
from jax import lax
import numpy as np

D_MODEL = 1024
BATCH = 32
SEQ = 2048
DEPTH = 2

GRID_W = 64
CTX_LEN = 256
EPS = 1e-6
N_BRANCH = 3
D_CONV_MIX = D_MODEL
SSD_D_INNER = D_MODEL
SSD_HEADDIM = 64
SSD_HEADS = SSD_D_INNER // SSD_HEADDIM
SSD_GROUPS = 2
SSD_STATE = 128
SSD_CHUNK = 128
XBC_DIM = SSD_D_INNER + 2 * SSD_GROUPS * SSD_STATE
NA_HEADS = 16
NA_HEAD_DIM = 64
NA_WIDTH = NA_HEADS * NA_HEAD_DIM
NA_KH = 8
NA_KW = 16
ROPE_BASE = 10000.0
D_FF = 4 * D_MODEL
IN_SPLITS = (D_CONV_MIX, D_CONV_MIX, D_CONV_MIX, SSD_D_INNER, XBC_DIM, 2 * SSD_HEADS, NA_WIDTH, NA_WIDTH, NA_WIDTH, N_BRANCH * D_MODEL)
IN_COLS = sum(IN_SPLITS)
IN_NAMES = ('conv_b', 'conv_c', 'conv_x', 'ssd_z', 'ssd_xbc', 'ssd_dt', 'na_q', 'na_k', 'na_v', 'gate')

kernel_name = 'hybrid_dit_conv_ssd_na_block'


def rmsnorm(x, w):
    xf = x.astype(jnp.float32)
    y = xf * lax.rsqrt(jnp.mean(xf * xf, axis=-1, keepdims=True) + EPS)
    return (y * w.astype(jnp.float32)).astype(x.dtype)


def modulate(x, shift, scale):
    return x * (1 + scale) + shift


def split_in(u):
    offs = [int(o) for o in np.cumsum(IN_SPLITS)[:-1]]
    return dict(zip(IN_NAMES, jnp.split(u, offs, axis=-1)))


def heads(t, n):
    return t.reshape(t.shape[0], t.shape[1], n, -1)


def dwconv3(x, w, b=None):
    xp = jnp.pad(x, ((0, 0), (1, 1), (0, 0)))
    y = xp[:, :-2] * w[0] + xp[:, 1:-1] * w[1] + xp[:, 2:] * w[2]
    return y if b is None else y + b


def _rev(t, d):
    return jnp.flip(t, axis=1) if d else t


def ssd_scan(X, a, Bm, Cm, h0, with_output):
    b, L, G, R, P = X.shape
    N = Bm.shape[-1]
    Q = SSD_CHUNK
    nc = L // Q
    X = X.reshape(b, nc, Q, G, R, P)
    Bm = Bm.reshape(b, nc, Q, G, N)
    Cm = Cm.reshape(b, nc, Q, G, N)
    a_cs = jnp.cumsum(a.astype(jnp.float32).reshape(b, nc, Q, G, R), axis=2)
    decay_states = jnp.exp(a_cs[:, :, -1:] - a_cs)
    states = jnp.einsum('bclgn,bclgrp->bcgrpn', Bm, X * decay_states[..., None])
    chunk_decay = jnp.exp(a_cs[:, :, -1])

    def step(h, inp):
        s_c, d_c = inp
        return h * d_c[..., None, None] + s_c, h

    h_final, h_before = lax.scan(step, h0.astype(jnp.float32), (jnp.swapaxes(states, 0, 1), jnp.swapaxes(chunk_decay, 0, 1)))
    if not with_output:
        return None, h_final
    h_before = jnp.swapaxes(h_before, 0, 1)
    seg = a_cs[:, :, :, None] - a_cs[:, :, None]
    tri = jnp.tril(jnp.ones((Q, Q), dtype=bool))[None, None, :, :, None, None]
    Lmat = jnp.exp(jnp.where(tri, seg, -jnp.inf))
    scores = jnp.einsum('bclgn,bcsgn->bclsg', Cm, Bm)
    y_diag = jnp.einsum('bclsgr,bcsgrp->bclgrp', scores[..., None] * Lmat, X)
    y_off = jnp.einsum('bclgn,bcgrpn->bclgrp', Cm, h_before) * jnp.exp(a_cs)[..., None]
    return (y_diag + y_off).reshape(b, L, G, R, P), h_final


def ssd_mix(xbc, z, dt_raw, conv_w, conv_b, a_log, dt_bias, d_skip, norm_w, h0, with_output):
    b, L, _ = xbc.shape
    G, R, P, N = SSD_GROUPS, SSD_HEADS // SSD_GROUPS, SSD_HEADDIM, SSD_STATE
    xbc = jax.nn.silu(dwconv3(xbc, conv_w, conv_b))
    xs, Bm, Cm = jnp.split(xbc, [SSD_D_INNER, SSD_D_INNER + G * N], axis=-1)
    xs = xs.reshape(b, L, G, R, P)
    Bm = Bm.reshape(b, L, G, N)
    Cm = Cm.reshape(b, L, G, N)
    dt = jax.nn.softplus(dt_raw.astype(jnp.float32).reshape(b, L, 2, G, R) + dt_bias.astype(jnp.float32).reshape(2, G, R))
    A = -jnp.exp(a_log.astype(jnp.float32)).reshape(2, G, R)
    ys, hs = [], []
    for d in range(2):
        dt_d = dt[:, :, d]
        y_d, h_d = ssd_scan(_rev(xs * dt_d[..., None], d), _rev(dt_d * A[d], d), _rev(Bm, d), _rev(Cm, d), h0[d], with_output)
        hs.append(h_d)
        if with_output:
            ys.append(_rev(y_d, d).astype(xs.dtype) + xs * d_skip[d].reshape(G, R)[:, :, None])
    if not with_output:
        return None, (hs[0], hs[1])
    y = (ys[0] + ys[1]).reshape(b, L, SSD_D_INNER) * jax.nn.silu(z)
    y = rmsnorm(y.reshape(b, L, G, -1), norm_w.reshape(G, -1)).reshape(b, L, SSD_D_INNER)
    return y, (hs[0], hs[1])


def axial_rope(L, dtype):
    t = jnp.arange(L, dtype=jnp.int32)
    row = (t // GRID_W).astype(jnp.float32)
    col = (t % GRID_W).astype(jnp.float32)
    half = NA_HEAD_DIM // 2
    inv = ROPE_BASE ** (-jnp.arange(0, half, 2, dtype=jnp.float32) / half)
    ang_r = row[:, None] * inv
    ang_c = col[:, None] * inv
    ang = jnp.concatenate([ang_r, ang_r, ang_c, ang_c], axis=-1)
    return jnp.cos(ang).astype(dtype), jnp.sin(ang).astype(dtype)


def apply_rope(x, cos, sin):
    def rot_half(u):
        u1, u2 = jnp.split(u, 2, axis=-1)
        return jnp.concatenate([-u2, u1], axis=-1)
    xr, xc = jnp.split(x, 2, axis=-1)
    rot = jnp.concatenate([rot_half(xr), rot_half(xc)], axis=-1)
    return x * cos[:, None, :] + rot * sin[:, None, :]


def na_attend(q, k, v, k_ctx, v_ctx, rpb):
    b, L, H, Dh = q.shape
    rows = L // GRID_W
    kh = min(NA_KH, rows)
    band = kh * GRID_W
    col = jnp.arange(GRID_W, dtype=jnp.int32)
    col_start = jnp.clip(col - NA_KW // 2, 0, GRID_W - NA_KW)
    col_ok = (col[None, :] >= col_start[:, None]) & (col[None, :] < col_start[:, None] + NA_KW)
    mask = jnp.broadcast_to(col_ok[:, None, :], (GRID_W, kh, GRID_W)).reshape(GRID_W, band)
    dc_idx = jnp.clip(col[None, :] - col[:, None], -(NA_KW - 1), NA_KW - 1) + NA_KW - 1
    q = q * (Dh ** -0.5)

    def one_row(r):
        r0 = jnp.clip(r - kh // 2, 0, rows - kh)
        qr = lax.dynamic_slice_in_dim(q, r * GRID_W, GRID_W, axis=1)
        kr = lax.dynamic_slice_in_dim(k, r0 * GRID_W, band, axis=1)
        vr = lax.dynamic_slice_in_dim(v, r0 * GRID_W, band, axis=1)
        dr_idx = r0 + jnp.arange(kh, dtype=jnp.int32) - r + NA_KH - 1
        bias = rpb[:, dr_idx[None, :, None], dc_idx[:, None, :]].reshape(H, GRID_W, band)
        s_lat = jnp.einsum('bqhd,bkhd->bhqk', qr, kr).astype(jnp.float32) + bias.astype(jnp.float32)[None]
        s_lat = jnp.where(mask[None, None], s_lat, -jnp.inf)
        s_ctx = jnp.einsum('bqhd,bkhd->bhqk', qr, k_ctx).astype(jnp.float32)
        p = jax.nn.softmax(jnp.concatenate([s_lat, s_ctx], axis=-1), axis=-1).astype(v.dtype)
        return jnp.einsum('bhqk,bkhd->bqhd', p[..., :band], vr) + jnp.einsum('bhqk,bkhd->bqhd', p[..., band:], v_ctx)

    out = lax.map(one_row, jnp.arange(rows, dtype=jnp.int32))
    return jnp.transpose(out, (1, 0, 2, 3, 4)).reshape(b, L, H * Dh)


def ctx_attend(q, k, v):
    b, L, H, Dh = q.shape
    s = jnp.einsum('bqhd,bkhd->bhqk', q * (Dh ** -0.5), k).astype(jnp.float32)
    p = jax.nn.softmax(s, axis=-1).astype(v.dtype)
    return jnp.einsum('bhqk,bkhd->bqhd', p, v).reshape(b, L, H * Dh)


def merge_branches(y_conv, y_ssd, y_na, gate_logits, w_br_conv, w_br_ssd, w_br_na, w_out):
    g_conv, g_ssd, g_na = jnp.split(jax.nn.sigmoid(gate_logits), N_BRANCH, axis=-1)
    merged = g_conv * (y_conv @ w_br_conv) + g_ssd * (y_ssd @ w_br_ssd) + g_na * (y_na @ w_br_na)
    return merged @ w_out


def sqrelu_mlp(x, w1, w2):
    return jnp.square(jax.nn.relu(x @ w1)) @ w2


def setup_inputs(seed: int = 0) -> dict:
    key = jax.random.key(seed)
    ks = iter(jax.random.split(key, 32))

    def nrm(shape, s):
        return jax.random.normal(next(ks), shape, jnp.float32) * s

    def gain(shape):
        return 1.0 + nrm(shape, 0.01)

    dt0 = jnp.exp(jax.random.uniform(next(ks), (DEPTH, 2, SSD_HEADS), jnp.float32) * (math.log(0.1) - math.log(0.001)) + math.log(0.001))
    return {
        'x': nrm((BATCH, SEQ, D_MODEL), 1.0),
        'c': nrm((BATCH, D_MODEL), 1.0),
        'ctx': nrm((BATCH, CTX_LEN, D_MODEL), 1.0),
        'c_ctx': nrm((D_MODEL,), 1.0),
        'w_ada': nrm((DEPTH, D_MODEL, 6 * D_MODEL), 0.5 * D_MODEL ** -0.5),
        'b_ada': nrm((DEPTH, 6 * D_MODEL), 0.01),
        'norm1_w': gain((DEPTH, D_MODEL)),
        'w_in': nrm((DEPTH, D_MODEL, IN_COLS), D_MODEL ** -0.5),
        'conv_mix_w': nrm((DEPTH, 3, D_CONV_MIX), 3 ** -0.5),
        'ssd_conv_w': nrm((DEPTH, 3, XBC_DIM), 3 ** -0.5),
        'ssd_conv_b': nrm((DEPTH, XBC_DIM), 0.01),
        'ssd_a_log': jnp.log(jax.random.uniform(next(ks), (DEPTH, 2, SSD_HEADS), jnp.float32, 1.0, 16.0)),
        'ssd_dt_bias': dt0 + jnp.log(-jnp.expm1(-dt0)),
        'ssd_d': gain((DEPTH, 2, SSD_HEADS)),
        'ssd_norm_w': gain((DEPTH, SSD_D_INNER)),
        'na_rpb': nrm((DEPTH, NA_HEADS, 2 * NA_KH - 1, 2 * NA_KW - 1), 0.02),
        'w_br_conv': nrm((DEPTH, D_CONV_MIX, D_MODEL), D_CONV_MIX ** -0.5),
        'w_br_ssd': nrm((DEPTH, SSD_D_INNER, D_MODEL), SSD_D_INNER ** -0.5),
        'w_br_na': nrm((DEPTH, NA_WIDTH, D_MODEL), NA_WIDTH ** -0.5),
        'w_out': nrm((DEPTH, D_MODEL, D_MODEL), D_MODEL ** -0.5),
        'norm2_w': gain((DEPTH, D_MODEL)),
        'w_ff1': nrm((DEPTH, D_MODEL, D_FF), D_MODEL ** -0.5),
        'w_ff2': nrm((DEPTH, D_FF, D_MODEL), D_FF ** -0.5),
        'final_norm_w': gain((D_MODEL,)),
    }


def reference(x, c, ctx, c_ctx, w_ada, b_ada, norm1_w, w_in, conv_mix_w, ssd_conv_w, ssd_conv_b, ssd_a_log, ssd_dt_bias, ssd_d, ssd_norm_w, na_rpb, w_br_conv, w_br_ssd, w_br_na, w_out, norm2_w, w_ff1, w_ff2, final_norm_w):
    b, L, _ = x.shape
    h, hc = x, ctx
    silu_c, silu_cc = jax.nn.silu(c), jax.nn.silu(c_ctx)
    cos, sin = axial_rope(L, x.dtype)
    G, R, P, N = SSD_GROUPS, SSD_HEADS // SSD_GROUPS, SSD_HEADDIM, SSD_STATE
    zero_state = jnp.zeros((b, G, R, P, N), jnp.float32)
    for l in range(DEPTH):
        last = l == DEPTH - 1
        mod = (silu_c @ w_ada[l] + b_ada[l])[:, None, :]
        mod_c = (silu_cc @ w_ada[l] + b_ada[l])[None, None, :]
        sh1, sc1, gt1, sh2, sc2, gt2 = jnp.split(mod, 6, axis=-1)
        csh1, csc1, cgt1, csh2, csc2, cgt2 = jnp.split(mod_c, 6, axis=-1)
        pl = split_in(modulate(rmsnorm(h, norm1_w[l]), sh1, sc1) @ w_in[l])
        pc = split_in(modulate(rmsnorm(hc, norm1_w[l]), csh1, csc1) @ w_in[l])
        ssd_p = (ssd_conv_w[l], ssd_conv_b[l], ssd_a_log[l], ssd_dt_bias[l], ssd_d[l], ssd_norm_w[l])
        y_ssd_c, ctx_states = ssd_mix(pc['ssd_xbc'], pc['ssd_z'], pc['ssd_dt'], *ssd_p, (zero_state, zero_state), not last)
        y_ssd, _ = ssd_mix(pl['ssd_xbc'], pl['ssd_z'], pl['ssd_dt'], *ssd_p, ctx_states, True)
        k_c, v_c = heads(pc['na_k'], NA_HEADS), heads(pc['na_v'], NA_HEADS)
        q_l = apply_rope(heads(pl['na_q'], NA_HEADS), cos, sin)
        k_l = apply_rope(heads(pl['na_k'], NA_HEADS), cos, sin)
        y_na = na_attend(q_l, k_l, heads(pl['na_v'], NA_HEADS), k_c, v_c, na_rpb[l])
        y_conv = pl['conv_b'] * dwconv3(pl['conv_c'] * pl['conv_x'], conv_mix_w[l])
        h = h + gt1 * merge_branches(y_conv, y_ssd, y_na, pl['gate'], w_br_conv[l], w_br_ssd[l], w_br_na[l], w_out[l])
        h = h + gt2 * sqrelu_mlp(modulate(rmsnorm(h, norm2_w[l]), sh2, sc2), w_ff1[l], w_ff2[l])
        if not last:
            y_conv_c = pc['conv_b'] * dwconv3(pc['conv_c'] * pc['conv_x'], conv_mix_w[l])
            y_na_c = ctx_attend(heads(pc['na_q'], NA_HEADS), k_c, v_c)
            hc = hc + cgt1 * merge_branches(y_conv_c, y_ssd_c, y_na_c, pc['gate'], w_br_conv[l], w_br_ssd[l], w_br_na[l], w_out[l])
            hc = hc + cgt2 * sqrelu_mlp(modulate(rmsnorm(hc, norm2_w[l]), csh2, csc2), w_ff1[l], w_ff2[l])
    return rmsnorm(h, final_norm_w)
```

```python
import functools
import math

import jax
import jax.numpy as jnp
import numpy as np
from jax import lax
from jax.experimental import pallas as pl
from jax.experimental.pallas import tpu as pltpu

F32 = jnp.float32
BF16 = jnp.bfloat16
HIGHEST = lax.Precision.HIGHEST

D = 1024
EPS = 1e-6
GRID_W = 64
N_HEADS = 16
HEAD_DIM = 64
SSD_GROUPS = 2
SSD_STATE = 128
CHUNK = 128
NA_KH = 8
NA_KW = 16
ROPE_BASE = 10000.0
D_FF = 4 * D
XBC = D + 2 * SSD_GROUPS * SSD_STATE
U_GATE, U_CB, U_CC, U_CX, U_Z, U_Q, U_K, U_V, U_XS, U_BC, U_COLS = (
    0, 3072, 4096, 5120, 6144, 7168, 8192, 9216, 10240, 11264, 11776)
R_CB, R_CC, R_CX, R_Z, R_XBC, R_DT, R_Q, R_K, R_V, R_GATE = (
    0, 1024, 2048, 3072, 4096, 5632, 5664, 6688, 7712, 8736)
DT_PAD = 128
VMEM_LIMIT = 56 * 1024 * 1024


def _cparams(sem):
    return pltpu.CompilerParams(dimension_semantics=sem, vmem_limit_bytes=VMEM_LIMIT)


def _nt(a, b):
    return lax.dot_general(a, b, (((1,), (1,)), ((), ())), preferred_element_type=F32)


def _tn(a, b):
    return lax.dot_general(a, b, (((0,), (0,)), ((), ())), preferred_element_type=F32)


def _dot(a, b):
    return jnp.dot(a, b, preferred_element_type=F32)


def _dot_hi(a, b):
    return jnp.dot(a, b, preferred_element_type=F32, precision=HIGHEST)


def _silu(x):
    return x * jax.nn.sigmoid(x)


def _norm_mod(x, nw, sc, sh):
    ms = jnp.mean(x * x, axis=-1, keepdims=True)
    y = x * lax.rsqrt(ms + EPS) * nw
    return y * (1.0 + sc) + sh


def _ada_kernel(c_ref, w_ref, b_ref, o_ref):
    o_ref[...] = _dot_hi(_silu(c_ref[...]), w_ref[...]) + b_ref[...]


def ada_mod(c_rows, w_ada, b_ada):
    rows = c_rows.shape[0]
    tn = 1536
    return pl.pallas_call(
        _ada_kernel,
        grid=(6 * D // tn,),
        in_specs=[pl.BlockSpec((rows, D), lambda j: (0, 0)),
                  pl.BlockSpec((D, tn), lambda j: (0, j)),
                  pl.BlockSpec((1, tn), lambda j: (0, j))],
        out_specs=pl.BlockSpec((rows, tn), lambda j: (0, j)),
        out_shape=jax.ShapeDtypeStruct((rows, 6 * D), F32),
        compiler_params=_cparams(("arbitrary",)),
        name="ada_mod",
    )(c_rows, w_ada, b_ada.reshape(1, 6 * D))


def _inproj_kernel(x_ref, sh_ref, sc_ref, nw_ref, w_ref, wdt_ref, wdtT_ref, u_ref, dt_ref, dtT_ref, xn_ref):
    @pl.when(pl.program_id(2) == 0)
    def _():
        xb = _norm_mod(x_ref[0], nw_ref[...], sc_ref[0], sh_ref[0]).astype(BF16)
        xn_ref[...] = xb
        dt_ref[0] = _dot(xb, wdt_ref[...])
        dtT_ref[0] = _nt(wdtT_ref[...], xb)

    u_ref[0] = _dot(xn_ref[...], w_ref[...]).astype(u_ref.dtype)


def in_proj(x, sh, sc, nw, w, wdt, wdtT, tm, tn=512):
    B, L, _ = x.shape
    grid = (B, L // tm, U_COLS // tn)
    return pl.pallas_call(
        _inproj_kernel,
        grid=grid,
        in_specs=[pl.BlockSpec((1, tm, D), lambda b, i, j: (b, i, 0)),
                  pl.BlockSpec((1, 1, D), lambda b, i, j: (b, 0, 0)),
                  pl.BlockSpec((1, 1, D), lambda b, i, j: (b, 0, 0)),
                  pl.BlockSpec((1, D), lambda b, i, j: (0, 0)),
                  pl.BlockSpec((D, tn), lambda b, i, j: (0, j)),
                  pl.BlockSpec((D, DT_PAD), lambda b, i, j: (0, 0)),
                  pl.BlockSpec((2 * N_HEADS, D), lambda b, i, j: (0, 0))],
        out_specs=[pl.BlockSpec((1, tm, tn), lambda b, i, j: (b, i, j)),
                   pl.BlockSpec((1, tm, DT_PAD), lambda b, i, j: (b, i, 0)),
                   pl.BlockSpec((1, 2 * N_HEADS, tm), lambda b, i, j: (b, 0, i))],
        out_shape=[jax.ShapeDtypeStruct((B, L, U_COLS), BF16),
                   jax.ShapeDtypeStruct((B, L, DT_PAD), F32),
                   jax.ShapeDtypeStruct((B, 2 * N_HEADS, L), F32)],
        scratch_shapes=[pltpu.VMEM((tm, D), BF16)],
        compiler_params=_cparams(("parallel", "parallel", "arbitrary")),
        name="in_proj",
    )(x, sh, sc, nw, w, wdt, wdtT)


def _dwconv3(p, w):
    L = p.shape[0]
    row = lax.broadcasted_iota(jnp.int32, p.shape, 0)
    prev = jnp.where(row == 0, 0.0, pltpu.roll(p, 1, axis=0))
    nxt = jnp.where(row == L - 1, 0.0, pltpu.roll(p, L - 1, axis=0))
    return prev * w[0:1] + p * w[1:2] + nxt * w[2:3]


def _convmix_kernel(b_ref, c_ref, x_ref, w_ref, o_ref):
    p = c_ref[0].astype(F32) * x_ref[0].astype(F32)
    o_ref[0] = (b_ref[0].astype(F32) * _dwconv3(p, w_ref[...])).astype(o_ref.dtype)


def conv_mix(u, w, tc=256):
    B, L, _ = u.shape
    ob, oc, ox = U_CB // tc, U_CC // tc, U_CX // tc
    return pl.pallas_call(
        _convmix_kernel,
        grid=(B, D // tc),
        in_specs=[pl.BlockSpec((1, L, tc), lambda b, j: (b, 0, ob + j)),
                  pl.BlockSpec((1, L, tc), lambda b, j: (b, 0, oc + j)),
                  pl.BlockSpec((1, L, tc), lambda b, j: (b, 0, ox + j)),
                  pl.BlockSpec((3, tc), lambda b, j: (0, j))],
        out_specs=pl.BlockSpec((1, L, tc), lambda b, j: (b, 0, j)),
        out_shape=jax.ShapeDtypeStruct((B, L, D), BF16),
        compiler_params=_cparams(("parallel", "parallel")),
        name="conv_mix",
    )(u, u, u, w)


def _xbcconv_kernel(x_ref, w_ref, b_ref, o_ref):
    y = _dwconv3(x_ref[0].astype(F32), w_ref[...]) + b_ref[...]
    o_ref[0] = _silu(y).astype(o_ref.dtype)


def xbc_conv(u, w, bias, tc=256):
    B, L, _ = u.shape
    off = U_XS // tc
    return pl.pallas_call(
        _xbcconv_kernel,
        grid=(B, XBC // tc),
        in_specs=[pl.BlockSpec((1, L, tc), lambda b, j: (b, 0, off + j)),
                  pl.BlockSpec((3, tc), lambda b, j: (0, j)),
                  pl.BlockSpec((1, tc), lambda b, j: (0, j))],
        out_specs=pl.BlockSpec((1, L, tc), lambda b, j: (b, 0, j)),
        out_shape=jax.ShapeDtypeStruct((B, L, XBC), BF16),
        compiler_params=_cparams(("parallel", "parallel")),
        name="xbc_conv",
    )(u, w, bias.reshape(1, XBC))


def _ssd_chunk(d, rows, xs_ref, bc_ref, z_ref, dt_ref, dtT_ref, alr_ref, alc_ref, dbr_ref, dbc_ref, dsk_ref, nw_ref,
               y_ref, yacc_ref, st_ref):
    H = N_HEADS
    xs = xs_ref[0].astype(F32)
    bc = bc_ref[0]
    a_row = -jnp.exp(alr_ref[...])
    a_col = -jnp.exp(alc_ref[...])
    dt_c = jax.nn.softplus(dt_ref[0][:, :2 * H] + dbr_ref[...])
    dtT_c = jax.nn.softplus(dtT_ref[0] + dbc_ref[...])
    a_c = dt_c * a_row
    aT_c = dtT_c * a_col

    ri = lax.broadcasted_iota(jnp.int32, (CHUNK, CHUNK), 0)
    ci = lax.broadcasted_iota(jnp.int32, (CHUNK, CHUNK), 1)
    tri = (ri >= ci) if d == 0 else (ri <= ci)
    cs32 = _dot_hi(tri.astype(F32), a_c)
    triT = (ri <= ci) if d == 0 else (ri >= ci)
    csT32 = _dot_hi(aT_c, triT.astype(F32))

    kk = lax.broadcasted_iota(jnp.int32, (2 * H, D), 0)
    jj = lax.broadcasted_iota(jnp.int32, (2 * H, D), 1)
    expand = (kk == d * H + jj // HEAD_DIM).astype(F32)
    cs_e = _dot_hi(cs32, expand)
    dt_e = _dot_hi(dt_c, expand)
    last = cs_e[CHUNK - 1:CHUNK] if d == 0 else cs_e[0:1]
    dec_e = jnp.exp(last - cs_e)
    ein_e = jnp.exp(cs_e)
    chunk_decay = jnp.exp(last)
    X = xs * dt_e
    Xb = X.astype(BF16)
    Xdec = (X * dec_e).astype(BF16)
    lane = lax.broadcasted_iota(jnp.int32, (CHUNK, 2 * HEAD_DIM), 1)

    ys = []
    GW = D // SSD_GROUPS
    for g in range(SSD_GROUPS):
        Bg = bc[:, g * SSD_STATE:(g + 1) * SSD_STATE]
        Cg = bc[:, (SSD_GROUPS + g) * SSD_STATE:(SSD_GROUPS + g + 1) * SSD_STATE]
        S = _nt(Cg, Bg)
        st = st_ref[d, g]
        y_off = _dot(Cg, st.astype(BF16)) * ein_e[:, g * GW:(g + 1) * GW]
        y_diag = []
        for p in range(GW // (2 * HEAD_DIM)):
            hA = g * (H // SSD_GROUPS) + 2 * p
            Ms = []
            for h in (hA, hA + 1):
                col = cs32[:, d * H + h:d * H + h + 1]
                rowv = csT32[d * H + h:d * H + h + 1, :]
                Lm = jnp.exp(jnp.where(tri, col - rowv, -jnp.inf))
                Ms.append((S * Lm).astype(BF16))
            Mcat = jnp.concatenate(Ms, axis=1)
            Xp = Xb[:, hA * HEAD_DIM:(hA + 2) * HEAD_DIM]
            zero = jnp.zeros_like(Xp)
            Xbd = jnp.concatenate([jnp.where(lane < HEAD_DIM, Xp, zero),
                                   jnp.where(lane >= HEAD_DIM, Xp, zero)], axis=0)
            y_diag.append(_dot(Mcat, Xbd))
        ys.append(y_off + jnp.concatenate(y_diag, axis=1))
        st_ref[d, g] = st * chunk_decay[:, g * GW:(g + 1) * GW] + _tn(Bg, Xdec[:, g * GW:(g + 1) * GW])
    y = jnp.concatenate(ys, axis=1)

    if d == 0:
        yacc_ref[rows, :] = y
    else:
        ysum = yacc_ref[rows, :] + y + xs * (dsk_ref[0:1] + dsk_ref[1:2])
        yg = ysum * _silu(z_ref[0].astype(F32))
        parts = []
        for g in range(SSD_GROUPS):
            v = yg[:, g * GW:(g + 1) * GW]
            ms = jnp.mean(v * v, axis=-1, keepdims=True)
            parts.append(v * lax.rsqrt(ms + EPS) * nw_ref[:, g * GW:(g + 1) * GW])
        y_ref[0] = jnp.concatenate(parts, axis=1).astype(y_ref.dtype)


def _ssd_kernel(xs_ref, bc_ref, z_ref, dt_ref, dtT_ref, h0_ref, alr_ref, alc_ref, dbr_ref, dbc_ref, dsk_ref, nw_ref,
                y_ref, sto_ref, yacc_ref, st_ref, *, nc):
    t = pl.program_id(1)

    @pl.when(t == 0)
    def _():
        st_ref[...] = h0_ref[0]

    args = (xs_ref, bc_ref, z_ref, dt_ref, dtT_ref, alr_ref, alc_ref, dbr_ref, dbc_ref, dsk_ref, nw_ref,
            y_ref, yacc_ref, st_ref)

    @pl.when(t < nc)
    def _():
        _ssd_chunk(0, pl.ds(pl.multiple_of(t * CHUNK, CHUNK), CHUNK), *args)

    @pl.when(t >= nc)
    def _():
        _ssd_chunk(1, pl.ds(pl.multiple_of((2 * nc - 1 - t) * CHUNK, CHUNK), CHUNK), *args)

    @pl.when(t == 2 * nc - 1)
    def _():
        sto_ref[0] = st_ref[...]


def ssd_mix(xbc_c, u, dt, dtT, h0, a_log, dt_bias, d_skip_e, norm_w):
    B, L, _ = u.shape
    nc = L // CHUNK
    H2 = 2 * N_HEADS

    def chunk(t):
        return jnp.where(t < nc, t, 2 * nc - 1 - t)

    def late(t):
        return jnp.where(t < nc, nc - 1, 2 * nc - 1 - t)

    st_shape = (2, SSD_GROUPS, SSD_STATE, D // SSD_GROUPS)
    st_spec = pl.BlockSpec((1,) + st_shape, lambda b, t: (b, 0, 0, 0, 0))
    small = lambda shape: pl.BlockSpec(shape, lambda b, t: (0,) * len(shape))
    return pl.pallas_call(
        functools.partial(_ssd_kernel, nc=nc),
        grid=(B, 2 * nc),
        in_specs=[pl.BlockSpec((1, CHUNK, D), lambda b, t: (b, chunk(t), 0)),
                  pl.BlockSpec((1, CHUNK, XBC - D), lambda b, t: (b, chunk(t), D // (XBC - D))),
                  pl.BlockSpec((1, CHUNK, D), lambda b, t: (b, late(t), U_Z // D)),
                  pl.BlockSpec((1, CHUNK, DT_PAD), lambda b, t: (b, chunk(t), 0)),
                  pl.BlockSpec((1, H2, CHUNK), lambda b, t: (b, 0, chunk(t))),
                  st_spec,
                  small((1, H2)), small((H2, 1)), small((1, H2)), small((H2, 1)),
                  small((2, D)), small((1, D))],
        out_specs=[pl.BlockSpec((1, CHUNK, D), lambda b, t: (b, late(t), 0)),
                   st_spec],
        out_shape=[jax.ShapeDtypeStruct((B, L, D), BF16),
                   jax.ShapeDtypeStruct((B,) + st_shape, F32)],
        scratch_shapes=[pltpu.VMEM((L, D), F32), pltpu.VMEM(st_shape, F32)],
        compiler_params=_cparams(("parallel", "arbitrary")),
        name="ssd_mix",
    )(xbc_c, xbc_c, u, dt, dtT, h0,
      a_log.reshape(1, H2), a_log.reshape(H2, 1), dt_bias.reshape(1, H2), dt_bias.reshape(H2, 1),
      d_skip_e, norm_w.reshape(1, D))


def _head_stack(x2, lane):
    zero = jnp.zeros_like(x2)
    return jnp.concatenate([jnp.where(lane < HEAD_DIM, x2, zero), jnp.where(lane >= HEAD_DIM, x2, zero)], axis=0)


def _softmax2(s_a, s_b):
    m = jnp.maximum(jnp.max(s_a, axis=-1, keepdims=True), jnp.max(s_b, axis=-1, keepdims=True))
    p_a = jnp.exp(s_a - m)
    p_b = jnp.exp(s_b - m)
    inv = 1.0 / (jnp.sum(p_a, axis=-1, keepdims=True) + jnp.sum(p_b, axis=-1, keepdims=True))
    return (p_a * inv).astype(BF16), (p_b * inv).astype(BF16)


def _na_kernel(q_ref, k_ref, v_ref, kc_ref, vc_ref, cos_ref, sa_ref, sb_ref, bias_ref, o_ref, qr_ref, kr_ref, *, rows):
    W = GRID_W
    band = NA_KH * W

    def rope(x):
        return (x * cos_ref[...] + pltpu.roll(x, 128 - 16, axis=1) * sa_ref[...]
                + pltpu.roll(x, 16, axis=1) * sb_ref[...])

    qr_ref[...] = (rope(q_ref[0].astype(F32)) * (HEAD_DIM ** -0.5)).astype(BF16)
    kr_ref[...] = rope(k_ref[0].astype(F32)).astype(BF16)
    lane = lax.broadcasted_iota(jnp.int32, (W, 2 * HEAD_DIM), 1)
    kc = kc_ref[0]
    vc = vc_ref[0]

    def row_body(r, carry):
        r0 = jnp.clip(r - NA_KH // 2, 0, rows - NA_KH)
        qs = _head_stack(qr_ref[pl.ds(pl.multiple_of(r * W, W), W), :], lane)
        kb = kr_ref[pl.ds(pl.multiple_of(r0 * W, W), band), :]
        vb = v_ref[0, pl.ds(pl.multiple_of(r0 * W, W), band), :]
        s_lat = _nt(qs, kb) + bias_ref[0, r - r0]
        s_ctx = _nt(qs, kc)
        p_lat, p_ctx = _softmax2(s_lat, s_ctx)
        o2 = _dot(p_lat, vb) + _dot(p_ctx, vc)
        o = jnp.where(lane < HEAD_DIM, o2[:W], o2[W:])
        o_ref[0, pl.ds(pl.multiple_of(r * W, W), W), :] = o.astype(o_ref.dtype)
        return carry

    lax.fori_loop(0, rows, row_body, 0)


def na_attend(u, u_ctx, cos, sa, sb, bias):
    B, L, _ = u.shape
    Lc = u_ctx.shape[1]
    HP = N_HEADS // 2
    lw = 2 * HEAD_DIM
    oq, ok, ov = U_Q // lw, U_K // lw, U_V // lw
    return pl.pallas_call(
        functools.partial(_na_kernel, rows=L // GRID_W),
        grid=(HP, B),
        in_specs=[pl.BlockSpec((1, L, lw), lambda h, b: (b, 0, oq + h)),
                  pl.BlockSpec((1, L, lw), lambda h, b: (b, 0, ok + h)),
                  pl.BlockSpec((1, L, lw), lambda h, b: (b, 0, ov + h)),
                  pl.BlockSpec((1, Lc, lw), lambda h, b: (b, 0, ok + h)),
                  pl.BlockSpec((1, Lc, lw), lambda h, b: (b, 0, ov + h)),
                  pl.BlockSpec((L, lw), lambda h, b: (0, 0)),
                  pl.BlockSpec((L, lw), lambda h, b: (0, 0)),
                  pl.BlockSpec((L, lw), lambda h, b: (0, 0)),
                  pl.BlockSpec((1, NA_KH, 2 * GRID_W, NA_KH * GRID_W), lambda h, b: (h, 0, 0, 0))],
        out_specs=pl.BlockSpec((1, L, lw), lambda h, b: (b, 0, h)),
        out_shape=jax.ShapeDtypeStruct((B, L, D), BF16),
        scratch_shapes=[pltpu.VMEM((L, lw), BF16), pltpu.VMEM((L, lw), BF16)],
        compiler_params=_cparams(("parallel", "parallel")),
        name="na_attend",
    )(u, u, u, u_ctx, u_ctx, cos, sa, sb, bias)


def _ctxattn_kernel(q_ref, k_ref, v_ref, o_ref):
    Lc = q_ref.shape[1]
    lane = lax.broadcasted_iota(jnp.int32, (Lc, 2 * HEAD_DIM), 1)
    q2 = (q_ref[0].astype(F32) * (HEAD_DIM ** -0.5)).astype(BF16)
    s = _nt(_head_stack(q2, lane), k_ref[0])
    m = jnp.max(s, axis=-1, keepdims=True)
    p = jnp.exp(s - m)
    p = (p * (1.0 / jnp.sum(p, axis=-1, keepdims=True))).astype(BF16)
    o2 = _dot(p, v_ref[0])
    o_ref[0] = jnp.where(lane < HEAD_DIM, o2[:Lc], o2[Lc:]).astype(o_ref.dtype)


def ctx_attend(u_ctx):
    B, Lc, _ = u_ctx.shape
    HP = N_HEADS // 2
    lw = 2 * HEAD_DIM
    oq, ok, ov = U_Q // lw, U_K // lw, U_V // lw
    return pl.pallas_call(
        _ctxattn_kernel,
        grid=(HP, B),
        in_specs=[pl.BlockSpec((1, Lc, lw), lambda h, b: (b, 0, oq + h)),
                  pl.BlockSpec((1, Lc, lw), lambda h, b: (b, 0, ok + h)),
                  pl.BlockSpec((1, Lc, lw), lambda h, b: (b, 0, ov + h))],
        out_specs=pl.BlockSpec((1, Lc, lw), lambda h, b: (b, 0, h)),
        out_shape=jax.ShapeDtypeStruct((B, Lc, D), BF16),
        compiler_params=_cparams(("parallel", "parallel")),
        name="ctx_attend",
    )(u_ctx, u_ctx, u_ctx)


def _merge_kernel(yc_ref, ys_ref, yn_ref, g_ref, h_ref, gt_ref, wc_ref, ws_ref, wn_ref, wo_ref, o_ref):
    g = jax.nn.sigmoid(g_ref[0].astype(F32))
    m = (g[:, 0:D] * _dot(yc_ref[0], wc_ref[...])
         + g[:, D:2 * D] * _dot(ys_ref[0], ws_ref[...])
         + g[:, 2 * D:3 * D] * _dot(yn_ref[0], wn_ref[...]))
    o_ref[0] = h_ref[0] + gt_ref[0] * _dot(m.astype(BF16), wo_ref[...])


def merge(yc, ys, yn, u, h, gt, wc, ws, wn, wo, tm):
    B, L, _ = h.shape
    tok = lambda: pl.BlockSpec((1, tm, D), lambda b, i: (b, i, 0))
    wsp = lambda: pl.BlockSpec((D, D), lambda b, i: (0, 0))
    return pl.pallas_call(
        _merge_kernel,
        grid=(B, L // tm),
        in_specs=[tok(), tok(), tok(),
                  pl.BlockSpec((1, tm, 3 * D), lambda b, i: (b, i, U_GATE // (3 * D))),
                  tok(),
                  pl.BlockSpec((1, 1, D), lambda b, i: (b, 0, 0)),
                  wsp(), wsp(), wsp(), wsp()],
        out_specs=tok(),
        out_shape=jax.ShapeDtypeStruct((B, L, D), F32),
        compiler_params=_cparams(("parallel", "parallel")),
        name="merge",
    )(yc, ys, yn, u, h, gt, wc, ws, wn, wo)


def _mlp_kernel(h_ref, sh_ref, sc_ref, gt_ref, nw_ref, fw_ref, w1_ref, w2_ref, o_ref, xn_ref, acc_ref, *, nf, final):
    k = pl.program_id(2)

    @pl.when(k == 0)
    def _():
        xn_ref[...] = _norm_mod(h_ref[0], nw_ref[...], sc_ref[0], sh_ref[0]).astype(BF16)
        acc_ref[...] = jnp.zeros_like(acc_ref)

    a = jnp.square(jnp.maximum(_dot(xn_ref[...], w1_ref[...]), 0.0))
    acc_ref[...] += _dot(a.astype(BF16), w2_ref[...])

    @pl.when(k == nf - 1)
    def _():
        o = h_ref[0] + gt_ref[0] * acc_ref[...]
        if final:
            ms = jnp.mean(o * o, axis=-1, keepdims=True)
            o = o * lax.rsqrt(ms + EPS) * fw_ref[...]
        o_ref[0] = o


def mlp(h, sh, sc, gt, nw, fw, w1, w2, tm, final, tf=512):
    B, L, _ = h.shape
    nf = D_FF // tf
    vec = lambda: pl.BlockSpec((1, 1, D), lambda b, i, k: (b, 0, 0))
    par = lambda: pl.BlockSpec((1, D), lambda b, i, k: (0, 0))
    return pl.pallas_call(
        functools.partial(_mlp_kernel, nf=nf, final=final),
        grid=(B, L // tm, nf),
        in_specs=[pl.BlockSpec((1, tm, D), lambda b, i, k: (b, i, 0)),
                  vec(), vec(), vec(), par(), par(),
                  pl.BlockSpec((D, tf), lambda b, i, k: (0, k)),
                  pl.BlockSpec((tf, D), lambda b, i, k: (k, 0))],
        out_specs=pl.BlockSpec((1, tm, D), lambda b, i, k: (b, i, 0)),
        out_shape=jax.ShapeDtypeStruct((B, L, D), F32),
        scratch_shapes=[pltpu.VMEM((tm, D), BF16), pltpu.VMEM((tm, D), F32)],
        compiler_params=_cparams(("parallel", "parallel", "arbitrary")),
        name="mlp",
    )(h, sh, sc, gt, nw, fw, w1, w2)


def _rope_tables(L):
    t = jnp.arange(L, dtype=jnp.int32)
    row = (t // GRID_W).astype(F32)
    col = (t % GRID_W).astype(F32)
    half = HEAD_DIM // 2
    inv = ROPE_BASE ** (-jnp.arange(0, half, 2, dtype=F32) / half)
    ang_r = row[:, None] * inv
    ang_c = col[:, None] * inv
    ang = jnp.concatenate([ang_r, ang_r, ang_c, ang_c], axis=-1)
    cos = jnp.tile(jnp.cos(ang), (1, 2))
    sin = jnp.tile(jnp.sin(ang), (1, 2))
    even = ((jnp.arange(2 * HEAD_DIM) // (half // 2)) % 2 == 0)[None, :]
    return cos, jnp.where(even, -sin, 0.0), jnp.where(even, 0.0, sin)


def _na_bias_table(rpb):
    H = rpb.shape[0]
    col = np.arange(GRID_W)
    col_start = np.clip(col - NA_KW // 2, 0, GRID_W - NA_KW)
    col_ok = (col[None, :] >= col_start[:, None]) & (col[None, :] < col_start[:, None] + NA_KW)
    dc_idx = np.clip(col[None, :] - col[:, None], -(NA_KW - 1), NA_KW - 1) + NA_KW - 1
    var = np.arange(NA_KH)
    dr_idx = np.arange(NA_KH)[None, :] - var[:, None] + NA_KH - 1
    tab = rpb[:, dr_idx[:, :, None, None], dc_idx[None, None, :, :]]
    tab = jnp.where(col_ok[None, None, None], tab, -jnp.inf)
    tab = jnp.transpose(tab, (0, 1, 3, 2, 4)).reshape(H, NA_KH, GRID_W, NA_KH * GRID_W)
    tab = tab.reshape(H // 2, 2, NA_KH, GRID_W, NA_KH * GRID_W)
    return jnp.transpose(tab, (0, 2, 1, 3, 4)).reshape(H // 2, NA_KH, 2 * GRID_W, NA_KH * GRID_W)


def _prep_w_in(w_in):
    order = [(R_GATE, 3 * D), (R_CB, D), (R_CC, D), (R_CX, D), (R_Z, D), (R_Q, D), (R_K, D), (R_V, D), (R_XBC, XBC)]
    w = jnp.concatenate([w_in[:, o:o + n] for o, n in order], axis=1).astype(BF16)
    wdt = w_in[:, R_DT:R_DT + 2 * N_HEADS].astype(BF16)
    return w, jnp.pad(wdt, ((0, 0), (0, DT_PAD - 2 * N_HEADS))), wdt.T


def kernel(x, c, ctx, c_ctx, w_ada, b_ada, norm1_w, w_in, conv_mix_w, ssd_conv_w, ssd_conv_b, ssd_a_log, ssd_dt_bias,
           ssd_d, ssd_norm_w, na_rpb, w_br_conv, w_br_ssd, w_br_na, w_out, norm2_w, w_ff1, w_ff2, final_norm_w):
    B, L, _ = x.shape
    Lc = ctx.shape[1]
    depth = w_in.shape[0]
    cos, sa, sb = _rope_tables(L)
    n_mod = B + 1
    pad = (-n_mod) % 8
    c_rows = jnp.concatenate([c, c_ctx[None, :], jnp.zeros((pad, D), F32)], axis=0)
    zero_state = jnp.zeros((B, 2, SSD_GROUPS, SSD_STATE, D // SSD_GROUPS), F32)
    fw = final_norm_w.reshape(1, D)
    h, hc = x, ctx
    for l in range(depth):
        last = l == depth - 1
        mod = ada_mod(c_rows, w_ada[l], b_ada[l])
        m_lat = mod[:B].reshape(B, 1, 6, D)
        m_ctx = jnp.broadcast_to(mod[B:B + 1], (B, 6 * D)).reshape(B, 1, 6, D)
        sh1, sc1, gt1, sh2, sc2, gt2 = (m_lat[:, :, i] for i in range(6))
        csh1, csc1, cgt1, csh2, csc2, cgt2 = (m_ctx[:, :, i] for i in range(6))
        w, wdt, wdtT = _prep_w_in(w_in[l])
        nw1 = norm1_w[l].reshape(1, D)
        u, dt, dtT = in_proj(h, sh1, sc1, nw1, w, wdt, wdtT, tm=1024)
        uc, dtc, dtTc = in_proj(hc, csh1, csc1, nw1, w, wdt, wdtT, tm=Lc)
        d_skip_e = jnp.repeat(ssd_d[l], HEAD_DIM, axis=1)
        ssd_p = (ssd_a_log[l], ssd_dt_bias[l], d_skip_e, ssd_norm_w[l])
        xbc_c = xbc_conv(uc, ssd_conv_w[l], ssd_conv_b[l])
        y_ssd_c, ctx_states = ssd_mix(xbc_c, uc, dtc, dtTc, zero_state, *ssd_p)
        xbc_l = xbc_conv(u, ssd_conv_w[l], ssd_conv_b[l])
        y_ssd, _ = ssd_mix(xbc_l, u, dt, dtT, ctx_states, *ssd_p)
        y_na = na_attend(u, uc, cos, sa, sb, _na_bias_table(na_rpb[l]))
        y_conv = conv_mix(u, conv_mix_w[l])
        wb = [t[l].astype(BF16) for t in (w_br_conv, w_br_ssd, w_br_na, w_out)]
        w1, w2 = w_ff1[l].astype(BF16), w_ff2[l].astype(BF16)
        nw2 = norm2_w[l].reshape(1, D)
        h = merge(y_conv, y_ssd, y_na, u, h, gt1, *wb, tm=512)
        h = mlp(h, sh2, sc2, gt2, nw2, fw, w1, w2, tm=1024, final=last)
        if not last:
            y_conv_c = conv_mix(uc, conv_mix_w[l])
            y_na_c = ctx_attend(uc)
            hc = merge(y_conv_c, y_ssd_c, y_na_c, uc, hc, cgt1, *wb, tm=Lc)
            hc = mlp(hc, csh2, csc2, cgt2, nw2, fw, w1, w2, tm=Lc, final=False)
    return h
```

```python
import functools
import math

import jax
import jax.numpy as jnp
import numpy as np
from jax import lax
from jax.experimental import pallas as pl
from jax.experimental.pallas import tpu as pltpu

F32 = jnp.float32
BF16 = jnp.bfloat16
HIGHEST = lax.Precision.HIGHEST

D = 1024
EPS = 1e-6
GRID_W = 64
N_HEADS = 16
HEAD_DIM = 64
SSD_GROUPS = 2
SSD_STATE = 128
CHUNK = 128
NA_KH = 8
NA_KW = 16
ROPE_BASE = 10000.0
D_FF = 4 * D
XBC = D + 2 * SSD_GROUPS * SSD_STATE
U_GATE, U_CB, U_CC, U_CX, U_Z, U_Q, U_K, U_V, U_XS, U_BC, U_COLS = (
    0, 3072, 4096, 5120, 6144, 7168, 8192, 9216, 10240, 11264, 11776)
R_CB, R_CC, R_CX, R_Z, R_XBC, R_DT, R_Q, R_K, R_V, R_GATE = (
    0, 1024, 2048, 3072, 4096, 5632, 5664, 6688, 7712, 8736)
DT_PAD = 128
VMEM_LIMIT = 56 * 1024 * 1024


def _cparams(sem):
    return pltpu.CompilerParams(dimension_semantics=sem, vmem_limit_bytes=VMEM_LIMIT)


def _nt(a, b):
    return lax.dot_general(a, b, (((1,), (1,)), ((), ())), preferred_element_type=F32)


def _tn(a, b):
    return lax.dot_general(a, b, (((0,), (0,)), ((), ())), preferred_element_type=F32)


def _dot(a, b):
    return jnp.dot(a, b, preferred_element_type=F32)


def _dot_hi(a, b):
    return jnp.dot(a, b, preferred_element_type=F32, precision=HIGHEST)


def _silu(x):
    return x * jax.nn.sigmoid(x)


def _norm_mod(x, nw, sc, sh):
    ms = jnp.mean(x * x, axis=-1, keepdims=True)
    y = x * lax.rsqrt(ms + EPS) * nw
    return y * (1.0 + sc) + sh


def _ada_kernel(c_ref, w_ref, b_ref, o_ref):
    o_ref[...] = _dot_hi(_silu(c_ref[...]), w_ref[...]) + b_ref[...]


def ada_mod(c_rows, w_ada, b_ada):
    rows = c_rows.shape[0]
    tn = 1536
    return pl.pallas_call(
        _ada_kernel,
        grid=(6 * D // tn,),
        in_specs=[pl.BlockSpec((rows, D), lambda j: (0, 0)),
                  pl.BlockSpec((D, tn), lambda j: (0, j)),
                  pl.BlockSpec((1, tn), lambda j: (0, j))],
        out_specs=pl.BlockSpec((rows, tn), lambda j: (0, j)),
        out_shape=jax.ShapeDtypeStruct((rows, 6 * D), F32),
        compiler_params=_cparams(("arbitrary",)),
        name="ada_mod",
    )(c_rows, w_ada, b_ada.reshape(1, 6 * D))


def _inproj_kernel(x_ref, sh_ref, sc_ref, nw_ref, w_ref, wdt_ref, wdtT_ref, u_ref, dt_ref, dtT_ref, xn_ref):
    @pl.when(pl.program_id(2) == 0)
    def _():
        xb = _norm_mod(x_ref[0], nw_ref[...], sc_ref[0], sh_ref[0]).astype(BF16)
        xn_ref[...] = xb
        dt_ref[0] = _dot(xb, wdt_ref[...])
        dtT_ref[0] = _nt(wdtT_ref[...], xb)

    u_ref[0] = _dot(xn_ref[...], w_ref[...]).astype(u_ref.dtype)


def in_proj(x, sh, sc, nw, w, wdt, wdtT, tm, tn=512):
    B, L, _ = x.shape
    grid = (B, L // tm, U_COLS // tn)
    return pl.pallas_call(
        _inproj_kernel,
        grid=grid,
        in_specs=[pl.BlockSpec((1, tm, D), lambda b, i, j: (b, i, 0)),
                  pl.BlockSpec((1, 1, D), lambda b, i, j: (b, 0, 0)),
                  pl.BlockSpec((1, 1, D), lambda b, i, j: (b, 0, 0)),
                  pl.BlockSpec((1, D), lambda b, i, j: (0, 0)),
                  pl.BlockSpec((D, tn), lambda b, i, j: (0, j)),
                  pl.BlockSpec((D, DT_PAD), lambda b, i, j: (0, 0)),
                  pl.BlockSpec((2 * N_HEADS, D), lambda b, i, j: (0, 0))],
        out_specs=[pl.BlockSpec((1, tm, tn), lambda b, i, j: (b, i, j)),
                   pl.BlockSpec((1, tm, DT_PAD), lambda b, i, j: (b, i, 0)),
                   pl.BlockSpec((1, 2 * N_HEADS, tm), lambda b, i, j: (b, 0, i))],
        out_shape=[jax.ShapeDtypeStruct((B, L, U_COLS), BF16),
                   jax.ShapeDtypeStruct((B, L, DT_PAD), F32),
                   jax.ShapeDtypeStruct((B, 2 * N_HEADS, L), F32)],
        scratch_shapes=[pltpu.VMEM((tm, D), BF16)],
        compiler_params=_cparams(("parallel", "parallel", "arbitrary")),
        name="in_proj",
    )(x, sh, sc, nw, w, wdt, wdtT)


def _dwconv3(p, w):
    L = p.shape[0]
    row = lax.broadcasted_iota(jnp.int32, p.shape, 0)
    prev = jnp.where(row == 0, 0.0, pltpu.roll(p, 1, axis=0))
    nxt = jnp.where(row == L - 1, 0.0, pltpu.roll(p, L - 1, axis=0))
    return prev * w[0:1] + p * w[1:2] + nxt * w[2:3]


def _convmix_kernel(b_ref, c_ref, x_ref, w_ref, o_ref):
    p = c_ref[0].astype(F32) * x_ref[0].astype(F32)
    o_ref[0] = (b_ref[0].astype(F32) * _dwconv3(p, w_ref[...])).astype(o_ref.dtype)


def conv_mix(u, w, tc=256):
    B, L, _ = u.shape
    ob, oc, ox = U_CB // tc, U_CC // tc, U_CX // tc
    return pl.pallas_call(
        _convmix_kernel,
        grid=(B, D // tc),
        in_specs=[pl.BlockSpec((1, L, tc), lambda b, j: (b, 0, ob + j)),
                  pl.BlockSpec((1, L, tc), lambda b, j: (b, 0, oc + j)),
                  pl.BlockSpec((1, L, tc), lambda b, j: (b, 0, ox + j)),
                  pl.BlockSpec((3, tc), lambda b, j: (0, j))],
        out_specs=pl.BlockSpec((1, L, tc), lambda b, j: (b, 0, j)),
        out_shape=jax.ShapeDtypeStruct((B, L, D), BF16),
        compiler_params=_cparams(("parallel", "parallel")),
        name="conv_mix",
    )(u, u, u, w)


def _xbcconv_kernel(x_ref, w_ref, b_ref, o_ref):
    y = _dwconv3(x_ref[0].astype(F32), w_ref[...]) + b_ref[...]
    o_ref[0] = _silu(y).astype(o_ref.dtype)


def xbc_conv(u, w, bias, tc=256):
    B, L, _ = u.shape
    off = U_XS // tc
    return pl.pallas_call(
        _xbcconv_kernel,
        grid=(B, XBC // tc),
        in_specs=[pl.BlockSpec((1, L, tc), lambda b, j: (b, 0, off + j)),
                  pl.BlockSpec((3, tc), lambda b, j: (0, j)),
                  pl.BlockSpec((1, tc), lambda b, j: (0, j))],
        out_specs=pl.BlockSpec((1, L, tc), lambda b, j: (b, 0, j)),
        out_shape=jax.ShapeDtypeStruct((B, L, XBC), BF16),
        compiler_params=_cparams(("parallel", "parallel")),
        name="xbc_conv",
    )(u, w, bias.reshape(1, XBC))


def _split3(x):
    hi = x.astype(BF16)
    r1 = x - hi.astype(F32)
    mid = r1.astype(BF16)
    lo = (r1 - mid.astype(F32)).astype(BF16)
    return hi, mid, lo


def _head_stack(x2, lane):
    zero = jnp.zeros_like(x2)
    return jnp.concatenate([jnp.where(lane < HEAD_DIM, x2, zero), jnp.where(lane >= HEAD_DIM, x2, zero)], axis=0)


def _ssd_chunk(d, rows, xs_ref, bc_ref, z_ref, dt_ref, dtT_ref, alr_ref, alc_ref, dbr_ref, dbc_ref, dsk_ref, nw_ref,
               ex_ref, y_ref, yacc_ref, st_ref):
    H = N_HEADS
    xsb = xs_ref[0]
    xs = xsb.astype(F32)
    bc = bc_ref[0]
    dt_c = jax.nn.softplus(dt_ref[0] + dbr_ref[...])
    a_c = dt_c * -jnp.exp(alr_ref[...])
    dtT_c = jax.nn.softplus(dtT_ref[0] + dbc_ref[...])
    aT_c = dtT_c * -jnp.exp(alc_ref[...])

    ri = lax.broadcasted_iota(jnp.int32, (CHUNK, CHUNK), 0)
    ci = lax.broadcasted_iota(jnp.int32, (CHUNK, CHUNK), 1)
    tri = (ri >= ci) if d == 0 else (ri <= ci)
    trib = tri.astype(BF16)
    tribT = ((ri <= ci) if d == 0 else (ri >= ci)).astype(BF16)
    cs = _dot(jnp.concatenate([trib] * 3, axis=1), jnp.concatenate(_split3(a_c), axis=0))
    csT = _dot(jnp.concatenate(_split3(aT_c), axis=1), jnp.concatenate([tribT] * 3, axis=0))
    last = cs[CHUNK - 1:CHUNK] if d == 0 else cs[0:1]
    w_c = dt_c * jnp.exp(last - cs)
    ein_c = jnp.exp(cs)
    ex = ex_ref[d]
    w_e = _dot(jnp.concatenate(_split3(w_c), axis=1), ex)
    ein_e = _dot(jnp.concatenate(_split3(ein_c), axis=1), ex)
    cd_e = ein_e[CHUNK - 1:CHUNK] if d == 0 else ein_e[0:1]
    Xdec = (xs * w_e).astype(BF16)
    rowT = csT - jnp.log(dtT_c)
    lane = lax.broadcasted_iota(jnp.int32, (CHUNK, 2 * HEAD_DIM), 1)

    ys = []
    GW = D // SSD_GROUPS
    for g in range(SSD_GROUPS):
        Bg = bc[:, g * SSD_STATE:(g + 1) * SSD_STATE]
        Cg = bc[:, (SSD_GROUPS + g) * SSD_STATE:(SSD_GROUPS + g + 1) * SSD_STATE]
        S = _nt(Cg, Bg)
        st = st_ref[d, g]
        y_off = _dot(Cg, st.astype(BF16)) * ein_e[:, g * GW:(g + 1) * GW]
        y_diag = []
        for p in range(GW // (2 * HEAD_DIM)):
            hA = g * (H // SSD_GROUPS) + 2 * p
            Ms = []
            for h in (hA, hA + 1):
                col = cs[:, d * H + h:d * H + h + 1]
                rowv = rowT[d * H + h:d * H + h + 1, :]
                Ms.append((S * jnp.exp(jnp.where(tri, col - rowv, -jnp.inf))).astype(BF16))
            Mcat = jnp.concatenate(Ms, axis=1)
            Xp = xsb[:, hA * HEAD_DIM:(hA + 2) * HEAD_DIM]
            y_diag.append(_dot(Mcat, _head_stack(Xp, lane)))
        ys.append(y_off + jnp.concatenate(y_diag, axis=1))
        st_ref[d, g] = st * cd_e[:, g * GW:(g + 1) * GW] + _tn(Bg, Xdec[:, g * GW:(g + 1) * GW])
    y = jnp.concatenate(ys, axis=1)

    if d == 0:
        yacc_ref[rows, :] = y
    else:
        ysum = yacc_ref[rows, :] + y + xs * (dsk_ref[0:1] + dsk_ref[1:2])
        yg = ysum * _silu(z_ref[0].astype(F32))
        parts = []
        for g in range(SSD_GROUPS):
            v = yg[:, g * GW:(g + 1) * GW]
            ms = jnp.mean(v * v, axis=-1, keepdims=True)
            parts.append(v * lax.rsqrt(ms + EPS) * nw_ref[:, g * GW:(g + 1) * GW])
        y_ref[0] = jnp.concatenate(parts, axis=1).astype(y_ref.dtype)


def _ssd_kernel(xs_ref, bc_ref, z_ref, dt_ref, dtT_ref, h0_ref, alr_ref, alc_ref, dbr_ref, dbc_ref, dsk_ref, nw_ref,
                ex_ref, y_ref, sto_ref, yacc_ref, st_ref, *, nc):
    t = pl.program_id(1)

    @pl.when(t == 0)
    def _():
        st_ref[...] = h0_ref[0]

    args = (xs_ref, bc_ref, z_ref, dt_ref, dtT_ref, alr_ref, alc_ref, dbr_ref, dbc_ref, dsk_ref, nw_ref,
            ex_ref, y_ref, yacc_ref, st_ref)

    @pl.when(t < nc)
    def _():
        _ssd_chunk(0, pl.ds(pl.multiple_of(t * CHUNK, CHUNK), CHUNK), *args)

    @pl.when(t >= nc)
    def _():
        _ssd_chunk(1, pl.ds(pl.multiple_of((2 * nc - 1 - t) * CHUNK, CHUNK), CHUNK), *args)

    @pl.when(t == 2 * nc - 1)
    def _():
        sto_ref[0] = st_ref[...]


def ssd_mix(xbc_c, u, dt, dtT, h0, a_log, dt_bias, d_skip_e, norm_w):
    B, L, _ = u.shape
    nc = L // CHUNK
    H2 = 2 * N_HEADS

    def chunk(t):
        return jnp.where(t < nc, t, 2 * nc - 1 - t)

    def late(t):
        return jnp.where(t < nc, nc - 1, 2 * nc - 1 - t)

    st_shape = (2, SSD_GROUPS, SSD_STATE, D // SSD_GROUPS)
    st_spec = pl.BlockSpec((1,) + st_shape, lambda b, t: (b, 0, 0, 0, 0))
    small = lambda shape: pl.BlockSpec(shape, lambda b, t: (0,) * len(shape))
    krow = jnp.arange(3 * DT_PAD, dtype=jnp.int32)[:, None] % DT_PAD
    head = jnp.arange(D, dtype=jnp.int32)[None, :] // HEAD_DIM
    ex = jnp.stack([krow == d * N_HEADS + head for d in range(2)]).astype(BF16)
    lane_pad = lambda v: jnp.pad(v.reshape(1, H2), ((0, 0), (0, DT_PAD - H2)))
    return pl.pallas_call(
        functools.partial(_ssd_kernel, nc=nc),
        grid=(B, 2 * nc),
        in_specs=[pl.BlockSpec((1, CHUNK, D), lambda b, t: (b, chunk(t), 0)),
                  pl.BlockSpec((1, CHUNK, XBC - D), lambda b, t: (b, chunk(t), D // (XBC - D))),
                  pl.BlockSpec((1, CHUNK, D), lambda b, t: (b, late(t), U_Z // D)),
                  pl.BlockSpec((1, CHUNK, DT_PAD), lambda b, t: (b, chunk(t), 0)),
                  pl.BlockSpec((1, H2, CHUNK), lambda b, t: (b, 0, chunk(t))),
                  st_spec,
                  small((1, DT_PAD)), small((H2, 1)), small((1, DT_PAD)), small((H2, 1)),
                  small((2, D)), small((1, D)), small((2, 3 * DT_PAD, D))],
        out_specs=[pl.BlockSpec((1, CHUNK, D), lambda b, t: (b, late(t), 0)),
                   st_spec],
        out_shape=[jax.ShapeDtypeStruct((B, L, D), BF16),
                   jax.ShapeDtypeStruct((B,) + st_shape, F32)],
        scratch_shapes=[pltpu.VMEM((L, D), F32), pltpu.VMEM(st_shape, F32)],
        compiler_params=_cparams(("parallel", "arbitrary")),
        name="ssd_mix",
    )(xbc_c, xbc_c, u, dt, dtT, h0,
      lane_pad(a_log), a_log.reshape(H2, 1), lane_pad(dt_bias), dt_bias.reshape(H2, 1),
      d_skip_e, norm_w.reshape(1, D), ex)


NA_UNROLL = 4


def _na_kernel(q_ref, k_ref, v_ref, kc_ref, vc_ref, cos_ref, sa_ref, sb_ref, bias_ref, o_ref, qr_ref, kr_ref, *, rows):
    W = GRID_W
    band = NA_KH * W

    def rope(x):
        return (x * cos_ref[...] + pltpu.roll(x, 128 - 16, axis=1) * sa_ref[...]
                + pltpu.roll(x, 16, axis=1) * sb_ref[...])

    qr_ref[...] = (rope(q_ref[0].astype(F32)) * (HEAD_DIM ** -0.5)).astype(BF16)
    kr_ref[...] = rope(k_ref[0].astype(F32)).astype(BF16)
    lane = lax.broadcasted_iota(jnp.int32, (W, 2 * HEAD_DIM), 1)
    kc = kc_ref[0]
    vc = vc_ref[0]

    def row_body(r, carry):
        r0 = jnp.clip(r - NA_KH // 2, 0, rows - NA_KH)
        qs = _head_stack(qr_ref[pl.ds(pl.multiple_of(r * W, W), W), :], lane)
        kb = kr_ref[pl.ds(pl.multiple_of(r0 * W, W), band), :]
        vb = v_ref[0, pl.ds(pl.multiple_of(r0 * W, W), band), :]
        base = r0 - r + NA_KH - 1
        bias = jnp.concatenate([bias_ref[0, base + 2 * jp] for jp in range(NA_KH // 2)], axis=1)
        s_lat = _nt(qs, kb) + bias
        s_ctx = _nt(qs, kc)
        m = jnp.maximum(jnp.max(s_lat, axis=-1, keepdims=True), jnp.max(s_ctx, axis=-1, keepdims=True))
        p_lat = jnp.exp(s_lat - m)
        p_ctx = jnp.exp(s_ctx - m)
        den = jnp.sum(p_lat, axis=-1, keepdims=True) + jnp.sum(p_ctx, axis=-1, keepdims=True)
        o2 = (_dot(p_lat.astype(BF16), vb) + _dot(p_ctx.astype(BF16), vc)) * (1.0 / den)
        o = jnp.where(lane < HEAD_DIM, o2[:W], o2[W:])
        o_ref[0, pl.ds(pl.multiple_of(r * W, W), W), :] = o.astype(o_ref.dtype)
        return carry

    lax.fori_loop(0, rows, row_body, 0, unroll=NA_UNROLL)


def na_attend(u, u_ctx, cos, sa, sb, bias):
    B, L, _ = u.shape
    Lc = u_ctx.shape[1]
    HP = N_HEADS // 2
    lw = 2 * HEAD_DIM
    oq, ok, ov = U_Q // lw, U_K // lw, U_V // lw
    return pl.pallas_call(
        functools.partial(_na_kernel, rows=L // GRID_W),
        grid=(HP, B),
        in_specs=[pl.BlockSpec((1, L, lw), lambda h, b: (b, 0, oq + h)),
                  pl.BlockSpec((1, L, lw), lambda h, b: (b, 0, ok + h)),
                  pl.BlockSpec((1, L, lw), lambda h, b: (b, 0, ov + h)),
                  pl.BlockSpec((1, Lc, lw), lambda h, b: (b, 0, ok + h)),
                  pl.BlockSpec((1, Lc, lw), lambda h, b: (b, 0, ov + h)),
                  pl.BlockSpec((L, lw), lambda h, b: (0, 0)),
                  pl.BlockSpec((L, lw), lambda h, b: (0, 0)),
                  pl.BlockSpec((L, lw), lambda h, b: (0, 0)),
                  pl.BlockSpec((1, 2 * NA_KH - 2, 2 * GRID_W, 2 * GRID_W), lambda h, b: (h, 0, 0, 0))],
        out_specs=pl.BlockSpec((1, L, lw), lambda h, b: (b, 0, h)),
        out_shape=jax.ShapeDtypeStruct((B, L, D), BF16),
        scratch_shapes=[pltpu.VMEM((L, lw), BF16), pltpu.VMEM((L, lw), BF16)],
        compiler_params=_cparams(("parallel", "parallel")),
        name="na_attend",
    )(u, u, u, u_ctx, u_ctx, cos, sa, sb, bias)


def _ctxattn_kernel(q_ref, k_ref, v_ref, o_ref):
    Lc = q_ref.shape[1]
    lane = lax.broadcasted_iota(jnp.int32, (Lc, 2 * HEAD_DIM), 1)
    q2 = (q_ref[0].astype(F32) * (HEAD_DIM ** -0.5)).astype(BF16)
    s = _nt(_head_stack(q2, lane), k_ref[0])
    m = jnp.max(s, axis=-1, keepdims=True)
    p = jnp.exp(s - m)
    p = (p * (1.0 / jnp.sum(p, axis=-1, keepdims=True))).astype(BF16)
    o2 = _dot(p, v_ref[0])
    o_ref[0] = jnp.where(lane < HEAD_DIM, o2[:Lc], o2[Lc:]).astype(o_ref.dtype)


def ctx_attend(u_ctx):
    B, Lc, _ = u_ctx.shape
    HP = N_HEADS // 2
    lw = 2 * HEAD_DIM
    oq, ok, ov = U_Q // lw, U_K // lw, U_V // lw
    return pl.pallas_call(
        _ctxattn_kernel,
        grid=(HP, B),
        in_specs=[pl.BlockSpec((1, Lc, lw), lambda h, b: (b, 0, oq + h)),
                  pl.BlockSpec((1, Lc, lw), lambda h, b: (b, 0, ok + h)),
                  pl.BlockSpec((1, Lc, lw), lambda h, b: (b, 0, ov + h))],
        out_specs=pl.BlockSpec((1, Lc, lw), lambda h, b: (b, 0, h)),
        out_shape=jax.ShapeDtypeStruct((B, Lc, D), BF16),
        compiler_params=_cparams(("parallel", "parallel")),
        name="ctx_attend",
    )(u_ctx, u_ctx, u_ctx)


def _merge_kernel(yc_ref, ys_ref, yn_ref, g_ref, h_ref, gt_ref, wc_ref, ws_ref, wn_ref, wo_ref, o_ref):
    g = jax.nn.sigmoid(g_ref[0].astype(F32))
    m = (g[:, 0:D] * _dot(yc_ref[0], wc_ref[...])
         + g[:, D:2 * D] * _dot(ys_ref[0], ws_ref[...])
         + g[:, 2 * D:3 * D] * _dot(yn_ref[0], wn_ref[...]))
    o_ref[0] = h_ref[0] + gt_ref[0] * _dot(m.astype(BF16), wo_ref[...])


def merge(yc, ys, yn, u, h, gt, wc, ws, wn, wo, tm):
    B, L, _ = h.shape
    tok = lambda: pl.BlockSpec((1, tm, D), lambda b, i: (b, i, 0))
    wsp = lambda: pl.BlockSpec((D, D), lambda b, i: (0, 0))
    return pl.pallas_call(
        _merge_kernel,
        grid=(B, L // tm),
        in_specs=[tok(), tok(), tok(),
                  pl.BlockSpec((1, tm, 3 * D), lambda b, i: (b, i, U_GATE // (3 * D))),
                  tok(),
                  pl.BlockSpec((1, 1, D), lambda b, i: (b, 0, 0)),
                  wsp(), wsp(), wsp(), wsp()],
        out_specs=tok(),
        out_shape=jax.ShapeDtypeStruct((B, L, D), F32),
        compiler_params=_cparams(("parallel", "parallel")),
        name="merge",
    )(yc, ys, yn, u, h, gt, wc, ws, wn, wo)


def _mlp_kernel(h_ref, sh_ref, sc_ref, gt_ref, nw_ref, fw_ref, w1_ref, w2_ref, o_ref, xn_ref, acc_ref, *, nf, final):
    k = pl.program_id(2)

    @pl.when(k == 0)
    def _():
        xn_ref[...] = _norm_mod(h_ref[0], nw_ref[...], sc_ref[0], sh_ref[0]).astype(BF16)
        acc_ref[...] = jnp.zeros_like(acc_ref)

    a = jnp.square(jnp.maximum(_dot(xn_ref[...], w1_ref[...]), 0.0))
    acc_ref[...] += _dot(a.astype(BF16), w2_ref[...])

    @pl.when(k == nf - 1)
    def _():
        o = h_ref[0] + gt_ref[0] * acc_ref[...]
        if final:
            ms = jnp.mean(o * o, axis=-1, keepdims=True)
            o = o * lax.rsqrt(ms + EPS) * fw_ref[...]
        o_ref[0] = o


def mlp(h, sh, sc, gt, nw, fw, w1, w2, tm, final, tf=512):
    B, L, _ = h.shape
    nf = D_FF // tf
    vec = lambda: pl.BlockSpec((1, 1, D), lambda b, i, k: (b, 0, 0))
    par = lambda: pl.BlockSpec((1, D), lambda b, i, k: (0, 0))
    return pl.pallas_call(
        functools.partial(_mlp_kernel, nf=nf, final=final),
        grid=(B, L // tm, nf),
        in_specs=[pl.BlockSpec((1, tm, D), lambda b, i, k: (b, i, 0)),
                  vec(), vec(), vec(), par(), par(),
                  pl.BlockSpec((D, tf), lambda b, i, k: (0, k)),
                  pl.BlockSpec((tf, D), lambda b, i, k: (k, 0))],
        out_specs=pl.BlockSpec((1, tm, D), lambda b, i, k: (b, i, 0)),
        out_shape=jax.ShapeDtypeStruct((B, L, D), F32),
        scratch_shapes=[pltpu.VMEM((tm, D), BF16), pltpu.VMEM((tm, D), F32)],
        compiler_params=_cparams(("parallel", "parallel", "arbitrary")),
        name="mlp",
    )(h, sh, sc, gt, nw, fw, w1, w2)


def _rope_tables(L):
    t = jnp.arange(L, dtype=jnp.int32)
    row = (t // GRID_W).astype(F32)
    col = (t % GRID_W).astype(F32)
    half = HEAD_DIM // 2
    inv = ROPE_BASE ** (-jnp.arange(0, half, 2, dtype=F32) / half)
    ang_r = row[:, None] * inv
    ang_c = col[:, None] * inv
    ang = jnp.concatenate([ang_r, ang_r, ang_c, ang_c], axis=-1)
    cos = jnp.tile(jnp.cos(ang), (1, 2))
    sin = jnp.tile(jnp.sin(ang), (1, 2))
    even = ((jnp.arange(2 * HEAD_DIM) // (half // 2)) % 2 == 0)[None, :]
    return cos, jnp.where(even, -sin, 0.0), jnp.where(even, 0.0, sin)


def _rpb_kernel(r_ref, oh_ref, ok_ref, o_ref):
    val = _dot(jnp.concatenate(_split3(r_ref[...]), axis=1), oh_ref[...])
    o_ref[...] = jnp.where(ok_ref[...] > 0.0, val, -jnp.inf)


def _na_bias_table(rpb):
    H, NR, NC = rpb.shape
    W = GRID_W
    col = np.arange(W)
    col_start = np.clip(col - NA_KW // 2, 0, W - NA_KW)
    col_ok = (col[None, :] >= col_start[:, None]) & (col[None, :] < col_start[:, None] + NA_KW)
    dc_idx = np.clip(col[None, :] - col[:, None], -(NA_KW - 1), NA_KW - 1) + NA_KW - 1
    rows = jnp.transpose(rpb.reshape(H // 2, 2, NR, NC), (0, 2, 1, 3)).reshape(H * NR, NC)
    rows = jnp.pad(rows, ((0, 0), (0, DT_PAD - NC)))
    krow = jnp.arange(3 * DT_PAD, dtype=jnp.int32)[:, None] % DT_PAD
    onehot = (krow == jnp.asarray(dc_idx.reshape(1, W * W), jnp.int32)).astype(BF16)
    ok = jnp.asarray(col_ok.reshape(1, W * W), F32)
    tn = 1024
    tab = pl.pallas_call(
        _rpb_kernel,
        grid=(W * W // tn,),
        in_specs=[pl.BlockSpec((H * NR, DT_PAD), lambda j: (0, 0)),
                  pl.BlockSpec((3 * DT_PAD, tn), lambda j: (0, j)),
                  pl.BlockSpec((1, tn), lambda j: (0, j))],
        out_specs=pl.BlockSpec((H * NR, tn), lambda j: (0, j)),
        out_shape=jax.ShapeDtypeStruct((H * NR, W * W), F32),
        compiler_params=_cparams(("arbitrary",)),
        name="rpb_table",
    )(rows, onehot, ok)
    tab = tab.reshape(H // 2, NR, 2 * W, W)
    return jnp.concatenate([tab[:, :NR - 1], tab[:, 1:]], axis=-1)


def _prep_w_in(w_in):
    order = [(R_GATE, 3 * D), (R_CB, D), (R_CC, D), (R_CX, D), (R_Z, D), (R_Q, D), (R_K, D), (R_V, D), (R_XBC, XBC)]
    w = jnp.concatenate([w_in[:, o:o + n] for o, n in order], axis=1).astype(BF16)
    wdt = w_in[:, R_DT:R_DT + 2 * N_HEADS].astype(BF16)
    return w, jnp.pad(wdt, ((0, 0), (0, DT_PAD - 2 * N_HEADS))), wdt.T


def kernel(x, c, ctx, c_ctx, w_ada, b_ada, norm1_w, w_in, conv_mix_w, ssd_conv_w, ssd_conv_b, ssd_a_log, ssd_dt_bias,
           ssd_d, ssd_norm_w, na_rpb, w_br_conv, w_br_ssd, w_br_na, w_out, norm2_w, w_ff1, w_ff2, final_norm_w):
    B, L, _ = x.shape
    Lc = ctx.shape[1]
    depth = w_in.shape[0]
    cos, sa, sb = _rope_tables(L)
    n_mod = B + 1
    pad = (-n_mod) % 8
    c_rows = jnp.concatenate([c, c_ctx[None, :], jnp.zeros((pad, D), F32)], axis=0)
    zero_state = jnp.zeros((B, 2, SSD_GROUPS, SSD_STATE, D // SSD_GROUPS), F32)
    fw = final_norm_w.reshape(1, D)
    Tc = B * Lc
    tmc = min(1024, Tc)
    flat = lambda a: a.reshape(1, Tc, a.shape[-1])
    h, hc = x, flat(ctx)
    for l in range(depth):
        last = l == depth - 1
        mod = ada_mod(c_rows, w_ada[l], b_ada[l])
        m_lat = mod[:B].reshape(B, 1, 6, D)
        m_ctx = mod[B:B + 1].reshape(1, 1, 6, D)
        sh1, sc1, gt1, sh2, sc2, gt2 = (m_lat[:, :, i] for i in range(6))
        csh1, csc1, cgt1, csh2, csc2, cgt2 = (m_ctx[:, :, i] for i in range(6))
        w, wdt, wdtT = _prep_w_in(w_in[l])
        nw1 = norm1_w[l].reshape(1, D)
        u, dt, dtT = in_proj(h, sh1, sc1, nw1, w, wdt, wdtT, tm=1024)
        uc, dtc, dtTc = in_proj(hc, csh1, csc1, nw1, w, wdt, wdtT, tm=tmc)
        uc = uc.reshape(B, Lc, U_COLS)
        dtc = dtc.reshape(B, Lc, DT_PAD)
        dtTc = jnp.transpose(dtTc.reshape(2 * N_HEADS, B, Lc), (1, 0, 2))
        d_skip_e = jnp.repeat(ssd_d[l], HEAD_DIM, axis=1)
        ssd_p = (ssd_a_log[l], ssd_dt_bias[l], d_skip_e, ssd_norm_w[l])
        xbc_c = xbc_conv(uc, ssd_conv_w[l], ssd_conv_b[l])
        y_ssd_c, ctx_states = ssd_mix(xbc_c, uc, dtc, dtTc, zero_state, *ssd_p)
        xbc_l = xbc_conv(u, ssd_conv_w[l], ssd_conv_b[l])
        y_ssd, _ = ssd_mix(xbc_l, u, dt, dtT, ctx_states, *ssd_p)
        y_na = na_attend(u, uc, cos, sa, sb, _na_bias_table(na_rpb[l]))
        y_conv = conv_mix(u, conv_mix_w[l])
        wb = [t[l].astype(BF16) for t in (w_br_conv, w_br_ssd, w_br_na, w_out)]
        w1, w2 = w_ff1[l].astype(BF16), w_ff2[l].astype(BF16)
        nw2 = norm2_w[l].reshape(1, D)
        h = merge(y_conv, y_ssd, y_na, u, h, gt1, *wb, tm=512)
        h = mlp(h, sh2, sc2, gt2, nw2, fw, w1, w2, tm=1024, final=last)
        if not last:
            y_conv_c = conv_mix(uc, conv_mix_w[l])
            y_na_c = ctx_attend(uc)
            hc = merge(flat(y_conv_c), flat(y_ssd_c), flat(y_na_c), flat(uc), hc, cgt1, *wb, tm=min(512, Tc))
            hc = mlp(hc, csh2, csc2, cgt2, nw2, fw, w1, w2, tm=tmc, final=False)
    return h
```

```python
import functools
import math

import jax
import jax.numpy as jnp
import numpy as np
from jax import lax
from jax.experimental import pallas as pl
from jax.experimental.pallas import tpu as pltpu

F32 = jnp.float32
BF16 = jnp.bfloat16
HIGHEST = lax.Precision.HIGHEST

D = 1024
EPS = 1e-6
GRID_W = 64
N_HEADS = 16
HEAD_DIM = 64
SSD_GROUPS = 2
SSD_STATE = 128
CHUNK = 128
NA_KH = 8
NA_KW = 16
ROPE_BASE = 10000.0
D_FF = 4 * D
XBC = D + 2 * SSD_GROUPS * SSD_STATE
U_Q, U_K, U_V, U_GATE, U_CB, U_CC, U_CX, U_Z, U_XS, U_BC, U_COLS = (
    0, 1024, 2048, 3072, 6144, 7168, 8192, 9216, 10240, 11264, 11776)
U_TILE = 2944
U_SUB = 512
R_CB, R_CC, R_CX, R_Z, R_XBC, R_DT, R_Q, R_K, R_V, R_GATE = (
    0, 1024, 2048, 3072, 4096, 5632, 5664, 6688, 7712, 8736)
DT_PAD = 128
VMEM_LIMIT = 56 * 1024 * 1024


def _cparams(sem):
    return pltpu.CompilerParams(dimension_semantics=sem, vmem_limit_bytes=VMEM_LIMIT)


def _nt(a, b):
    return lax.dot_general(a, b, (((1,), (1,)), ((), ())), preferred_element_type=F32)


def _tn(a, b):
    return lax.dot_general(a, b, (((0,), (0,)), ((), ())), preferred_element_type=F32)


def _dot(a, b):
    return jnp.dot(a, b, preferred_element_type=F32)


def _dot_hi(a, b):
    return jnp.dot(a, b, preferred_element_type=F32, precision=HIGHEST)


def _silu(x):
    return x * jax.nn.sigmoid(x)


def _norm_mod(x, nw, sc, sh):
    ms = jnp.mean(x * x, axis=-1, keepdims=True)
    y = x * lax.rsqrt(ms + EPS) * nw
    return y * (1.0 + sc) + sh


def _ada_kernel(c_ref, w_ref, b_ref, o_ref):
    o_ref[...] = _dot_hi(_silu(c_ref[...]), w_ref[...]) + b_ref[...]


def ada_mod(c_rows, w_ada, b_ada):
    rows = c_rows.shape[0]
    tn = 1536
    return pl.pallas_call(
        _ada_kernel,
        grid=(6 * D // tn,),
        in_specs=[pl.BlockSpec((rows, D), lambda j: (0, 0)),
                  pl.BlockSpec((D, tn), lambda j: (0, j)),
                  pl.BlockSpec((1, tn), lambda j: (0, j))],
        out_specs=pl.BlockSpec((rows, tn), lambda j: (0, j)),
        out_shape=jax.ShapeDtypeStruct((rows, 6 * D), F32),
        compiler_params=_cparams(("arbitrary",)),
        name="ada_mod",
    )(c_rows, w_ada, b_ada.reshape(1, 6 * D))


def _rope_slab(x, cos, sa, sb):
    return x * cos + pltpu.roll(x, 2 * HEAD_DIM - 16, axis=1) * sa + pltpu.roll(x, 16, axis=1) * sb


def _inproj_kernel(x_ref, sh_ref, sc_ref, nw_ref, w_ref, wdt_ref, wdtT_ref, *rest, rope):
    if rope:
        cos_ref, sa_ref, sb_ref, u_ref, dt_ref, dtT_ref, xn_ref = rest
    else:
        u_ref, dt_ref, dtT_ref, xn_ref = rest
    j = pl.program_id(2)

    @pl.when(j == 0)
    def _():
        xb = _norm_mod(x_ref[0], nw_ref[...], sc_ref[0], sh_ref[0]).astype(BF16)
        xn_ref[...] = xb
        dt_ref[0] = _dot(xb, wdt_ref[...])
        dtT_ref[0] = _nt(wdtT_ref[...], xb)

    def tile(rope_cols):
        for off in range(0, U_TILE, U_SUB):
            wd = min(U_SUB, U_TILE - off)
            r = _dot(xn_ref[...], w_ref[:, off:off + wd])
            if off < rope_cols:
                scale = HEAD_DIM ** -0.5 if off < U_K else 1.0
                lw = 2 * HEAD_DIM
                r = jnp.concatenate(
                    [_rope_slab(r[:, s:s + lw], cos_ref[...], sa_ref[...], sb_ref[...]) * scale
                     for s in range(0, wd, lw)], axis=1)
            u_ref[0, :, off:off + wd] = r.astype(u_ref.dtype)

    if rope:
        @pl.when(j == 0)
        def _():
            tile(U_V)

        @pl.when(j > 0)
        def _():
            tile(0)
    else:
        tile(0)


def in_proj(x, sh, sc, nw, w, wdt, wdtT, tm, rope_tables=None):
    B, L, _ = x.shape
    tn = U_TILE
    grid = (B, L // tm, U_COLS // tn)
    rope = rope_tables is not None
    rope_specs = [pl.BlockSpec((tm, 2 * HEAD_DIM), lambda b, i, j: (i, 0))] * 3 if rope else []
    return pl.pallas_call(
        functools.partial(_inproj_kernel, rope=rope),
        grid=grid,
        in_specs=[pl.BlockSpec((1, tm, D), lambda b, i, j: (b, i, 0)),
                  pl.BlockSpec((1, 1, D), lambda b, i, j: (b, 0, 0)),
                  pl.BlockSpec((1, 1, D), lambda b, i, j: (b, 0, 0)),
                  pl.BlockSpec((1, D), lambda b, i, j: (0, 0)),
                  pl.BlockSpec((D, tn), lambda b, i, j: (0, j)),
                  pl.BlockSpec((D, DT_PAD), lambda b, i, j: (0, 0)),
                  pl.BlockSpec((2 * N_HEADS, D), lambda b, i, j: (0, 0))] + rope_specs,
        out_specs=[pl.BlockSpec((1, tm, tn), lambda b, i, j: (b, i, j)),
                   pl.BlockSpec((1, tm, DT_PAD), lambda b, i, j: (b, i, 0)),
                   pl.BlockSpec((1, 2 * N_HEADS, tm), lambda b, i, j: (b, 0, i))],
        out_shape=[jax.ShapeDtypeStruct((B, L, U_COLS), BF16),
                   jax.ShapeDtypeStruct((B, L, DT_PAD), F32),
                   jax.ShapeDtypeStruct((B, 2 * N_HEADS, L), F32)],
        scratch_shapes=[pltpu.VMEM((tm, D), BF16)],
        compiler_params=_cparams(("parallel", "parallel", "arbitrary")),
        name="in_proj",
    )(x, sh, sc, nw, w, wdt, wdtT, *(rope_tables if rope else ()))


def _dwconv3(p, w):
    L = p.shape[0]
    row = lax.broadcasted_iota(jnp.int32, p.shape, 0)
    prev = jnp.where(row == 0, 0.0, pltpu.roll(p, 1, axis=0))
    nxt = jnp.where(row == L - 1, 0.0, pltpu.roll(p, L - 1, axis=0))
    return prev * w[0:1] + p * w[1:2] + nxt * w[2:3]


def _convmix_kernel(b_ref, c_ref, x_ref, w_ref, o_ref):
    p = c_ref[0].astype(F32) * x_ref[0].astype(F32)
    o_ref[0] = (b_ref[0].astype(F32) * _dwconv3(p, w_ref[...])).astype(o_ref.dtype)


def conv_mix(u, w, tc=256):
    B, L, _ = u.shape
    ob, oc, ox = U_CB // tc, U_CC // tc, U_CX // tc
    return pl.pallas_call(
        _convmix_kernel,
        grid=(B, D // tc),
        in_specs=[pl.BlockSpec((1, L, tc), lambda b, j: (b, 0, ob + j)),
                  pl.BlockSpec((1, L, tc), lambda b, j: (b, 0, oc + j)),
                  pl.BlockSpec((1, L, tc), lambda b, j: (b, 0, ox + j)),
                  pl.BlockSpec((3, tc), lambda b, j: (0, j))],
        out_specs=pl.BlockSpec((1, L, tc), lambda b, j: (b, 0, j)),
        out_shape=jax.ShapeDtypeStruct((B, L, D), BF16),
        compiler_params=_cparams(("parallel", "parallel")),
        name="conv_mix",
    )(u, u, u, w)


def _xbcconv_kernel(x_ref, w_ref, b_ref, o_ref):
    y = _dwconv3(x_ref[0].astype(F32), w_ref[...]) + b_ref[...]
    o_ref[0] = _silu(y).astype(o_ref.dtype)


def xbc_conv(u, w, bias, tc=256):
    B, L, _ = u.shape
    off = U_XS // tc
    return pl.pallas_call(
        _xbcconv_kernel,
        grid=(B, XBC // tc),
        in_specs=[pl.BlockSpec((1, L, tc), lambda b, j: (b, 0, off + j)),
                  pl.BlockSpec((3, tc), lambda b, j: (0, j)),
                  pl.BlockSpec((1, tc), lambda b, j: (0, j))],
        out_specs=pl.BlockSpec((1, L, tc), lambda b, j: (b, 0, j)),
        out_shape=jax.ShapeDtypeStruct((B, L, XBC), BF16),
        compiler_params=_cparams(("parallel", "parallel")),
        name="xbc_conv",
    )(u, w, bias.reshape(1, XBC))


def _split3(x):
    hi = x.astype(BF16)
    r1 = x - hi.astype(F32)
    mid = r1.astype(BF16)
    lo = (r1 - mid.astype(F32)).astype(BF16)
    return hi, mid, lo


def _head_stack(x2, lane):
    zero = jnp.zeros_like(x2)
    return jnp.concatenate([jnp.where(lane < HEAD_DIM, x2, zero), jnp.where(lane >= HEAD_DIM, x2, zero)], axis=0)


def _ssd_chunk(d, rows, xs_ref, bc_ref, z_ref, dt_ref, dtT_ref, alr_ref, alc_ref, dbr_ref, dbc_ref, dsk_ref, nw_ref,
               ex_ref, y_ref, yacc_ref, st_ref):
    H = N_HEADS
    xsb = xs_ref[0]
    xs = xsb.astype(F32)
    bc = bc_ref[0]
    dt_c = jax.nn.softplus(dt_ref[0] + dbr_ref[...])
    a_c = dt_c * -jnp.exp(alr_ref[...])
    dtT_c = jax.nn.softplus(dtT_ref[0] + dbc_ref[...])
    aT_c = dtT_c * -jnp.exp(alc_ref[...])

    ri = lax.broadcasted_iota(jnp.int32, (CHUNK, CHUNK), 0)
    ci = lax.broadcasted_iota(jnp.int32, (CHUNK, CHUNK), 1)
    tri = (ri >= ci) if d == 0 else (ri <= ci)
    trib = tri.astype(BF16)
    tribT = ((ri <= ci) if d == 0 else (ri >= ci)).astype(BF16)
    cs = _dot(jnp.concatenate([trib] * 3, axis=1), jnp.concatenate(_split3(a_c), axis=0))
    csT = _dot(jnp.concatenate(_split3(aT_c), axis=1), jnp.concatenate([tribT] * 3, axis=0))
    last = cs[CHUNK - 1:CHUNK] if d == 0 else cs[0:1]
    w_c = dt_c * jnp.exp(last - cs)
    ein_c = jnp.exp(cs)
    ex = ex_ref[d]
    w_e = _dot(jnp.concatenate(_split3(w_c), axis=1), ex)
    ein_e = _dot(jnp.concatenate(_split3(ein_c), axis=1), ex)
    cd_e = ein_e[CHUNK - 1:CHUNK] if d == 0 else ein_e[0:1]
    Xdec = (xs * w_e).astype(BF16)
    rowT = csT - jnp.log(dtT_c)
    lane = lax.broadcasted_iota(jnp.int32, (CHUNK, 2 * HEAD_DIM), 1)

    ys = []
    GW = D // SSD_GROUPS
    for g in range(SSD_GROUPS):
        Bg = bc[:, g * SSD_STATE:(g + 1) * SSD_STATE]
        Cg = bc[:, (SSD_GROUPS + g) * SSD_STATE:(SSD_GROUPS + g + 1) * SSD_STATE]
        S = _nt(Cg, Bg)
        st = st_ref[d, g]
        y_off = _dot(Cg, st.astype(BF16)) * ein_e[:, g * GW:(g + 1) * GW]
        y_diag = []
        for p in range(GW // (2 * HEAD_DIM)):
            hA = g * (H // SSD_GROUPS) + 2 * p
            Ms = []
            for h in (hA, hA + 1):
                col = cs[:, d * H + h:d * H + h + 1]
                rowv = rowT[d * H + h:d * H + h + 1, :]
                Ms.append((S * jnp.exp(jnp.where(tri, col - rowv, -jnp.inf))).astype(BF16))
            Mcat = jnp.concatenate(Ms, axis=1)
            Xp = xsb[:, hA * HEAD_DIM:(hA + 2) * HEAD_DIM]
            y_diag.append(_dot(Mcat, _head_stack(Xp, lane)))
        ys.append(y_off + jnp.concatenate(y_diag, axis=1))
        st_ref[d, g] = st * cd_e[:, g * GW:(g + 1) * GW] + _tn(Bg, Xdec[:, g * GW:(g + 1) * GW])
    y = jnp.concatenate(ys, axis=1)

    if d == 0:
        yacc_ref[rows, :] = y
    else:
        ysum = yacc_ref[rows, :] + y + xs * (dsk_ref[0:1] + dsk_ref[1:2])
        yg = ysum * _silu(z_ref[0].astype(F32))
        parts = []
        for g in range(SSD_GROUPS):
            v = yg[:, g * GW:(g + 1) * GW]
            ms = jnp.mean(v * v, axis=-1, keepdims=True)
            parts.append(v * lax.rsqrt(ms + EPS) * nw_ref[:, g * GW:(g + 1) * GW])
        y_ref[0] = jnp.concatenate(parts, axis=1).astype(y_ref.dtype)


def _ssd_kernel(xs_ref, bc_ref, z_ref, dt_ref, dtT_ref, h0_ref, alr_ref, alc_ref, dbr_ref, dbc_ref, dsk_ref, nw_ref,
                ex_ref, y_ref, sto_ref, yacc_ref, st_ref, *, nc):
    t = pl.program_id(1)

    @pl.when(t == 0)
    def _():
        st_ref[...] = h0_ref[0]

    args = (xs_ref, bc_ref, z_ref, dt_ref, dtT_ref, alr_ref, alc_ref, dbr_ref, dbc_ref, dsk_ref, nw_ref,
            ex_ref, y_ref, yacc_ref, st_ref)

    @pl.when(t < nc)
    def _():
        _ssd_chunk(0, pl.ds(pl.multiple_of(t * CHUNK, CHUNK), CHUNK), *args)

    @pl.when(t >= nc)
    def _():
        _ssd_chunk(1, pl.ds(pl.multiple_of((2 * nc - 1 - t) * CHUNK, CHUNK), CHUNK), *args)

    @pl.when(t == 2 * nc - 1)
    def _():
        sto_ref[0] = st_ref[...]


def ssd_mix(xbc_c, u, dt, dtT, h0, a_log, dt_bias, d_skip_e, norm_w):
    B, L, _ = u.shape
    nc = L // CHUNK
    H2 = 2 * N_HEADS

    def chunk(t):
        return jnp.where(t < nc, t, 2 * nc - 1 - t)

    def late(t):
        return jnp.where(t < nc, nc - 1, 2 * nc - 1 - t)

    st_shape = (2, SSD_GROUPS, SSD_STATE, D // SSD_GROUPS)
    st_spec = pl.BlockSpec((1,) + st_shape, lambda b, t: (b, 0, 0, 0, 0))
    small = lambda shape: pl.BlockSpec(shape, lambda b, t: (0,) * len(shape))
    krow = jnp.arange(3 * DT_PAD, dtype=jnp.int32)[:, None] % DT_PAD
    head = jnp.arange(D, dtype=jnp.int32)[None, :] // HEAD_DIM
    ex = jnp.stack([krow == d * N_HEADS + head for d in range(2)]).astype(BF16)
    lane_pad = lambda v: jnp.pad(v.reshape(1, H2), ((0, 0), (0, DT_PAD - H2)))
    return pl.pallas_call(
        functools.partial(_ssd_kernel, nc=nc),
        grid=(B, 2 * nc),
        in_specs=[pl.BlockSpec((1, CHUNK, D), lambda b, t: (b, chunk(t), 0)),
                  pl.BlockSpec((1, CHUNK, XBC - D), lambda b, t: (b, chunk(t), D // (XBC - D))),
                  pl.BlockSpec((1, CHUNK, D), lambda b, t: (b, late(t), U_Z // D)),
                  pl.BlockSpec((1, CHUNK, DT_PAD), lambda b, t: (b, chunk(t), 0)),
                  pl.BlockSpec((1, H2, CHUNK), lambda b, t: (b, 0, chunk(t))),
                  st_spec,
                  small((1, DT_PAD)), small((H2, 1)), small((1, DT_PAD)), small((H2, 1)),
                  small((2, D)), small((1, D)), small((2, 3 * DT_PAD, D))],
        out_specs=[pl.BlockSpec((1, CHUNK, D), lambda b, t: (b, late(t), 0)),
                   st_spec],
        out_shape=[jax.ShapeDtypeStruct((B, L, D), BF16),
                   jax.ShapeDtypeStruct((B,) + st_shape, F32)],
        scratch_shapes=[pltpu.VMEM((L, D), F32), pltpu.VMEM(st_shape, F32)],
        compiler_params=_cparams(("parallel", "arbitrary")),
        name="ssd_mix",
    )(xbc_c, xbc_c, u, dt, dtT, h0,
      lane_pad(a_log), a_log.reshape(H2, 1), lane_pad(dt_bias), dt_bias.reshape(H2, 1),
      d_skip_e, norm_w.reshape(1, D), ex)


NA_GROUP = 4


def _na_kernel(q_ref, k_ref, v_ref, kc_ref, vc_ref, bias_ref, o_ref, s_ref, p_ref, inv_ref, *, rows):
    W = GRID_W
    band = NA_KH * W
    lane = lax.broadcasted_iota(jnp.int32, (W, 2 * HEAD_DIM), 1)
    G = NA_GROUP
    n_groups = rows // G
    M2 = 2 * W

    def band_start(r):
        return min(max(r - NA_KH // 2, 0), rows - NA_KH)

    def stage_scores(g, slot):
        for i in range(G):
            r = g * G + i
            r0 = band_start(r)
            qs = _head_stack(q_ref[0, r * W:(r + 1) * W, :], lane)
            kb = k_ref[0, r0 * W:r0 * W + band, :]
            base = r0 - r + NA_KH - 1
            for jp in range(NA_KH // 2):
                cols = slice(jp * M2, (jp + 1) * M2)
                s_ref[slot, i * M2:(i + 1) * M2, cols] = _nt(qs, kb[jp * M2:(jp + 1) * M2]) + bias_ref[0, base + 2 * jp]
            s_ref[slot, i * M2:(i + 1) * M2, band:] = _nt(qs, kc_ref[0])

    def stage_softmax(slot):
        for i in range(G):
            s = s_ref[slot, i * M2:(i + 1) * M2, :]
            p = jnp.exp(s - jnp.max(s, axis=-1, keepdims=True))
            inv = 1.0 / jnp.sum(p, axis=-1, keepdims=True)
            p_ref[slot, i * M2:(i + 1) * M2, :] = p.astype(BF16)
            inv_ref[slot, i * M2:(i + 1) * M2, :] = jnp.broadcast_to(inv, (M2, M2))

    def stage_values(g, slot):
        for i in range(G):
            r = g * G + i
            r0 = band_start(r)
            vb = v_ref[0, r0 * W:r0 * W + band, :]
            p = p_ref[slot, i * M2:(i + 1) * M2, :]
            o2 = (_dot(p[:, :band], vb) + _dot(p[:, band:], vc_ref[0])) * inv_ref[slot, i * M2:(i + 1) * M2, :]
            o = jnp.where(lane < HEAD_DIM, o2[:W], o2[W:])
            o_ref[0, r * W:(r + 1) * W, :] = o.astype(o_ref.dtype)

    stage_scores(0, 0)
    stage_scores(1, 1)
    stage_softmax(0)

    for g in range(n_groups - 2):
        stage_scores(g + 2, g % 2)
        stage_softmax((g + 1) % 2)
        stage_values(g, g % 2)
    stage_softmax((n_groups - 1) % 2)
    stage_values(n_groups - 2, (n_groups - 2) % 2)
    stage_values(n_groups - 1, (n_groups - 1) % 2)


def na_attend(u, u_ctx, bias):
    B, L, _ = u.shape
    Lc = u_ctx.shape[1]
    HP = N_HEADS // 2
    lw = 2 * HEAD_DIM
    oq, ok, ov = U_Q // lw, U_K // lw, U_V // lw
    return pl.pallas_call(
        functools.partial(_na_kernel, rows=L // GRID_W),
        grid=(HP, B),
        in_specs=[pl.BlockSpec((1, L, lw), lambda h, b: (b, 0, oq + h)),
                  pl.BlockSpec((1, L, lw), lambda h, b: (b, 0, ok + h)),
                  pl.BlockSpec((1, L, lw), lambda h, b: (b, 0, ov + h)),
                  pl.BlockSpec((1, Lc, lw), lambda h, b: (b, 0, ok + h)),
                  pl.BlockSpec((1, Lc, lw), lambda h, b: (b, 0, ov + h)),
                  pl.BlockSpec((1, 2 * NA_KH - 2, 2 * GRID_W, 2 * GRID_W), lambda h, b: (h, 0, 0, 0))],
        out_specs=pl.BlockSpec((1, L, lw), lambda h, b: (b, 0, h)),
        out_shape=jax.ShapeDtypeStruct((B, L, D), BF16),
        scratch_shapes=[pltpu.VMEM((2, NA_GROUP * lw, NA_KH * GRID_W + Lc), F32),
                        pltpu.VMEM((2, NA_GROUP * lw, NA_KH * GRID_W + Lc), BF16),
                        pltpu.VMEM((2, NA_GROUP * lw, lw), F32)],
        compiler_params=_cparams(("parallel", "parallel")),
        name="na_attend",
    )(u, u, u, u_ctx, u_ctx, bias)


def _ctxattn_kernel(q_ref, k_ref, v_ref, o_ref):
    Lc = q_ref.shape[1]
    lane = lax.broadcasted_iota(jnp.int32, (Lc, 2 * HEAD_DIM), 1)
    q2 = (q_ref[0].astype(F32) * (HEAD_DIM ** -0.5)).astype(BF16)
    s = _nt(_head_stack(q2, lane), k_ref[0])
    m = jnp.max(s, axis=-1, keepdims=True)
    p = jnp.exp(s - m)
    p = (p * (1.0 / jnp.sum(p, axis=-1, keepdims=True))).astype(BF16)
    o2 = _dot(p, v_ref[0])
    o_ref[0] = jnp.where(lane < HEAD_DIM, o2[:Lc], o2[Lc:]).astype(o_ref.dtype)


def ctx_attend(u_ctx):
    B, Lc, _ = u_ctx.shape
    HP = N_HEADS // 2
    lw = 2 * HEAD_DIM
    oq, ok, ov = U_Q // lw, U_K // lw, U_V // lw
    return pl.pallas_call(
        _ctxattn_kernel,
        grid=(HP, B),
        in_specs=[pl.BlockSpec((1, Lc, lw), lambda h, b: (b, 0, oq + h)),
                  pl.BlockSpec((1, Lc, lw), lambda h, b: (b, 0, ok + h)),
                  pl.BlockSpec((1, Lc, lw), lambda h, b: (b, 0, ov + h))],
        out_specs=pl.BlockSpec((1, Lc, lw), lambda h, b: (b, 0, h)),
        out_shape=jax.ShapeDtypeStruct((B, Lc, D), BF16),
        compiler_params=_cparams(("parallel", "parallel")),
        name="ctx_attend",
    )(u_ctx, u_ctx, u_ctx)


def _merge_kernel(yc_ref, ys_ref, yn_ref, g_ref, h_ref, gt_ref, wc_ref, ws_ref, wn_ref, wo_ref, o_ref):
    g = jax.nn.sigmoid(g_ref[0].astype(F32))
    m = (g[:, 0:D] * _dot(yc_ref[0], wc_ref[...])
         + g[:, D:2 * D] * _dot(ys_ref[0], ws_ref[...])
         + g[:, 2 * D:3 * D] * _dot(yn_ref[0], wn_ref[...]))
    o_ref[0] = h_ref[0] + gt_ref[0] * _dot(m.astype(BF16), wo_ref[...])


def merge(yc, ys, yn, u, h, gt, wc, ws, wn, wo, tm):
    B, L, _ = h.shape
    tok = lambda: pl.BlockSpec((1, tm, D), lambda b, i: (b, i, 0))
    wsp = lambda: pl.BlockSpec((D, D), lambda b, i: (0, 0))
    return pl.pallas_call(
        _merge_kernel,
        grid=(B, L // tm),
        in_specs=[tok(), tok(), tok(),
                  pl.BlockSpec((1, tm, 3 * D), lambda b, i: (b, i, U_GATE // (3 * D))),
                  tok(),
                  pl.BlockSpec((1, 1, D), lambda b, i: (b, 0, 0)),
                  wsp(), wsp(), wsp(), wsp()],
        out_specs=tok(),
        out_shape=jax.ShapeDtypeStruct((B, L, D), F32),
        compiler_params=_cparams(("parallel", "parallel")),
        name="merge",
    )(yc, ys, yn, u, h, gt, wc, ws, wn, wo)


def _mlp_kernel(h_ref, sh_ref, sc_ref, gt_ref, nw_ref, fw_ref, w1_ref, w2_ref, o_ref, xn_ref, acc_ref, *, nf, final):
    k = pl.program_id(2)

    @pl.when(k == 0)
    def _():
        xn_ref[...] = _norm_mod(h_ref[0], nw_ref[...], sc_ref[0], sh_ref[0]).astype(BF16)
        acc_ref[...] = jnp.zeros_like(acc_ref)

    a = jnp.square(jnp.maximum(_dot(xn_ref[...], w1_ref[...]), 0.0))
    acc_ref[...] += _dot(a.astype(BF16), w2_ref[...])

    @pl.when(k == nf - 1)
    def _():
        o = h_ref[0] + gt_ref[0] * acc_ref[...]
        if final:
            ms = jnp.mean(o * o, axis=-1, keepdims=True)
            o = o * lax.rsqrt(ms + EPS) * fw_ref[...]
        o_ref[0] = o


def mlp(h, sh, sc, gt, nw, fw, w1, w2, tm, final, tf=512):
    B, L, _ = h.shape
    nf = D_FF // tf
    vec = lambda: pl.BlockSpec((1, 1, D), lambda b, i, k: (b, 0, 0))
    par = lambda: pl.BlockSpec((1, D), lambda b, i, k: (0, 0))
    return pl.pallas_call(
        functools.partial(_mlp_kernel, nf=nf, final=final),
        grid=(B, L // tm, nf),
        in_specs=[pl.BlockSpec((1, tm, D), lambda b, i, k: (b, i, 0)),
                  vec(), vec(), vec(), par(), par(),
                  pl.BlockSpec((D, tf), lambda b, i, k: (0, k)),
                  pl.BlockSpec((tf, D), lambda b, i, k: (k, 0))],
        out_specs=pl.BlockSpec((1, tm, D), lambda b, i, k: (b, i, 0)),
        out_shape=jax.ShapeDtypeStruct((B, L, D), F32),
        scratch_shapes=[pltpu.VMEM((tm, D), BF16), pltpu.VMEM((tm, D), F32)],
        compiler_params=_cparams(("parallel", "parallel", "arbitrary")),
        name="mlp",
    )(h, sh, sc, gt, nw, fw, w1, w2)


def _rope_tables(L):
    t = jnp.arange(L, dtype=jnp.int32)
    row = (t // GRID_W).astype(F32)
    col = (t % GRID_W).astype(F32)
    half = HEAD_DIM // 2
    inv = ROPE_BASE ** (-jnp.arange(0, half, 2, dtype=F32) / half)
    ang_r = row[:, None] * inv
    ang_c = col[:, None] * inv
    ang = jnp.concatenate([ang_r, ang_r, ang_c, ang_c], axis=-1)
    cos = jnp.tile(jnp.cos(ang), (1, 2))
    sin = jnp.tile(jnp.sin(ang), (1, 2))
    even = ((jnp.arange(2 * HEAD_DIM) // (half // 2)) % 2 == 0)[None, :]
    return cos, jnp.where(even, -sin, 0.0), jnp.where(even, 0.0, sin)


def _rpb_kernel(r_ref, oh_ref, ok_ref, o_ref):
    val = _dot(jnp.concatenate(_split3(r_ref[...]), axis=1), oh_ref[...])
    o_ref[...] = jnp.where(ok_ref[...] > 0.0, val, -jnp.inf)


def _na_bias_table(rpb):
    H, NR, NC = rpb.shape
    W = GRID_W
    col = np.arange(W)
    col_start = np.clip(col - NA_KW // 2, 0, W - NA_KW)
    col_ok = (col[None, :] >= col_start[:, None]) & (col[None, :] < col_start[:, None] + NA_KW)
    dc_idx = np.clip(col[None, :] - col[:, None], -(NA_KW - 1), NA_KW - 1) + NA_KW - 1
    rows = jnp.transpose(rpb.reshape(H // 2, 2, NR, NC), (0, 2, 1, 3)).reshape(H * NR, NC)
    rows = jnp.pad(rows, ((0, 0), (0, DT_PAD - NC)))
    krow = jnp.arange(3 * DT_PAD, dtype=jnp.int32)[:, None] % DT_PAD
    onehot = (krow == jnp.asarray(dc_idx.reshape(1, W * W), jnp.int32)).astype(BF16)
    ok = jnp.asarray(col_ok.reshape(1, W * W), F32)
    tn = 1024
    tab = pl.pallas_call(
        _rpb_kernel,
        grid=(W * W // tn,),
        in_specs=[pl.BlockSpec((H * NR, DT_PAD), lambda j: (0, 0)),
                  pl.BlockSpec((3 * DT_PAD, tn), lambda j: (0, j)),
                  pl.BlockSpec((1, tn), lambda j: (0, j))],
        out_specs=pl.BlockSpec((H * NR, tn), lambda j: (0, j)),
        out_shape=jax.ShapeDtypeStruct((H * NR, W * W), F32),
        compiler_params=_cparams(("arbitrary",)),
        name="rpb_table",
    )(rows, onehot, ok)
    tab = tab.reshape(H // 2, NR, 2 * W, W)
    return jnp.concatenate([tab[:, :NR - 1], tab[:, 1:]], axis=-1)


def _prep_w_in(w_in):
    order = [(R_Q, D), (R_K, D), (R_V, D), (R_GATE, 3 * D), (R_CB, D), (R_CC, D), (R_CX, D), (R_Z, D), (R_XBC, XBC)]
    w = jnp.concatenate([w_in[:, o:o + n] for o, n in order], axis=1).astype(BF16)
    wdt = w_in[:, R_DT:R_DT + 2 * N_HEADS].astype(BF16)
    return w, jnp.pad(wdt, ((0, 0), (0, DT_PAD - 2 * N_HEADS))), wdt.T


def kernel(x, c, ctx, c_ctx, w_ada, b_ada, norm1_w, w_in, conv_mix_w, ssd_conv_w, ssd_conv_b, ssd_a_log, ssd_dt_bias,
           ssd_d, ssd_norm_w, na_rpb, w_br_conv, w_br_ssd, w_br_na, w_out, norm2_w, w_ff1, w_ff2, final_norm_w):
    B, L, _ = x.shape
    Lc = ctx.shape[1]
    depth = w_in.shape[0]
    cos, sa, sb = _rope_tables(L)
    n_mod = B + 1
    pad = (-n_mod) % 8
    c_rows = jnp.concatenate([c, c_ctx[None, :], jnp.zeros((pad, D), F32)], axis=0)
    zero_state = jnp.zeros((B, 2, SSD_GROUPS, SSD_STATE, D // SSD_GROUPS), F32)
    fw = final_norm_w.reshape(1, D)
    Tc = B * Lc
    tmc = min(1024, Tc)
    flat = lambda a: a.reshape(1, Tc, a.shape[-1])
    h, hc = x, flat(ctx)
    for l in range(depth):
        last = l == depth - 1
        mod = ada_mod(c_rows, w_ada[l], b_ada[l])
        m_lat = mod[:B].reshape(B, 1, 6, D)
        m_ctx = mod[B:B + 1].reshape(1, 1, 6, D)
        sh1, sc1, gt1, sh2, sc2, gt2 = (m_lat[:, :, i] for i in range(6))
        csh1, csc1, cgt1, csh2, csc2, cgt2 = (m_ctx[:, :, i] for i in range(6))
        w, wdt, wdtT = _prep_w_in(w_in[l])
        nw1 = norm1_w[l].reshape(1, D)
        u, dt, dtT = in_proj(h, sh1, sc1, nw1, w, wdt, wdtT, tm=1024, rope_tables=(cos, sa, sb))
        uc, dtc, dtTc = in_proj(hc, csh1, csc1, nw1, w, wdt, wdtT, tm=tmc)
        uc = uc.reshape(B, Lc, U_COLS)
        dtc = dtc.reshape(B, Lc, DT_PAD)
        dtTc = jnp.transpose(dtTc.reshape(2 * N_HEADS, B, Lc), (1, 0, 2))
        d_skip_e = jnp.repeat(ssd_d[l], HEAD_DIM, axis=1)
        ssd_p = (ssd_a_log[l], ssd_dt_bias[l], d_skip_e, ssd_norm_w[l])
        xbc_c = xbc_conv(uc, ssd_conv_w[l], ssd_conv_b[l])
        y_ssd_c, ctx_states = ssd_mix(xbc_c, uc, dtc, dtTc, zero_state, *ssd_p)
        xbc_l = xbc_conv(u, ssd_conv_w[l], ssd_conv_b[l])
        y_ssd, _ = ssd_mix(xbc_l, u, dt, dtT, ctx_states, *ssd_p)
        y_na = na_attend(u, uc, _na_bias_table(na_rpb[l]))
        y_conv = conv_mix(u, conv_mix_w[l])
        wb = [t[l].astype(BF16) for t in (w_br_conv, w_br_ssd, w_br_na, w_out)]
        w1, w2 = w_ff1[l].astype(BF16), w_ff2[l].astype(BF16)
        nw2 = norm2_w[l].reshape(1, D)
        h = merge(y_conv, y_ssd, y_na, u, h, gt1, *wb, tm=512)
        h = mlp(h, sh2, sc2, gt2, nw2, fw, w1, w2, tm=1024, final=last)
        if not last:
            y_conv_c = conv_mix(uc, conv_mix_w[l])
            y_na_c = ctx_attend(uc)
            hc = merge(flat(y_conv_c), flat(y_ssd_c), flat(y_na_c), flat(uc), hc, cgt1, *wb, tm=min(512, Tc))
            hc = mlp(hc, csh2, csc2, cgt2, nw2, fw, w1, w2, tm=tmc, final=False)
    return h
```

```python
import functools
import math

import jax
import jax.numpy as jnp
import numpy as np
from jax import lax
from jax.experimental import pallas as pl
from jax.experimental.pallas import tpu as pltpu

F32 = jnp.float32
BF16 = jnp.bfloat16
HIGHEST = lax.Precision.HIGHEST

D = 1024
EPS = 1e-6
GRID_W = 64
N_HEADS = 16
HEAD_DIM = 64
SSD_GROUPS = 2
SSD_STATE = 128
CHUNK = 128
NA_KH = 8
NA_KW = 16
ROPE_BASE = 10000.0
D_FF = 4 * D
XBC = D + 2 * SSD_GROUPS * SSD_STATE
U_Q, U_K, U_V, U_GATE, U_CB, U_CC, U_CX, U_Z, U_XS, U_BC, U_COLS = (
    0, 1024, 2048, 3072, 6144, 7168, 8192, 9216, 10240, 11264, 11776)
U_TILE = 2944
U_SUB = 512
R_CB, R_CC, R_CX, R_Z, R_XBC, R_DT, R_Q, R_K, R_V, R_GATE = (
    0, 1024, 2048, 3072, 4096, 5632, 5664, 6688, 7712, 8736)
DT_PAD = 128
VMEM_LIMIT = 56 * 1024 * 1024


def _cparams(sem):
    return pltpu.CompilerParams(dimension_semantics=sem, vmem_limit_bytes=VMEM_LIMIT)


def _nt(a, b):
    return lax.dot_general(a, b, (((1,), (1,)), ((), ())), preferred_element_type=F32)


def _tn(a, b):
    return lax.dot_general(a, b, (((0,), (0,)), ((), ())), preferred_element_type=F32)


def _dot(a, b):
    return jnp.dot(a, b, preferred_element_type=F32)


def _dot_hi(a, b):
    return jnp.dot(a, b, preferred_element_type=F32, precision=HIGHEST)


def _silu(x):
    return x * jax.nn.sigmoid(x)


def _norm_mod(x, nw, sc, sh):
    ms = jnp.mean(x * x, axis=-1, keepdims=True)
    y = x * lax.rsqrt(ms + EPS) * nw
    return y * (1.0 + sc) + sh


def _ada_kernel(c_ref, w_ref, b_ref, o_ref):
    o_ref[...] = _dot_hi(_silu(c_ref[...]), w_ref[...]) + b_ref[...]


def ada_mod(c_rows, w_ada, b_ada):
    rows = c_rows.shape[0]
    tn = 1536
    return pl.pallas_call(
        _ada_kernel,
        grid=(6 * D // tn,),
        in_specs=[pl.BlockSpec((rows, D), lambda j: (0, 0)),
                  pl.BlockSpec((D, tn), lambda j: (0, j)),
                  pl.BlockSpec((1, tn), lambda j: (0, j))],
        out_specs=pl.BlockSpec((rows, tn), lambda j: (0, j)),
        out_shape=jax.ShapeDtypeStruct((rows, 6 * D), F32),
        compiler_params=_cparams(("arbitrary",)),
        name="ada_mod",
    )(c_rows, w_ada, b_ada.reshape(1, 6 * D))


def _rope_slab(x, cos, sa, sb):
    return x * cos + pltpu.roll(x, 2 * HEAD_DIM - 16, axis=1) * sa + pltpu.roll(x, 16, axis=1) * sb


def _inproj_kernel(x_ref, sh_ref, sc_ref, nw_ref, w_ref, wdt_ref, wdtT_ref, *rest, rope):
    if rope:
        cos_ref, sa_ref, sb_ref, u_ref, dt_ref, dtT_ref, xn_ref = rest
    else:
        u_ref, dt_ref, dtT_ref, xn_ref = rest
    j = pl.program_id(2)

    def normalise():
        xb = _norm_mod(x_ref[0], nw_ref[...], sc_ref[0], sh_ref[0]).astype(BF16)
        xn_ref[...] = xb
        dt_ref[0] = _dot(xb, wdt_ref[...])
        dtT_ref[0] = _nt(wdtT_ref[...], xb)
        return xb

    def tile(rope_cols, xb=None):
        for off in range(0, U_TILE, U_SUB):
            wd = min(U_SUB, U_TILE - off)
            r = _dot(xn_ref[...] if xb is None else xb, w_ref[:, off:off + wd])
            if off < rope_cols:
                scale = HEAD_DIM ** -0.5 if off < U_K else 1.0
                lw = 2 * HEAD_DIM
                r = jnp.concatenate(
                    [_rope_slab(r[:, s:s + lw], cos_ref[...], sa_ref[...], sb_ref[...]) * scale
                     for s in range(0, wd, lw)], axis=1)
            u_ref[0, :, off:off + wd] = r.astype(u_ref.dtype)

    @pl.when(j == 0)
    def _():
        tile(U_V if rope else 0, normalise())

    @pl.when(j > 0)
    def _():
        tile(0)


def in_proj(x, sh, sc, nw, w, wdt, wdtT, tm, rope_tables=None):
    B, L, _ = x.shape
    tn = U_TILE
    grid = (B, L // tm, U_COLS // tn)
    rope = rope_tables is not None
    rope_specs = [pl.BlockSpec((tm, 2 * HEAD_DIM), lambda b, i, j: (i, 0))] * 3 if rope else []
    return pl.pallas_call(
        functools.partial(_inproj_kernel, rope=rope),
        grid=grid,
        in_specs=[pl.BlockSpec((1, tm, D), lambda b, i, j: (b, i, 0)),
                  pl.BlockSpec((1, 1, D), lambda b, i, j: (b, 0, 0)),
                  pl.BlockSpec((1, 1, D), lambda b, i, j: (b, 0, 0)),
                  pl.BlockSpec((1, D), lambda b, i, j: (0, 0)),
                  pl.BlockSpec((D, tn), lambda b, i, j: (0, j)),
                  pl.BlockSpec((D, DT_PAD), lambda b, i, j: (0, 0)),
                  pl.BlockSpec((2 * N_HEADS, D), lambda b, i, j: (0, 0))] + rope_specs,
        out_specs=[pl.BlockSpec((1, tm, tn), lambda b, i, j: (b, i, j)),
                   pl.BlockSpec((1, tm, DT_PAD), lambda b, i, j: (b, i, 0)),
                   pl.BlockSpec((1, 2 * N_HEADS, tm), lambda b, i, j: (b, 0, i))],
        out_shape=[jax.ShapeDtypeStruct((B, L, U_COLS), BF16),
                   jax.ShapeDtypeStruct((B, L, DT_PAD), F32),
                   jax.ShapeDtypeStruct((B, 2 * N_HEADS, L), F32)],
        scratch_shapes=[pltpu.VMEM((tm, D), BF16)],
        compiler_params=_cparams(("parallel", "parallel", "arbitrary")),
        name="in_proj",
    )(x, sh, sc, nw, w, wdt, wdtT, *(rope_tables if rope else ()))


def _dwconv3(p, w):
    L = p.shape[0]
    row = lax.broadcasted_iota(jnp.int32, p.shape, 0)
    prev = jnp.where(row == 0, 0.0, pltpu.roll(p, 1, axis=0))
    nxt = jnp.where(row == L - 1, 0.0, pltpu.roll(p, L - 1, axis=0))
    return prev * w[0:1] + p * w[1:2] + nxt * w[2:3]


def _xbcconv_kernel(x_ref, w_ref, b_ref, o_ref):
    y = _dwconv3(x_ref[0].astype(F32), w_ref[...]) + b_ref[...]
    o_ref[0] = _silu(y).astype(o_ref.dtype)


def xbc_conv(u, w, bias, tc=256):
    B, L, _ = u.shape
    off = U_XS // tc
    return pl.pallas_call(
        _xbcconv_kernel,
        grid=(B, XBC // tc),
        in_specs=[pl.BlockSpec((1, L, tc), lambda b, j: (b, 0, off + j)),
                  pl.BlockSpec((3, tc), lambda b, j: (0, j)),
                  pl.BlockSpec((1, tc), lambda b, j: (0, j))],
        out_specs=pl.BlockSpec((1, L, tc), lambda b, j: (b, 0, j)),
        out_shape=jax.ShapeDtypeStruct((B, L, XBC), BF16),
        compiler_params=_cparams(("parallel", "parallel")),
        name="xbc_conv",
    )(u, w, bias.reshape(1, XBC))


def _split3(x):
    hi = x.astype(BF16)
    r1 = x - hi.astype(F32)
    mid = r1.astype(BF16)
    lo = (r1 - mid.astype(F32)).astype(BF16)
    return hi, mid, lo


def _head_stack(x2, lane):
    zero = jnp.zeros_like(x2)
    return jnp.concatenate([jnp.where(lane < HEAD_DIM, x2, zero), jnp.where(lane >= HEAD_DIM, x2, zero)], axis=0)


def _ssd_chunk(d, rows, xs_ref, bc_ref, z_ref, dt_ref, dtT_ref, alr_ref, alc_ref, dbr_ref, dbc_ref, dsk_ref, nw_ref,
               ex_ref, y_ref, yacc_ref, st_ref):
    H = N_HEADS
    xsb = xs_ref[0]
    xs = xsb.astype(F32)
    bc = bc_ref[0]
    dt_c = jax.nn.softplus(dt_ref[0] + dbr_ref[...])
    a_c = dt_c * -jnp.exp(alr_ref[...])
    dtT_c = jax.nn.softplus(dtT_ref[0] + dbc_ref[...])
    aT_c = dtT_c * -jnp.exp(alc_ref[...])

    ri = lax.broadcasted_iota(jnp.int32, (CHUNK, CHUNK), 0)
    ci = lax.broadcasted_iota(jnp.int32, (CHUNK, CHUNK), 1)
    tri = (ri >= ci) if d == 0 else (ri <= ci)
    trib = tri.astype(BF16)
    tribT = ((ri <= ci) if d == 0 else (ri >= ci)).astype(BF16)
    cs = _dot(jnp.concatenate([trib] * 3, axis=1), jnp.concatenate(_split3(a_c), axis=0))
    csT = _dot(jnp.concatenate(_split3(aT_c), axis=1), jnp.concatenate([tribT] * 3, axis=0))
    last = cs[CHUNK - 1:CHUNK] if d == 0 else cs[0:1]
    w_c = dt_c * jnp.exp(last - cs)
    ein_c = jnp.exp(cs)
    ex = ex_ref[d]
    w_e = _dot(jnp.concatenate(_split3(w_c)[:2], axis=1), ex)
    ein_e = _dot(jnp.concatenate(_split3(ein_c)[:2], axis=1), ex)
    cd_e = ein_e[CHUNK - 1:CHUNK] if d == 0 else ein_e[0:1]
    Xdec = (xs * w_e).astype(BF16)
    rowT = csT - jnp.log(dtT_c)
    lane = lax.broadcasted_iota(jnp.int32, (CHUNK, 2 * HEAD_DIM), 1)

    ys = []
    GW = D // SSD_GROUPS
    for g in range(SSD_GROUPS):
        Bg = bc[:, g * SSD_STATE:(g + 1) * SSD_STATE]
        Cg = bc[:, (SSD_GROUPS + g) * SSD_STATE:(SSD_GROUPS + g + 1) * SSD_STATE]
        S = _nt(Cg, Bg)
        st = st_ref[d, g]
        y_off = _dot(Cg, st.astype(BF16)) * ein_e[:, g * GW:(g + 1) * GW]
        y_diag = []
        for p in range(GW // (2 * HEAD_DIM)):
            hA = g * (H // SSD_GROUPS) + 2 * p
            Ms = []
            for h in (hA, hA + 1):
                col = cs[:, d * H + h:d * H + h + 1]
                rowv = rowT[d * H + h:d * H + h + 1, :]
                Ms.append((S * jnp.exp(jnp.where(tri, col - rowv, -jnp.inf))).astype(BF16))
            Mcat = jnp.concatenate(Ms, axis=1)
            Xp = xsb[:, hA * HEAD_DIM:(hA + 2) * HEAD_DIM]
            y_diag.append(_dot(Mcat, _head_stack(Xp, lane)))
        ys.append(y_off + jnp.concatenate(y_diag, axis=1))
        st_ref[d, g] = st * cd_e[:, g * GW:(g + 1) * GW] + _tn(Bg, Xdec[:, g * GW:(g + 1) * GW])
    y = jnp.concatenate(ys, axis=1)

    if d == 0:
        yacc_ref[rows, :] = y
    else:
        ysum = yacc_ref[rows, :] + y + xs * (dsk_ref[0:1] + dsk_ref[1:2])
        yg = ysum * _silu(z_ref[0].astype(F32))
        parts = []
        for g in range(SSD_GROUPS):
            v = yg[:, g * GW:(g + 1) * GW]
            ms = jnp.mean(v * v, axis=-1, keepdims=True)
            parts.append(v * lax.rsqrt(ms + EPS) * nw_ref[:, g * GW:(g + 1) * GW])
        y_ref[0] = jnp.concatenate(parts, axis=1).astype(y_ref.dtype)


def _ssd_kernel(xs_ref, bc_ref, z_ref, dt_ref, dtT_ref, h0_ref, alr_ref, alc_ref, dbr_ref, dbc_ref, dsk_ref, nw_ref,
                ex_ref, y_ref, sto_ref, yacc_ref, st_ref, *, nc):
    t = pl.program_id(1)

    @pl.when(t == 0)
    def _():
        st_ref[...] = h0_ref[0]

    args = (xs_ref, bc_ref, z_ref, dt_ref, dtT_ref, alr_ref, alc_ref, dbr_ref, dbc_ref, dsk_ref, nw_ref,
            ex_ref, y_ref, yacc_ref, st_ref)

    @pl.when(t < nc)
    def _():
        _ssd_chunk(0, pl.ds(pl.multiple_of(t * CHUNK, CHUNK), CHUNK), *args)

    @pl.when(t >= nc)
    def _():
        _ssd_chunk(1, pl.ds(pl.multiple_of((2 * nc - 1 - t) * CHUNK, CHUNK), CHUNK), *args)

    @pl.when(t == 2 * nc - 1)
    def _():
        sto_ref[0] = st_ref[...]


def ssd_mix(xbc_c, u, dt, dtT, h0, a_log, dt_bias, d_skip_e, norm_w):
    B, L, _ = u.shape
    nc = L // CHUNK
    H2 = 2 * N_HEADS

    def chunk(t):
        return jnp.where(t < nc, t, 2 * nc - 1 - t)

    def late(t):
        return jnp.where(t < nc, nc - 1, 2 * nc - 1 - t)

    st_shape = (2, SSD_GROUPS, SSD_STATE, D // SSD_GROUPS)
    st_spec = pl.BlockSpec((1,) + st_shape, lambda b, t: (b, 0, 0, 0, 0))
    small = lambda shape: pl.BlockSpec(shape, lambda b, t: (0,) * len(shape))
    krow = jnp.arange(2 * DT_PAD, dtype=jnp.int32)[:, None] % DT_PAD
    head = jnp.arange(D, dtype=jnp.int32)[None, :] // HEAD_DIM
    ex = jnp.stack([krow == d * N_HEADS + head for d in range(2)]).astype(BF16)
    lane_pad = lambda v: jnp.pad(v.reshape(1, H2), ((0, 0), (0, DT_PAD - H2)))
    return pl.pallas_call(
        functools.partial(_ssd_kernel, nc=nc),
        grid=(B, 2 * nc),
        in_specs=[pl.BlockSpec((1, CHUNK, D), lambda b, t: (b, chunk(t), 0)),
                  pl.BlockSpec((1, CHUNK, XBC - D), lambda b, t: (b, chunk(t), D // (XBC - D))),
                  pl.BlockSpec((1, CHUNK, D), lambda b, t: (b, late(t), U_Z // D)),
                  pl.BlockSpec((1, CHUNK, DT_PAD), lambda b, t: (b, chunk(t), 0)),
                  pl.BlockSpec((1, H2, CHUNK), lambda b, t: (b, 0, chunk(t))),
                  st_spec,
                  small((1, DT_PAD)), small((H2, 1)), small((1, DT_PAD)), small((H2, 1)),
                  small((2, D)), small((1, D)), small((2, 2 * DT_PAD, D))],
        out_specs=[pl.BlockSpec((1, CHUNK, D), lambda b, t: (b, late(t), 0)),
                   st_spec],
        out_shape=[jax.ShapeDtypeStruct((B, L, D), BF16),
                   jax.ShapeDtypeStruct((B,) + st_shape, F32)],
        scratch_shapes=[pltpu.VMEM((L, D), F32), pltpu.VMEM(st_shape, F32)],
        compiler_params=_cparams(("parallel", "arbitrary")),
        name="ssd_mix",
    )(xbc_c, xbc_c, u, dt, dtT, h0,
      lane_pad(a_log), a_log.reshape(H2, 1), lane_pad(dt_bias), dt_bias.reshape(H2, 1),
      d_skip_e, norm_w.reshape(1, D), ex)


NA_GROUP = 4


def _na_kernel(q_ref, k_ref, v_ref, kc_ref, vc_ref, bias_ref, o_ref, s_ref, p_ref, inv_ref, *, rows):
    W = GRID_W
    band = NA_KH * W
    lane = lax.broadcasted_iota(jnp.int32, (W, 2 * HEAD_DIM), 1)
    G = NA_GROUP
    n_groups = rows // G
    M2 = 2 * W

    def band_start(r):
        return min(max(r - NA_KH // 2, 0), rows - NA_KH)

    def stage_scores(g, slot):
        for i in range(G):
            r = g * G + i
            r0 = band_start(r)
            qs = _head_stack(q_ref[0, r * W:(r + 1) * W, :], lane)
            kb = k_ref[0, r0 * W:r0 * W + band, :]
            base = r0 - r + NA_KH - 1
            for jp in range(NA_KH // 2):
                cols = slice(jp * M2, (jp + 1) * M2)
                s_ref[slot, i * M2:(i + 1) * M2, cols] = _nt(qs, kb[jp * M2:(jp + 1) * M2]) + bias_ref[0, base + 2 * jp]
            s_ref[slot, i * M2:(i + 1) * M2, band:] = _nt(qs, kc_ref[0])

    def stage_softmax(slot):
        for i in range(G):
            s = s_ref[slot, i * M2:(i + 1) * M2, :]
            p = jnp.exp(s - jnp.max(s, axis=-1, keepdims=True))
            inv = 1.0 / jnp.sum(p, axis=-1, keepdims=True)
            p_ref[slot, i * M2:(i + 1) * M2, :] = p.astype(BF16)
            inv_ref[slot, i * M2:(i + 1) * M2, :] = jnp.broadcast_to(inv, (M2, M2))

    def stage_values(g, slot):
        for i in range(G):
            r = g * G + i
            r0 = band_start(r)
            vb = v_ref[0, r0 * W:r0 * W + band, :]
            p = p_ref[slot, i * M2:(i + 1) * M2, :]
            o2 = (_dot(p[:, :band], vb) + _dot(p[:, band:], vc_ref[0])) * inv_ref[slot, i * M2:(i + 1) * M2, :]
            o = jnp.where(lane < HEAD_DIM, o2[:W], o2[W:])
            o_ref[0, r * W:(r + 1) * W, :] = o.astype(o_ref.dtype)

    stage_scores(0, 0)
    stage_scores(1, 1)
    stage_softmax(0)

    for g in range(n_groups - 2):
        stage_scores(g + 2, g % 2)
        stage_softmax((g + 1) % 2)
        stage_values(g, g % 2)
    stage_softmax((n_groups - 1) % 2)
    stage_values(n_groups - 2, (n_groups - 2) % 2)
    stage_values(n_groups - 1, (n_groups - 1) % 2)


def na_attend(u, u_ctx, bias):
    B, L, _ = u.shape
    Lc = u_ctx.shape[1]
    HP = N_HEADS // 2
    lw = 2 * HEAD_DIM
    oq, ok, ov = U_Q // lw, U_K // lw, U_V // lw
    return pl.pallas_call(
        functools.partial(_na_kernel, rows=L // GRID_W),
        grid=(HP, B),
        in_specs=[pl.BlockSpec((1, L, lw), lambda h, b: (b, 0, oq + h)),
                  pl.BlockSpec((1, L, lw), lambda h, b: (b, 0, ok + h)),
                  pl.BlockSpec((1, L, lw), lambda h, b: (b, 0, ov + h)),
                  pl.BlockSpec((1, Lc, lw), lambda h, b: (b, 0, ok + h)),
                  pl.BlockSpec((1, Lc, lw), lambda h, b: (b, 0, ov + h)),
                  pl.BlockSpec((1, 2 * NA_KH - 2, 2 * GRID_W, 2 * GRID_W), lambda h, b: (h, 0, 0, 0))],
        out_specs=pl.BlockSpec((1, L, lw), lambda h, b: (b, 0, h)),
        out_shape=jax.ShapeDtypeStruct((B, L, D), BF16),
        scratch_shapes=[pltpu.VMEM((2, NA_GROUP * lw, NA_KH * GRID_W + Lc), F32),
                        pltpu.VMEM((2, NA_GROUP * lw, NA_KH * GRID_W + Lc), BF16),
                        pltpu.VMEM((2, NA_GROUP * lw, lw), F32)],
        compiler_params=_cparams(("parallel", "parallel")),
        name="na_attend",
    )(u, u, u, u_ctx, u_ctx, bias)


def _ctxattn_kernel(q_ref, k_ref, v_ref, o_ref):
    Lc = q_ref.shape[1]
    lane = lax.broadcasted_iota(jnp.int32, (Lc, 2 * HEAD_DIM), 1)
    q2 = (q_ref[0].astype(F32) * (HEAD_DIM ** -0.5)).astype(BF16)
    s = _nt(_head_stack(q2, lane), k_ref[0])
    m = jnp.max(s, axis=-1, keepdims=True)
    p = jnp.exp(s - m)
    p = (p * (1.0 / jnp.sum(p, axis=-1, keepdims=True))).astype(BF16)
    o2 = _dot(p, v_ref[0])
    o_ref[0] = jnp.where(lane < HEAD_DIM, o2[:Lc], o2[Lc:]).astype(o_ref.dtype)


def ctx_attend(u_ctx):
    B, Lc, _ = u_ctx.shape
    HP = N_HEADS // 2
    lw = 2 * HEAD_DIM
    oq, ok, ov = U_Q // lw, U_K // lw, U_V // lw
    return pl.pallas_call(
        _ctxattn_kernel,
        grid=(HP, B),
        in_specs=[pl.BlockSpec((1, Lc, lw), lambda h, b: (b, 0, oq + h)),
                  pl.BlockSpec((1, Lc, lw), lambda h, b: (b, 0, ok + h)),
                  pl.BlockSpec((1, Lc, lw), lambda h, b: (b, 0, ov + h))],
        out_specs=pl.BlockSpec((1, Lc, lw), lambda h, b: (b, 0, h)),
        out_shape=jax.ShapeDtypeStruct((B, Lc, D), BF16),
        compiler_params=_cparams(("parallel", "parallel")),
        name="ctx_attend",
    )(u_ctx, u_ctx, u_ctx)


HALO = 16


def _merge_kernel(cb_ref, cc_ref, cx_ref, ccp_ref, cxp_ref, ccn_ref, cxn_ref, cw_ref, ys_ref, yn_ref, g_ref, h_ref,
                  gt_ref, wc_ref, ws_ref, wn_ref, wo_ref, o_ref, *, seq_len):
    tm = h_ref.shape[1]
    p = cc_ref[0].astype(F32) * cx_ref[0].astype(F32)
    p_before = (ccp_ref[0].astype(F32) * cxp_ref[0].astype(F32))[HALO - 1:HALO]
    p_after = (ccn_ref[0].astype(F32) * cxn_ref[0].astype(F32))[0:1]
    row = lax.broadcasted_iota(jnp.int32, (tm, D), 0)
    pos = (pl.program_id(1) * tm + row) & (seq_len - 1)
    prev = jnp.where(row == 0, p_before, pltpu.roll(p, 1, axis=0))
    prev = jnp.where(pos == 0, 0.0, prev)
    nxt = jnp.where(row == tm - 1, p_after, pltpu.roll(p, tm - 1, axis=0))
    nxt = jnp.where(pos == seq_len - 1, 0.0, nxt)
    w = cw_ref[...]
    yc = (cb_ref[0].astype(F32) * (prev * w[0:1] + p * w[1:2] + nxt * w[2:3])).astype(BF16)

    g = jax.nn.sigmoid(g_ref[0].astype(F32))
    m = (g[:, 0:D] * _dot(yc, wc_ref[...])
         + g[:, D:2 * D] * _dot(ys_ref[0], ws_ref[...])
         + g[:, 2 * D:3 * D] * _dot(yn_ref[0], wn_ref[...]))
    o_ref[0] = h_ref[0] + gt_ref[0] * _dot(m.astype(BF16), wo_ref[...])


def merge(u, ys, yn, h, gt, cw, wc, ws, wn, wo, tm, seq_len):
    B, L, _ = h.shape
    assert seq_len & (seq_len - 1) == 0 and (tm % seq_len == 0 or seq_len % tm == 0)
    nb, hb = tm // HALO, L // HALO
    tok = lambda: pl.BlockSpec((1, tm, D), lambda b, i: (b, i, 0))
    ucol = lambda c: pl.BlockSpec((1, tm, D), lambda b, i: (b, i, c // D))
    before = lambda c: pl.BlockSpec((1, HALO, D), lambda b, i: (b, jnp.maximum(i * nb - 1, 0), c // D))
    after = lambda c: pl.BlockSpec((1, HALO, D), lambda b, i: (b, jnp.minimum((i + 1) * nb, hb - 1), c // D))
    wsp = lambda: pl.BlockSpec((D, D), lambda b, i: (0, 0))
    return pl.pallas_call(
        functools.partial(_merge_kernel, seq_len=seq_len),
        grid=(B, L // tm),
        in_specs=[ucol(U_CB), ucol(U_CC), ucol(U_CX), before(U_CC), before(U_CX), after(U_CC), after(U_CX),
                  pl.BlockSpec((3, D), lambda b, i: (0, 0)),
                  tok(), tok(),
                  pl.BlockSpec((1, tm, 3 * D), lambda b, i: (b, i, U_GATE // (3 * D))),
                  tok(),
                  pl.BlockSpec((1, 1, D), lambda b, i: (b, 0, 0)),
                  wsp(), wsp(), wsp(), wsp()],
        out_specs=tok(),
        out_shape=jax.ShapeDtypeStruct((B, L, D), F32),
        compiler_params=_cparams(("parallel", "parallel")),
        name="merge",
    )(u, u, u, u, u, u, u, cw, ys, yn, u, h, gt, wc, ws, wn, wo)


MLP_TF = 1024


def _mlp_kernel(h_ref, sh_ref, sc_ref, gt_ref, nw_ref, fw_ref, w1_ref, w2_ref, o_ref, *, final):
    h = h_ref[0]
    xn = _norm_mod(h, nw_ref[...], sc_ref[0], sh_ref[0]).astype(BF16)
    acc = None
    for k in range(0, D_FF, MLP_TF):
        a = jnp.square(jnp.maximum(_dot(xn, w1_ref[:, k:k + MLP_TF]), 0.0)).astype(BF16)
        t = _dot(a, w2_ref[k:k + MLP_TF, :])
        acc = t if acc is None else acc + t
    o = h + gt_ref[0] * acc
    if final:
        ms = jnp.mean(o * o, axis=-1, keepdims=True)
        o = o * lax.rsqrt(ms + EPS) * fw_ref[...]
    o_ref[0] = o


def mlp(h, sh, sc, gt, nw, fw, w1, w2, tm, final):
    B, L, _ = h.shape
    vec = lambda: pl.BlockSpec((1, 1, D), lambda b, i: (b, 0, 0))
    par = lambda: pl.BlockSpec((1, D), lambda b, i: (0, 0))
    resident = lambda shape: pl.BlockSpec(shape, lambda b, i: (0, 0), pipeline_mode=pl.Buffered(1))
    return pl.pallas_call(
        functools.partial(_mlp_kernel, final=final),
        grid=(B, L // tm),
        in_specs=[pl.BlockSpec((1, tm, D), lambda b, i: (b, i, 0)),
                  vec(), vec(), vec(), par(), par(),
                  resident((D, D_FF)), resident((D_FF, D))],
        out_specs=pl.BlockSpec((1, tm, D), lambda b, i: (b, i, 0)),
        out_shape=jax.ShapeDtypeStruct((B, L, D), F32),
        compiler_params=_cparams(("parallel", "parallel")),
        name="mlp",
    )(h, sh, sc, gt, nw, fw, w1, w2)


def _rope_tables(L):
    t = jnp.arange(L, dtype=jnp.int32)
    row = (t // GRID_W).astype(F32)
    col = (t % GRID_W).astype(F32)
    half = HEAD_DIM // 2
    inv = ROPE_BASE ** (-jnp.arange(0, half, 2, dtype=F32) / half)
    ang_r = row[:, None] * inv
    ang_c = col[:, None] * inv
    ang = jnp.concatenate([ang_r, ang_r, ang_c, ang_c], axis=-1)
    cos = jnp.tile(jnp.cos(ang), (1, 2))
    sin = jnp.tile(jnp.sin(ang), (1, 2))
    even = ((jnp.arange(2 * HEAD_DIM) // (half // 2)) % 2 == 0)[None, :]
    return cos, jnp.where(even, -sin, 0.0), jnp.where(even, 0.0, sin)


def _rpb_kernel(r_ref, oh_ref, ok_ref, o_ref):
    val = _dot(jnp.concatenate(_split3(r_ref[...]), axis=1), oh_ref[...])
    o_ref[...] = jnp.where(ok_ref[...] > 0.0, val, -jnp.inf)


def _na_bias_table(rpb):
    H, NR, NC = rpb.shape
    W = GRID_W
    col = np.arange(W)
    col_start = np.clip(col - NA_KW // 2, 0, W - NA_KW)
    col_ok = (col[None, :] >= col_start[:, None]) & (col[None, :] < col_start[:, None] + NA_KW)
    dc_idx = np.clip(col[None, :] - col[:, None], -(NA_KW - 1), NA_KW - 1) + NA_KW - 1
    rows = jnp.transpose(rpb.reshape(H // 2, 2, NR, NC), (0, 2, 1, 3)).reshape(H * NR, NC)
    rows = jnp.pad(rows, ((0, 0), (0, DT_PAD - NC)))
    krow = jnp.arange(3 * DT_PAD, dtype=jnp.int32)[:, None] % DT_PAD
    onehot = (krow == jnp.asarray(dc_idx.reshape(1, W * W), jnp.int32)).astype(BF16)
    ok = jnp.asarray(col_ok.reshape(1, W * W), F32)
    tn = 1024
    tab = pl.pallas_call(
        _rpb_kernel,
        grid=(W * W // tn,),
        in_specs=[pl.BlockSpec((H * NR, DT_PAD), lambda j: (0, 0)),
                  pl.BlockSpec((3 * DT_PAD, tn), lambda j: (0, j)),
                  pl.BlockSpec((1, tn), lambda j: (0, j))],
        out_specs=pl.BlockSpec((H * NR, tn), lambda j: (0, j)),
        out_shape=jax.ShapeDtypeStruct((H * NR, W * W), F32),
        compiler_params=_cparams(("arbitrary",)),
        name="rpb_table",
    )(rows, onehot, ok)
    tab = tab.reshape(H // 2, NR, 2 * W, W)
    return jnp.concatenate([tab[:, :NR - 1], tab[:, 1:]], axis=-1)


def _prep_w_in(w_in):
    order = [(R_Q, D), (R_K, D), (R_V, D), (R_GATE, 3 * D), (R_CB, D), (R_CC, D), (R_CX, D), (R_Z, D), (R_XBC, XBC)]
    w = jnp.concatenate([w_in[:, o:o + n] for o, n in order], axis=1).astype(BF16)
    wdt = w_in[:, R_DT:R_DT + 2 * N_HEADS].astype(BF16)
    return w, jnp.pad(wdt, ((0, 0), (0, DT_PAD - 2 * N_HEADS))), wdt.T


def kernel(x, c, ctx, c_ctx, w_ada, b_ada, norm1_w, w_in, conv_mix_w, ssd_conv_w, ssd_conv_b, ssd_a_log, ssd_dt_bias,
           ssd_d, ssd_norm_w, na_rpb, w_br_conv, w_br_ssd, w_br_na, w_out, norm2_w, w_ff1, w_ff2, final_norm_w):
    B, L, _ = x.shape
    Lc = ctx.shape[1]
    depth = w_in.shape[0]
    cos, sa, sb = _rope_tables(L)
    n_mod = B + 1
    pad = (-n_mod) % 8
    c_rows = jnp.concatenate([c, c_ctx[None, :], jnp.zeros((pad, D), F32)], axis=0)
    zero_state = jnp.zeros((B, 2, SSD_GROUPS, SSD_STATE, D // SSD_GROUPS), F32)
    fw = final_norm_w.reshape(1, D)
    Tc = B * Lc
    tmc = min(1024, Tc)
    flat = lambda a: a.reshape(1, Tc, a.shape[-1])
    h, hc = x, flat(ctx)
    for l in range(depth):
        last = l == depth - 1
        mod = ada_mod(c_rows, w_ada[l], b_ada[l])
        m_lat = mod[:B].reshape(B, 1, 6, D)
        m_ctx = mod[B:B + 1].reshape(1, 1, 6, D)
        sh1, sc1, gt1, sh2, sc2, gt2 = (m_lat[:, :, i] for i in range(6))
        csh1, csc1, cgt1, csh2, csc2, cgt2 = (m_ctx[:, :, i] for i in range(6))
        w, wdt, wdtT = _prep_w_in(w_in[l])
        nw1 = norm1_w[l].reshape(1, D)
        u, dt, dtT = in_proj(h, sh1, sc1, nw1, w, wdt, wdtT, tm=1024, rope_tables=(cos, sa, sb))
        uc, dtc, dtTc = in_proj(hc, csh1, csc1, nw1, w, wdt, wdtT, tm=tmc)
        uc = uc.reshape(B, Lc, U_COLS)
        dtc = dtc.reshape(B, Lc, DT_PAD)
        dtTc = jnp.transpose(dtTc.reshape(2 * N_HEADS, B, Lc), (1, 0, 2))
        d_skip_e = jnp.repeat(ssd_d[l], HEAD_DIM, axis=1)
        ssd_p = (ssd_a_log[l], ssd_dt_bias[l], d_skip_e, ssd_norm_w[l])
        xbc_c = xbc_conv(uc, ssd_conv_w[l], ssd_conv_b[l])
        y_ssd_c, ctx_states = ssd_mix(xbc_c, uc, dtc, dtTc, zero_state, *ssd_p)
        xbc_l = xbc_conv(u, ssd_conv_w[l], ssd_conv_b[l])
        y_ssd, _ = ssd_mix(xbc_l, u, dt, dtT, ctx_states, *ssd_p)
        y_na = na_attend(u, uc, _na_bias_table(na_rpb[l]))
        wb = [t[l].astype(BF16) for t in (w_br_conv, w_br_ssd, w_br_na, w_out)]
        w1, w2 = w_ff1[l].astype(BF16), w_ff2[l].astype(BF16)
        nw2 = norm2_w[l].reshape(1, D)
        h = merge(u, y_ssd, y_na, h, gt1, conv_mix_w[l], *wb, tm=512, seq_len=L)
        h = mlp(h, sh2, sc2, gt2, nw2, fw, w1, w2, tm=1024, final=last)
        if not last:
            y_na_c = ctx_attend(uc)
            hc = merge(flat(uc), flat(y_ssd_c), flat(y_na_c), hc, cgt1, conv_mix_w[l], *wb, tm=min(512, Tc), seq_len=Lc)
            hc = mlp(hc, csh2, csc2, cgt2, nw2, fw, w1, w2, tm=tmc, final=False)
    return h
```

```python
import functools
import math

import jax
import jax.numpy as jnp
import numpy as np
from jax import lax
from jax.experimental import pallas as pl
from jax.experimental.pallas import tpu as pltpu

F32 = jnp.float32
BF16 = jnp.bfloat16
HIGHEST = lax.Precision.HIGHEST

D = 1024
EPS = 1e-6
GRID_W = 64
N_HEADS = 16
HEAD_DIM = 64
SSD_GROUPS = 2
SSD_STATE = 128
CHUNK = 128
SSD_SUB = 4
NA_KH = 8
NA_KW = 16
ROPE_BASE = 10000.0
D_FF = 4 * D
XBC = D + 2 * SSD_GROUPS * SSD_STATE
U_Q, U_K, U_V, U_GATE, U_CB, U_CC, U_CX, U_Z, U_XS, U_BC, U_COLS = (
    0, 1024, 2048, 3072, 6144, 7168, 8192, 9216, 10240, 11264, 11776)
U_TILE = 2944
U_SUB = 512
R_CB, R_CC, R_CX, R_Z, R_XBC, R_DT, R_Q, R_K, R_V, R_GATE = (
    0, 1024, 2048, 3072, 4096, 5632, 5664, 6688, 7712, 8736)
DT_PAD = 128
VMEM_LIMIT = 56 * 1024 * 1024


def _cparams(sem):
    return pltpu.CompilerParams(dimension_semantics=sem, vmem_limit_bytes=VMEM_LIMIT)


def _nt(a, b):
    return lax.dot_general(a, b, (((1,), (1,)), ((), ())), preferred_element_type=F32)


def _tn(a, b):
    return lax.dot_general(a, b, (((0,), (0,)), ((), ())), preferred_element_type=F32)


def _dot(a, b):
    return jnp.dot(a, b, preferred_element_type=F32)


def _dot_hi(a, b):
    return jnp.dot(a, b, preferred_element_type=F32, precision=HIGHEST)


def _silu(x):
    return x * jax.nn.sigmoid(x)


def _norm_mod(x, nw, sc, sh):
    ms = jnp.mean(x * x, axis=-1, keepdims=True)
    y = x * lax.rsqrt(ms + EPS) * nw
    return y * (1.0 + sc) + sh


def _ada_kernel(c_ref, w_ref, b_ref, o_ref):
    o_ref[...] = _dot_hi(_silu(c_ref[...]), w_ref[...]) + b_ref[...]


def ada_mod(c_rows, w_ada, b_ada):
    rows = c_rows.shape[0]
    tn = 1536
    return pl.pallas_call(
        _ada_kernel,
        grid=(6 * D // tn,),
        in_specs=[pl.BlockSpec((rows, D), lambda j: (0, 0)),
                  pl.BlockSpec((D, tn), lambda j: (0, j)),
                  pl.BlockSpec((1, tn), lambda j: (0, j))],
        out_specs=pl.BlockSpec((rows, tn), lambda j: (0, j)),
        out_shape=jax.ShapeDtypeStruct((rows, 6 * D), F32),
        compiler_params=_cparams(("arbitrary",)),
        name="ada_mod",
    )(c_rows, w_ada, b_ada.reshape(1, 6 * D))


def _rope_slab(x, cos, sa, sb):
    return x * cos + pltpu.roll(x, 2 * HEAD_DIM - 16, axis=1) * sa + pltpu.roll(x, 16, axis=1) * sb


def _inproj_kernel(x_ref, sh_ref, sc_ref, nw_ref, w_ref, wdt_ref, wdtT_ref, *rest, rope):
    if rope:
        cos_ref, sa_ref, sb_ref, u_ref, dt_ref, dtT_ref, xn_ref = rest
    else:
        u_ref, dt_ref, dtT_ref, xn_ref = rest
    j = pl.program_id(2)

    def normalise():
        xb = _norm_mod(x_ref[0], nw_ref[...], sc_ref[0], sh_ref[0]).astype(BF16)
        xn_ref[...] = xb
        dt_ref[0] = _dot(xb, wdt_ref[...])
        dtT_ref[0] = _nt(wdtT_ref[...], xb)
        return xb

    def tile(rope_cols, xb=None):
        for off in range(0, U_TILE, U_SUB):
            wd = min(U_SUB, U_TILE - off)
            r = _dot(xn_ref[...] if xb is None else xb, w_ref[:, off:off + wd])
            if off < rope_cols:
                scale = HEAD_DIM ** -0.5 if off < U_K else 1.0
                lw = 2 * HEAD_DIM
                r = jnp.concatenate(
                    [_rope_slab(r[:, s:s + lw], cos_ref[...], sa_ref[...], sb_ref[...]) * scale
                     for s in range(0, wd, lw)], axis=1)
            u_ref[0, :, off:off + wd] = r.astype(u_ref.dtype)

    @pl.when(j == 0)
    def _():
        tile(U_V if rope else 0, normalise())

    @pl.when(j > 0)
    def _():
        tile(0)


def in_proj(x, sh, sc, nw, w, wdt, wdtT, tm, rope_tables=None):
    B, L, _ = x.shape
    tn = U_TILE
    grid = (B, L // tm, U_COLS // tn)
    rope = rope_tables is not None
    rope_specs = [pl.BlockSpec((tm, 2 * HEAD_DIM), lambda b, i, j: (i, 0))] * 3 if rope else []
    return pl.pallas_call(
        functools.partial(_inproj_kernel, rope=rope),
        grid=grid,
        in_specs=[pl.BlockSpec((1, tm, D), lambda b, i, j: (b, i, 0)),
                  pl.BlockSpec((1, 1, D), lambda b, i, j: (b, 0, 0)),
                  pl.BlockSpec((1, 1, D), lambda b, i, j: (b, 0, 0)),
                  pl.BlockSpec((1, D), lambda b, i, j: (0, 0)),
                  pl.BlockSpec((D, tn), lambda b, i, j: (0, j)),
                  pl.BlockSpec((D, DT_PAD), lambda b, i, j: (0, 0)),
                  pl.BlockSpec((2 * N_HEADS, D), lambda b, i, j: (0, 0))] + rope_specs,
        out_specs=[pl.BlockSpec((1, tm, tn), lambda b, i, j: (b, i, j)),
                   pl.BlockSpec((1, tm, DT_PAD), lambda b, i, j: (b, i, 0)),
                   pl.BlockSpec((1, 2 * N_HEADS, tm), lambda b, i, j: (b, 0, i))],
        out_shape=[jax.ShapeDtypeStruct((B, L, U_COLS), BF16),
                   jax.ShapeDtypeStruct((B, L, DT_PAD), F32),
                   jax.ShapeDtypeStruct((B, 2 * N_HEADS, L), F32)],
        scratch_shapes=[pltpu.VMEM((tm, D), BF16)],
        compiler_params=_cparams(("parallel", "parallel", "arbitrary")),
        name="in_proj",
    )(x, sh, sc, nw, w, wdt, wdtT, *(rope_tables if rope else ()))


def _dwconv3(p, w):
    L = p.shape[0]
    row = lax.broadcasted_iota(jnp.int32, p.shape, 0)
    prev = jnp.where(row == 0, 0.0, pltpu.roll(p, 1, axis=0))
    nxt = jnp.where(row == L - 1, 0.0, pltpu.roll(p, L - 1, axis=0))
    return prev * w[0:1] + p * w[1:2] + nxt * w[2:3]


def _xbcconv_kernel(x_ref, w_ref, b_ref, o_ref):
    y = _dwconv3(x_ref[0].astype(F32), w_ref[...]) + b_ref[...]
    o_ref[0] = _silu(y).astype(o_ref.dtype)


def xbc_conv(u, w, bias, tc=256):
    B, L, _ = u.shape
    off = U_XS // tc
    return pl.pallas_call(
        _xbcconv_kernel,
        grid=(B, XBC // tc),
        in_specs=[pl.BlockSpec((1, L, tc), lambda b, j: (b, 0, off + j)),
                  pl.BlockSpec((3, tc), lambda b, j: (0, j)),
                  pl.BlockSpec((1, tc), lambda b, j: (0, j))],
        out_specs=pl.BlockSpec((1, L, tc), lambda b, j: (b, 0, j)),
        out_shape=jax.ShapeDtypeStruct((B, L, XBC), BF16),
        compiler_params=_cparams(("parallel", "parallel")),
        name="xbc_conv",
    )(u, w, bias.reshape(1, XBC))


def _split3(x):
    hi = x.astype(BF16)
    r1 = x - hi.astype(F32)
    mid = r1.astype(BF16)
    lo = (r1 - mid.astype(F32)).astype(BF16)
    return hi, mid, lo


def _head_stack(x2, lane):
    zero = jnp.zeros_like(x2)
    return jnp.concatenate([jnp.where(lane < HEAD_DIM, x2, zero), jnp.where(lane >= HEAD_DIM, x2, zero)], axis=0)


def _ssd_chunk(d, sub, rows, xs_ref, bc_ref, z_ref, dt_ref, dtT_ref, alr_ref, alc_ref, dbr_ref, dbc_ref, dsk_ref,
               nw_ref, ex_ref, y_ref, yacc_ref, st_ref):
    H = N_HEADS
    blk = slice(sub * CHUNK, (sub + 1) * CHUNK)
    xsb = xs_ref[0, blk, :]
    xs = xsb.astype(F32)
    bc = bc_ref[0, blk, :]
    dt_c = jax.nn.softplus(dt_ref[0, blk, :] + dbr_ref[...])
    a_c = dt_c * -jnp.exp(alr_ref[...])
    dtT_c = jax.nn.softplus(dtT_ref[0, :, blk] + dbc_ref[...])
    aT_c = dtT_c * -jnp.exp(alc_ref[...])

    ri = lax.broadcasted_iota(jnp.int32, (CHUNK, CHUNK), 0)
    ci = lax.broadcasted_iota(jnp.int32, (CHUNK, CHUNK), 1)
    tri = (ri >= ci) if d == 0 else (ri <= ci)
    trib = tri.astype(BF16)
    tribT = ((ri <= ci) if d == 0 else (ri >= ci)).astype(BF16)
    cs = _dot(jnp.concatenate([trib] * 3, axis=1), jnp.concatenate(_split3(a_c), axis=0))
    csT = _dot(jnp.concatenate(_split3(aT_c), axis=1), jnp.concatenate([tribT] * 3, axis=0))
    last = cs[CHUNK - 1:CHUNK] if d == 0 else cs[0:1]
    w_c = dt_c * jnp.exp(last - cs)
    ein_c = jnp.exp(cs)
    ex = ex_ref[d]
    w_e = _dot(jnp.concatenate(_split3(w_c)[:2], axis=1), ex)
    ein_e = _dot(jnp.concatenate(_split3(ein_c)[:2], axis=1), ex)
    cd_e = ein_e[CHUNK - 1:CHUNK] if d == 0 else ein_e[0:1]
    Xdec = (xs * w_e).astype(BF16)
    rowT = csT - jnp.log(dtT_c)
    lane = lax.broadcasted_iota(jnp.int32, (CHUNK, 2 * HEAD_DIM), 1)

    ys = []
    GW = D // SSD_GROUPS
    for g in range(SSD_GROUPS):
        Bg = bc[:, g * SSD_STATE:(g + 1) * SSD_STATE]
        Cg = bc[:, (SSD_GROUPS + g) * SSD_STATE:(SSD_GROUPS + g + 1) * SSD_STATE]
        S = _nt(Cg, Bg)
        st = st_ref[d, g]
        y_off = _dot(Cg, st.astype(BF16)) * ein_e[:, g * GW:(g + 1) * GW]
        y_diag = []
        for p in range(GW // (2 * HEAD_DIM)):
            hA = g * (H // SSD_GROUPS) + 2 * p
            Ms = []
            for h in (hA, hA + 1):
                col = cs[:, d * H + h:d * H + h + 1]
                rowv = rowT[d * H + h:d * H + h + 1, :]
                Ms.append((S * jnp.exp(jnp.where(tri, col - rowv, -jnp.inf))).astype(BF16))
            Mcat = jnp.concatenate(Ms, axis=1)
            Xp = xsb[:, hA * HEAD_DIM:(hA + 2) * HEAD_DIM]
            y_diag.append(_dot(Mcat, _head_stack(Xp, lane)))
        ys.append(y_off + jnp.concatenate(y_diag, axis=1))
        st_ref[d, g] = st * cd_e[:, g * GW:(g + 1) * GW] + _tn(Bg, Xdec[:, g * GW:(g + 1) * GW])
    y = jnp.concatenate(ys, axis=1)

    if d == 0:
        yacc_ref[rows, :] = y
    else:
        ysum = yacc_ref[rows, :] + y + xs * (dsk_ref[0:1] + dsk_ref[1:2])
        yg = ysum * _silu(z_ref[0, blk, :].astype(F32))
        parts = []
        for g in range(SSD_GROUPS):
            v = yg[:, g * GW:(g + 1) * GW]
            ms = jnp.mean(v * v, axis=-1, keepdims=True)
            parts.append(v * lax.rsqrt(ms + EPS) * nw_ref[:, g * GW:(g + 1) * GW])
        y_ref[0, blk, :] = jnp.concatenate(parts, axis=1).astype(y_ref.dtype)


def _ssd_kernel(xs_ref, bc_ref, z_ref, dt_ref, dtT_ref, h0_ref, alr_ref, alc_ref, dbr_ref, dbc_ref, dsk_ref, nw_ref,
                ex_ref, y_ref, sto_ref, yacc_ref, st_ref, *, ns, nsub):
    t = pl.program_id(1)

    @pl.when(t == 0)
    def _():
        st_ref[...] = h0_ref[0]

    args = (xs_ref, bc_ref, z_ref, dt_ref, dtT_ref, alr_ref, alc_ref, dbr_ref, dbc_ref, dsk_ref, nw_ref,
            ex_ref, y_ref, yacc_ref, st_ref)

    def seq_rows(step, sub):
        return pl.ds(pl.multiple_of((step * nsub + sub) * CHUNK, CHUNK), CHUNK)

    @pl.when(t < ns)
    def _():
        for sub in range(nsub):
            _ssd_chunk(0, sub, seq_rows(t, sub), *args)

    @pl.when(t >= ns)
    def _():
        for sub in reversed(range(nsub)):
            _ssd_chunk(1, sub, seq_rows(2 * ns - 1 - t, sub), *args)

    @pl.when(t == 2 * ns - 1)
    def _():
        sto_ref[0] = st_ref[...]


def ssd_mix(xbc_c, u, dt, dtT, h0, a_log, dt_bias, d_skip_e, norm_w):
    B, L, _ = u.shape
    nsub = min(SSD_SUB, L // CHUNK)
    RB = nsub * CHUNK
    ns = L // RB
    H2 = 2 * N_HEADS

    def chunk(t):
        return jnp.where(t < ns, t, 2 * ns - 1 - t)

    def late(t):
        return jnp.where(t < ns, ns - 1, 2 * ns - 1 - t)

    st_shape = (2, SSD_GROUPS, SSD_STATE, D // SSD_GROUPS)
    st_spec = pl.BlockSpec((1,) + st_shape, lambda b, t: (b, 0, 0, 0, 0))
    small = lambda shape: pl.BlockSpec(shape, lambda b, t: (0,) * len(shape))
    krow = jnp.arange(2 * DT_PAD, dtype=jnp.int32)[:, None] % DT_PAD
    head = jnp.arange(D, dtype=jnp.int32)[None, :] // HEAD_DIM
    ex = jnp.stack([krow == d * N_HEADS + head for d in range(2)]).astype(BF16)
    lane_pad = lambda v: jnp.pad(v.reshape(1, H2), ((0, 0), (0, DT_PAD - H2)))
    return pl.pallas_call(
        functools.partial(_ssd_kernel, ns=ns, nsub=nsub),
        grid=(B, 2 * ns),
        in_specs=[pl.BlockSpec((1, RB, D), lambda b, t: (b, chunk(t), 0)),
                  pl.BlockSpec((1, RB, XBC - D), lambda b, t: (b, chunk(t), D // (XBC - D))),
                  pl.BlockSpec((1, RB, D), lambda b, t: (b, late(t), U_Z // D)),
                  pl.BlockSpec((1, RB, DT_PAD), lambda b, t: (b, chunk(t), 0)),
                  pl.BlockSpec((1, H2, RB), lambda b, t: (b, 0, chunk(t))),
                  st_spec,
                  small((1, DT_PAD)), small((H2, 1)), small((1, DT_PAD)), small((H2, 1)),
                  small((2, D)), small((1, D)), small((2, 2 * DT_PAD, D))],
        out_specs=[pl.BlockSpec((1, RB, D), lambda b, t: (b, late(t), 0)),
                   st_spec],
        out_shape=[jax.ShapeDtypeStruct((B, L, D), BF16),
                   jax.ShapeDtypeStruct((B,) + st_shape, F32)],
        scratch_shapes=[pltpu.VMEM((L, D), F32), pltpu.VMEM(st_shape, F32)],
        compiler_params=_cparams(("parallel", "arbitrary")),
        name="ssd_mix",
    )(xbc_c, xbc_c, u, dt, dtT, h0,
      lane_pad(a_log), a_log.reshape(H2, 1), lane_pad(dt_bias), dt_bias.reshape(H2, 1),
      d_skip_e, norm_w.reshape(1, D), ex)


NA_GROUP = 1


def _na_kernel(q_ref, k_ref, v_ref, kc_ref, vc_ref, bias_ref, o_ref, s_ref, p_ref, inv_ref, *, rows):
    W = GRID_W
    band = NA_KH * W
    lane = lax.broadcasted_iota(jnp.int32, (W, 2 * HEAD_DIM), 1)
    G = NA_GROUP
    n_groups = rows // G
    M2 = 2 * W

    def band_start(r):
        return min(max(r - NA_KH // 2, 0), rows - NA_KH)

    def stage_scores(g, slot):
        for i in range(G):
            r = g * G + i
            r0 = band_start(r)
            qs = _head_stack(q_ref[0, r * W:(r + 1) * W, :], lane)
            kb = k_ref[0, r0 * W:r0 * W + band, :]
            base = r0 - r + NA_KH - 1
            for jp in range(NA_KH // 2):
                cols = slice(jp * M2, (jp + 1) * M2)
                s_ref[slot, i * M2:(i + 1) * M2, cols] = _nt(qs, kb[jp * M2:(jp + 1) * M2]) + bias_ref[0, base + 2 * jp]
            s_ref[slot, i * M2:(i + 1) * M2, band:] = _nt(qs, kc_ref[0])

    def stage_softmax(slot):
        for i in range(G):
            s = s_ref[slot, i * M2:(i + 1) * M2, :]
            p = jnp.exp(s - jnp.max(s, axis=-1, keepdims=True))
            inv = 1.0 / jnp.sum(p, axis=-1, keepdims=True)
            p_ref[slot, i * M2:(i + 1) * M2, :] = p.astype(BF16)
            inv_ref[slot, i * M2:(i + 1) * M2, :] = jnp.broadcast_to(inv, (M2, M2))

    def stage_values(g, slot):
        for i in range(G):
            r = g * G + i
            r0 = band_start(r)
            vb = v_ref[0, r0 * W:r0 * W + band, :]
            p = p_ref[slot, i * M2:(i + 1) * M2, :]
            o2 = (_dot(p[:, :band], vb) + _dot(p[:, band:], vc_ref[0])) * inv_ref[slot, i * M2:(i + 1) * M2, :]
            o = jnp.where(lane < HEAD_DIM, o2[:W], o2[W:])
            o_ref[0, r * W:(r + 1) * W, :] = o.astype(o_ref.dtype)

    stage_scores(0, 0)
    stage_scores(1, 1)
    stage_softmax(0)

    for g in range(n_groups - 2):
        stage_scores(g + 2, g % 2)
        stage_softmax((g + 1) % 2)
        stage_values(g, g % 2)
    stage_softmax((n_groups - 1) % 2)
    stage_values(n_groups - 2, (n_groups - 2) % 2)
    stage_values(n_groups - 1, (n_groups - 1) % 2)


def na_attend(u, u_ctx, bias):
    B, L, _ = u.shape
    Lc = u_ctx.shape[1]
    HP = N_HEADS // 2
    lw = 2 * HEAD_DIM
    oq, ok, ov = U_Q // lw, U_K // lw, U_V // lw
    return pl.pallas_call(
        functools.partial(_na_kernel, rows=L // GRID_W),
        grid=(HP, B),
        in_specs=[pl.BlockSpec((1, L, lw), lambda h, b: (b, 0, oq + h)),
                  pl.BlockSpec((1, L, lw), lambda h, b: (b, 0, ok + h)),
                  pl.BlockSpec((1, L, lw), lambda h, b: (b, 0, ov + h)),
                  pl.BlockSpec((1, Lc, lw), lambda h, b: (b, 0, ok + h)),
                  pl.BlockSpec((1, Lc, lw), lambda h, b: (b, 0, ov + h)),
                  pl.BlockSpec((1, 2 * NA_KH - 2, 2 * GRID_W, 2 * GRID_W), lambda h, b: (h, 0, 0, 0))],
        out_specs=pl.BlockSpec((1, L, lw), lambda h, b: (b, 0, h)),
        out_shape=jax.ShapeDtypeStruct((B, L, D), BF16),
        scratch_shapes=[pltpu.VMEM((2, NA_GROUP * lw, NA_KH * GRID_W + Lc), F32),
                        pltpu.VMEM((2, NA_GROUP * lw, NA_KH * GRID_W + Lc), BF16),
                        pltpu.VMEM((2, NA_GROUP * lw, lw), F32)],
        compiler_params=_cparams(("parallel", "parallel")),
        name="na_attend",
    )(u, u, u, u_ctx, u_ctx, bias)


def _ctxattn_kernel(q_ref, k_ref, v_ref, o_ref):
    Lc = q_ref.shape[1]
    lane = lax.broadcasted_iota(jnp.int32, (Lc, 2 * HEAD_DIM), 1)
    q2 = (q_ref[0].astype(F32) * (HEAD_DIM ** -0.5)).astype(BF16)
    s = _nt(_head_stack(q2, lane), k_ref[0])
    m = jnp.max(s, axis=-1, keepdims=True)
    p = jnp.exp(s - m)
    p = (p * (1.0 / jnp.sum(p, axis=-1, keepdims=True))).astype(BF16)
    o2 = _dot(p, v_ref[0])
    o_ref[0] = jnp.where(lane < HEAD_DIM, o2[:Lc], o2[Lc:]).astype(o_ref.dtype)


def ctx_attend(u_ctx):
    B, Lc, _ = u_ctx.shape
    HP = N_HEADS // 2
    lw = 2 * HEAD_DIM
    oq, ok, ov = U_Q // lw, U_K // lw, U_V // lw
    return pl.pallas_call(
        _ctxattn_kernel,
        grid=(HP, B),
        in_specs=[pl.BlockSpec((1, Lc, lw), lambda h, b: (b, 0, oq + h)),
                  pl.BlockSpec((1, Lc, lw), lambda h, b: (b, 0, ok + h)),
                  pl.BlockSpec((1, Lc, lw), lambda h, b: (b, 0, ov + h))],
        out_specs=pl.BlockSpec((1, Lc, lw), lambda h, b: (b, 0, h)),
        out_shape=jax.ShapeDtypeStruct((B, Lc, D), BF16),
        compiler_params=_cparams(("parallel", "parallel")),
        name="ctx_attend",
    )(u_ctx, u_ctx, u_ctx)


HALO = 16


def _merge_kernel(cb_ref, cc_ref, cx_ref, ccp_ref, cxp_ref, ccn_ref, cxn_ref, cw_ref, ys_ref, yn_ref, g_ref, h_ref,
                  gt_ref, wc_ref, ws_ref, wn_ref, wo_ref, o_ref, *, seq_len):
    tm = h_ref.shape[1]
    p = cc_ref[0].astype(F32) * cx_ref[0].astype(F32)
    p_before = (ccp_ref[0].astype(F32) * cxp_ref[0].astype(F32))[HALO - 1:HALO]
    p_after = (ccn_ref[0].astype(F32) * cxn_ref[0].astype(F32))[0:1]
    row = lax.broadcasted_iota(jnp.int32, (tm, D), 0)
    pos = (pl.program_id(1) * tm + row) & (seq_len - 1)
    prev = jnp.where(row == 0, p_before, pltpu.roll(p, 1, axis=0))
    prev = jnp.where(pos == 0, 0.0, prev)
    nxt = jnp.where(row == tm - 1, p_after, pltpu.roll(p, tm - 1, axis=0))
    nxt = jnp.where(pos == seq_len - 1, 0.0, nxt)
    w = cw_ref[...]
    yc = (cb_ref[0].astype(F32) * (prev * w[0:1] + p * w[1:2] + nxt * w[2:3])).astype(BF16)

    g = jax.nn.sigmoid(g_ref[0].astype(F32))
    m = (g[:, 0:D] * _dot(yc, wc_ref[...])
         + g[:, D:2 * D] * _dot(ys_ref[0], ws_ref[...])
         + g[:, 2 * D:3 * D] * _dot(yn_ref[0], wn_ref[...]))
    o_ref[0] = h_ref[0] + gt_ref[0] * _dot(m.astype(BF16), wo_ref[...])


def merge(u, ys, yn, h, gt, cw, wc, ws, wn, wo, tm, seq_len):
    B, L, _ = h.shape
    assert seq_len & (seq_len - 1) == 0 and (tm % seq_len == 0 or seq_len % tm == 0)
    nb, hb = tm // HALO, L // HALO
    tok = lambda: pl.BlockSpec((1, tm, D), lambda b, i: (b, i, 0))
    ucol = lambda c: pl.BlockSpec((1, tm, D), lambda b, i: (b, i, c // D))
    before = lambda c: pl.BlockSpec((1, HALO, D), lambda b, i: (b, jnp.maximum(i * nb - 1, 0), c // D))
    after = lambda c: pl.BlockSpec((1, HALO, D), lambda b, i: (b, jnp.minimum((i + 1) * nb, hb - 1), c // D))
    wsp = lambda: pl.BlockSpec((D, D), lambda b, i: (0, 0))
    return pl.pallas_call(
        functools.partial(_merge_kernel, seq_len=seq_len),
        grid=(B, L // tm),
        in_specs=[ucol(U_CB), ucol(U_CC), ucol(U_CX), before(U_CC), before(U_CX), after(U_CC), after(U_CX),
                  pl.BlockSpec((3, D), lambda b, i: (0, 0)),
                  tok(), tok(),
                  pl.BlockSpec((1, tm, 3 * D), lambda b, i: (b, i, U_GATE // (3 * D))),
                  tok(),
                  pl.BlockSpec((1, 1, D), lambda b, i: (b, 0, 0)),
                  wsp(), wsp(), wsp(), wsp()],
        out_specs=tok(),
        out_shape=jax.ShapeDtypeStruct((B, L, D), F32),
        compiler_params=_cparams(("parallel", "parallel")),
        name="merge",
    )(u, u, u, u, u, u, u, cw, ys, yn, u, h, gt, wc, ws, wn, wo)


MLP_TF = 1024


def _mlp_kernel(h_ref, sh_ref, sc_ref, gt_ref, nw_ref, fw_ref, w1_ref, w2_ref, o_ref, *, final):
    h = h_ref[0]
    xn = _norm_mod(h, nw_ref[...], sc_ref[0], sh_ref[0]).astype(BF16)
    acc = None
    for k in range(0, D_FF, MLP_TF):
        a = jnp.square(jnp.maximum(_dot(xn, w1_ref[:, k:k + MLP_TF]), 0.0)).astype(BF16)
        t = _dot(a, w2_ref[k:k + MLP_TF, :])
        acc = t if acc is None else acc + t
    o = h + gt_ref[0] * acc
    if final:
        ms = jnp.mean(o * o, axis=-1, keepdims=True)
        o = o * lax.rsqrt(ms + EPS) * fw_ref[...]
    o_ref[0] = o


def mlp(h, sh, sc, gt, nw, fw, w1, w2, tm, final):
    B, L, _ = h.shape
    vec = lambda: pl.BlockSpec((1, 1, D), lambda b, i: (b, 0, 0))
    par = lambda: pl.BlockSpec((1, D), lambda b, i: (0, 0))
    resident = lambda shape: pl.BlockSpec(shape, lambda b, i: (0, 0), pipeline_mode=pl.Buffered(1))
    return pl.pallas_call(
        functools.partial(_mlp_kernel, final=final),
        grid=(B, L // tm),
        in_specs=[pl.BlockSpec((1, tm, D), lambda b, i: (b, i, 0)),
                  vec(), vec(), vec(), par(), par(),
                  resident((D, D_FF)), resident((D_FF, D))],
        out_specs=pl.BlockSpec((1, tm, D), lambda b, i: (b, i, 0)),
        out_shape=jax.ShapeDtypeStruct((B, L, D), F32),
        compiler_params=_cparams(("parallel", "parallel")),
        name="mlp",
    )(h, sh, sc, gt, nw, fw, w1, w2)


def _rope_tables(L):
    t = jnp.arange(L, dtype=jnp.int32)
    row = (t // GRID_W).astype(F32)
    col = (t % GRID_W).astype(F32)
    half = HEAD_DIM // 2
    inv = ROPE_BASE ** (-jnp.arange(0, half, 2, dtype=F32) / half)
    ang_r = row[:, None] * inv
    ang_c = col[:, None] * inv
    ang = jnp.concatenate([ang_r, ang_r, ang_c, ang_c], axis=-1)
    cos = jnp.tile(jnp.cos(ang), (1, 2))
    sin = jnp.tile(jnp.sin(ang), (1, 2))
    even = ((jnp.arange(2 * HEAD_DIM) // (half // 2)) % 2 == 0)[None, :]
    return cos, jnp.where(even, -sin, 0.0), jnp.where(even, 0.0, sin)


def _rpb_kernel(r_ref, oh_ref, ok_ref, o_ref):
    val = _dot(jnp.concatenate(_split3(r_ref[...]), axis=1), oh_ref[...])
    o_ref[...] = jnp.where(ok_ref[...] > 0.0, val, -jnp.inf)


def _na_bias_table(rpb):
    H, NR, NC = rpb.shape
    W = GRID_W
    col = np.arange(W)
    col_start = np.clip(col - NA_KW // 2, 0, W - NA_KW)
    col_ok = (col[None, :] >= col_start[:, None]) & (col[None, :] < col_start[:, None] + NA_KW)
    dc_idx = np.clip(col[None, :] - col[:, None], -(NA_KW - 1), NA_KW - 1) + NA_KW - 1
    rows = jnp.transpose(rpb.reshape(H // 2, 2, NR, NC), (0, 2, 1, 3)).reshape(H * NR, NC)
    rows = jnp.pad(rows, ((0, 0), (0, DT_PAD - NC)))
    krow = jnp.arange(3 * DT_PAD, dtype=jnp.int32)[:, None] % DT_PAD
    onehot = (krow == jnp.asarray(dc_idx.reshape(1, W * W), jnp.int32)).astype(BF16)
    ok = jnp.asarray(col_ok.reshape(1, W * W), F32)
    tn = 1024
    tab = pl.pallas_call(
        _rpb_kernel,
        grid=(W * W // tn,),
        in_specs=[pl.BlockSpec((H * NR, DT_PAD), lambda j: (0, 0)),
                  pl.BlockSpec((3 * DT_PAD, tn), lambda j: (0, j)),
                  pl.BlockSpec((1, tn), lambda j: (0, j))],
        out_specs=pl.BlockSpec((H * NR, tn), lambda j: (0, j)),
        out_shape=jax.ShapeDtypeStruct((H * NR, W * W), F32),
        compiler_params=_cparams(("arbitrary",)),
        name="rpb_table",
    )(rows, onehot, ok)
    tab = tab.reshape(H // 2, NR, 2 * W, W)
    return jnp.concatenate([tab[:, :NR - 1], tab[:, 1:]], axis=-1)


def _prep_w_in(w_in):
    order = [(R_Q, D), (R_K, D), (R_V, D), (R_GATE, 3 * D), (R_CB, D), (R_CC, D), (R_CX, D), (R_Z, D), (R_XBC, XBC)]
    w = jnp.concatenate([w_in[:, o:o + n] for o, n in order], axis=1).astype(BF16)
    wdt = w_in[:, R_DT:R_DT + 2 * N_HEADS].astype(BF16)
    return w, jnp.pad(wdt, ((0, 0), (0, DT_PAD - 2 * N_HEADS))), wdt.T


def kernel(x, c, ctx, c_ctx, w_ada, b_ada, norm1_w, w_in, conv_mix_w, ssd_conv_w, ssd_conv_b, ssd_a_log, ssd_dt_bias,
           ssd_d, ssd_norm_w, na_rpb, w_br_conv, w_br_ssd, w_br_na, w_out, norm2_w, w_ff1, w_ff2, final_norm_w):
    B, L, _ = x.shape
    Lc = ctx.shape[1]
    depth = w_in.shape[0]
    cos, sa, sb = _rope_tables(L)
    n_mod = B + 1
    pad = (-n_mod) % 8
    c_rows = jnp.concatenate([c, c_ctx[None, :], jnp.zeros((pad, D), F32)], axis=0)
    zero_state = jnp.zeros((B, 2, SSD_GROUPS, SSD_STATE, D // SSD_GROUPS), F32)
    fw = final_norm_w.reshape(1, D)
    Tc = B * Lc
    tmc = min(1024, Tc)
    flat = lambda a: a.reshape(1, Tc, a.shape[-1])
    h, hc = x, flat(ctx)
    for l in range(depth):
        last = l == depth - 1
        mod = ada_mod(c_rows, w_ada[l], b_ada[l])
        m_lat = mod[:B].reshape(B, 1, 6, D)
        m_ctx = mod[B:B + 1].reshape(1, 1, 6, D)
        sh1, sc1, gt1, sh2, sc2, gt2 = (m_lat[:, :, i] for i in range(6))
        csh1, csc1, cgt1, csh2, csc2, cgt2 = (m_ctx[:, :, i] for i in range(6))
        w, wdt, wdtT = _prep_w_in(w_in[l])
        nw1 = norm1_w[l].reshape(1, D)
        u, dt, dtT = in_proj(h, sh1, sc1, nw1, w, wdt, wdtT, tm=1024, rope_tables=(cos, sa, sb))
        uc, dtc, dtTc = in_proj(hc, csh1, csc1, nw1, w, wdt, wdtT, tm=tmc)
        uc = uc.reshape(B, Lc, U_COLS)
        dtc = dtc.reshape(B, Lc, DT_PAD)
        dtTc = jnp.transpose(dtTc.reshape(2 * N_HEADS, B, Lc), (1, 0, 2))
        d_skip_e = jnp.repeat(ssd_d[l], HEAD_DIM, axis=1)
        ssd_p = (ssd_a_log[l], ssd_dt_bias[l], d_skip_e, ssd_norm_w[l])
        xbc_c = xbc_conv(uc, ssd_conv_w[l], ssd_conv_b[l])
        y_ssd_c, ctx_states = ssd_mix(xbc_c, uc, dtc, dtTc, zero_state, *ssd_p)
        xbc_l = xbc_conv(u, ssd_conv_w[l], ssd_conv_b[l])
        y_ssd, _ = ssd_mix(xbc_l, u, dt, dtT, ctx_states, *ssd_p)
        y_na = na_attend(u, uc, _na_bias_table(na_rpb[l]))
        wb = [t[l].astype(BF16) for t in (w_br_conv, w_br_ssd, w_br_na, w_out)]
        w1, w2 = w_ff1[l].astype(BF16), w_ff2[l].astype(BF16)
        nw2 = norm2_w[l].reshape(1, D)
        h = merge(u, y_ssd, y_na, h, gt1, conv_mix_w[l], *wb, tm=512, seq_len=L)
        h = mlp(h, sh2, sc2, gt2, nw2, fw, w1, w2, tm=1024, final=last)
        if not last:
            y_na_c = ctx_attend(uc)
            hc = merge(flat(uc), flat(y_ssd_c), flat(y_na_c), hc, cgt1, conv_mix_w[l], *wb, tm=min(512, Tc), seq_len=Lc)
            hc = mlp(hc, csh2, csc2, cgt2, nw2, fw, w1, w2, tm=tmc, final=False)
    return h
```

```python
import functools
import math

import jax
import jax.numpy as jnp
import numpy as np
from jax import lax
from jax.experimental import pallas as pl
from jax.experimental.pallas import tpu as pltpu

F32 = jnp.float32
BF16 = jnp.bfloat16
HIGHEST = lax.Precision.HIGHEST

D = 1024
EPS = 1e-6
GRID_W = 64
N_HEADS = 16
HEAD_DIM = 64
SSD_GROUPS = 2
SSD_STATE = 128
CHUNK = 128
SSD_SUB = 4
NA_KH = 8
NA_KW = 16
ROPE_BASE = 10000.0
D_FF = 4 * D
XBC = D + 2 * SSD_GROUPS * SSD_STATE
U_Q, U_K, U_V, U_GATE, U_CB, U_CC, U_CX, U_Z, U_XS, U_BC, U_COLS = (
    0, 1024, 2048, 3072, 6144, 7168, 8192, 9216, 10240, 11264, 11776)
U_TILE = 2944
U_SUB = 512
R_CB, R_CC, R_CX, R_Z, R_XBC, R_DT, R_Q, R_K, R_V, R_GATE = (
    0, 1024, 2048, 3072, 4096, 5632, 5664, 6688, 7712, 8736)
DT_PAD = 128
HALO = 16
VMEM_LIMIT = 56 * 1024 * 1024


def _cparams(sem):
    return pltpu.CompilerParams(dimension_semantics=sem, vmem_limit_bytes=VMEM_LIMIT)


def _nt(a, b):
    return lax.dot_general(a, b, (((1,), (1,)), ((), ())), preferred_element_type=F32)


def _tn(a, b):
    return lax.dot_general(a, b, (((0,), (0,)), ((), ())), preferred_element_type=F32)


def _dot(a, b):
    return jnp.dot(a, b, preferred_element_type=F32)


def _dot_hi(a, b):
    return jnp.dot(a, b, preferred_element_type=F32, precision=HIGHEST)


def _sigmoid(x):
    return 0.5 * jnp.tanh(0.5 * x) + 0.5


def _silu(x):
    return x * _sigmoid(x)


def _norm_mod(x, nw, sc, sh):
    ms = jnp.mean(x * x, axis=-1, keepdims=True)
    y = x * lax.rsqrt(ms + EPS) * nw
    return y * (1.0 + sc) + sh


def _ada_kernel(c_ref, w_ref, b_ref, o_ref):
    o_ref[...] = _dot_hi(_silu(c_ref[...]), w_ref[...]) + b_ref[...]


def ada_mod(c_rows, w_ada, b_ada):
    rows = c_rows.shape[0]
    tn = 1536
    return pl.pallas_call(
        _ada_kernel,
        grid=(6 * D // tn,),
        in_specs=[pl.BlockSpec((rows, D), lambda j: (0, 0)),
                  pl.BlockSpec((D, tn), lambda j: (0, j)),
                  pl.BlockSpec((1, tn), lambda j: (0, j))],
        out_specs=pl.BlockSpec((rows, tn), lambda j: (0, j)),
        out_shape=jax.ShapeDtypeStruct((rows, 6 * D), F32),
        compiler_params=_cparams(("arbitrary",)),
        name="ada_mod",
    )(c_rows, w_ada, b_ada.reshape(1, 6 * D))


def _rope_slab(x, cos, sa, sb):
    return x * cos + pltpu.roll(x, 2 * HEAD_DIM - 16, axis=1) * sa + pltpu.roll(x, 16, axis=1) * sb


def _inproj_kernel(x_ref, sh_ref, sc_ref, nw_ref, w_ref, wdt_ref, wdtT_ref, *rest, rope):
    if rope:
        cos_ref, sa_ref, sb_ref, u_ref, dt_ref, dtT_ref, xn_ref = rest
    else:
        u_ref, dt_ref, dtT_ref, xn_ref = rest
    j = pl.program_id(2)

    def normalise():
        xb = _norm_mod(x_ref[0], nw_ref[...], sc_ref[0], sh_ref[0]).astype(BF16)
        xn_ref[...] = xb
        dt_ref[0] = _dot(xb, wdt_ref[...])
        dtT_ref[0] = _nt(wdtT_ref[...], xb)
        return xb

    def tile(rope_cols, xb=None):
        for off in range(0, U_TILE, U_SUB):
            wd = min(U_SUB, U_TILE - off)
            r = _dot(xn_ref[...] if xb is None else xb, w_ref[:, off:off + wd])
            if off < rope_cols:
                scale = HEAD_DIM ** -0.5 if off < U_K else 1.0
                lw = 2 * HEAD_DIM
                r = jnp.concatenate(
                    [_rope_slab(r[:, s:s + lw], cos_ref[...], sa_ref[...], sb_ref[...]) * scale
                     for s in range(0, wd, lw)], axis=1)
            u_ref[0, :, off:off + wd] = r.astype(u_ref.dtype)

    @pl.when(j == 0)
    def _():
        tile(U_V if rope else 0, normalise())

    @pl.when(j > 0)
    def _():
        tile(0)


def in_proj(x, sh, sc, nw, w, wdt, wdtT, tm, rope_tables=None):
    B, L, _ = x.shape
    tn = U_TILE
    grid = (B, L // tm, U_COLS // tn)
    rope = rope_tables is not None
    rope_specs = [pl.BlockSpec((tm, 2 * HEAD_DIM), lambda b, i, j: (i, 0))] * 3 if rope else []
    return pl.pallas_call(
        functools.partial(_inproj_kernel, rope=rope),
        grid=grid,
        in_specs=[pl.BlockSpec((1, tm, D), lambda b, i, j: (b, i, 0)),
                  pl.BlockSpec((1, 1, D), lambda b, i, j: (b, 0, 0)),
                  pl.BlockSpec((1, 1, D), lambda b, i, j: (b, 0, 0)),
                  pl.BlockSpec((1, D), lambda b, i, j: (0, 0)),
                  pl.BlockSpec((D, tn), lambda b, i, j: (0, j)),
                  pl.BlockSpec((D, DT_PAD), lambda b, i, j: (0, 0)),
                  pl.BlockSpec((2 * N_HEADS, D), lambda b, i, j: (0, 0))] + rope_specs,
        out_specs=[pl.BlockSpec((1, tm, tn), lambda b, i, j: (b, i, j)),
                   pl.BlockSpec((1, tm, DT_PAD), lambda b, i, j: (b, i, 0)),
                   pl.BlockSpec((1, 2 * N_HEADS, tm), lambda b, i, j: (b, 0, i))],
        out_shape=[jax.ShapeDtypeStruct((B, L, U_COLS), BF16),
                   jax.ShapeDtypeStruct((B, L, DT_PAD), F32),
                   jax.ShapeDtypeStruct((B, 2 * N_HEADS, L), F32)],
        scratch_shapes=[pltpu.VMEM((tm, D), BF16)],
        compiler_params=_cparams(("parallel", "parallel", "arbitrary")),
        name="in_proj",
    )(x, sh, sc, nw, w, wdt, wdtT, *(rope_tables if rope else ()))


def _dwconv3(p, w):
    L = p.shape[0]
    row = lax.broadcasted_iota(jnp.int32, p.shape, 0)
    prev = jnp.where(row == 0, 0.0, pltpu.roll(p, 1, axis=0))
    nxt = jnp.where(row == L - 1, 0.0, pltpu.roll(p, L - 1, axis=0))
    return prev * w[0:1] + p * w[1:2] + nxt * w[2:3]


def _xbcconv_kernel(x_ref, w_ref, b_ref, o_ref):
    L = x_ref.shape[1]
    E = 24
    w = w_ref[...]
    b = b_ref[...]

    def act(h):
        return (h + h * jnp.tanh(h)).astype(o_ref.dtype)

    x = x_ref[0].astype(F32)
    o_ref[0] = act(pltpu.roll(x, 1, axis=0) * w[0:1] + x * w[1:2] + pltpu.roll(x, L - 1, axis=0) * w[2:3] + b)
    o_ref[0, 0:HALO, :] = act(_dwconv3(x[0:E], w) + b)[0:HALO]
    o_ref[0, L - HALO:L, :] = act(_dwconv3(x[L - E:L], w) + b)[E - HALO:E]


def xbc_conv(u, w, bias, tc=256):
    B, L, _ = u.shape
    off = U_XS // tc
    return pl.pallas_call(
        _xbcconv_kernel,
        grid=(B, XBC // tc),
        in_specs=[pl.BlockSpec((1, L, tc), lambda b, j: (b, 0, off + j)),
                  pl.BlockSpec((3, tc), lambda b, j: (0, j)),
                  pl.BlockSpec((1, tc), lambda b, j: (0, j))],
        out_specs=pl.BlockSpec((1, L, tc), lambda b, j: (b, 0, j)),
        out_shape=jax.ShapeDtypeStruct((B, L, XBC), BF16),
        compiler_params=_cparams(("parallel", "parallel")),
        name="xbc_conv",
    )(u, 0.5 * w, 0.5 * bias.reshape(1, XBC))


def _split3(x):
    hi = x.astype(BF16)
    r1 = x - hi.astype(F32)
    mid = r1.astype(BF16)
    lo = (r1 - mid.astype(F32)).astype(BF16)
    return hi, mid, lo


def _head_stack(x2, lane):
    zero = jnp.zeros_like(x2)
    return jnp.concatenate([jnp.where(lane < HEAD_DIM, x2, zero), jnp.where(lane >= HEAD_DIM, x2, zero)], axis=0)


def _ssd_chunk(d, sub, rows, xs_ref, bc_ref, z_ref, dt_ref, dtT_ref, alr_ref, alc_ref, dbr_ref, dbc_ref, dsk_ref,
               nw_ref, ex_ref, y_ref, yacc_ref, st_ref):
    H = N_HEADS
    blk = slice(sub * CHUNK, (sub + 1) * CHUNK)
    xsb = xs_ref[0, blk, :]
    xs = xsb.astype(F32)
    bc = bc_ref[0, blk, :]
    dt_c = jax.nn.softplus(dt_ref[0, blk, :] + dbr_ref[...])
    a_c = dt_c * -jnp.exp(alr_ref[...])
    dtT_c = jax.nn.softplus(dtT_ref[0, :, blk] + dbc_ref[...])
    aT_c = dtT_c * -jnp.exp(alc_ref[...])

    ri = lax.broadcasted_iota(jnp.int32, (CHUNK, CHUNK), 0)
    ci = lax.broadcasted_iota(jnp.int32, (CHUNK, CHUNK), 1)
    tri = (ri >= ci) if d == 0 else (ri <= ci)
    trib = tri.astype(BF16)
    tribT = ((ri <= ci) if d == 0 else (ri >= ci)).astype(BF16)
    cs = _dot(jnp.concatenate([trib] * 3, axis=1), jnp.concatenate(_split3(a_c), axis=0))
    csT = _dot(jnp.concatenate(_split3(aT_c), axis=1), jnp.concatenate([tribT] * 3, axis=0))
    last = cs[CHUNK - 1:CHUNK] if d == 0 else cs[0:1]
    w_c = dt_c * jnp.exp(last - cs)
    ein_c = jnp.exp(cs)
    ex = ex_ref[d]
    w_e = _dot(jnp.concatenate(_split3(w_c)[:2], axis=1), ex)
    ein_e = _dot(jnp.concatenate(_split3(ein_c)[:2], axis=1), ex)
    cd_e = ein_e[CHUNK - 1:CHUNK] if d == 0 else ein_e[0:1]
    Xdec = (xs * w_e).astype(BF16)
    rowT = csT - jnp.log(dtT_c)
    lane = lax.broadcasted_iota(jnp.int32, (CHUNK, 2 * HEAD_DIM), 1)

    ys = []
    GW = D // SSD_GROUPS
    for g in range(SSD_GROUPS):
        Bg = bc[:, g * SSD_STATE:(g + 1) * SSD_STATE]
        Cg = bc[:, (SSD_GROUPS + g) * SSD_STATE:(SSD_GROUPS + g + 1) * SSD_STATE]
        S = _nt(Cg, Bg)
        st = st_ref[d, g]
        y_off = _dot(Cg, st.astype(BF16)) * ein_e[:, g * GW:(g + 1) * GW]
        y_diag = []
        for p in range(GW // (2 * HEAD_DIM)):
            hA = g * (H // SSD_GROUPS) + 2 * p
            Ms = []
            for h in (hA, hA + 1):
                col = cs[:, d * H + h:d * H + h + 1]
                rowv = rowT[d * H + h:d * H + h + 1, :]
                Ms.append((S * jnp.exp(jnp.where(tri, col - rowv, -jnp.inf))).astype(BF16))
            Mcat = jnp.concatenate(Ms, axis=1)
            Xp = xsb[:, hA * HEAD_DIM:(hA + 2) * HEAD_DIM]
            y_diag.append(_dot(Mcat, _head_stack(Xp, lane)))
        ys.append(y_off + jnp.concatenate(y_diag, axis=1))
        st_ref[d, g] = st * cd_e[:, g * GW:(g + 1) * GW] + _tn(Bg, Xdec[:, g * GW:(g + 1) * GW])
    y = jnp.concatenate(ys, axis=1)

    if d == 0:
        yacc_ref[rows, :] = y
    else:
        ysum = yacc_ref[rows, :] + y + xs * (dsk_ref[0:1] + dsk_ref[1:2])
        yg = ysum * _silu(z_ref[0, blk, :].astype(F32))
        parts = []
        for g in range(SSD_GROUPS):
            v = yg[:, g * GW:(g + 1) * GW]
            ms = jnp.mean(v * v, axis=-1, keepdims=True)
            parts.append(v * lax.rsqrt(ms + EPS) * nw_ref[:, g * GW:(g + 1) * GW])
        y_ref[0, blk, :] = jnp.concatenate(parts, axis=1).astype(y_ref.dtype)


def _ssd_kernel(xs_ref, bc_ref, z_ref, dt_ref, dtT_ref, h0_ref, alr_ref, alc_ref, dbr_ref, dbc_ref, dsk_ref, nw_ref,
                ex_ref, y_ref, sto_ref, yacc_ref, st_ref, *, ns, nsub):
    t = pl.program_id(1)

    @pl.when(t == 0)
    def _():
        st_ref[...] = h0_ref[0]

    args = (xs_ref, bc_ref, z_ref, dt_ref, dtT_ref, alr_ref, alc_ref, dbr_ref, dbc_ref, dsk_ref, nw_ref,
            ex_ref, y_ref, yacc_ref, st_ref)

    def seq_rows(step, sub):
        return pl.ds(pl.multiple_of((step * nsub + sub) * CHUNK, CHUNK), CHUNK)

    @pl.when(t < ns)
    def _():
        for sub in range(nsub):
            _ssd_chunk(0, sub, seq_rows(t, sub), *args)

    @pl.when(t >= ns)
    def _():
        for sub in reversed(range(nsub)):
            _ssd_chunk(1, sub, seq_rows(2 * ns - 1 - t, sub), *args)

    @pl.when(t == 2 * ns - 1)
    def _():
        sto_ref[0] = st_ref[...]


def ssd_mix(xbc_c, u, dt, dtT, h0, a_log, dt_bias, d_skip_e, norm_w):
    B, L, _ = u.shape
    nsub = min(SSD_SUB, L // CHUNK)
    RB = nsub * CHUNK
    ns = L // RB
    H2 = 2 * N_HEADS

    def chunk(t):
        return jnp.where(t < ns, t, 2 * ns - 1 - t)

    def late(t):
        return jnp.where(t < ns, ns - 1, 2 * ns - 1 - t)

    st_shape = (2, SSD_GROUPS, SSD_STATE, D // SSD_GROUPS)
    st_spec = pl.BlockSpec((1,) + st_shape, lambda b, t: (b, 0, 0, 0, 0))
    small = lambda shape: pl.BlockSpec(shape, lambda b, t: (0,) * len(shape))
    krow = jnp.arange(2 * DT_PAD, dtype=jnp.int32)[:, None] % DT_PAD
    head = jnp.arange(D, dtype=jnp.int32)[None, :] // HEAD_DIM
    ex = jnp.stack([krow == d * N_HEADS + head for d in range(2)]).astype(BF16)
    lane_pad = lambda v: jnp.pad(v.reshape(1, H2), ((0, 0), (0, DT_PAD - H2)))
    return pl.pallas_call(
        functools.partial(_ssd_kernel, ns=ns, nsub=nsub),
        grid=(B, 2 * ns),
        in_specs=[pl.BlockSpec((1, RB, D), lambda b, t: (b, chunk(t), 0)),
                  pl.BlockSpec((1, RB, XBC - D), lambda b, t: (b, chunk(t), D // (XBC - D))),
                  pl.BlockSpec((1, RB, D), lambda b, t: (b, late(t), U_Z // D)),
                  pl.BlockSpec((1, RB, DT_PAD), lambda b, t: (b, chunk(t), 0)),
                  pl.BlockSpec((1, H2, RB), lambda b, t: (b, 0, chunk(t))),
                  st_spec,
                  small((1, DT_PAD)), small((H2, 1)), small((1, DT_PAD)), small((H2, 1)),
                  small((2, D)), small((1, D)), small((2, 2 * DT_PAD, D))],
        out_specs=[pl.BlockSpec((1, RB, D), lambda b, t: (b, late(t), 0)),
                   st_spec],
        out_shape=[jax.ShapeDtypeStruct((B, L, D), BF16),
                   jax.ShapeDtypeStruct((B,) + st_shape, F32)],
        scratch_shapes=[pltpu.VMEM((L, D), F32), pltpu.VMEM(st_shape, F32)],
        compiler_params=_cparams(("parallel", "arbitrary")),
        name="ssd_mix",
    )(xbc_c, xbc_c, u, dt, dtT, h0,
      lane_pad(a_log), a_log.reshape(H2, 1), lane_pad(dt_bias), dt_bias.reshape(H2, 1),
      d_skip_e, norm_w.reshape(1, D), ex)


NA_GROUP = 1


def _na_kernel(q_ref, k_ref, v_ref, kc_ref, vc_ref, bias_ref, o_ref, s_ref, p_ref, inv_ref, kt_ref, kto_ref, *, rows):
    W = GRID_W
    band = NA_KH * W
    lane = lax.broadcasted_iota(jnp.int32, (W, 2 * HEAD_DIM), 1)
    G = NA_GROUP
    n_groups = rows // G
    M2 = 2 * W
    kt_ref[...] = k_ref[0].T
    kto_ref[...] = k_ref[0, W:W + kto_ref.shape[1], :].T
    kct = kc_ref[0].T

    def band_start(r):
        return min(max(r - NA_KH // 2, 0), rows - NA_KH)

    def stage_scores(g, slot):
        for i in range(G):
            r = g * G + i
            r0 = band_start(r)
            qs = _head_stack(q_ref[0, r * W:(r + 1) * W, :], lane)
            ktb, t0 = (kt_ref, r0 * W) if r0 % 2 == 0 else (kto_ref, (r0 - 1) * W)
            base = r0 - r + NA_KH - 1
            for jp in range(NA_KH // 2):
                cols = slice(jp * M2, (jp + 1) * M2)
                s_ref[slot, i * M2:(i + 1) * M2, cols] = (_dot(qs, ktb[:, t0 + jp * M2:t0 + (jp + 1) * M2])
                                                          + bias_ref[0, base + 2 * jp])
            s_ref[slot, i * M2:(i + 1) * M2, band:] = _dot(qs, kct)

    def stage_softmax(slot):
        for i in range(G):
            s = s_ref[slot, i * M2:(i + 1) * M2, :]
            p = jnp.exp(s - jnp.max(s, axis=-1, keepdims=True))
            inv = 1.0 / jnp.sum(p, axis=-1, keepdims=True)
            p_ref[slot, i * M2:(i + 1) * M2, :] = p.astype(BF16)
            inv_ref[slot, i * M2:(i + 1) * M2, :] = jnp.broadcast_to(inv, (M2, M2))

    def stage_values(g, slot):
        for i in range(G):
            r = g * G + i
            r0 = band_start(r)
            vb = v_ref[0, r0 * W:r0 * W + band, :]
            p = p_ref[slot, i * M2:(i + 1) * M2, :]
            o2 = (_dot(p[:, :band], vb) + _dot(p[:, band:], vc_ref[0])) * inv_ref[slot, i * M2:(i + 1) * M2, :]
            o = jnp.where(lane < HEAD_DIM, o2[:W], o2[W:])
            o_ref[0, r * W:(r + 1) * W, :] = o.astype(o_ref.dtype)

    stage_scores(0, 0)
    stage_scores(1, 1)
    stage_softmax(0)

    for g in range(n_groups - 2):
        stage_scores(g + 2, g % 2)
        stage_softmax((g + 1) % 2)
        stage_values(g, g % 2)
    stage_softmax((n_groups - 1) % 2)
    stage_values(n_groups - 2, (n_groups - 2) % 2)
    stage_values(n_groups - 1, (n_groups - 1) % 2)


def na_attend(u, u_ctx, bias):
    B, L, _ = u.shape
    Lc = u_ctx.shape[1]
    HP = N_HEADS // 2
    lw = 2 * HEAD_DIM
    oq, ok, ov = U_Q // lw, U_K // lw, U_V // lw
    return pl.pallas_call(
        functools.partial(_na_kernel, rows=L // GRID_W),
        grid=(HP, B),
        in_specs=[pl.BlockSpec((1, L, lw), lambda h, b: (b, 0, oq + h)),
                  pl.BlockSpec((1, L, lw), lambda h, b: (b, 0, ok + h)),
                  pl.BlockSpec((1, L, lw), lambda h, b: (b, 0, ov + h)),
                  pl.BlockSpec((1, Lc, lw), lambda h, b: (b, 0, ok + h)),
                  pl.BlockSpec((1, Lc, lw), lambda h, b: (b, 0, ov + h)),
                  pl.BlockSpec((1, 2 * NA_KH - 2, 2 * GRID_W, 2 * GRID_W), lambda h, b: (h, 0, 0, 0))],
        out_specs=pl.BlockSpec((1, L, lw), lambda h, b: (b, 0, h)),
        out_shape=jax.ShapeDtypeStruct((B, L, D), BF16),
        scratch_shapes=[pltpu.VMEM((2, NA_GROUP * lw, NA_KH * GRID_W + Lc), F32),
                        pltpu.VMEM((2, NA_GROUP * lw, NA_KH * GRID_W + Lc), BF16),
                        pltpu.VMEM((2, NA_GROUP * lw, lw), F32),
                        pltpu.VMEM((lw, L), BF16), pltpu.VMEM((lw, L - 2 * GRID_W), BF16)],
        compiler_params=_cparams(("parallel", "parallel")),
        name="na_attend",
    )(u, u, u, u_ctx, u_ctx, bias)


def _ctxattn_kernel(q_ref, k_ref, v_ref, o_ref):
    Lc = q_ref.shape[1]
    lane = lax.broadcasted_iota(jnp.int32, (Lc, 2 * HEAD_DIM), 1)
    q2 = (q_ref[0].astype(F32) * (HEAD_DIM ** -0.5)).astype(BF16)
    s = _nt(_head_stack(q2, lane), k_ref[0])
    m = jnp.max(s, axis=-1, keepdims=True)
    p = jnp.exp(s - m)
    p = (p * (1.0 / jnp.sum(p, axis=-1, keepdims=True))).astype(BF16)
    o2 = _dot(p, v_ref[0])
    o_ref[0] = jnp.where(lane < HEAD_DIM, o2[:Lc], o2[Lc:]).astype(o_ref.dtype)


def ctx_attend(u_ctx):
    B, Lc, _ = u_ctx.shape
    HP = N_HEADS // 2
    lw = 2 * HEAD_DIM
    oq, ok, ov = U_Q // lw, U_K // lw, U_V // lw
    return pl.pallas_call(
        _ctxattn_kernel,
        grid=(HP, B),
        in_specs=[pl.BlockSpec((1, Lc, lw), lambda h, b: (b, 0, oq + h)),
                  pl.BlockSpec((1, Lc, lw), lambda h, b: (b, 0, ok + h)),
                  pl.BlockSpec((1, Lc, lw), lambda h, b: (b, 0, ov + h))],
        out_specs=pl.BlockSpec((1, Lc, lw), lambda h, b: (b, 0, h)),
        out_shape=jax.ShapeDtypeStruct((B, Lc, D), BF16),
        compiler_params=_cparams(("parallel", "parallel")),
        name="ctx_attend",
    )(u_ctx, u_ctx, u_ctx)


MERGE_CK = 256


def _merge_kernel(cb_ref, cc_ref, cx_ref, ccp_ref, cxp_ref, ccn_ref, cxn_ref, cw_ref, ys_ref, yn_ref, g_ref, h_ref,
                  gt_ref, wc_ref, ws_ref, wn_ref, wo_ref, o_ref, *, seq_len):
    tm = h_ref.shape[1]
    row = lax.broadcasted_iota(jnp.int32, (tm, MERGE_CK), 0)
    pos = (pl.program_id(1) * tm + row) & (seq_len - 1)
    first, last = pos == 0, pos == seq_len - 1
    top, bottom = row == 0, row == tm - 1
    yc_proj = None
    for c in range(0, D, MERGE_CK):
        ch = slice(c, c + MERGE_CK)
        p = cc_ref[0, :, ch].astype(F32) * cx_ref[0, :, ch].astype(F32)
        p_before = (ccp_ref[0, :, ch].astype(F32) * cxp_ref[0, :, ch].astype(F32))[HALO - 1:HALO]
        p_after = (ccn_ref[0, :, ch].astype(F32) * cxn_ref[0, :, ch].astype(F32))[0:1]
        prev = jnp.where(first, 0.0, jnp.where(top, p_before, pltpu.roll(p, 1, axis=0)))
        nxt = jnp.where(last, 0.0, jnp.where(bottom, p_after, pltpu.roll(p, tm - 1, axis=0)))
        w = cw_ref[:, ch]
        yc = (cb_ref[0, :, ch].astype(F32) * (prev * w[0:1] + p * w[1:2] + nxt * w[2:3])).astype(BF16)
        t = _dot(yc, wc_ref[ch, :])
        yc_proj = t if yc_proj is None else yc_proj + t

    g = _sigmoid(g_ref[0].astype(F32))
    m = (g[:, 0:D] * yc_proj
         + g[:, D:2 * D] * _dot(ys_ref[0], ws_ref[...])
         + g[:, 2 * D:3 * D] * _dot(yn_ref[0], wn_ref[...]))
    o_ref[0] = h_ref[0] + gt_ref[0] * _dot(m.astype(BF16), wo_ref[...])


def merge(u, ys, yn, h, gt, cw, wc, ws, wn, wo, tm, seq_len):
    B, L, _ = h.shape
    assert seq_len & (seq_len - 1) == 0 and (tm % seq_len == 0 or seq_len % tm == 0)
    nb, hb = tm // HALO, L // HALO
    tok = lambda: pl.BlockSpec((1, tm, D), lambda b, i: (b, i, 0))
    ucol = lambda c: pl.BlockSpec((1, tm, D), lambda b, i: (b, i, c // D))
    before = lambda c: pl.BlockSpec((1, HALO, D), lambda b, i: (b, jnp.maximum(i * nb - 1, 0), c // D))
    after = lambda c: pl.BlockSpec((1, HALO, D), lambda b, i: (b, jnp.minimum((i + 1) * nb, hb - 1), c // D))
    wsp = lambda: pl.BlockSpec((D, D), lambda b, i: (0, 0))
    return pl.pallas_call(
        functools.partial(_merge_kernel, seq_len=seq_len),
        grid=(B, L // tm),
        in_specs=[ucol(U_CB), ucol(U_CC), ucol(U_CX), before(U_CC), before(U_CX), after(U_CC), after(U_CX),
                  pl.BlockSpec((3, D), lambda b, i: (0, 0)),
                  tok(), tok(),
                  pl.BlockSpec((1, tm, 3 * D), lambda b, i: (b, i, U_GATE // (3 * D))),
                  tok(),
                  pl.BlockSpec((1, 1, D), lambda b, i: (b, 0, 0)),
                  wsp(), wsp(), wsp(), wsp()],
        out_specs=tok(),
        out_shape=jax.ShapeDtypeStruct((B, L, D), F32),
        compiler_params=_cparams(("parallel", "parallel")),
        name="merge",
    )(u, u, u, u, u, u, u, cw, ys, yn, u, h, gt, wc, ws, wn, wo)


MLP_TF = 1024


def _mlp_kernel(h_ref, sh_ref, sc_ref, gt_ref, nw_ref, fw_ref, w1_ref, w2_ref, o_ref, *, final):
    h = h_ref[0]
    xn = _norm_mod(h, nw_ref[...], sc_ref[0], sh_ref[0]).astype(BF16)
    acc = None
    for k in range(0, D_FF, MLP_TF):
        a = jnp.square(jnp.maximum(_dot(xn, w1_ref[:, k:k + MLP_TF]), 0.0)).astype(BF16)
        t = _dot(a, w2_ref[k:k + MLP_TF, :])
        acc = t if acc is None else acc + t
    o = h + gt_ref[0] * acc
    if final:
        ms = jnp.mean(o * o, axis=-1, keepdims=True)
        o = o * lax.rsqrt(ms + EPS) * fw_ref[...]
    o_ref[0] = o


def mlp(h, sh, sc, gt, nw, fw, w1, w2, tm, final):
    B, L, _ = h.shape
    vec = lambda: pl.BlockSpec((1, 1, D), lambda b, i: (b, 0, 0))
    par = lambda: pl.BlockSpec((1, D), lambda b, i: (0, 0))
    resident = lambda shape: pl.BlockSpec(shape, lambda b, i: (0, 0), pipeline_mode=pl.Buffered(1))
    return pl.pallas_call(
        functools.partial(_mlp_kernel, final=final),
        grid=(B, L // tm),
        in_specs=[pl.BlockSpec((1, tm, D), lambda b, i: (b, i, 0)),
                  vec(), vec(), vec(), par(), par(),
                  resident((D, D_FF)), resident((D_FF, D))],
        out_specs=pl.BlockSpec((1, tm, D), lambda b, i: (b, i, 0)),
        out_shape=jax.ShapeDtypeStruct((B, L, D), F32),
        compiler_params=_cparams(("parallel", "parallel")),
        name="mlp",
    )(h, sh, sc, gt, nw, fw, w1, w2)


def _rope_tables(L):
    t = jnp.arange(L, dtype=jnp.int32)
    row = (t // GRID_W).astype(F32)
    col = (t % GRID_W).astype(F32)
    half = HEAD_DIM // 2
    inv = ROPE_BASE ** (-jnp.arange(0, half, 2, dtype=F32) / half)
    ang_r = row[:, None] * inv
    ang_c = col[:, None] * inv
    ang = jnp.concatenate([ang_r, ang_r, ang_c, ang_c], axis=-1)
    cos = jnp.tile(jnp.cos(ang), (1, 2))
    sin = jnp.tile(jnp.sin(ang), (1, 2))
    even = ((jnp.arange(2 * HEAD_DIM) // (half // 2)) % 2 == 0)[None, :]
    return cos, jnp.where(even, -sin, 0.0), jnp.where(even, 0.0, sin)


def _rpb_kernel(r_ref, oh_ref, ok_ref, o_ref):
    val = _dot(jnp.concatenate(_split3(r_ref[...]), axis=1), oh_ref[...])
    o_ref[...] = jnp.where(ok_ref[...] > 0.0, val, -jnp.inf)


def _na_bias_table(rpb):
    H, NR, NC = rpb.shape
    W = GRID_W
    col = np.arange(W)
    col_start = np.clip(col - NA_KW // 2, 0, W - NA_KW)
    col_ok = (col[None, :] >= col_start[:, None]) & (col[None, :] < col_start[:, None] + NA_KW)
    dc_idx = np.clip(col[None, :] - col[:, None], -(NA_KW - 1), NA_KW - 1) + NA_KW - 1
    rows = jnp.transpose(rpb.reshape(H // 2, 2, NR, NC), (0, 2, 1, 3)).reshape(H * NR, NC)
    rows = jnp.pad(rows, ((0, 0), (0, DT_PAD - NC)))
    krow = jnp.arange(3 * DT_PAD, dtype=jnp.int32)[:, None] % DT_PAD
    onehot = (krow == jnp.asarray(dc_idx.reshape(1, W * W), jnp.int32)).astype(BF16)
    ok = jnp.asarray(col_ok.reshape(1, W * W), F32)
    tn = 1024
    tab = pl.pallas_call(
        _rpb_kernel,
        grid=(W * W // tn,),
        in_specs=[pl.BlockSpec((H * NR, DT_PAD), lambda j: (0, 0)),
                  pl.BlockSpec((3 * DT_PAD, tn), lambda j: (0, j)),
                  pl.BlockSpec((1, tn), lambda j: (0, j))],
        out_specs=pl.BlockSpec((H * NR, tn), lambda j: (0, j)),
        out_shape=jax.ShapeDtypeStruct((H * NR, W * W), F32),
        compiler_params=_cparams(("arbitrary",)),
        name="rpb_table",
    )(rows, onehot, ok)
    tab = tab.reshape(H // 2, NR, 2 * W, W)
    return jnp.concatenate([tab[:, :NR - 1], tab[:, 1:]], axis=-1)


def _prep_w_in(w_in):
    order = [(R_Q, D), (R_K, D), (R_V, D), (R_GATE, 3 * D), (R_CB, D), (R_CC, D), (R_CX, D), (R_Z, D), (R_XBC, XBC)]
    w = jnp.concatenate([w_in[:, o:o + n] for o, n in order], axis=1).astype(BF16)
    wdt = w_in[:, R_DT:R_DT + 2 * N_HEADS].astype(BF16)
    return w, jnp.pad(wdt, ((0, 0), (0, DT_PAD - 2 * N_HEADS))), wdt.T


def kernel(x, c, ctx, c_ctx, w_ada, b_ada, norm1_w, w_in, conv_mix_w, ssd_conv_w, ssd_conv_b, ssd_a_log, ssd_dt_bias,
           ssd_d, ssd_norm_w, na_rpb, w_br_conv, w_br_ssd, w_br_na, w_out, norm2_w, w_ff1, w_ff2, final_norm_w):
    B, L, _ = x.shape
    Lc = ctx.shape[1]
    depth = w_in.shape[0]
    cos, sa, sb = _rope_tables(L)
    n_mod = B + 1
    pad = (-n_mod) % 8
    c_rows = jnp.concatenate([c, c_ctx[None, :], jnp.zeros((pad, D), F32)], axis=0)
    zero_state = jnp.zeros((B, 2, SSD_GROUPS, SSD_STATE, D // SSD_GROUPS), F32)
    fw = final_norm_w.reshape(1, D)
    Tc = B * Lc
    tmc = min(1024, Tc)
    flat = lambda a: a.reshape(1, Tc, a.shape[-1])
    h, hc = x, flat(ctx)
    for l in range(depth):
        last = l == depth - 1
        mod = ada_mod(c_rows, w_ada[l], b_ada[l])
        m_lat = mod[:B].reshape(B, 1, 6, D)
        m_ctx = mod[B:B + 1].reshape(1, 1, 6, D)
        sh1, sc1, gt1, sh2, sc2, gt2 = (m_lat[:, :, i] for i in range(6))
        csh1, csc1, cgt1, csh2, csc2, cgt2 = (m_ctx[:, :, i] for i in range(6))
        w, wdt, wdtT = _prep_w_in(w_in[l])
        nw1 = norm1_w[l].reshape(1, D)
        u, dt, dtT = in_proj(h, sh1, sc1, nw1, w, wdt, wdtT, tm=1024, rope_tables=(cos, sa, sb))
        uc, dtc, dtTc = in_proj(hc, csh1, csc1, nw1, w, wdt, wdtT, tm=tmc)
        uc = uc.reshape(B, Lc, U_COLS)
        dtc = dtc.reshape(B, Lc, DT_PAD)
        dtTc = jnp.transpose(dtTc.reshape(2 * N_HEADS, B, Lc), (1, 0, 2))
        d_skip_e = jnp.repeat(ssd_d[l], HEAD_DIM, axis=1)
        ssd_p = (ssd_a_log[l], ssd_dt_bias[l], d_skip_e, ssd_norm_w[l])
        xbc_c = xbc_conv(uc, ssd_conv_w[l], ssd_conv_b[l])
        y_ssd_c, ctx_states = ssd_mix(xbc_c, uc, dtc, dtTc, zero_state, *ssd_p)
        xbc_l = xbc_conv(u, ssd_conv_w[l], ssd_conv_b[l])
        y_ssd, _ = ssd_mix(xbc_l, u, dt, dtT, ctx_states, *ssd_p)
        y_na = na_attend(u, uc, _na_bias_table(na_rpb[l]))
        wb = [t[l].astype(BF16) for t in (w_br_conv, w_br_ssd, w_br_na, w_out)]
        w1, w2 = w_ff1[l].astype(BF16), w_ff2[l].astype(BF16)
        nw2 = norm2_w[l].reshape(1, D)
        h = merge(u, y_ssd, y_na, h, gt1, conv_mix_w[l], *wb, tm=512, seq_len=L)
        h = mlp(h, sh2, sc2, gt2, nw2, fw, w1, w2, tm=1024, final=last)
        if not last:
            y_na_c = ctx_attend(uc)
            hc = merge(flat(uc), flat(y_ssd_c), flat(y_na_c), hc, cgt1, conv_mix_w[l], *wb, tm=min(512, Tc), seq_len=Lc)
            hc = mlp(hc, csh2, csc2, cgt2, nw2, fw, w1, w2, tm=tmc, final=False)
    return h
```

```python
import functools
import math

import jax
import jax.numpy as jnp
import numpy as np
from jax import lax
from jax.experimental import pallas as pl
from jax.experimental.pallas import tpu as pltpu

F32 = jnp.float32
BF16 = jnp.bfloat16
HIGHEST = lax.Precision.HIGHEST
LOG2E = math.log2(math.e)

D = 1024
EPS = 1e-6
GRID_W = 64
N_HEADS = 16
HEAD_DIM = 64
SSD_GROUPS = 2
SSD_STATE = 128
CHUNK = 128
SSD_SUB = 4
NA_KH = 8
NA_KW = 16
ROPE_BASE = 10000.0
D_FF = 4 * D
XBC = D + 2 * SSD_GROUPS * SSD_STATE
U_Q, U_K, U_V, U_GATE, U_CB, U_CC, U_CX, U_Z, U_XS, U_BC, U_COLS = (
    0, 1024, 2048, 3072, 6144, 7168, 8192, 9216, 10240, 11264, 11776)
U_TILE = 2944
U_SUB = 512
R_CB, R_CC, R_CX, R_Z, R_XBC, R_DT, R_Q, R_K, R_V, R_GATE = (
    0, 1024, 2048, 3072, 4096, 5632, 5664, 6688, 7712, 8736)
DT_PAD = 128
HALO = 16
VMEM_LIMIT = 56 * 1024 * 1024


def _cparams(sem):
    return pltpu.CompilerParams(dimension_semantics=sem, vmem_limit_bytes=VMEM_LIMIT)


def _nt(a, b):
    return lax.dot_general(a, b, (((1,), (1,)), ((), ())), preferred_element_type=F32)


def _tn(a, b):
    return lax.dot_general(a, b, (((0,), (0,)), ((), ())), preferred_element_type=F32)


def _dot(a, b):
    return jnp.dot(a, b, preferred_element_type=F32)


def _dot_hi(a, b):
    return jnp.dot(a, b, preferred_element_type=F32, precision=HIGHEST)


def _sigmoid(x):
    return 0.5 * jnp.tanh(0.5 * x) + 0.5


def _silu(x):
    return x * _sigmoid(x)


def _norm_mod(x, nw, sc, sh):
    ms = jnp.mean(x * x, axis=-1, keepdims=True)
    y = x * lax.rsqrt(ms + EPS) * nw
    return y * (1.0 + sc) + sh


def _ada_kernel(c_ref, w_ref, b_ref, o_ref):
    o_ref[...] = _dot_hi(_silu(c_ref[...]), w_ref[...]) + b_ref[...]


def ada_mod(c_rows, w_ada, b_ada):
    rows = c_rows.shape[0]
    tn = 1536
    return pl.pallas_call(
        _ada_kernel,
        grid=(6 * D // tn,),
        in_specs=[pl.BlockSpec((rows, D), lambda j: (0, 0)),
                  pl.BlockSpec((D, tn), lambda j: (0, j)),
                  pl.BlockSpec((1, tn), lambda j: (0, j))],
        out_specs=pl.BlockSpec((rows, tn), lambda j: (0, j)),
        out_shape=jax.ShapeDtypeStruct((rows, 6 * D), F32),
        compiler_params=_cparams(("arbitrary",)),
        name="ada_mod",
    )(c_rows, w_ada, b_ada.reshape(1, 6 * D))


def _rope_slab(x, cos, sa, sb):
    return x * cos + pltpu.roll(x, 2 * HEAD_DIM - 16, axis=1) * sa + pltpu.roll(x, 16, axis=1) * sb


def _inproj_kernel(x_ref, sh_ref, sc_ref, nw_ref, w_ref, wdt_ref, wdtT_ref, *rest, rope):
    if rope:
        cos_ref, sa_ref, sb_ref, u_ref, dt_ref, dtT_ref, xn_ref = rest
    else:
        u_ref, dt_ref, dtT_ref, xn_ref = rest
    j, i = pl.program_id(1), pl.program_id(2)

    def normalise():
        xb = _norm_mod(x_ref[0], nw_ref[...], sc_ref[0], sh_ref[0]).astype(BF16)
        xn_ref[i] = xb
        dt_ref[0] = _dot(xb, wdt_ref[...])
        dtT_ref[0] = _nt(wdtT_ref[...], xb)
        return xb

    def tile(rope_cols, xb=None):
        for off in range(0, U_TILE, U_SUB):
            wd = min(U_SUB, U_TILE - off)
            r = _dot(xn_ref[i] if xb is None else xb, w_ref[:, off:off + wd])
            if off < rope_cols:
                scale = LOG2E * HEAD_DIM ** -0.5 if off < U_K else 1.0
                lw = 2 * HEAD_DIM
                r = jnp.concatenate(
                    [_rope_slab(r[:, s:s + lw], cos_ref[...], sa_ref[...], sb_ref[...]) * scale
                     for s in range(0, wd, lw)], axis=1)
            u_ref[0, :, off:off + wd] = r.astype(u_ref.dtype)

    @pl.when(j == 0)
    def _():
        tile(U_V if rope else 0, normalise())

    @pl.when(j > 0)
    def _():
        tile(0)


def in_proj(x, sh, sc, nw, w, wdt, wdtT, tm, rope_tables=None):
    B, L, _ = x.shape
    tn = U_TILE
    ni = L // tm
    grid = (B, U_COLS // tn, ni)
    rope = rope_tables is not None

    def first(j, i):
        return jnp.where(j == 0, i, ni - 1)

    rope_specs = [pl.BlockSpec((tm, 2 * HEAD_DIM), lambda b, j, i: (first(j, i), 0))] * 3 if rope else []
    return pl.pallas_call(
        functools.partial(_inproj_kernel, rope=rope),
        grid=grid,
        in_specs=[pl.BlockSpec((1, tm, D), lambda b, j, i: (b, first(j, i), 0)),
                  pl.BlockSpec((1, 1, D), lambda b, j, i: (b, 0, 0)),
                  pl.BlockSpec((1, 1, D), lambda b, j, i: (b, 0, 0)),
                  pl.BlockSpec((1, D), lambda b, j, i: (0, 0)),
                  pl.BlockSpec((D, tn), lambda b, j, i: (0, j)),
                  pl.BlockSpec((D, DT_PAD), lambda b, j, i: (0, 0)),
                  pl.BlockSpec((2 * N_HEADS, D), lambda b, j, i: (0, 0))] + rope_specs,
        out_specs=[pl.BlockSpec((1, tm, tn), lambda b, j, i: (b, i, j)),
                   pl.BlockSpec((1, tm, DT_PAD), lambda b, j, i: (b, first(j, i), 0)),
                   pl.BlockSpec((1, 2 * N_HEADS, tm), lambda b, j, i: (b, 0, first(j, i)))],
        out_shape=[jax.ShapeDtypeStruct((B, L, U_COLS), BF16),
                   jax.ShapeDtypeStruct((B, L, DT_PAD), F32),
                   jax.ShapeDtypeStruct((B, 2 * N_HEADS, L), F32)],
        scratch_shapes=[pltpu.VMEM((ni, tm, D), BF16)],
        compiler_params=_cparams(("parallel", "arbitrary", "arbitrary")),
        name="in_proj",
    )(x, sh, sc, nw, w, wdt, wdtT, *(rope_tables if rope else ()))


def _dwconv3(p, w):
    L = p.shape[0]
    row = lax.broadcasted_iota(jnp.int32, p.shape, 0)
    prev = jnp.where(row == 0, 0.0, pltpu.roll(p, 1, axis=0))
    nxt = jnp.where(row == L - 1, 0.0, pltpu.roll(p, L - 1, axis=0))
    return prev * w[0:1] + p * w[1:2] + nxt * w[2:3]


def _xbcconv_kernel(x_ref, w_ref, b_ref, o_ref):
    L = x_ref.shape[1]
    E = 24
    w = w_ref[...]
    b = b_ref[...]

    def act(h):
        return (h + h * jnp.tanh(h)).astype(o_ref.dtype)

    x = x_ref[0].astype(F32)
    o_ref[0] = act(pltpu.roll(x, 1, axis=0) * w[0:1] + x * w[1:2] + pltpu.roll(x, L - 1, axis=0) * w[2:3] + b)
    o_ref[0, 0:HALO, :] = act(_dwconv3(x[0:E], w) + b)[0:HALO]
    o_ref[0, L - HALO:L, :] = act(_dwconv3(x[L - E:L], w) + b)[E - HALO:E]


def xbc_conv(u, w, bias, tc=256):
    B, L, _ = u.shape
    off = U_XS // tc
    return pl.pallas_call(
        _xbcconv_kernel,
        grid=(B, XBC // tc),
        in_specs=[pl.BlockSpec((1, L, tc), lambda b, j: (b, 0, off + j)),
                  pl.BlockSpec((3, tc), lambda b, j: (0, j)),
                  pl.BlockSpec((1, tc), lambda b, j: (0, j))],
        out_specs=pl.BlockSpec((1, L, tc), lambda b, j: (b, 0, j)),
        out_shape=jax.ShapeDtypeStruct((B, L, XBC), BF16),
        compiler_params=_cparams(("parallel", "parallel")),
        name="xbc_conv",
    )(u, 0.5 * w, 0.5 * bias.reshape(1, XBC))


def _split3(x):
    hi = x.astype(BF16)
    r1 = x - hi.astype(F32)
    mid = r1.astype(BF16)
    lo = (r1 - mid.astype(F32)).astype(BF16)
    return hi, mid, lo


def _head_stack(x2, lane):
    zero = jnp.zeros_like(x2)
    return jnp.concatenate([jnp.where(lane < HEAD_DIM, x2, zero), jnp.where(lane >= HEAD_DIM, x2, zero)], axis=0)


def _ssd_chunk(d, sub, rows, xs_ref, bc_ref, z_ref, dt_ref, dtT_ref, alr_ref, alc_ref, dbr_ref, dbc_ref, dsk_ref,
               nw_ref, ex_ref, y_ref, yacc_ref, st_ref):
    H = N_HEADS
    blk = slice(sub * CHUNK, (sub + 1) * CHUNK)
    xsb = xs_ref[0, blk, :]
    xs = xsb.astype(F32)
    bc = bc_ref[0, blk, :]
    dt_c = jax.nn.softplus(dt_ref[0, blk, :] + dbr_ref[...])
    a_c = dt_c * (-LOG2E * jnp.exp(alr_ref[...]))
    dtT_c = jax.nn.softplus(dtT_ref[0, :, blk] + dbc_ref[...])
    aT_c = dtT_c * (-LOG2E * jnp.exp(alc_ref[...]))

    ri = lax.broadcasted_iota(jnp.int32, (CHUNK, CHUNK), 0)
    ci = lax.broadcasted_iota(jnp.int32, (CHUNK, CHUNK), 1)
    tri = (ri >= ci) if d == 0 else (ri <= ci)
    trib = tri.astype(BF16)
    tribT = ((ri <= ci) if d == 0 else (ri >= ci)).astype(BF16)
    cs = _dot(jnp.concatenate([trib] * 3, axis=1), jnp.concatenate(_split3(a_c), axis=0))
    csT = _dot(jnp.concatenate(_split3(aT_c), axis=1), jnp.concatenate([tribT] * 3, axis=0))
    last = cs[CHUNK - 1:CHUNK] if d == 0 else cs[0:1]
    w_c = dt_c * jnp.exp2(last - cs)
    ein_c = jnp.exp2(cs)
    ex = ex_ref[d]
    w_e = _dot(jnp.concatenate(_split3(w_c)[:2], axis=1), ex)
    ein_e = _dot(jnp.concatenate(_split3(ein_c)[:2], axis=1), ex)
    cd_e = ein_e[CHUNK - 1:CHUNK] if d == 0 else ein_e[0:1]
    Xdec = (xs * w_e).astype(BF16)
    rowT = csT - jnp.log2(dtT_c)
    lane = lax.broadcasted_iota(jnp.int32, (CHUNK, 2 * HEAD_DIM), 1)

    ys = []
    GW = D // SSD_GROUPS
    for g in range(SSD_GROUPS):
        Bg = bc[:, g * SSD_STATE:(g + 1) * SSD_STATE]
        Cg = bc[:, (SSD_GROUPS + g) * SSD_STATE:(SSD_GROUPS + g + 1) * SSD_STATE]
        S = _nt(Cg, Bg)
        st = st_ref[d, g]
        y_off = _dot(Cg, st.astype(BF16)) * ein_e[:, g * GW:(g + 1) * GW]
        y_diag = []
        for p in range(GW // (2 * HEAD_DIM)):
            hA = g * (H // SSD_GROUPS) + 2 * p
            Ms = []
            for h in (hA, hA + 1):
                col = cs[:, d * H + h:d * H + h + 1]
                rowv = rowT[d * H + h:d * H + h + 1, :]
                Ms.append((S * jnp.exp2(jnp.where(tri, col - rowv, -jnp.inf))).astype(BF16))
            Mcat = jnp.concatenate(Ms, axis=1)
            Xp = xsb[:, hA * HEAD_DIM:(hA + 2) * HEAD_DIM]
            y_diag.append(_dot(Mcat, _head_stack(Xp, lane)))
        ys.append(y_off + jnp.concatenate(y_diag, axis=1))
        st_ref[d, g] = st * cd_e[:, g * GW:(g + 1) * GW] + _tn(Bg, Xdec[:, g * GW:(g + 1) * GW])
    y = jnp.concatenate(ys, axis=1)

    if d == 0:
        yacc_ref[rows, :] = y
    else:
        ysum = yacc_ref[rows, :] + y + xs * (dsk_ref[0:1] + dsk_ref[1:2])
        yg = ysum * _silu(z_ref[0, blk, :].astype(F32))
        parts = []
        for g in range(SSD_GROUPS):
            v = yg[:, g * GW:(g + 1) * GW]
            ms = jnp.mean(v * v, axis=-1, keepdims=True)
            parts.append(v * lax.rsqrt(ms + EPS) * nw_ref[:, g * GW:(g + 1) * GW])
        y_ref[0, blk, :] = jnp.concatenate(parts, axis=1).astype(y_ref.dtype)


def _ssd_kernel(xs_ref, bc_ref, z_ref, dt_ref, dtT_ref, h0_ref, alr_ref, alc_ref, dbr_ref, dbc_ref, dsk_ref, nw_ref,
                ex_ref, y_ref, sto_ref, yacc_ref, st_ref, *, ns, nsub):
    t = pl.program_id(1)

    @pl.when(t == 0)
    def _():
        st_ref[...] = h0_ref[0]

    args = (xs_ref, bc_ref, z_ref, dt_ref, dtT_ref, alr_ref, alc_ref, dbr_ref, dbc_ref, dsk_ref, nw_ref,
            ex_ref, y_ref, yacc_ref, st_ref)

    def seq_rows(step, sub):
        return pl.ds(pl.multiple_of((step * nsub + sub) * CHUNK, CHUNK), CHUNK)

    @pl.when(t < ns)
    def _():
        for sub in range(nsub):
            _ssd_chunk(0, sub, seq_rows(t, sub), *args)

    @pl.when(t >= ns)
    def _():
        for sub in reversed(range(nsub)):
            _ssd_chunk(1, sub, seq_rows(2 * ns - 1 - t, sub), *args)

    @pl.when(t == 2 * ns - 1)
    def _():
        sto_ref[0] = st_ref[...]


def ssd_mix(xbc_c, u, dt, dtT, h0, a_log, dt_bias, d_skip_e, norm_w):
    B, L, _ = u.shape
    nsub = min(SSD_SUB, L // CHUNK)
    RB = nsub * CHUNK
    ns = L // RB
    H2 = 2 * N_HEADS

    def chunk(t):
        return jnp.where(t < ns, t, 2 * ns - 1 - t)

    def late(t):
        return jnp.where(t < ns, ns - 1, 2 * ns - 1 - t)

    st_shape = (2, SSD_GROUPS, SSD_STATE, D // SSD_GROUPS)
    st_spec = pl.BlockSpec((1,) + st_shape, lambda b, t: (b, 0, 0, 0, 0))
    small = lambda shape: pl.BlockSpec(shape, lambda b, t: (0,) * len(shape))
    krow = jnp.arange(2 * DT_PAD, dtype=jnp.int32)[:, None] % DT_PAD
    head = jnp.arange(D, dtype=jnp.int32)[None, :] // HEAD_DIM
    ex = jnp.stack([krow == d * N_HEADS + head for d in range(2)]).astype(BF16)
    lane_pad = lambda v: jnp.pad(v.reshape(1, H2), ((0, 0), (0, DT_PAD - H2)))
    return pl.pallas_call(
        functools.partial(_ssd_kernel, ns=ns, nsub=nsub),
        grid=(B, 2 * ns),
        in_specs=[pl.BlockSpec((1, RB, D), lambda b, t: (b, chunk(t), 0)),
                  pl.BlockSpec((1, RB, XBC - D), lambda b, t: (b, chunk(t), D // (XBC - D))),
                  pl.BlockSpec((1, RB, D), lambda b, t: (b, late(t), U_Z // D)),
                  pl.BlockSpec((1, RB, DT_PAD), lambda b, t: (b, chunk(t), 0)),
                  pl.BlockSpec((1, H2, RB), lambda b, t: (b, 0, chunk(t))),
                  st_spec,
                  small((1, DT_PAD)), small((H2, 1)), small((1, DT_PAD)), small((H2, 1)),
                  small((2, D)), small((1, D)), small((2, 2 * DT_PAD, D))],
        out_specs=[pl.BlockSpec((1, RB, D), lambda b, t: (b, late(t), 0)),
                   st_spec],
        out_shape=[jax.ShapeDtypeStruct((B, L, D), BF16),
                   jax.ShapeDtypeStruct((B,) + st_shape, F32)],
        scratch_shapes=[pltpu.VMEM((L, D), F32), pltpu.VMEM(st_shape, F32)],
        compiler_params=_cparams(("parallel", "arbitrary")),
        name="ssd_mix",
    )(xbc_c, xbc_c, u, dt, dtT, h0,
      lane_pad(a_log), a_log.reshape(H2, 1), lane_pad(dt_bias), dt_bias.reshape(H2, 1),
      d_skip_e, norm_w.reshape(1, D), ex)


NA_GROUP = 1


def _na_kernel(q_ref, k_ref, v_ref, kc_ref, vc_ref, bias_ref, o_ref, s_ref, p_ref, inv_ref, kt_ref, kto_ref, *, rows):
    W = GRID_W
    band = NA_KH * W
    lane = lax.broadcasted_iota(jnp.int32, (W, 2 * HEAD_DIM), 1)
    G = NA_GROUP
    n_groups = rows // G
    M2 = 2 * W
    kt_ref[...] = k_ref[0].T
    kto_ref[...] = k_ref[0, W:W + kto_ref.shape[1], :].T
    kct = kc_ref[0].T

    def band_start(r):
        return min(max(r - NA_KH // 2, 0), rows - NA_KH)

    def stage_scores(g, slot):
        for i in range(G):
            r = g * G + i
            r0 = band_start(r)
            qs = _head_stack(q_ref[0, r * W:(r + 1) * W, :], lane)
            ktb, t0 = (kt_ref, r0 * W) if r0 % 2 == 0 else (kto_ref, (r0 - 1) * W)
            base = r0 - r + NA_KH - 1
            for jp in range(NA_KH // 2):
                cols = slice(jp * M2, (jp + 1) * M2)
                s_ref[slot, i * M2:(i + 1) * M2, cols] = (_dot(qs, ktb[:, t0 + jp * M2:t0 + (jp + 1) * M2])
                                                          + bias_ref[0, base + 2 * jp])
            s_ref[slot, i * M2:(i + 1) * M2, band:] = _dot(qs, kct)

    def stage_softmax(slot):
        for i in range(G):
            s = s_ref[slot, i * M2:(i + 1) * M2, :]
            p = jnp.exp2(s - jnp.max(s, axis=-1, keepdims=True))
            inv = 1.0 / jnp.sum(p, axis=-1, keepdims=True)
            p_ref[slot, i * M2:(i + 1) * M2, :] = p.astype(BF16)
            inv_ref[slot, i * M2:(i + 1) * M2, :] = jnp.broadcast_to(inv, (M2, M2))

    def stage_values(g, slot):
        for i in range(G):
            r = g * G + i
            r0 = band_start(r)
            vb = v_ref[0, r0 * W:r0 * W + band, :]
            p = p_ref[slot, i * M2:(i + 1) * M2, :]
            o2 = (_dot(p[:, :band], vb) + _dot(p[:, band:], vc_ref[0])) * inv_ref[slot, i * M2:(i + 1) * M2, :]
            o = jnp.where(lane < HEAD_DIM, o2[:W], o2[W:])
            o_ref[0, r * W:(r + 1) * W, :] = o.astype(o_ref.dtype)

    stage_scores(0, 0)
    stage_scores(1, 1)
    stage_softmax(0)

    for g in range(n_groups - 2):
        stage_scores(g + 2, g % 2)
        stage_softmax((g + 1) % 2)
        stage_values(g, g % 2)
    stage_softmax((n_groups - 1) % 2)
    stage_values(n_groups - 2, (n_groups - 2) % 2)
    stage_values(n_groups - 1, (n_groups - 1) % 2)


def na_attend(u, u_ctx, bias):
    B, L, _ = u.shape
    Lc = u_ctx.shape[1]
    HP = N_HEADS // 2
    lw = 2 * HEAD_DIM
    oq, ok, ov = U_Q // lw, U_K // lw, U_V // lw
    return pl.pallas_call(
        functools.partial(_na_kernel, rows=L // GRID_W),
        grid=(HP, B),
        in_specs=[pl.BlockSpec((1, L, lw), lambda h, b: (b, 0, oq + h)),
                  pl.BlockSpec((1, L, lw), lambda h, b: (b, 0, ok + h)),
                  pl.BlockSpec((1, L, lw), lambda h, b: (b, 0, ov + h)),
                  pl.BlockSpec((1, Lc, lw), lambda h, b: (b, 0, ok + h)),
                  pl.BlockSpec((1, Lc, lw), lambda h, b: (b, 0, ov + h)),
                  pl.BlockSpec((1, 2 * NA_KH - 2, 2 * GRID_W, 2 * GRID_W), lambda h, b: (h, 0, 0, 0))],
        out_specs=pl.BlockSpec((1, L, lw), lambda h, b: (b, 0, h)),
        out_shape=jax.ShapeDtypeStruct((B, L, D), BF16),
        scratch_shapes=[pltpu.VMEM((2, NA_GROUP * lw, NA_KH * GRID_W + Lc), F32),
                        pltpu.VMEM((2, NA_GROUP * lw, NA_KH * GRID_W + Lc), BF16),
                        pltpu.VMEM((2, NA_GROUP * lw, lw), F32),
                        pltpu.VMEM((lw, L), BF16), pltpu.VMEM((lw, L - 2 * GRID_W), BF16)],
        compiler_params=_cparams(("parallel", "parallel")),
        name="na_attend",
    )(u, u, u, u_ctx, u_ctx, bias)


def _ctxattn_kernel(q_ref, k_ref, v_ref, o_ref):
    Lc = q_ref.shape[1]
    lane = lax.broadcasted_iota(jnp.int32, (Lc, 2 * HEAD_DIM), 1)
    q2 = (q_ref[0].astype(F32) * (HEAD_DIM ** -0.5)).astype(BF16)
    s = _nt(_head_stack(q2, lane), k_ref[0])
    m = jnp.max(s, axis=-1, keepdims=True)
    p = jnp.exp(s - m)
    p = (p * (1.0 / jnp.sum(p, axis=-1, keepdims=True))).astype(BF16)
    o2 = _dot(p, v_ref[0])
    o_ref[0] = jnp.where(lane < HEAD_DIM, o2[:Lc], o2[Lc:]).astype(o_ref.dtype)


def ctx_attend(u_ctx):
    B, Lc, _ = u_ctx.shape
    HP = N_HEADS // 2
    lw = 2 * HEAD_DIM
    oq, ok, ov = U_Q // lw, U_K // lw, U_V // lw
    return pl.pallas_call(
        _ctxattn_kernel,
        grid=(HP, B),
        in_specs=[pl.BlockSpec((1, Lc, lw), lambda h, b: (b, 0, oq + h)),
                  pl.BlockSpec((1, Lc, lw), lambda h, b: (b, 0, ok + h)),
                  pl.BlockSpec((1, Lc, lw), lambda h, b: (b, 0, ov + h))],
        out_specs=pl.BlockSpec((1, Lc, lw), lambda h, b: (b, 0, h)),
        out_shape=jax.ShapeDtypeStruct((B, Lc, D), BF16),
        compiler_params=_cparams(("parallel", "parallel")),
        name="ctx_attend",
    )(u_ctx, u_ctx, u_ctx)


MERGE_CK = 256


def _merge_kernel(cb_ref, cc_ref, cx_ref, ccp_ref, cxp_ref, ccn_ref, cxn_ref, cw_ref, ys_ref, yn_ref, g_ref, h_ref,
                  gt_ref, wc_ref, ws_ref, wn_ref, wo_ref, o_ref, *, seq_len):
    tm = h_ref.shape[1]
    row = lax.broadcasted_iota(jnp.int32, (tm, MERGE_CK), 0)
    pos = (pl.program_id(1) * tm + row) & (seq_len - 1)
    first, last = pos == 0, pos == seq_len - 1
    top, bottom = row == 0, row == tm - 1
    yc_proj = None
    for c in range(0, D, MERGE_CK):
        ch = slice(c, c + MERGE_CK)
        p = cc_ref[0, :, ch].astype(F32) * cx_ref[0, :, ch].astype(F32)
        p_before = (ccp_ref[0, :, ch].astype(F32) * cxp_ref[0, :, ch].astype(F32))[HALO - 1:HALO]
        p_after = (ccn_ref[0, :, ch].astype(F32) * cxn_ref[0, :, ch].astype(F32))[0:1]
        prev = jnp.where(first, 0.0, jnp.where(top, p_before, pltpu.roll(p, 1, axis=0)))
        nxt = jnp.where(last, 0.0, jnp.where(bottom, p_after, pltpu.roll(p, tm - 1, axis=0)))
        w = cw_ref[:, ch]
        yc = (cb_ref[0, :, ch].astype(F32) * (prev * w[0:1] + p * w[1:2] + nxt * w[2:3])).astype(BF16)
        t = _dot(yc, wc_ref[ch, :])
        yc_proj = t if yc_proj is None else yc_proj + t

    g = _sigmoid(g_ref[0].astype(F32))
    m = (g[:, 0:D] * yc_proj
         + g[:, D:2 * D] * _dot(ys_ref[0], ws_ref[...])
         + g[:, 2 * D:3 * D] * _dot(yn_ref[0], wn_ref[...]))
    o_ref[0] = h_ref[0] + gt_ref[0] * _dot(m.astype(BF16), wo_ref[...])


def merge(u, ys, yn, h, gt, cw, wc, ws, wn, wo, tm, seq_len):
    B, L, _ = h.shape
    assert seq_len & (seq_len - 1) == 0 and (tm % seq_len == 0 or seq_len % tm == 0)
    nb, hb = tm // HALO, L // HALO
    tok = lambda: pl.BlockSpec((1, tm, D), lambda b, i: (b, i, 0))
    ucol = lambda c: pl.BlockSpec((1, tm, D), lambda b, i: (b, i, c // D))
    before = lambda c: pl.BlockSpec((1, HALO, D), lambda b, i: (b, jnp.maximum(i * nb - 1, 0), c // D))
    after = lambda c: pl.BlockSpec((1, HALO, D), lambda b, i: (b, jnp.minimum((i + 1) * nb, hb - 1), c // D))
    wsp = lambda: pl.BlockSpec((D, D), lambda b, i: (0, 0))
    return pl.pallas_call(
        functools.partial(_merge_kernel, seq_len=seq_len),
        grid=(B, L // tm),
        in_specs=[ucol(U_CB), ucol(U_CC), ucol(U_CX), before(U_CC), before(U_CX), after(U_CC), after(U_CX),
                  pl.BlockSpec((3, D), lambda b, i: (0, 0)),
                  tok(), tok(),
                  pl.BlockSpec((1, tm, 3 * D), lambda b, i: (b, i, U_GATE // (3 * D))),
                  tok(),
                  pl.BlockSpec((1, 1, D), lambda b, i: (b, 0, 0)),
                  wsp(), wsp(), wsp(), wsp()],
        out_specs=tok(),
        out_shape=jax.ShapeDtypeStruct((B, L, D), F32),
        compiler_params=_cparams(("parallel", "parallel")),
        name="merge",
    )(u, u, u, u, u, u, u, cw, ys, yn, u, h, gt, wc, ws, wn, wo)


MLP_TF = 1024


def _mlp_kernel(h_ref, sh_ref, sc_ref, gt_ref, nw_ref, fw_ref, w1_ref, w2_ref, o_ref, *, final):
    h = h_ref[0]
    xn = _norm_mod(h, nw_ref[...], sc_ref[0], sh_ref[0]).astype(BF16)
    acc = None
    for k in range(0, D_FF, MLP_TF):
        a = jnp.square(jnp.maximum(_dot(xn, w1_ref[:, k:k + MLP_TF]), 0.0)).astype(BF16)
        t = _dot(a, w2_ref[k:k + MLP_TF, :])
        acc = t if acc is None else acc + t
    o = h + gt_ref[0] * acc
    if final:
        ms = jnp.mean(o * o, axis=-1, keepdims=True)
        o = o * lax.rsqrt(ms + EPS) * fw_ref[...]
    o_ref[0] = o


def mlp(h, sh, sc, gt, nw, fw, w1, w2, tm, final):
    B, L, _ = h.shape
    vec = lambda: pl.BlockSpec((1, 1, D), lambda b, i: (b, 0, 0))
    par = lambda: pl.BlockSpec((1, D), lambda b, i: (0, 0))
    resident = lambda shape: pl.BlockSpec(shape, lambda b, i: (0, 0), pipeline_mode=pl.Buffered(1))
    return pl.pallas_call(
        functools.partial(_mlp_kernel, final=final),
        grid=(B, L // tm),
        in_specs=[pl.BlockSpec((1, tm, D), lambda b, i: (b, i, 0)),
                  vec(), vec(), vec(), par(), par(),
                  resident((D, D_FF)), resident((D_FF, D))],
        out_specs=pl.BlockSpec((1, tm, D), lambda b, i: (b, i, 0)),
        out_shape=jax.ShapeDtypeStruct((B, L, D), F32),
        compiler_params=_cparams(("parallel", "parallel")),
        name="mlp",
    )(h, sh, sc, gt, nw, fw, w1, w2)


def _rope_tables(L):
    t = jnp.arange(L, dtype=jnp.int32)
    row = (t // GRID_W).astype(F32)
    col = (t % GRID_W).astype(F32)
    half = HEAD_DIM // 2
    inv = ROPE_BASE ** (-jnp.arange(0, half, 2, dtype=F32) / half)
    ang_r = row[:, None] * inv
    ang_c = col[:, None] * inv
    ang = jnp.concatenate([ang_r, ang_r, ang_c, ang_c], axis=-1)
    cos = jnp.tile(jnp.cos(ang), (1, 2))
    sin = jnp.tile(jnp.sin(ang), (1, 2))
    even = ((jnp.arange(2 * HEAD_DIM) // (half // 2)) % 2 == 0)[None, :]
    return cos, jnp.where(even, -sin, 0.0), jnp.where(even, 0.0, sin)


def _rpb_kernel(r_ref, oh_ref, ok_ref, o_ref):
    val = _dot(jnp.concatenate(_split3(r_ref[...]), axis=1), oh_ref[...])
    o_ref[...] = jnp.where(ok_ref[...] > 0.0, LOG2E * val, -jnp.inf)


def _na_bias_table(rpb):
    H, NR, NC = rpb.shape
    W = GRID_W
    col = np.arange(W)
    col_start = np.clip(col - NA_KW // 2, 0, W - NA_KW)
    col_ok = (col[None, :] >= col_start[:, None]) & (col[None, :] < col_start[:, None] + NA_KW)
    dc_idx = np.clip(col[None, :] - col[:, None], -(NA_KW - 1), NA_KW - 1) + NA_KW - 1
    rows = jnp.transpose(rpb.reshape(H // 2, 2, NR, NC), (0, 2, 1, 3)).reshape(H * NR, NC)
    rows = jnp.pad(rows, ((0, 0), (0, DT_PAD - NC)))
    krow = jnp.arange(3 * DT_PAD, dtype=jnp.int32)[:, None] % DT_PAD
    onehot = (krow == jnp.asarray(dc_idx.reshape(1, W * W), jnp.int32)).astype(BF16)
    ok = jnp.asarray(col_ok.reshape(1, W * W), F32)
    tn = 1024
    tab = pl.pallas_call(
        _rpb_kernel,
        grid=(W * W // tn,),
        in_specs=[pl.BlockSpec((H * NR, DT_PAD), lambda j: (0, 0)),
                  pl.BlockSpec((3 * DT_PAD, tn), lambda j: (0, j)),
                  pl.BlockSpec((1, tn), lambda j: (0, j))],
        out_specs=pl.BlockSpec((H * NR, tn), lambda j: (0, j)),
        out_shape=jax.ShapeDtypeStruct((H * NR, W * W), F32),
        compiler_params=_cparams(("arbitrary",)),
        name="rpb_table",
    )(rows, onehot, ok)
    tab = tab.reshape(H // 2, NR, 2 * W, W)
    return jnp.concatenate([tab[:, :NR - 1], tab[:, 1:]], axis=-1)


def _prep_w_in(w_in):
    order = [(R_Q, D), (R_K, D), (R_V, D), (R_GATE, 3 * D), (R_CB, D), (R_CC, D), (R_CX, D), (R_Z, D), (R_XBC, XBC)]
    w = jnp.concatenate([w_in[:, o:o + n] for o, n in order], axis=1).astype(BF16)
    wdt = w_in[:, R_DT:R_DT + 2 * N_HEADS].astype(BF16)
    return w, jnp.pad(wdt, ((0, 0), (0, DT_PAD - 2 * N_HEADS))), wdt.T


def kernel(x, c, ctx, c_ctx, w_ada, b_ada, norm1_w, w_in, conv_mix_w, ssd_conv_w, ssd_conv_b, ssd_a_log, ssd_dt_bias,
           ssd_d, ssd_norm_w, na_rpb, w_br_conv, w_br_ssd, w_br_na, w_out, norm2_w, w_ff1, w_ff2, final_norm_w):
    B, L, _ = x.shape
    Lc = ctx.shape[1]
    depth = w_in.shape[0]
    cos, sa, sb = _rope_tables(L)
    n_mod = B + 1
    pad = (-n_mod) % 8
    c_rows = jnp.concatenate([c, c_ctx[None, :], jnp.zeros((pad, D), F32)], axis=0)
    zero_state = jnp.zeros((B, 2, SSD_GROUPS, SSD_STATE, D // SSD_GROUPS), F32)
    fw = final_norm_w.reshape(1, D)
    Tc = B * Lc
    tmc = min(1024, Tc)
    flat = lambda a: a.reshape(1, Tc, a.shape[-1])
    h, hc = x, flat(ctx)
    for l in range(depth):
        last = l == depth - 1
        mod = ada_mod(c_rows, w_ada[l], b_ada[l])
        m_lat = mod[:B].reshape(B, 1, 6, D)
        m_ctx = mod[B:B + 1].reshape(1, 1, 6, D)
        sh1, sc1, gt1, sh2, sc2, gt2 = (m_lat[:, :, i] for i in range(6))
        csh1, csc1, cgt1, csh2, csc2, cgt2 = (m_ctx[:, :, i] for i in range(6))
        w, wdt, wdtT = _prep_w_in(w_in[l])
        nw1 = norm1_w[l].reshape(1, D)
        u, dt, dtT = in_proj(h, sh1, sc1, nw1, w, wdt, wdtT, tm=1024, rope_tables=(cos, sa, sb))
        uc, dtc, dtTc = in_proj(hc, csh1, csc1, nw1, w, wdt, wdtT, tm=tmc)
        uc = uc.reshape(B, Lc, U_COLS)
        dtc = dtc.reshape(B, Lc, DT_PAD)
        dtTc = jnp.transpose(dtTc.reshape(2 * N_HEADS, B, Lc), (1, 0, 2))
        d_skip_e = jnp.repeat(ssd_d[l], HEAD_DIM, axis=1)
        ssd_p = (ssd_a_log[l], ssd_dt_bias[l], d_skip_e, ssd_norm_w[l])
        xbc_c = xbc_conv(uc, ssd_conv_w[l], ssd_conv_b[l])
        y_ssd_c, ctx_states = ssd_mix(xbc_c, uc, dtc, dtTc, zero_state, *ssd_p)
        xbc_l = xbc_conv(u, ssd_conv_w[l], ssd_conv_b[l])
        y_ssd, _ = ssd_mix(xbc_l, u, dt, dtT, ctx_states, *ssd_p)
        y_na = na_attend(u, uc, _na_bias_table(na_rpb[l]))
        wb = [t[l].astype(BF16) for t in (w_br_conv, w_br_ssd, w_br_na, w_out)]
        w1, w2 = w_ff1[l].astype(BF16), w_ff2[l].astype(BF16)
        nw2 = norm2_w[l].reshape(1, D)
        h = merge(u, y_ssd, y_na, h, gt1, conv_mix_w[l], *wb, tm=512, seq_len=L)
        h = mlp(h, sh2, sc2, gt2, nw2, fw, w1, w2, tm=1024, final=last)
        if not last:
            y_na_c = ctx_attend(uc)
            hc = merge(flat(uc), flat(y_ssd_c), flat(y_na_c), hc, cgt1, conv_mix_w[l], *wb, tm=min(512, Tc), seq_len=Lc)
            hc = mlp(hc, csh2, csc2, cgt2, nw2, fw, w1, w2, tm=tmc, final=False)
    return h
```

```python
import functools
import math

import jax
import jax.numpy as jnp
import numpy as np
from jax import lax
from jax.experimental import pallas as pl
from jax.experimental.pallas import tpu as pltpu

F32 = jnp.float32
BF16 = jnp.bfloat16
HIGHEST = lax.Precision.HIGHEST
LOG2E = math.log2(math.e)

D = 1024
EPS = 1e-6
GRID_W = 64
N_HEADS = 16
HEAD_DIM = 64
SSD_GROUPS = 2
SSD_STATE = 128
CHUNK = 128
SSD_SUB = 8
NA_KH = 8
NA_KW = 16
ROPE_BASE = 10000.0
D_FF = 4 * D
XBC = D + 2 * SSD_GROUPS * SSD_STATE
U_Q, U_K, U_V, U_GATE, U_CB, U_CC, U_CX, U_Z, U_XS, U_BC, U_COLS = (
    0, 1024, 2048, 3072, 6144, 7168, 8192, 9216, 10240, 11264, 11776)
U_TILE = 2944
U_SUB = 512
R_CB, R_CC, R_CX, R_Z, R_XBC, R_DT, R_Q, R_K, R_V, R_GATE = (
    0, 1024, 2048, 3072, 4096, 5632, 5664, 6688, 7712, 8736)
DT_PAD = 128
HALO = 16
VMEM_LIMIT = 56 * 1024 * 1024


def _cparams(sem):
    return pltpu.CompilerParams(dimension_semantics=sem, vmem_limit_bytes=VMEM_LIMIT)


def _nt(a, b):
    return lax.dot_general(a, b, (((1,), (1,)), ((), ())), preferred_element_type=F32)


def _tn(a, b):
    return lax.dot_general(a, b, (((0,), (0,)), ((), ())), preferred_element_type=F32)


def _dot(a, b):
    return jnp.dot(a, b, preferred_element_type=F32)


def _dot_hi(a, b):
    return jnp.dot(a, b, preferred_element_type=F32, precision=HIGHEST)


def _sigmoid(x):
    return 0.5 * jnp.tanh(0.5 * x) + 0.5


def _silu(x):
    return x * _sigmoid(x)


def _norm_mod(x, nw, sc, sh):
    ms = jnp.mean(x * x, axis=-1, keepdims=True)
    y = x * lax.rsqrt(ms + EPS) * nw
    return y * (1.0 + sc) + sh


def _ada_kernel(c_ref, w_ref, b_ref, o_ref):
    o_ref[...] = _dot_hi(_silu(c_ref[...]), w_ref[...]) + b_ref[...]


def ada_mod(c_rows, w_ada, b_ada):
    rows = c_rows.shape[0]
    tn = 1536
    return pl.pallas_call(
        _ada_kernel,
        grid=(6 * D // tn,),
        in_specs=[pl.BlockSpec((rows, D), lambda j: (0, 0)),
                  pl.BlockSpec((D, tn), lambda j: (0, j)),
                  pl.BlockSpec((1, tn), lambda j: (0, j))],
        out_specs=pl.BlockSpec((rows, tn), lambda j: (0, j)),
        out_shape=jax.ShapeDtypeStruct((rows, 6 * D), F32),
        compiler_params=_cparams(("arbitrary",)),
        name="ada_mod",
    )(c_rows, w_ada, b_ada.reshape(1, 6 * D))


def _rope_slab(x, cos, sa, sb):
    return x * cos + pltpu.roll(x, 2 * HEAD_DIM - 16, axis=1) * sa + pltpu.roll(x, 16, axis=1) * sb


def _inproj_kernel(x_ref, sh_ref, sc_ref, nw_ref, w_ref, wdt_ref, wdtT_ref, *rest, rope):
    if rope:
        cos_ref, sa_ref, sb_ref, u_ref, dt_ref, dtT_ref, xn_ref = rest
    else:
        u_ref, dt_ref, dtT_ref, xn_ref = rest
    j, i = pl.program_id(1), pl.program_id(2)

    def normalise():
        xb = _norm_mod(x_ref[0], nw_ref[...], sc_ref[0], sh_ref[0]).astype(BF16)
        xn_ref[i] = xb
        dt_ref[0] = _dot(xb, wdt_ref[...])
        dtT_ref[0] = _nt(wdtT_ref[...], xb)
        return xb

    def tile(rope_cols, xb=None):
        for off in range(0, U_TILE, U_SUB):
            wd = min(U_SUB, U_TILE - off)
            r = _dot(xn_ref[i] if xb is None else xb, w_ref[:, off:off + wd])
            if off < rope_cols:
                scale = LOG2E * HEAD_DIM ** -0.5 if off < U_K else 1.0
                lw = 2 * HEAD_DIM
                r = jnp.concatenate(
                    [_rope_slab(r[:, s:s + lw], cos_ref[...], sa_ref[...], sb_ref[...]) * scale
                     for s in range(0, wd, lw)], axis=1)
            u_ref[0, :, off:off + wd] = r.astype(u_ref.dtype)

    @pl.when(j == 0)
    def _():
        tile(U_V if rope else 0, normalise())

    @pl.when(j > 0)
    def _():
        tile(0)


def in_proj(x, sh, sc, nw, w, wdt, wdtT, tm, rope_tables=None):
    B, L, _ = x.shape
    tn = U_TILE
    ni = L // tm
    grid = (B, U_COLS // tn, ni)
    rope = rope_tables is not None

    def first(j, i):
        return jnp.where(j == 0, i, ni - 1)

    rope_specs = [pl.BlockSpec((tm, 2 * HEAD_DIM), lambda b, j, i: (first(j, i), 0))] * 3 if rope else []
    return pl.pallas_call(
        functools.partial(_inproj_kernel, rope=rope),
        grid=grid,
        in_specs=[pl.BlockSpec((1, tm, D), lambda b, j, i: (b, first(j, i), 0)),
                  pl.BlockSpec((1, 1, D), lambda b, j, i: (b, 0, 0)),
                  pl.BlockSpec((1, 1, D), lambda b, j, i: (b, 0, 0)),
                  pl.BlockSpec((1, D), lambda b, j, i: (0, 0)),
                  pl.BlockSpec((D, tn), lambda b, j, i: (0, j)),
                  pl.BlockSpec((D, DT_PAD), lambda b, j, i: (0, 0)),
                  pl.BlockSpec((2 * N_HEADS, D), lambda b, j, i: (0, 0))] + rope_specs,
        out_specs=[pl.BlockSpec((1, tm, tn), lambda b, j, i: (b, i, j)),
                   pl.BlockSpec((1, tm, DT_PAD), lambda b, j, i: (b, first(j, i), 0)),
                   pl.BlockSpec((1, 2 * N_HEADS, tm), lambda b, j, i: (b, 0, first(j, i)))],
        out_shape=[jax.ShapeDtypeStruct((B, L, U_COLS), BF16),
                   jax.ShapeDtypeStruct((B, L, DT_PAD), F32),
                   jax.ShapeDtypeStruct((B, 2 * N_HEADS, L), F32)],
        scratch_shapes=[pltpu.VMEM((ni, tm, D), BF16)],
        compiler_params=_cparams(("parallel", "arbitrary", "arbitrary")),
        name="in_proj",
    )(x, sh, sc, nw, w, wdt, wdtT, *(rope_tables if rope else ()))


def _dwconv3(p, w):
    L = p.shape[0]
    row = lax.broadcasted_iota(jnp.int32, p.shape, 0)
    prev = jnp.where(row == 0, 0.0, pltpu.roll(p, 1, axis=0))
    nxt = jnp.where(row == L - 1, 0.0, pltpu.roll(p, L - 1, axis=0))
    return prev * w[0:1] + p * w[1:2] + nxt * w[2:3]


def _xbcconv_kernel(x_ref, w_ref, b_ref, o_ref):
    L = x_ref.shape[1]
    E = 24
    w = w_ref[...]
    b = b_ref[...]

    def act(h):
        return (h + h * jnp.tanh(h)).astype(o_ref.dtype)

    x = x_ref[0].astype(F32)
    o_ref[0] = act(pltpu.roll(x, 1, axis=0) * w[0:1] + x * w[1:2] + pltpu.roll(x, L - 1, axis=0) * w[2:3] + b)
    o_ref[0, 0:HALO, :] = act(_dwconv3(x[0:E], w) + b)[0:HALO]
    o_ref[0, L - HALO:L, :] = act(_dwconv3(x[L - E:L], w) + b)[E - HALO:E]


def xbc_conv(u, w, bias, tc=256):
    B, L, _ = u.shape
    off = U_XS // tc
    return pl.pallas_call(
        _xbcconv_kernel,
        grid=(B, XBC // tc),
        in_specs=[pl.BlockSpec((1, L, tc), lambda b, j: (b, 0, off + j)),
                  pl.BlockSpec((3, tc), lambda b, j: (0, j)),
                  pl.BlockSpec((1, tc), lambda b, j: (0, j))],
        out_specs=pl.BlockSpec((1, L, tc), lambda b, j: (b, 0, j)),
        out_shape=jax.ShapeDtypeStruct((B, L, XBC), BF16),
        compiler_params=_cparams(("parallel", "parallel")),
        name="xbc_conv",
    )(u, 0.5 * w, 0.5 * bias.reshape(1, XBC))


def _split3(x):
    hi = x.astype(BF16)
    r1 = x - hi.astype(F32)
    mid = r1.astype(BF16)
    lo = (r1 - mid.astype(F32)).astype(BF16)
    return hi, mid, lo


def _head_stack(x2, lane):
    zero = jnp.zeros_like(x2)
    return jnp.concatenate([jnp.where(lane < HEAD_DIM, x2, zero), jnp.where(lane >= HEAD_DIM, x2, zero)], axis=0)


def _ssd_chunk(d, sub, rows, xs_ref, bc_ref, z_ref, dt_ref, dtT_ref, alr_ref, alc_ref, dbr_ref, dbc_ref, dsk_ref,
               nw_ref, ex_ref, y_ref, yacc_ref, st_ref):
    H = N_HEADS
    blk = slice(sub * CHUNK, (sub + 1) * CHUNK)
    xsb = xs_ref[0, blk, :]
    xs = xsb.astype(F32)
    bc = bc_ref[0, blk, :]
    dt_c = jax.nn.softplus(dt_ref[0, blk, :] + dbr_ref[...])
    a_c = dt_c * (-LOG2E * jnp.exp(alr_ref[...]))
    dtT_c = jax.nn.softplus(dtT_ref[0, :, blk] + dbc_ref[...])
    aT_c = dtT_c * (-LOG2E * jnp.exp(alc_ref[...]))

    ri = lax.broadcasted_iota(jnp.int32, (CHUNK, CHUNK), 0)
    ci = lax.broadcasted_iota(jnp.int32, (CHUNK, CHUNK), 1)
    tri = (ri >= ci) if d == 0 else (ri <= ci)
    trib = tri.astype(BF16)
    tribT = ((ri <= ci) if d == 0 else (ri >= ci)).astype(BF16)
    cs = _dot(jnp.concatenate([trib] * 3, axis=1), jnp.concatenate(_split3(a_c), axis=0))
    csT = _dot(jnp.concatenate(_split3(aT_c), axis=1), jnp.concatenate([tribT] * 3, axis=0))
    last = cs[CHUNK - 1:CHUNK] if d == 0 else cs[0:1]
    w_c = dt_c * jnp.exp2(last - cs)
    ein_c = jnp.exp2(cs)
    ex = ex_ref[d]
    w_e = _dot(jnp.concatenate(_split3(w_c)[:2], axis=1), ex)
    ein_e = _dot(jnp.concatenate(_split3(ein_c)[:2], axis=1), ex)
    cd_e = ein_e[CHUNK - 1:CHUNK] if d == 0 else ein_e[0:1]
    Xdec = (xs * w_e).astype(BF16)
    rowT = csT - jnp.log2(dtT_c)
    lane = lax.broadcasted_iota(jnp.int32, (CHUNK, 2 * HEAD_DIM), 1)

    ys = []
    GW = D // SSD_GROUPS
    for g in range(SSD_GROUPS):
        Bg = bc[:, g * SSD_STATE:(g + 1) * SSD_STATE]
        Cg = bc[:, (SSD_GROUPS + g) * SSD_STATE:(SSD_GROUPS + g + 1) * SSD_STATE]
        S = _nt(Cg, Bg)
        st = st_ref[d, g]
        y_off = _dot(Cg, st.astype(BF16)) * ein_e[:, g * GW:(g + 1) * GW]
        y_diag = []
        for p in range(GW // (2 * HEAD_DIM)):
            hA = g * (H // SSD_GROUPS) + 2 * p
            Ms = []
            for h in (hA, hA + 1):
                col = cs[:, d * H + h:d * H + h + 1]
                rowv = rowT[d * H + h:d * H + h + 1, :]
                Ms.append((S * jnp.exp2(jnp.where(tri, col - rowv, -jnp.inf))).astype(BF16))
            Mcat = jnp.concatenate(Ms, axis=1)
            Xp = xsb[:, hA * HEAD_DIM:(hA + 2) * HEAD_DIM]
            y_diag.append(_dot(Mcat, _head_stack(Xp, lane)))
        ys.append(y_off + jnp.concatenate(y_diag, axis=1))
        st_ref[d, g] = st * cd_e[:, g * GW:(g + 1) * GW] + _tn(Bg, Xdec[:, g * GW:(g + 1) * GW])
    y = jnp.concatenate(ys, axis=1)

    if d == 0:
        yacc_ref[rows, :] = y
    else:
        ysum = yacc_ref[rows, :] + y + xs * (dsk_ref[0:1] + dsk_ref[1:2])
        yg = ysum * _silu(z_ref[0, blk, :].astype(F32))
        parts = []
        for g in range(SSD_GROUPS):
            v = yg[:, g * GW:(g + 1) * GW]
            ms = jnp.mean(v * v, axis=-1, keepdims=True)
            parts.append(v * lax.rsqrt(ms + EPS) * nw_ref[:, g * GW:(g + 1) * GW])
        y_ref[0, blk, :] = jnp.concatenate(parts, axis=1).astype(y_ref.dtype)


def _ssd_kernel(xs_ref, bc_ref, z_ref, dt_ref, dtT_ref, h0_ref, alr_ref, alc_ref, dbr_ref, dbc_ref, dsk_ref, nw_ref,
                ex_ref, y_ref, sto_ref, yacc_ref, st_ref, *, ns, nsub):
    t = pl.program_id(1)

    @pl.when(t == 0)
    def _():
        st_ref[...] = h0_ref[0]

    args = (xs_ref, bc_ref, z_ref, dt_ref, dtT_ref, alr_ref, alc_ref, dbr_ref, dbc_ref, dsk_ref, nw_ref,
            ex_ref, y_ref, yacc_ref, st_ref)

    def seq_rows(step, sub):
        return pl.ds(pl.multiple_of((step * nsub + sub) * CHUNK, CHUNK), CHUNK)

    @pl.when(t < ns)
    def _():
        for sub in range(nsub):
            _ssd_chunk(0, sub, seq_rows(t, sub), *args)

    @pl.when(t >= ns)
    def _():
        for sub in reversed(range(nsub)):
            _ssd_chunk(1, sub, seq_rows(2 * ns - 1 - t, sub), *args)

    @pl.when(t == 2 * ns - 1)
    def _():
        sto_ref[0] = st_ref[...]


def ssd_mix(xbc_c, u, dt, dtT, h0, a_log, dt_bias, d_skip_e, norm_w):
    B, L, _ = u.shape
    nsub = min(SSD_SUB, L // CHUNK)
    RB = nsub * CHUNK
    ns = L // RB
    H2 = 2 * N_HEADS

    def chunk(t):
        return jnp.where(t < ns, t, 2 * ns - 1 - t)

    def late(t):
        return jnp.where(t < ns, ns - 1, 2 * ns - 1 - t)

    st_shape = (2, SSD_GROUPS, SSD_STATE, D // SSD_GROUPS)
    st_spec = pl.BlockSpec((1,) + st_shape, lambda b, t: (b, 0, 0, 0, 0))
    small = lambda shape: pl.BlockSpec(shape, lambda b, t: (0,) * len(shape))
    krow = jnp.arange(2 * DT_PAD, dtype=jnp.int32)[:, None] % DT_PAD
    head = jnp.arange(D, dtype=jnp.int32)[None, :] // HEAD_DIM
    ex = jnp.stack([krow == d * N_HEADS + head for d in range(2)]).astype(BF16)
    lane_pad = lambda v: jnp.pad(v.reshape(1, H2), ((0, 0), (0, DT_PAD - H2)))
    return pl.pallas_call(
        functools.partial(_ssd_kernel, ns=ns, nsub=nsub),
        grid=(B, 2 * ns),
        in_specs=[pl.BlockSpec((1, RB, D), lambda b, t: (b, chunk(t), 0)),
                  pl.BlockSpec((1, RB, XBC - D), lambda b, t: (b, chunk(t), D // (XBC - D))),
                  pl.BlockSpec((1, RB, D), lambda b, t: (b, late(t), U_Z // D)),
                  pl.BlockSpec((1, RB, DT_PAD), lambda b, t: (b, chunk(t), 0)),
                  pl.BlockSpec((1, H2, RB), lambda b, t: (b, 0, chunk(t))),
                  st_spec,
                  small((1, DT_PAD)), small((H2, 1)), small((1, DT_PAD)), small((H2, 1)),
                  small((2, D)), small((1, D)), small((2, 2 * DT_PAD, D))],
        out_specs=[pl.BlockSpec((1, RB, D), lambda b, t: (b, late(t), 0)),
                   st_spec],
        out_shape=[jax.ShapeDtypeStruct((B, L, D), BF16),
                   jax.ShapeDtypeStruct((B,) + st_shape, F32)],
        scratch_shapes=[pltpu.VMEM((L, D), F32), pltpu.VMEM(st_shape, F32)],
        compiler_params=_cparams(("parallel", "arbitrary")),
        name="ssd_mix",
    )(xbc_c, xbc_c, u, dt, dtT, h0,
      lane_pad(a_log), a_log.reshape(H2, 1), lane_pad(dt_bias), dt_bias.reshape(H2, 1),
      d_skip_e, norm_w.reshape(1, D), ex)


NA_GROUP = 1


def _na_kernel(q_ref, k_ref, v_ref, kc_ref, vc_ref, bias_ref, o_ref, s_ref, p_ref, inv_ref, kt_ref, kto_ref, *, rows):
    W = GRID_W
    band = NA_KH * W
    lane = lax.broadcasted_iota(jnp.int32, (W, 2 * HEAD_DIM), 1)
    G = NA_GROUP
    n_groups = rows // G
    M2 = 2 * W
    kt_ref[...] = k_ref[0].T
    kto_ref[...] = k_ref[0, W:W + kto_ref.shape[1], :].T
    kct = kc_ref[0].T

    def band_start(r):
        return min(max(r - NA_KH // 2, 0), rows - NA_KH)

    def stage_scores(g, slot):
        for i in range(G):
            r = g * G + i
            r0 = band_start(r)
            qs = _head_stack(q_ref[0, r * W:(r + 1) * W, :], lane)
            ktb, t0 = (kt_ref, r0 * W) if r0 % 2 == 0 else (kto_ref, (r0 - 1) * W)
            base = r0 - r + NA_KH - 1
            for jp in range(NA_KH // 2):
                cols = slice(jp * M2, (jp + 1) * M2)
                s_ref[slot, i * M2:(i + 1) * M2, cols] = (_dot(qs, ktb[:, t0 + jp * M2:t0 + (jp + 1) * M2])
                                                          + bias_ref[0, base + 2 * jp])
            s_ref[slot, i * M2:(i + 1) * M2, band:] = _dot(qs, kct)

    def stage_softmax(slot):
        for i in range(G):
            s = s_ref[slot, i * M2:(i + 1) * M2, :]
            p = jnp.exp2(s - jnp.max(s, axis=-1, keepdims=True))
            inv = 1.0 / jnp.sum(p, axis=-1, keepdims=True)
            p_ref[slot, i * M2:(i + 1) * M2, :] = p.astype(BF16)
            inv_ref[slot, i * M2:(i + 1) * M2, :] = jnp.broadcast_to(inv, (M2, M2))

    def stage_values(g, slot):
        for i in range(G):
            r = g * G + i
            r0 = band_start(r)
            vb = v_ref[0, r0 * W:r0 * W + band, :]
            p = p_ref[slot, i * M2:(i + 1) * M2, :]
            o2 = (_dot(p[:, :band], vb) + _dot(p[:, band:], vc_ref[0])) * inv_ref[slot, i * M2:(i + 1) * M2, :]
            o = jnp.where(lane < HEAD_DIM, o2[:W], o2[W:])
            o_ref[0, r * W:(r + 1) * W, :] = o.astype(o_ref.dtype)

    stage_scores(0, 0)
    stage_scores(1, 1)
    stage_softmax(0)

    for g in range(n_groups - 2):
        stage_scores(g + 2, g % 2)
        stage_softmax((g + 1) % 2)
        stage_values(g, g % 2)
    stage_softmax((n_groups - 1) % 2)
    stage_values(n_groups - 2, (n_groups - 2) % 2)
    stage_values(n_groups - 1, (n_groups - 1) % 2)


def na_attend(u, u_ctx, bias):
    B, L, _ = u.shape
    Lc = u_ctx.shape[1]
    HP = N_HEADS // 2
    lw = 2 * HEAD_DIM
    oq, ok, ov = U_Q // lw, U_K // lw, U_V // lw
    return pl.pallas_call(
        functools.partial(_na_kernel, rows=L // GRID_W),
        grid=(HP, B),
        in_specs=[pl.BlockSpec((1, L, lw), lambda h, b: (b, 0, oq + h)),
                  pl.BlockSpec((1, L, lw), lambda h, b: (b, 0, ok + h)),
                  pl.BlockSpec((1, L, lw), lambda h, b: (b, 0, ov + h)),
                  pl.BlockSpec((1, Lc, lw), lambda h, b: (b, 0, ok + h)),
                  pl.BlockSpec((1, Lc, lw), lambda h, b: (b, 0, ov + h)),
                  pl.BlockSpec((1, 2 * NA_KH - 2, 2 * GRID_W, 2 * GRID_W), lambda h, b: (h, 0, 0, 0))],
        out_specs=pl.BlockSpec((1, L, lw), lambda h, b: (b, 0, h)),
        out_shape=jax.ShapeDtypeStruct((B, L, D), BF16),
        scratch_shapes=[pltpu.VMEM((2, NA_GROUP * lw, NA_KH * GRID_W + Lc), F32),
                        pltpu.VMEM((2, NA_GROUP * lw, NA_KH * GRID_W + Lc), BF16),
                        pltpu.VMEM((2, NA_GROUP * lw, lw), F32),
                        pltpu.VMEM((lw, L), BF16), pltpu.VMEM((lw, L - 2 * GRID_W), BF16)],
        compiler_params=_cparams(("parallel", "parallel")),
        name="na_attend",
    )(u, u, u, u_ctx, u_ctx, bias)


def _ctxattn_kernel(q_ref, k_ref, v_ref, o_ref):
    Lc = q_ref.shape[1]
    lane = lax.broadcasted_iota(jnp.int32, (Lc, 2 * HEAD_DIM), 1)
    q2 = (q_ref[0].astype(F32) * (HEAD_DIM ** -0.5)).astype(BF16)
    s = _nt(_head_stack(q2, lane), k_ref[0])
    m = jnp.max(s, axis=-1, keepdims=True)
    p = jnp.exp(s - m)
    p = (p * (1.0 / jnp.sum(p, axis=-1, keepdims=True))).astype(BF16)
    o2 = _dot(p, v_ref[0])
    o_ref[0] = jnp.where(lane < HEAD_DIM, o2[:Lc], o2[Lc:]).astype(o_ref.dtype)


def ctx_attend(u_ctx):
    B, Lc, _ = u_ctx.shape
    HP = N_HEADS // 2
    lw = 2 * HEAD_DIM
    oq, ok, ov = U_Q // lw, U_K // lw, U_V // lw
    return pl.pallas_call(
        _ctxattn_kernel,
        grid=(HP, B),
        in_specs=[pl.BlockSpec((1, Lc, lw), lambda h, b: (b, 0, oq + h)),
                  pl.BlockSpec((1, Lc, lw), lambda h, b: (b, 0, ok + h)),
                  pl.BlockSpec((1, Lc, lw), lambda h, b: (b, 0, ov + h))],
        out_specs=pl.BlockSpec((1, Lc, lw), lambda h, b: (b, 0, h)),
        out_shape=jax.ShapeDtypeStruct((B, Lc, D), BF16),
        compiler_params=_cparams(("parallel", "parallel")),
        name="ctx_attend",
    )(u_ctx, u_ctx, u_ctx)


MERGE_CK = 256


def _merge_kernel(cv_ref, cvp_ref, cvn_ref, cw_ref, ys_ref, yn_ref, g_ref, h_ref, gt_ref, wc_ref, ws_ref, wn_ref, wo_ref,
                  o_ref, *, seq_len):
    tm = h_ref.shape[1]
    row = lax.broadcasted_iota(jnp.int32, (tm, MERGE_CK), 0)
    pos = (pl.program_id(1) * tm + row) & (seq_len - 1)
    first, last = pos == 0, pos == seq_len - 1
    top, bottom = row == 0, row == tm - 1
    yc_proj = None
    for c in range(0, D, MERGE_CK):
        ch = slice(c, c + MERGE_CK)
        chc, chx = slice(D + c, D + c + MERGE_CK), slice(2 * D + c, 2 * D + c + MERGE_CK)
        p = cv_ref[0, :, chc].astype(F32) * cv_ref[0, :, chx].astype(F32)
        p_before = (cvp_ref[0, :, chc].astype(F32) * cvp_ref[0, :, chx].astype(F32))[HALO - 1:HALO]
        p_after = (cvn_ref[0, :, chc].astype(F32) * cvn_ref[0, :, chx].astype(F32))[0:1]
        prev = jnp.where(first, 0.0, jnp.where(top, p_before, pltpu.roll(p, 1, axis=0)))
        nxt = jnp.where(last, 0.0, jnp.where(bottom, p_after, pltpu.roll(p, tm - 1, axis=0)))
        w = cw_ref[:, ch]
        yc = (cv_ref[0, :, ch].astype(F32) * (prev * w[0:1] + p * w[1:2] + nxt * w[2:3])).astype(BF16)
        t = _dot(yc, wc_ref[ch, :])
        yc_proj = t if yc_proj is None else yc_proj + t

    g = _sigmoid(g_ref[0].astype(F32))
    m = (g[:, 0:D] * yc_proj
         + g[:, D:2 * D] * _dot(ys_ref[0], ws_ref[...])
         + g[:, 2 * D:3 * D] * _dot(yn_ref[0], wn_ref[...]))
    o_ref[0] = h_ref[0] + gt_ref[0] * _dot(m.astype(BF16), wo_ref[...])


def merge(u, ys, yn, h, gt, cw, wc, ws, wn, wo, tm, seq_len):
    B, L, _ = h.shape
    assert seq_len & (seq_len - 1) == 0 and (tm % seq_len == 0 or seq_len % tm == 0)
    nb, hb = tm // HALO, L // HALO
    tok = lambda: pl.BlockSpec((1, tm, D), lambda b, i: (b, i, 0))
    cvw = 3 * D
    assert U_CC == U_CB + D and U_CX == U_CB + 2 * D and U_CB % cvw == 0
    wsp = lambda: pl.BlockSpec((D, D), lambda b, i: (0, 0))
    return pl.pallas_call(
        functools.partial(_merge_kernel, seq_len=seq_len),
        grid=(B, L // tm),
        in_specs=[pl.BlockSpec((1, tm, cvw), lambda b, i: (b, i, U_CB // cvw)),
                  pl.BlockSpec((1, HALO, cvw), lambda b, i: (b, jnp.maximum(i * nb - 1, 0), U_CB // cvw)),
                  pl.BlockSpec((1, HALO, cvw), lambda b, i: (b, jnp.minimum((i + 1) * nb, hb - 1), U_CB // cvw)),
                  pl.BlockSpec((3, D), lambda b, i: (0, 0)),
                  tok(), tok(),
                  pl.BlockSpec((1, tm, 3 * D), lambda b, i: (b, i, U_GATE // (3 * D))),
                  tok(),
                  pl.BlockSpec((1, 1, D), lambda b, i: (b, 0, 0)),
                  wsp(), wsp(), wsp(), wsp()],
        out_specs=tok(),
        out_shape=jax.ShapeDtypeStruct((B, L, D), F32),
        compiler_params=_cparams(("parallel", "parallel")),
        name="merge",
    )(u, u, u, cw, ys, yn, u, h, gt, wc, ws, wn, wo)


MLP_TF = 1024


def _mlp_kernel(h_ref, sh_ref, sc_ref, gt_ref, nw_ref, fw_ref, w1_ref, w2_ref, o_ref, *, final):
    h = h_ref[0]
    xn = _norm_mod(h, nw_ref[...], sc_ref[0], sh_ref[0]).astype(BF16)
    acc = None
    for k in range(0, D_FF, MLP_TF):
        a = jnp.square(jnp.maximum(_dot(xn, w1_ref[:, k:k + MLP_TF]), 0.0)).astype(BF16)
        t = _dot(a, w2_ref[k:k + MLP_TF, :])
        acc = t if acc is None else acc + t
    o = h + gt_ref[0] * acc
    if final:
        ms = jnp.mean(o * o, axis=-1, keepdims=True)
        o = o * lax.rsqrt(ms + EPS) * fw_ref[...]
    o_ref[0] = o


def mlp(h, sh, sc, gt, nw, fw, w1, w2, tm, final):
    B, L, _ = h.shape
    vec = lambda: pl.BlockSpec((1, 1, D), lambda b, i: (b, 0, 0))
    par = lambda: pl.BlockSpec((1, D), lambda b, i: (0, 0))
    resident = lambda shape: pl.BlockSpec(shape, lambda b, i: (0, 0), pipeline_mode=pl.Buffered(1))
    return pl.pallas_call(
        functools.partial(_mlp_kernel, final=final),
        grid=(B, L // tm),
        in_specs=[pl.BlockSpec((1, tm, D), lambda b, i: (b, i, 0)),
                  vec(), vec(), vec(), par(), par(),
                  resident((D, D_FF)), resident((D_FF, D))],
        out_specs=pl.BlockSpec((1, tm, D), lambda b, i: (b, i, 0)),
        out_shape=jax.ShapeDtypeStruct((B, L, D), F32),
        compiler_params=_cparams(("parallel", "parallel")),
        name="mlp",
    )(h, sh, sc, gt, nw, fw, w1, w2)


def _rope_tables(L):
    t = jnp.arange(L, dtype=jnp.int32)
    row = (t // GRID_W).astype(F32)
    col = (t % GRID_W).astype(F32)
    half = HEAD_DIM // 2
    inv = ROPE_BASE ** (-jnp.arange(0, half, 2, dtype=F32) / half)
    ang_r = row[:, None] * inv
    ang_c = col[:, None] * inv
    ang = jnp.concatenate([ang_r, ang_r, ang_c, ang_c], axis=-1)
    cos = jnp.tile(jnp.cos(ang), (1, 2))
    sin = jnp.tile(jnp.sin(ang), (1, 2))
    even = ((jnp.arange(2 * HEAD_DIM) // (half // 2)) % 2 == 0)[None, :]
    return cos, jnp.where(even, -sin, 0.0), jnp.where(even, 0.0, sin)


def _rpb_kernel(r_ref, oh_ref, ok_ref, o_ref):
    val = _dot(jnp.concatenate(_split3(r_ref[...]), axis=1), oh_ref[...])
    o_ref[...] = jnp.where(ok_ref[...] > 0.0, LOG2E * val, -jnp.inf)


def _na_bias_table(rpb):
    H, NR, NC = rpb.shape
    W = GRID_W
    col = np.arange(W)
    col_start = np.clip(col - NA_KW // 2, 0, W - NA_KW)
    col_ok = (col[None, :] >= col_start[:, None]) & (col[None, :] < col_start[:, None] + NA_KW)
    dc_idx = np.clip(col[None, :] - col[:, None], -(NA_KW - 1), NA_KW - 1) + NA_KW - 1
    rows = jnp.transpose(rpb.reshape(H // 2, 2, NR, NC), (0, 2, 1, 3)).reshape(H * NR, NC)
    rows = jnp.pad(rows, ((0, 0), (0, DT_PAD - NC)))
    krow = jnp.arange(3 * DT_PAD, dtype=jnp.int32)[:, None] % DT_PAD
    onehot = (krow == jnp.asarray(dc_idx.reshape(1, W * W), jnp.int32)).astype(BF16)
    ok = jnp.asarray(col_ok.reshape(1, W * W), F32)
    tn = 1024
    tab = pl.pallas_call(
        _rpb_kernel,
        grid=(W * W // tn,),
        in_specs=[pl.BlockSpec((H * NR, DT_PAD), lambda j: (0, 0)),
                  pl.BlockSpec((3 * DT_PAD, tn), lambda j: (0, j)),
                  pl.BlockSpec((1, tn), lambda j: (0, j))],
        out_specs=pl.BlockSpec((H * NR, tn), lambda j: (0, j)),
        out_shape=jax.ShapeDtypeStruct((H * NR, W * W), F32),
        compiler_params=_cparams(("arbitrary",)),
        name="rpb_table",
    )(rows, onehot, ok)
    tab = tab.reshape(H // 2, NR, 2 * W, W)
    return jnp.concatenate([tab[:, :NR - 1], tab[:, 1:]], axis=-1)


def _prep_w_in(w_in):
    order = [(R_Q, D), (R_K, D), (R_V, D), (R_GATE, 3 * D), (R_CB, D), (R_CC, D), (R_CX, D), (R_Z, D), (R_XBC, XBC)]
    w = jnp.concatenate([w_in[:, o:o + n] for o, n in order], axis=1).astype(BF16)
    wdt = w_in[:, R_DT:R_DT + 2 * N_HEADS].astype(BF16)
    return w, jnp.pad(wdt, ((0, 0), (0, DT_PAD - 2 * N_HEADS))), wdt.T


def kernel(x, c, ctx, c_ctx, w_ada, b_ada, norm1_w, w_in, conv_mix_w, ssd_conv_w, ssd_conv_b, ssd_a_log, ssd_dt_bias,
           ssd_d, ssd_norm_w, na_rpb, w_br_conv, w_br_ssd, w_br_na, w_out, norm2_w, w_ff1, w_ff2, final_norm_w):
    B, L, _ = x.shape
    Lc = ctx.shape[1]
    depth = w_in.shape[0]
    cos, sa, sb = _rope_tables(L)
    n_mod = B + 1
    pad = (-n_mod) % 8
    c_rows = jnp.concatenate([c, c_ctx[None, :], jnp.zeros((pad, D), F32)], axis=0)
    zero_state = jnp.zeros((B, 2, SSD_GROUPS, SSD_STATE, D // SSD_GROUPS), F32)
    fw = final_norm_w.reshape(1, D)
    Tc = B * Lc
    tmc = min(1024, Tc)
    flat = lambda a: a.reshape(1, Tc, a.shape[-1])
    h, hc = x, flat(ctx)
    for l in range(depth):
        last = l == depth - 1
        mod = ada_mod(c_rows, w_ada[l], b_ada[l])
        m_lat = mod[:B].reshape(B, 1, 6, D)
        m_ctx = mod[B:B + 1].reshape(1, 1, 6, D)
        sh1, sc1, gt1, sh2, sc2, gt2 = (m_lat[:, :, i] for i in range(6))
        csh1, csc1, cgt1, csh2, csc2, cgt2 = (m_ctx[:, :, i] for i in range(6))
        w, wdt, wdtT = _prep_w_in(w_in[l])
        nw1 = norm1_w[l].reshape(1, D)
        u, dt, dtT = in_proj(h, sh1, sc1, nw1, w, wdt, wdtT, tm=1024, rope_tables=(cos, sa, sb))
        uc, dtc, dtTc = in_proj(hc, csh1, csc1, nw1, w, wdt, wdtT, tm=tmc)
        uc = uc.reshape(B, Lc, U_COLS)
        dtc = dtc.reshape(B, Lc, DT_PAD)
        dtTc = jnp.transpose(dtTc.reshape(2 * N_HEADS, B, Lc), (1, 0, 2))
        d_skip_e = jnp.repeat(ssd_d[l], HEAD_DIM, axis=1)
        ssd_p = (ssd_a_log[l], ssd_dt_bias[l], d_skip_e, ssd_norm_w[l])
        xbc_c = xbc_conv(uc, ssd_conv_w[l], ssd_conv_b[l])
        y_ssd_c, ctx_states = ssd_mix(xbc_c, uc, dtc, dtTc, zero_state, *ssd_p)
        xbc_l = xbc_conv(u, ssd_conv_w[l], ssd_conv_b[l])
        y_ssd, _ = ssd_mix(xbc_l, u, dt, dtT, ctx_states, *ssd_p)
        y_na = na_attend(u, uc, _na_bias_table(na_rpb[l]))
        wb = [t[l].astype(BF16) for t in (w_br_conv, w_br_ssd, w_br_na, w_out)]
        w1, w2 = w_ff1[l].astype(BF16), w_ff2[l].astype(BF16)
        nw2 = norm2_w[l].reshape(1, D)
        h = merge(u, y_ssd, y_na, h, gt1, conv_mix_w[l], *wb, tm=512, seq_len=L)
        h = mlp(h, sh2, sc2, gt2, nw2, fw, w1, w2, tm=1024, final=last)
        if not last:
            y_na_c = ctx_attend(uc)
            hc = merge(flat(uc), flat(y_ssd_c), flat(y_na_c), hc, cgt1, conv_mix_w[l], *wb, tm=min(512, Tc), seq_len=Lc)
            hc = mlp(hc, csh2, csc2, cgt2, nw2, fw, w1, w2, tm=tmc, final=False)
    return h
```

```python
import functools
import math

import jax
import jax.numpy as jnp
import numpy as np
from jax import lax
from jax.experimental import pallas as pl
from jax.experimental.pallas import tpu as pltpu

F32 = jnp.float32
BF16 = jnp.bfloat16
HIGHEST = lax.Precision.HIGHEST
LOG2E = math.log2(math.e)

D = 1024
EPS = 1e-6
GRID_W = 64
N_HEADS = 16
HEAD_DIM = 64
SSD_GROUPS = 2
SSD_STATE = 128
CHUNK = 128
SSD_SUB = 8
NA_KH = 8
NA_KW = 16
ROPE_BASE = 10000.0
D_FF = 4 * D
XBC = D + 2 * SSD_GROUPS * SSD_STATE
U_Q, U_K, U_V, U_GATE, U_CB, U_CC, U_CX, U_Z, U_XS, U_BC, U_COLS = (
    0, 1024, 2048, 3072, 6144, 7168, 8192, 9216, 10240, 11264, 11776)
U_TILE = 2944
U_SUB = 512
R_CB, R_CC, R_CX, R_Z, R_XBC, R_DT, R_Q, R_K, R_V, R_GATE = (
    0, 1024, 2048, 3072, 4096, 5632, 5664, 6688, 7712, 8736)
DT_PAD = 128
HALO = 16
VMEM_LIMIT = 56 * 1024 * 1024


def _cparams(sem):
    return pltpu.CompilerParams(dimension_semantics=sem, vmem_limit_bytes=VMEM_LIMIT)


def _nt(a, b):
    return lax.dot_general(a, b, (((1,), (1,)), ((), ())), preferred_element_type=F32)


def _tn(a, b):
    return lax.dot_general(a, b, (((0,), (0,)), ((), ())), preferred_element_type=F32)


def _dot(a, b):
    return jnp.dot(a, b, preferred_element_type=F32)


def _dot_hi(a, b):
    return jnp.dot(a, b, preferred_element_type=F32, precision=HIGHEST)


def _sigmoid(x):
    return 0.5 * jnp.tanh(0.5 * x) + 0.5


def _silu(x):
    return x * _sigmoid(x)


def _norm_mod(x, nw, sc, sh):
    ms = jnp.mean(x * x, axis=-1, keepdims=True)
    y = x * lax.rsqrt(ms + EPS) * nw
    return y * (1.0 + sc) + sh


def _ada_kernel(c_ref, w_ref, b_ref, o_ref):
    o_ref[...] = _dot_hi(_silu(c_ref[...]), w_ref[...]) + b_ref[...]


def ada_mod(c_rows, w_ada, b_ada):
    rows = c_rows.shape[0]
    tn = 1536
    return pl.pallas_call(
        _ada_kernel,
        grid=(6 * D // tn,),
        in_specs=[pl.BlockSpec((rows, D), lambda j: (0, 0)),
                  pl.BlockSpec((D, tn), lambda j: (0, j)),
                  pl.BlockSpec((1, tn), lambda j: (0, j))],
        out_specs=pl.BlockSpec((rows, tn), lambda j: (0, j)),
        out_shape=jax.ShapeDtypeStruct((rows, 6 * D), F32),
        compiler_params=_cparams(("arbitrary",)),
        name="ada_mod",
    )(c_rows, w_ada, b_ada.reshape(1, 6 * D))


def _rope_slab(x, cos, sa, sb):
    return x * cos + pltpu.roll(x, 2 * HEAD_DIM - 16, axis=1) * sa + pltpu.roll(x, 16, axis=1) * sb


def _inproj_kernel(x_ref, sh_ref, sc_ref, nw_ref, w_ref, wdt_ref, wdtT_ref, cw_ref, cb_ref, *rest, rope, seq_len):
    if rope:
        cos_ref, sa_ref, sb_ref, u_ref, dt_ref, dtT_ref, xn_ref = rest
    else:
        u_ref, dt_ref, dtT_ref, xn_ref = rest
    j, i = pl.program_id(1), pl.program_id(2)
    ni, tm = xn_ref.shape[0], xn_ref.shape[1]
    n_tiles = U_COLS // U_TILE
    conv_off = U_XS - (n_tiles - 1) * U_TILE

    def normalise():
        xb = _norm_mod(x_ref[0], nw_ref[...], sc_ref[0], sh_ref[0]).astype(BF16)
        xn_ref[i] = xb
        dt_ref[0] = _dot(xb, wdt_ref[...])
        dtT_ref[0] = _nt(wdtT_ref[...], xb)
        return xb

    def tile(rope_cols, xb=None, end=U_TILE):
        for off in range(0, end, U_SUB):
            wd = min(U_SUB, end - off)
            r = _dot(xn_ref[i] if xb is None else xb, w_ref[:, off:off + wd])
            if off < rope_cols:
                scale = LOG2E * HEAD_DIM ** -0.5 if off < U_K else 1.0
                lw = 2 * HEAD_DIM
                r = jnp.concatenate(
                    [_rope_slab(r[:, s:s + lw], cos_ref[...], sa_ref[...], sb_ref[...]) * scale
                     for s in range(0, wd, lw)], axis=1)
            u_ref[0, :, off:off + wd] = r.astype(u_ref.dtype)

    def conv_columns():
        before = xn_ref[jnp.maximum(i - 1, 0), tm - HALO:tm, :]
        after = xn_ref[jnp.minimum(i + 1, ni - 1), 0:HALO, :]
        for off in range(conv_off, U_TILE, U_SUB):
            wd = min(U_SUB, U_TILE - off)
            ws = w_ref[:, off:off + wd]
            r = _dot(xn_ref[i], ws)
            r_before = _dot(before, ws)[HALO - 1:HALO]
            r_after = _dot(after, ws)[0:1]
            row = lax.broadcasted_iota(jnp.int32, (tm, wd), 0)
            pos = (i * tm + row) & (seq_len - 1)
            prev = jnp.where(pos == 0, 0.0, jnp.where(row == 0, r_before, pltpu.roll(r, 1, axis=0)))
            nxt = jnp.where(pos == seq_len - 1, 0.0, jnp.where(row == tm - 1, r_after, pltpu.roll(r, tm - 1, axis=0)))
            c0 = off - conv_off
            cw = cw_ref[:, c0:c0 + wd]
            h = prev * cw[0:1] + r * cw[1:2] + nxt * cw[2:3] + cb_ref[:, c0:c0 + wd]
            u_ref[0, :, off:off + wd] = (h + h * jnp.tanh(h)).astype(u_ref.dtype)

    @pl.when(j == 0)
    def _():
        tile(U_V if rope else 0, normalise())

    @pl.when(jnp.logical_and(j > 0, j < n_tiles - 1))
    def _():
        tile(0)

    @pl.when(j == n_tiles - 1)
    def _():
        tile(0, end=conv_off)
        conv_columns()


def in_proj(x, sh, sc, nw, w, wdt, wdtT, conv_w, conv_b, tm, seq_len, rope_tables=None):
    B, L, _ = x.shape
    assert seq_len & (seq_len - 1) == 0 and (tm % seq_len == 0 or seq_len % tm == 0)
    assert U_COLS - U_XS == XBC and (U_XS - (U_COLS // U_TILE - 1) * U_TILE) % 128 == 0
    tn = U_TILE
    ni = L // tm
    grid = (B, U_COLS // tn, ni)
    rope = rope_tables is not None

    def first(j, i):
        return jnp.where(j == 0, i, ni - 1)

    rope_specs = [pl.BlockSpec((tm, 2 * HEAD_DIM), lambda b, j, i: (first(j, i), 0))] * 3 if rope else []
    return pl.pallas_call(
        functools.partial(_inproj_kernel, rope=rope, seq_len=seq_len),
        grid=grid,
        in_specs=[pl.BlockSpec((1, tm, D), lambda b, j, i: (b, first(j, i), 0)),
                  pl.BlockSpec((1, 1, D), lambda b, j, i: (b, 0, 0)),
                  pl.BlockSpec((1, 1, D), lambda b, j, i: (b, 0, 0)),
                  pl.BlockSpec((1, D), lambda b, j, i: (0, 0)),
                  pl.BlockSpec((D, tn), lambda b, j, i: (0, j)),
                  pl.BlockSpec((D, DT_PAD), lambda b, j, i: (0, 0)),
                  pl.BlockSpec((2 * N_HEADS, D), lambda b, j, i: (0, 0)),
                  pl.BlockSpec((3, XBC), lambda b, j, i: (0, 0)),
                  pl.BlockSpec((1, XBC), lambda b, j, i: (0, 0))] + rope_specs,
        out_specs=[pl.BlockSpec((1, tm, tn), lambda b, j, i: (b, i, j)),
                   pl.BlockSpec((1, tm, DT_PAD), lambda b, j, i: (b, first(j, i), 0)),
                   pl.BlockSpec((1, 2 * N_HEADS, tm), lambda b, j, i: (b, 0, first(j, i)))],
        out_shape=[jax.ShapeDtypeStruct((B, L, U_COLS), BF16),
                   jax.ShapeDtypeStruct((B, L, DT_PAD), F32),
                   jax.ShapeDtypeStruct((B, 2 * N_HEADS, L), F32)],
        scratch_shapes=[pltpu.VMEM((ni, tm, D), BF16)],
        compiler_params=_cparams(("parallel", "arbitrary", "arbitrary")),
        name="in_proj",
    )(x, sh, sc, nw, w, wdt, wdtT, 0.5 * conv_w, 0.5 * conv_b.reshape(1, XBC), *(rope_tables if rope else ()))


def _split3(x):
    hi = x.astype(BF16)
    r1 = x - hi.astype(F32)
    mid = r1.astype(BF16)
    lo = (r1 - mid.astype(F32)).astype(BF16)
    return hi, mid, lo


def _head_stack(x2, lane):
    zero = jnp.zeros_like(x2)
    return jnp.concatenate([jnp.where(lane < HEAD_DIM, x2, zero), jnp.where(lane >= HEAD_DIM, x2, zero)], axis=0)


def _ssd_chunk(d, sub, rows, xs_ref, bc_ref, z_ref, dt_ref, dtT_ref, alr_ref, alc_ref, dbr_ref, dbc_ref, dsk_ref,
               nw_ref, ex_ref, y_ref, yacc_ref, st_ref):
    H = N_HEADS
    blk = slice(sub * CHUNK, (sub + 1) * CHUNK)
    xsb = xs_ref[0, blk, :]
    xs = xsb.astype(F32)
    bc = bc_ref[0, blk, :]
    dt_c = jax.nn.softplus(dt_ref[0, blk, :] + dbr_ref[...])
    a_c = dt_c * (-LOG2E * jnp.exp(alr_ref[...]))
    dtT_c = jax.nn.softplus(dtT_ref[0, :, blk] + dbc_ref[...])
    aT_c = dtT_c * (-LOG2E * jnp.exp(alc_ref[...]))

    ri = lax.broadcasted_iota(jnp.int32, (CHUNK, CHUNK), 0)
    ci = lax.broadcasted_iota(jnp.int32, (CHUNK, CHUNK), 1)
    tri = (ri >= ci) if d == 0 else (ri <= ci)
    trib = tri.astype(BF16)
    tribT = ((ri <= ci) if d == 0 else (ri >= ci)).astype(BF16)
    cs = _dot(jnp.concatenate([trib] * 3, axis=1), jnp.concatenate(_split3(a_c), axis=0))
    csT = _dot(jnp.concatenate(_split3(aT_c), axis=1), jnp.concatenate([tribT] * 3, axis=0))
    last = cs[CHUNK - 1:CHUNK] if d == 0 else cs[0:1]
    w_c = dt_c * jnp.exp2(last - cs)
    ein_c = jnp.exp2(cs)
    ex = ex_ref[d]
    w_e = _dot(jnp.concatenate(_split3(w_c)[:2], axis=1), ex)
    ein_e = _dot(jnp.concatenate(_split3(ein_c)[:2], axis=1), ex)
    cd_e = ein_e[CHUNK - 1:CHUNK] if d == 0 else ein_e[0:1]
    Xdec = (xs * w_e).astype(BF16)
    rowT = csT - jnp.log2(dtT_c)
    lane = lax.broadcasted_iota(jnp.int32, (CHUNK, 2 * HEAD_DIM), 1)

    ys = []
    GW = D // SSD_GROUPS
    for g in range(SSD_GROUPS):
        Bg = bc[:, g * SSD_STATE:(g + 1) * SSD_STATE]
        Cg = bc[:, (SSD_GROUPS + g) * SSD_STATE:(SSD_GROUPS + g + 1) * SSD_STATE]
        S = _nt(Cg, Bg)
        st = st_ref[d, g]
        y_off = _dot(Cg, st.astype(BF16)) * ein_e[:, g * GW:(g + 1) * GW]
        y_diag = []
        for p in range(GW // (2 * HEAD_DIM)):
            hA = g * (H // SSD_GROUPS) + 2 * p
            Ms = []
            for h in (hA, hA + 1):
                col = cs[:, d * H + h:d * H + h + 1]
                rowv = rowT[d * H + h:d * H + h + 1, :]
                Ms.append((S * jnp.exp2(jnp.where(tri, col - rowv, -jnp.inf))).astype(BF16))
            Mcat = jnp.concatenate(Ms, axis=1)
            Xp = xsb[:, hA * HEAD_DIM:(hA + 2) * HEAD_DIM]
            y_diag.append(_dot(Mcat, _head_stack(Xp, lane)))
        ys.append(y_off + jnp.concatenate(y_diag, axis=1))
        st_ref[d, g] = st * cd_e[:, g * GW:(g + 1) * GW] + _tn(Bg, Xdec[:, g * GW:(g + 1) * GW])
    y = jnp.concatenate(ys, axis=1)

    if d == 0:
        yacc_ref[rows, :] = y
    else:
        ysum = yacc_ref[rows, :] + y + xs * (dsk_ref[0:1] + dsk_ref[1:2])
        yg = ysum * _silu(z_ref[0, blk, :].astype(F32))
        parts = []
        for g in range(SSD_GROUPS):
            v = yg[:, g * GW:(g + 1) * GW]
            ms = jnp.mean(v * v, axis=-1, keepdims=True)
            parts.append(v * lax.rsqrt(ms + EPS) * nw_ref[:, g * GW:(g + 1) * GW])
        y_ref[0, blk, :] = jnp.concatenate(parts, axis=1).astype(y_ref.dtype)


def _ssd_kernel(xs_ref, bc_ref, z_ref, dt_ref, dtT_ref, h0_ref, alr_ref, alc_ref, dbr_ref, dbc_ref, dsk_ref, nw_ref,
                ex_ref, y_ref, sto_ref, yacc_ref, st_ref, *, ns, nsub):
    t = pl.program_id(1)

    @pl.when(t == 0)
    def _():
        st_ref[...] = h0_ref[0]

    args = (xs_ref, bc_ref, z_ref, dt_ref, dtT_ref, alr_ref, alc_ref, dbr_ref, dbc_ref, dsk_ref, nw_ref,
            ex_ref, y_ref, yacc_ref, st_ref)

    def seq_rows(step, sub):
        return pl.ds(pl.multiple_of((step * nsub + sub) * CHUNK, CHUNK), CHUNK)

    @pl.when(t < ns)
    def _():
        for sub in range(nsub):
            _ssd_chunk(0, sub, seq_rows(t, sub), *args)

    @pl.when(t >= ns)
    def _():
        for sub in reversed(range(nsub)):
            _ssd_chunk(1, sub, seq_rows(2 * ns - 1 - t, sub), *args)

    @pl.when(t == 2 * ns - 1)
    def _():
        sto_ref[0] = st_ref[...]


def ssd_mix(u, dt, dtT, h0, a_log, dt_bias, d_skip_e, norm_w):
    B, L, _ = u.shape
    nsub = min(SSD_SUB, L // CHUNK)
    RB = nsub * CHUNK
    ns = L // RB
    H2 = 2 * N_HEADS

    def chunk(t):
        return jnp.where(t < ns, t, 2 * ns - 1 - t)

    def late(t):
        return jnp.where(t < ns, ns - 1, 2 * ns - 1 - t)

    st_shape = (2, SSD_GROUPS, SSD_STATE, D // SSD_GROUPS)
    st_spec = pl.BlockSpec((1,) + st_shape, lambda b, t: (b, 0, 0, 0, 0))
    small = lambda shape: pl.BlockSpec(shape, lambda b, t: (0,) * len(shape))
    krow = jnp.arange(2 * DT_PAD, dtype=jnp.int32)[:, None] % DT_PAD
    head = jnp.arange(D, dtype=jnp.int32)[None, :] // HEAD_DIM
    ex = jnp.stack([krow == d * N_HEADS + head for d in range(2)]).astype(BF16)
    lane_pad = lambda v: jnp.pad(v.reshape(1, H2), ((0, 0), (0, DT_PAD - H2)))
    return pl.pallas_call(
        functools.partial(_ssd_kernel, ns=ns, nsub=nsub),
        grid=(B, 2 * ns),
        in_specs=[pl.BlockSpec((1, RB, D), lambda b, t: (b, chunk(t), U_XS // D)),
                  pl.BlockSpec((1, RB, XBC - D), lambda b, t: (b, chunk(t), U_BC // (XBC - D))),
                  pl.BlockSpec((1, RB, D), lambda b, t: (b, late(t), U_Z // D)),
                  pl.BlockSpec((1, RB, DT_PAD), lambda b, t: (b, chunk(t), 0)),
                  pl.BlockSpec((1, H2, RB), lambda b, t: (b, 0, chunk(t))),
                  st_spec,
                  small((1, DT_PAD)), small((H2, 1)), small((1, DT_PAD)), small((H2, 1)),
                  small((2, D)), small((1, D)), small((2, 2 * DT_PAD, D))],
        out_specs=[pl.BlockSpec((1, RB, D), lambda b, t: (b, late(t), 0)),
                   st_spec],
        out_shape=[jax.ShapeDtypeStruct((B, L, D), BF16),
                   jax.ShapeDtypeStruct((B,) + st_shape, F32)],
        scratch_shapes=[pltpu.VMEM((L, D), F32), pltpu.VMEM(st_shape, F32)],
        compiler_params=_cparams(("parallel", "arbitrary")),
        name="ssd_mix",
    )(u, u, u, dt, dtT, h0,
      lane_pad(a_log), a_log.reshape(H2, 1), lane_pad(dt_bias), dt_bias.reshape(H2, 1),
      d_skip_e, norm_w.reshape(1, D), ex)


NA_GROUP = 1


def _na_kernel(q_ref, k_ref, v_ref, kc_ref, vc_ref, bias_ref, o_ref, s_ref, p_ref, inv_ref, kt_ref, kto_ref, *, rows):
    W = GRID_W
    band = NA_KH * W
    lane = lax.broadcasted_iota(jnp.int32, (W, 2 * HEAD_DIM), 1)
    G = NA_GROUP
    n_groups = rows // G
    M2 = 2 * W
    kt_ref[...] = k_ref[0].T
    kto_ref[...] = k_ref[0, W:W + kto_ref.shape[1], :].T
    kct = kc_ref[0].T

    def band_start(r):
        return min(max(r - NA_KH // 2, 0), rows - NA_KH)

    def stage_scores(g, slot):
        for i in range(G):
            r = g * G + i
            r0 = band_start(r)
            qs = _head_stack(q_ref[0, r * W:(r + 1) * W, :], lane)
            ktb, t0 = (kt_ref, r0 * W) if r0 % 2 == 0 else (kto_ref, (r0 - 1) * W)
            base = r0 - r + NA_KH - 1
            for jp in range(NA_KH // 2):
                cols = slice(jp * M2, (jp + 1) * M2)
                s_ref[slot, i * M2:(i + 1) * M2, cols] = (_dot(qs, ktb[:, t0 + jp * M2:t0 + (jp + 1) * M2])
                                                          + bias_ref[0, base + 2 * jp])
            s_ref[slot, i * M2:(i + 1) * M2, band:] = _dot(qs, kct)

    def stage_softmax(slot):
        for i in range(G):
            s = s_ref[slot, i * M2:(i + 1) * M2, :]
            p = jnp.exp2(s - jnp.max(s, axis=-1, keepdims=True))
            inv = 1.0 / jnp.sum(p, axis=-1, keepdims=True)
            p_ref[slot, i * M2:(i + 1) * M2, :] = p.astype(BF16)
            inv_ref[slot, i * M2:(i + 1) * M2, :] = jnp.broadcast_to(inv, (M2, M2))

    def stage_values(g, slot):
        for i in range(G):
            r = g * G + i
            r0 = band_start(r)
            vb = v_ref[0, r0 * W:r0 * W + band, :]
            p = p_ref[slot, i * M2:(i + 1) * M2, :]
            o2 = (_dot(p[:, :band], vb) + _dot(p[:, band:], vc_ref[0])) * inv_ref[slot, i * M2:(i + 1) * M2, :]
            o = jnp.where(lane < HEAD_DIM, o2[:W], o2[W:])
            o_ref[0, r * W:(r + 1) * W, :] = o.astype(o_ref.dtype)

    stage_scores(0, 0)
    stage_scores(1, 1)
    stage_softmax(0)

    for g in range(n_groups - 2):
        stage_scores(g + 2, g % 2)
        stage_softmax((g + 1) % 2)
        stage_values(g, g % 2)
    stage_softmax((n_groups - 1) % 2)
    stage_values(n_groups - 2, (n_groups - 2) % 2)
    stage_values(n_groups - 1, (n_groups - 1) % 2)


def na_attend(u, u_ctx, bias):
    B, L, _ = u.shape
    Lc = u_ctx.shape[1]
    HP = N_HEADS // 2
    lw = 2 * HEAD_DIM
    oq, ok, ov = U_Q // lw, U_K // lw, U_V // lw
    return pl.pallas_call(
        functools.partial(_na_kernel, rows=L // GRID_W),
        grid=(HP, B),
        in_specs=[pl.BlockSpec((1, L, lw), lambda h, b: (b, 0, oq + h)),
                  pl.BlockSpec((1, L, lw), lambda h, b: (b, 0, ok + h)),
                  pl.BlockSpec((1, L, lw), lambda h, b: (b, 0, ov + h)),
                  pl.BlockSpec((1, Lc, lw), lambda h, b: (b, 0, ok + h)),
                  pl.BlockSpec((1, Lc, lw), lambda h, b: (b, 0, ov + h)),
                  pl.BlockSpec((1, 2 * NA_KH - 2, 2 * GRID_W, 2 * GRID_W), lambda h, b: (h, 0, 0, 0))],
        out_specs=pl.BlockSpec((1, L, lw), lambda h, b: (b, 0, h)),
        out_shape=jax.ShapeDtypeStruct((B, L, D), BF16),
        scratch_shapes=[pltpu.VMEM((2, NA_GROUP * lw, NA_KH * GRID_W + Lc), F32),
                        pltpu.VMEM((2, NA_GROUP * lw, NA_KH * GRID_W + Lc), BF16),
                        pltpu.VMEM((2, NA_GROUP * lw, lw), F32),
                        pltpu.VMEM((lw, L), BF16), pltpu.VMEM((lw, L - 2 * GRID_W), BF16)],
        compiler_params=_cparams(("parallel", "parallel")),
        name="na_attend",
    )(u, u, u, u_ctx, u_ctx, bias)


def _ctxattn_kernel(q_ref, k_ref, v_ref, o_ref):
    Lc = q_ref.shape[1]
    lane = lax.broadcasted_iota(jnp.int32, (Lc, 2 * HEAD_DIM), 1)
    q2 = (q_ref[0].astype(F32) * (HEAD_DIM ** -0.5)).astype(BF16)
    s = _nt(_head_stack(q2, lane), k_ref[0])
    m = jnp.max(s, axis=-1, keepdims=True)
    p = jnp.exp(s - m)
    p = (p * (1.0 / jnp.sum(p, axis=-1, keepdims=True))).astype(BF16)
    o2 = _dot(p, v_ref[0])
    o_ref[0] = jnp.where(lane < HEAD_DIM, o2[:Lc], o2[Lc:]).astype(o_ref.dtype)


def ctx_attend(u_ctx):
    B, Lc, _ = u_ctx.shape
    HP = N_HEADS // 2
    lw = 2 * HEAD_DIM
    oq, ok, ov = U_Q // lw, U_K // lw, U_V // lw
    return pl.pallas_call(
        _ctxattn_kernel,
        grid=(HP, B),
        in_specs=[pl.BlockSpec((1, Lc, lw), lambda h, b: (b, 0, oq + h)),
                  pl.BlockSpec((1, Lc, lw), lambda h, b: (b, 0, ok + h)),
                  pl.BlockSpec((1, Lc, lw), lambda h, b: (b, 0, ov + h))],
        out_specs=pl.BlockSpec((1, Lc, lw), lambda h, b: (b, 0, h)),
        out_shape=jax.ShapeDtypeStruct((B, Lc, D), BF16),
        compiler_params=_cparams(("parallel", "parallel")),
        name="ctx_attend",
    )(u_ctx, u_ctx, u_ctx)


MERGE_CK = 256


def _merge_kernel(cv_ref, cvp_ref, cvn_ref, cw_ref, ys_ref, yn_ref, g_ref, h_ref, gt_ref, wc_ref, ws_ref, wn_ref, wo_ref,
                  o_ref, *, seq_len):
    tm = h_ref.shape[1]
    row = lax.broadcasted_iota(jnp.int32, (tm, MERGE_CK), 0)
    pos = (pl.program_id(1) * tm + row) & (seq_len - 1)
    first, last = pos == 0, pos == seq_len - 1
    top, bottom = row == 0, row == tm - 1
    yc_proj = None
    for c in range(0, D, MERGE_CK):
        ch = slice(c, c + MERGE_CK)
        chc, chx = slice(D + c, D + c + MERGE_CK), slice(2 * D + c, 2 * D + c + MERGE_CK)
        p = cv_ref[0, :, chc].astype(F32) * cv_ref[0, :, chx].astype(F32)
        p_before = (cvp_ref[0, :, chc].astype(F32) * cvp_ref[0, :, chx].astype(F32))[HALO - 1:HALO]
        p_after = (cvn_ref[0, :, chc].astype(F32) * cvn_ref[0, :, chx].astype(F32))[0:1]
        prev = jnp.where(first, 0.0, jnp.where(top, p_before, pltpu.roll(p, 1, axis=0)))
        nxt = jnp.where(last, 0.0, jnp.where(bottom, p_after, pltpu.roll(p, tm - 1, axis=0)))
        w = cw_ref[:, ch]
        yc = (cv_ref[0, :, ch].astype(F32) * (prev * w[0:1] + p * w[1:2] + nxt * w[2:3])).astype(BF16)
        t = _dot(yc, wc_ref[ch, :])
        yc_proj = t if yc_proj is None else yc_proj + t

    g = _sigmoid(g_ref[0].astype(F32))
    m = (g[:, 0:D] * yc_proj
         + g[:, D:2 * D] * _dot(ys_ref[0], ws_ref[...])
         + g[:, 2 * D:3 * D] * _dot(yn_ref[0], wn_ref[...]))
    o_ref[0] = h_ref[0] + gt_ref[0] * _dot(m.astype(BF16), wo_ref[...])


def merge(u, ys, yn, h, gt, cw, wc, ws, wn, wo, tm, seq_len):
    B, L, _ = h.shape
    assert seq_len & (seq_len - 1) == 0 and (tm % seq_len == 0 or seq_len % tm == 0)
    nb, hb = tm // HALO, L // HALO
    tok = lambda: pl.BlockSpec((1, tm, D), lambda b, i: (b, i, 0))
    cvw = 3 * D
    assert U_CC == U_CB + D and U_CX == U_CB + 2 * D and U_CB % cvw == 0
    wsp = lambda: pl.BlockSpec((D, D), lambda b, i: (0, 0))
    return pl.pallas_call(
        functools.partial(_merge_kernel, seq_len=seq_len),
        grid=(B, L // tm),
        in_specs=[pl.BlockSpec((1, tm, cvw), lambda b, i: (b, i, U_CB // cvw)),
                  pl.BlockSpec((1, HALO, cvw), lambda b, i: (b, jnp.maximum(i * nb - 1, 0), U_CB // cvw)),
                  pl.BlockSpec((1, HALO, cvw), lambda b, i: (b, jnp.minimum((i + 1) * nb, hb - 1), U_CB // cvw)),
                  pl.BlockSpec((3, D), lambda b, i: (0, 0)),
                  tok(), tok(),
                  pl.BlockSpec((1, tm, 3 * D), lambda b, i: (b, i, U_GATE // (3 * D))),
                  tok(),
                  pl.BlockSpec((1, 1, D), lambda b, i: (b, 0, 0)),
                  wsp(), wsp(), wsp(), wsp()],
        out_specs=tok(),
        out_shape=jax.ShapeDtypeStruct((B, L, D), F32),
        compiler_params=_cparams(("parallel", "parallel")),
        name="merge",
    )(u, u, u, cw, ys, yn, u, h, gt, wc, ws, wn, wo)


MLP_TF = 1024


def _mlp_kernel(h_ref, sh_ref, sc_ref, gt_ref, nw_ref, fw_ref, w1_ref, w2_ref, o_ref, *, final):
    h = h_ref[0]
    xn = _norm_mod(h, nw_ref[...], sc_ref[0], sh_ref[0]).astype(BF16)
    acc = None
    for k in range(0, D_FF, MLP_TF):
        a = jnp.square(jnp.maximum(_dot(xn, w1_ref[:, k:k + MLP_TF]), 0.0)).astype(BF16)
        t = _dot(a, w2_ref[k:k + MLP_TF, :])
        acc = t if acc is None else acc + t
    o = h + gt_ref[0] * acc
    if final:
        ms = jnp.mean(o * o, axis=-1, keepdims=True)
        o = o * lax.rsqrt(ms + EPS) * fw_ref[...]
    o_ref[0] = o


def mlp(h, sh, sc, gt, nw, fw, w1, w2, tm, final):
    B, L, _ = h.shape
    vec = lambda: pl.BlockSpec((1, 1, D), lambda b, i: (b, 0, 0))
    par = lambda: pl.BlockSpec((1, D), lambda b, i: (0, 0))
    resident = lambda shape: pl.BlockSpec(shape, lambda b, i: (0, 0), pipeline_mode=pl.Buffered(1))
    return pl.pallas_call(
        functools.partial(_mlp_kernel, final=final),
        grid=(B, L // tm),
        in_specs=[pl.BlockSpec((1, tm, D), lambda b, i: (b, i, 0)),
                  vec(), vec(), vec(), par(), par(),
                  resident((D, D_FF)), resident((D_FF, D))],
        out_specs=pl.BlockSpec((1, tm, D), lambda b, i: (b, i, 0)),
        out_shape=jax.ShapeDtypeStruct((B, L, D), F32),
        compiler_params=_cparams(("parallel", "parallel")),
        name="mlp",
    )(h, sh, sc, gt, nw, fw, w1, w2)


def _rope_tables(L):
    t = jnp.arange(L, dtype=jnp.int32)
    row = (t // GRID_W).astype(F32)
    col = (t % GRID_W).astype(F32)
    half = HEAD_DIM // 2
    inv = ROPE_BASE ** (-jnp.arange(0, half, 2, dtype=F32) / half)
    ang_r = row[:, None] * inv
    ang_c = col[:, None] * inv
    ang = jnp.concatenate([ang_r, ang_r, ang_c, ang_c], axis=-1)
    cos = jnp.tile(jnp.cos(ang), (1, 2))
    sin = jnp.tile(jnp.sin(ang), (1, 2))
    even = ((jnp.arange(2 * HEAD_DIM) // (half // 2)) % 2 == 0)[None, :]
    return cos, jnp.where(even, -sin, 0.0), jnp.where(even, 0.0, sin)


def _rpb_kernel(r_ref, oh_ref, ok_ref, o_ref):
    val = _dot(jnp.concatenate(_split3(r_ref[...]), axis=1), oh_ref[...])
    o_ref[...] = jnp.where(ok_ref[...] > 0.0, LOG2E * val, -jnp.inf)


def _na_bias_table(rpb):
    H, NR, NC = rpb.shape
    W = GRID_W
    col = np.arange(W)
    col_start = np.clip(col - NA_KW // 2, 0, W - NA_KW)
    col_ok = (col[None, :] >= col_start[:, None]) & (col[None, :] < col_start[:, None] + NA_KW)
    dc_idx = np.clip(col[None, :] - col[:, None], -(NA_KW - 1), NA_KW - 1) + NA_KW - 1
    rows = jnp.transpose(rpb.reshape(H // 2, 2, NR, NC), (0, 2, 1, 3)).reshape(H * NR, NC)
    rows = jnp.pad(rows, ((0, 0), (0, DT_PAD - NC)))
    krow = jnp.arange(3 * DT_PAD, dtype=jnp.int32)[:, None] % DT_PAD
    onehot = (krow == jnp.asarray(dc_idx.reshape(1, W * W), jnp.int32)).astype(BF16)
    ok = jnp.asarray(col_ok.reshape(1, W * W), F32)
    tn = 1024
    tab = pl.pallas_call(
        _rpb_kernel,
        grid=(W * W // tn,),
        in_specs=[pl.BlockSpec((H * NR, DT_PAD), lambda j: (0, 0)),
                  pl.BlockSpec((3 * DT_PAD, tn), lambda j: (0, j)),
                  pl.BlockSpec((1, tn), lambda j: (0, j))],
        out_specs=pl.BlockSpec((H * NR, tn), lambda j: (0, j)),
        out_shape=jax.ShapeDtypeStruct((H * NR, W * W), F32),
        compiler_params=_cparams(("arbitrary",)),
        name="rpb_table",
    )(rows, onehot, ok)
    tab = tab.reshape(H // 2, NR, 2 * W, W)
    return jnp.concatenate([tab[:, :NR - 1], tab[:, 1:]], axis=-1)


def _prep_w_in(w_in):
    order = [(R_Q, D), (R_K, D), (R_V, D), (R_GATE, 3 * D), (R_CB, D), (R_CC, D), (R_CX, D), (R_Z, D), (R_XBC, XBC)]
    w = jnp.concatenate([w_in[:, o:o + n] for o, n in order], axis=1).astype(BF16)
    wdt = w_in[:, R_DT:R_DT + 2 * N_HEADS].astype(BF16)
    return w, jnp.pad(wdt, ((0, 0), (0, DT_PAD - 2 * N_HEADS))), wdt.T


def kernel(x, c, ctx, c_ctx, w_ada, b_ada, norm1_w, w_in, conv_mix_w, ssd_conv_w, ssd_conv_b, ssd_a_log, ssd_dt_bias,
           ssd_d, ssd_norm_w, na_rpb, w_br_conv, w_br_ssd, w_br_na, w_out, norm2_w, w_ff1, w_ff2, final_norm_w):
    B, L, _ = x.shape
    Lc = ctx.shape[1]
    depth = w_in.shape[0]
    cos, sa, sb = _rope_tables(L)
    n_mod = B + 1
    pad = (-n_mod) % 8
    c_rows = jnp.concatenate([c, c_ctx[None, :], jnp.zeros((pad, D), F32)], axis=0)
    zero_state = jnp.zeros((B, 2, SSD_GROUPS, SSD_STATE, D // SSD_GROUPS), F32)
    fw = final_norm_w.reshape(1, D)
    Tc = B * Lc
    tmc = min(1024, Tc)
    flat = lambda a: a.reshape(1, Tc, a.shape[-1])
    h, hc = x, flat(ctx)
    for l in range(depth):
        last = l == depth - 1
        mod = ada_mod(c_rows, w_ada[l], b_ada[l])
        m_lat = mod[:B].reshape(B, 1, 6, D)
        m_ctx = mod[B:B + 1].reshape(1, 1, 6, D)
        sh1, sc1, gt1, sh2, sc2, gt2 = (m_lat[:, :, i] for i in range(6))
        csh1, csc1, cgt1, csh2, csc2, cgt2 = (m_ctx[:, :, i] for i in range(6))
        w, wdt, wdtT = _prep_w_in(w_in[l])
        nw1 = norm1_w[l].reshape(1, D)
        conv_p = (ssd_conv_w[l], ssd_conv_b[l])
        u, dt, dtT = in_proj(h, sh1, sc1, nw1, w, wdt, wdtT, *conv_p, tm=1024, seq_len=L, rope_tables=(cos, sa, sb))
        uc, dtc, dtTc = in_proj(hc, csh1, csc1, nw1, w, wdt, wdtT, *conv_p, tm=tmc, seq_len=Lc)
        uc = uc.reshape(B, Lc, U_COLS)
        dtc = dtc.reshape(B, Lc, DT_PAD)
        dtTc = jnp.transpose(dtTc.reshape(2 * N_HEADS, B, Lc), (1, 0, 2))
        d_skip_e = jnp.repeat(ssd_d[l], HEAD_DIM, axis=1)
        ssd_p = (ssd_a_log[l], ssd_dt_bias[l], d_skip_e, ssd_norm_w[l])
        y_ssd_c, ctx_states = ssd_mix(uc, dtc, dtTc, zero_state, *ssd_p)
        y_ssd, _ = ssd_mix(u, dt, dtT, ctx_states, *ssd_p)
        y_na = na_attend(u, uc, _na_bias_table(na_rpb[l]))
        wb = [t[l].astype(BF16) for t in (w_br_conv, w_br_ssd, w_br_na, w_out)]
        w1, w2 = w_ff1[l].astype(BF16), w_ff2[l].astype(BF16)
        nw2 = norm2_w[l].reshape(1, D)
        h = merge(u, y_ssd, y_na, h, gt1, conv_mix_w[l], *wb, tm=512, seq_len=L)
        h = mlp(h, sh2, sc2, gt2, nw2, fw, w1, w2, tm=1024, final=last)
        if not last:
            y_na_c = ctx_attend(uc)
            hc = merge(flat(uc), flat(y_ssd_c), flat(y_na_c), hc, cgt1, conv_mix_w[l], *wb, tm=min(512, Tc), seq_len=Lc)
            hc = mlp(hc, csh2, csc2, cgt2, nw2, fw, w1, w2, tm=tmc, final=False)
    return h
```

```python
import functools
import math

import jax
import jax.numpy as jnp
import numpy as np
from jax import lax
from jax.experimental import pallas as pl
from jax.experimental.pallas import tpu as pltpu

F32 = jnp.float32
BF16 = jnp.bfloat16
HIGHEST = lax.Precision.HIGHEST
LOG2E = math.log2(math.e)

D = 1024
EPS = 1e-6
GRID_W = 64
N_HEADS = 16
HEAD_DIM = 64
SSD_GROUPS = 2
SSD_STATE = 128
CHUNK = 128
NA_KH = 8
NA_KW = 16
ROPE_BASE = 10000.0
D_FF = 4 * D
XBC = D + 2 * SSD_GROUPS * SSD_STATE
R_CB, R_CC, R_CX, R_Z, R_XBC, R_DT, R_Q, R_K, R_V, R_GATE = (
    0, 1024, 2048, 3072, 4096, 5632, 5664, 6688, 7712, 8736)

LANES = 128
HALO = 16
VMEM_LIMIT = 56 * 1024 * 1024

U_Q, U_K, U_V, U_GATE, U_CB, U_CC, U_CX, U_Z, U_XS, U_BC, U_COLS = (
    0, 1024, 2048, 3072, 6144, 7168, 8192, 9216, 10240, 11264, 11776)
U_TILE = 23 * LANES
U_SUB = 512
DT_PAD = LANES
TM_PROJ = 1024
TM_MERGE = 512
ADA_TN = 1536
SSD_SUB = 8


def _cparams(sem):
    return pltpu.CompilerParams(dimension_semantics=sem, vmem_limit_bytes=VMEM_LIMIT)


def _nt(a, b):
    return lax.dot_general(a, b, (((1,), (1,)), ((), ())), preferred_element_type=F32)


def _tn(a, b):
    return lax.dot_general(a, b, (((0,), (0,)), ((), ())), preferred_element_type=F32)


def _dot(a, b):
    return jnp.dot(a, b, preferred_element_type=F32)


def _dot_hi(a, b):
    return jnp.dot(a, b, preferred_element_type=F32, precision=HIGHEST)


def _sigmoid(x):
    return 0.5 * jnp.tanh(0.5 * x) + 0.5


def _silu(x):
    return x * _sigmoid(x)


def _norm_mod(x, nw, sc, sh):
    ms = jnp.mean(x * x, axis=-1, keepdims=True)
    y = x * lax.rsqrt(ms + EPS) * nw
    return y * (1.0 + sc) + sh


def _ada_kernel(c_ref, w_ref, b_ref, o_ref):
    o_ref[...] = _dot_hi(_silu(c_ref[...]), w_ref[...]) + b_ref[...]


def ada_mod(c_rows, w_ada, b_ada):
    rows = c_rows.shape[0]
    tn = ADA_TN
    return pl.pallas_call(
        _ada_kernel,
        grid=(6 * D // tn,),
        in_specs=[pl.BlockSpec((rows, D), lambda j: (0, 0)),
                  pl.BlockSpec((D, tn), lambda j: (0, j)),
                  pl.BlockSpec((1, tn), lambda j: (0, j))],
        out_specs=pl.BlockSpec((rows, tn), lambda j: (0, j)),
        out_shape=jax.ShapeDtypeStruct((rows, 6 * D), F32),
        compiler_params=_cparams(("arbitrary",)),
        name="ada_mod",
    )(c_rows, w_ada, b_ada.reshape(1, 6 * D))


def _rope_slab(x, cos, sa, sb):
    return x * cos + pltpu.roll(x, 2 * HEAD_DIM - 16, axis=1) * sa + pltpu.roll(x, 16, axis=1) * sb


def _inproj_kernel(x_ref, sh_ref, sc_ref, nw_ref, w_ref, wdt_ref, wdtT_ref, cw_ref, cb_ref, *rest, rope, seq_len):
    if rope:
        cos_ref, sa_ref, sb_ref, u_ref, dt_ref, dtT_ref, xn_ref = rest
    else:
        u_ref, dt_ref, dtT_ref, xn_ref = rest
    j, i = pl.program_id(1), pl.program_id(2)
    ni, tm = xn_ref.shape[0], xn_ref.shape[1]
    n_tiles = U_COLS // U_TILE
    conv_off = U_XS - (n_tiles - 1) * U_TILE

    def normalise():
        xb = _norm_mod(x_ref[0], nw_ref[...], sc_ref[0], sh_ref[0]).astype(BF16)
        xn_ref[i] = xb
        dt_ref[0] = _dot(xb, wdt_ref[...])
        dtT_ref[0] = _nt(wdtT_ref[...], xb)
        return xb

    def tile(rope_cols, xb=None, end=U_TILE):
        for off in range(0, end, U_SUB):
            wd = min(U_SUB, end - off)
            r = _dot(xn_ref[i] if xb is None else xb, w_ref[:, off:off + wd])
            if off < rope_cols:
                scale = LOG2E * HEAD_DIM ** -0.5 if off < U_K else 1.0
                lw = 2 * HEAD_DIM
                r = jnp.concatenate(
                    [_rope_slab(r[:, s:s + lw], cos_ref[...], sa_ref[...], sb_ref[...]) * scale
                     for s in range(0, wd, lw)], axis=1)
            u_ref[0, :, off:off + wd] = r.astype(u_ref.dtype)

    def conv_columns():
        before = xn_ref[jnp.maximum(i - 1, 0), tm - HALO:tm, :]
        after = xn_ref[jnp.minimum(i + 1, ni - 1), 0:HALO, :]
        for off in range(conv_off, U_TILE, U_SUB):
            wd = min(U_SUB, U_TILE - off)
            ws = w_ref[:, off:off + wd]
            r = _dot(xn_ref[i], ws)
            r_before = _dot(before, ws)[HALO - 1:HALO]
            r_after = _dot(after, ws)[0:1]
            row = lax.broadcasted_iota(jnp.int32, (tm, wd), 0)
            if seq_len >= tm:
                r_before = jnp.where((i * tm) & (seq_len - 1) == 0, 0.0, r_before)
                r_after = jnp.where(((i + 1) * tm) & (seq_len - 1) == 0, 0.0, r_after)
                prev = jnp.where(row == 0, r_before, pltpu.roll(r, 1, axis=0))
                nxt = jnp.where(row == tm - 1, r_after, pltpu.roll(r, tm - 1, axis=0))
            else:
                pos = (i * tm + row) & (seq_len - 1)
                prev = jnp.where(pos == 0, 0.0, jnp.where(row == 0, r_before, pltpu.roll(r, 1, axis=0)))
                nxt = jnp.where(pos == seq_len - 1, 0.0,
                                jnp.where(row == tm - 1, r_after, pltpu.roll(r, tm - 1, axis=0)))
            c0 = off - conv_off
            cw = cw_ref[:, c0:c0 + wd]
            h = prev * cw[0:1] + r * cw[1:2] + nxt * cw[2:3] + cb_ref[:, c0:c0 + wd]
            u_ref[0, :, off:off + wd] = (h + h * jnp.tanh(h)).astype(u_ref.dtype)

    @pl.when(j == 0)
    def _():
        tile(U_V if rope else 0, normalise())

    @pl.when(jnp.logical_and(j > 0, j < n_tiles - 1))
    def _():
        tile(0)

    @pl.when(j == n_tiles - 1)
    def _():
        tile(0, end=conv_off)
        conv_columns()


def in_proj(x, sh, sc, nw, w, wdt, wdtT, conv_w, conv_b, tm, seq_len, rope_tables=None):
    B, L, _ = x.shape
    assert seq_len & (seq_len - 1) == 0 and (tm % seq_len == 0 or seq_len % tm == 0)
    assert U_COLS - U_XS == XBC and (U_XS - (U_COLS // U_TILE - 1) * U_TILE) % LANES == 0
    tn = U_TILE
    ni = L // tm
    grid = (B, U_COLS // tn, ni)
    rope = rope_tables is not None

    def first(j, i):
        return jnp.where(j == 0, i, ni - 1)

    rope_specs = [pl.BlockSpec((tm, 2 * HEAD_DIM), lambda b, j, i: (first(j, i), 0))] * 3 if rope else []
    return pl.pallas_call(
        functools.partial(_inproj_kernel, rope=rope, seq_len=seq_len),
        grid=grid,
        in_specs=[pl.BlockSpec((1, tm, D), lambda b, j, i: (b, first(j, i), 0)),
                  pl.BlockSpec((1, 1, D), lambda b, j, i: (b, 0, 0)),
                  pl.BlockSpec((1, 1, D), lambda b, j, i: (b, 0, 0)),
                  pl.BlockSpec((1, D), lambda b, j, i: (0, 0)),
                  pl.BlockSpec((D, tn), lambda b, j, i: (0, j)),
                  pl.BlockSpec((D, DT_PAD), lambda b, j, i: (0, 0)),
                  pl.BlockSpec((2 * N_HEADS, D), lambda b, j, i: (0, 0)),
                  pl.BlockSpec((3, XBC), lambda b, j, i: (0, 0)),
                  pl.BlockSpec((1, XBC), lambda b, j, i: (0, 0))] + rope_specs,
        out_specs=[pl.BlockSpec((1, tm, tn), lambda b, j, i: (b, i, j)),
                   pl.BlockSpec((1, tm, DT_PAD), lambda b, j, i: (b, first(j, i), 0)),
                   pl.BlockSpec((1, 2 * N_HEADS, tm), lambda b, j, i: (b, 0, first(j, i)))],
        out_shape=[jax.ShapeDtypeStruct((B, L, U_COLS), BF16),
                   jax.ShapeDtypeStruct((B, L, DT_PAD), F32),
                   jax.ShapeDtypeStruct((B, 2 * N_HEADS, L), F32)],
        scratch_shapes=[pltpu.VMEM((ni, tm, D), BF16)],
        compiler_params=_cparams(("parallel", "arbitrary", "arbitrary")),
        name="in_proj",
    )(x, sh, sc, nw, w, wdt, wdtT, 0.5 * conv_w, 0.5 * conv_b.reshape(1, XBC), *(rope_tables if rope else ()))


def _split3(x):
    hi = x.astype(BF16)
    r1 = x - hi.astype(F32)
    mid = r1.astype(BF16)
    lo = (r1 - mid.astype(F32)).astype(BF16)
    return hi, mid, lo


def _head_stack(x2, lane):
    zero = jnp.zeros_like(x2)
    return jnp.concatenate([jnp.where(lane < HEAD_DIM, x2, zero), jnp.where(lane >= HEAD_DIM, x2, zero)], axis=0)


def _ssd_chunk(d, sub, rows, xs_ref, bc_ref, z_ref, dt_ref, dtT_ref, alr_ref, alc_ref, dbr_ref, dbc_ref, dsk_ref,
               nw_ref, ex_ref, y_ref, yacc_ref, st_ref):
    H = N_HEADS
    blk = slice(sub * CHUNK, (sub + 1) * CHUNK)
    xsb = xs_ref[0, blk, :]
    xs = xsb.astype(F32)
    bc = bc_ref[0, blk, :]
    dt_c = jax.nn.softplus(dt_ref[0, blk, :] + dbr_ref[...])
    a_c = dt_c * (-LOG2E * jnp.exp(alr_ref[...]))
    dtT_c = jax.nn.softplus(dtT_ref[0, :, blk] + dbc_ref[...])
    aT_c = dtT_c * (-LOG2E * jnp.exp(alc_ref[...]))

    ri = lax.broadcasted_iota(jnp.int32, (CHUNK, CHUNK), 0)
    ci = lax.broadcasted_iota(jnp.int32, (CHUNK, CHUNK), 1)
    tri = (ri >= ci) if d == 0 else (ri <= ci)
    trib = tri.astype(BF16)
    tribT = ((ri <= ci) if d == 0 else (ri >= ci)).astype(BF16)
    cs = _dot(jnp.concatenate([trib] * 3, axis=1), jnp.concatenate(_split3(a_c), axis=0))
    csT = _dot(jnp.concatenate(_split3(aT_c), axis=1), jnp.concatenate([tribT] * 3, axis=0))
    last = cs[CHUNK - 1:CHUNK] if d == 0 else cs[0:1]
    w_c = dt_c * jnp.exp2(last - cs)
    ein_c = jnp.exp2(cs)
    ex = ex_ref[d]
    w_e = _dot(jnp.concatenate(_split3(w_c)[:2], axis=1), ex)
    ein_e = _dot(jnp.concatenate(_split3(ein_c)[:2], axis=1), ex)
    cd_e = ein_e[CHUNK - 1:CHUNK] if d == 0 else ein_e[0:1]
    Xdec = (xs * w_e).astype(BF16)
    rowT = csT - jnp.log2(dtT_c)
    lane = lax.broadcasted_iota(jnp.int32, (CHUNK, 2 * HEAD_DIM), 1)

    ys = []
    GW = D // SSD_GROUPS
    for g in range(SSD_GROUPS):
        Bg = bc[:, g * SSD_STATE:(g + 1) * SSD_STATE]
        Cg = bc[:, (SSD_GROUPS + g) * SSD_STATE:(SSD_GROUPS + g + 1) * SSD_STATE]
        S = _nt(Cg, Bg)
        st = st_ref[d, g]
        y_off = _dot(Cg, st.astype(BF16)) * ein_e[:, g * GW:(g + 1) * GW]
        y_diag = []
        for p in range(GW // (2 * HEAD_DIM)):
            hA = g * (H // SSD_GROUPS) + 2 * p
            Ms = []
            for h in (hA, hA + 1):
                col = cs[:, d * H + h:d * H + h + 1]
                rowv = rowT[d * H + h:d * H + h + 1, :]
                Ms.append((S * jnp.exp2(jnp.where(tri, col - rowv, -jnp.inf))).astype(BF16))
            Mcat = jnp.concatenate(Ms, axis=1)
            Xp = xsb[:, hA * HEAD_DIM:(hA + 2) * HEAD_DIM]
            y_diag.append(_dot(Mcat, _head_stack(Xp, lane)))
        ys.append(y_off + jnp.concatenate(y_diag, axis=1))
        st_ref[d, g] = st * cd_e[:, g * GW:(g + 1) * GW] + _tn(Bg, Xdec[:, g * GW:(g + 1) * GW])
    y = jnp.concatenate(ys, axis=1)

    if d == 0:
        yacc_ref[rows, :] = y
    else:
        ysum = yacc_ref[rows, :] + y + xs * (dsk_ref[0:1] + dsk_ref[1:2])
        yg = ysum * _silu(z_ref[0, blk, :].astype(F32))
        parts = []
        for g in range(SSD_GROUPS):
            v = yg[:, g * GW:(g + 1) * GW]
            ms = jnp.mean(v * v, axis=-1, keepdims=True)
            parts.append(v * lax.rsqrt(ms + EPS) * nw_ref[:, g * GW:(g + 1) * GW])
        y_ref[0, blk, :] = jnp.concatenate(parts, axis=1).astype(y_ref.dtype)


def _ssd_kernel(xs_ref, bc_ref, z_ref, dt_ref, dtT_ref, h0_ref, alr_ref, alc_ref, dbr_ref, dbc_ref, dsk_ref, nw_ref,
                ex_ref, y_ref, sto_ref, yacc_ref, st_ref, *, ns, nsub):
    t = pl.program_id(1)

    @pl.when(t == 0)
    def _():
        st_ref[...] = h0_ref[0]

    args = (xs_ref, bc_ref, z_ref, dt_ref, dtT_ref, alr_ref, alc_ref, dbr_ref, dbc_ref, dsk_ref, nw_ref,
            ex_ref, y_ref, yacc_ref, st_ref)

    def seq_rows(step, sub):
        return pl.ds(pl.multiple_of((step * nsub + sub) * CHUNK, CHUNK), CHUNK)

    @pl.when(t < ns)
    def _():
        for sub in range(nsub):
            _ssd_chunk(0, sub, seq_rows(t, sub), *args)

    @pl.when(t >= ns)
    def _():
        for sub in reversed(range(nsub)):
            _ssd_chunk(1, sub, seq_rows(2 * ns - 1 - t, sub), *args)

    @pl.when(t == 2 * ns - 1)
    def _():
        sto_ref[0] = st_ref[...]


def ssd_mix(u, dt, dtT, h0, a_log, dt_bias, d_skip_e, norm_w):
    B, L, _ = u.shape
    nsub = min(SSD_SUB, L // CHUNK)
    RB = nsub * CHUNK
    ns = L // RB
    H2 = 2 * N_HEADS

    def chunk(t):
        return jnp.where(t < ns, t, 2 * ns - 1 - t)

    def late(t):
        return jnp.where(t < ns, ns - 1, 2 * ns - 1 - t)

    st_shape = (2, SSD_GROUPS, SSD_STATE, D // SSD_GROUPS)
    st_spec = pl.BlockSpec((1,) + st_shape, lambda b, t: (b, 0, 0, 0, 0))
    small = lambda shape: pl.BlockSpec(shape, lambda b, t: (0,) * len(shape))
    krow = jnp.arange(2 * DT_PAD, dtype=jnp.int32)[:, None] % DT_PAD
    head = jnp.arange(D, dtype=jnp.int32)[None, :] // HEAD_DIM
    ex = jnp.stack([krow == d * N_HEADS + head for d in range(2)]).astype(BF16)
    lane_pad = lambda v: jnp.pad(v.reshape(1, H2), ((0, 0), (0, DT_PAD - H2)))
    return pl.pallas_call(
        functools.partial(_ssd_kernel, ns=ns, nsub=nsub),
        grid=(B, 2 * ns),
        in_specs=[pl.BlockSpec((1, RB, D), lambda b, t: (b, chunk(t), U_XS // D)),
                  pl.BlockSpec((1, RB, XBC - D), lambda b, t: (b, chunk(t), U_BC // (XBC - D))),
                  pl.BlockSpec((1, RB, D), lambda b, t: (b, late(t), U_Z // D)),
                  pl.BlockSpec((1, RB, DT_PAD), lambda b, t: (b, chunk(t), 0)),
                  pl.BlockSpec((1, H2, RB), lambda b, t: (b, 0, chunk(t))),
                  st_spec,
                  small((1, DT_PAD)), small((H2, 1)), small((1, DT_PAD)), small((H2, 1)),
                  small((2, D)), small((1, D)), small((2, 2 * DT_PAD, D))],
        out_specs=[pl.BlockSpec((1, RB, D), lambda b, t: (b, late(t), 0)),
                   st_spec],
        out_shape=[jax.ShapeDtypeStruct((B, L, D), BF16),
                   jax.ShapeDtypeStruct((B,) + st_shape, F32)],
        scratch_shapes=[pltpu.VMEM((L, D), F32), pltpu.VMEM(st_shape, F32)],
        compiler_params=_cparams(("parallel", "arbitrary")),
        name="ssd_mix",
    )(u, u, u, dt, dtT, h0,
      lane_pad(a_log), a_log.reshape(H2, 1), lane_pad(dt_bias), dt_bias.reshape(H2, 1),
      d_skip_e, norm_w.reshape(1, D), ex)


NA_GROUP = 1


def _na_kernel(q_ref, k_ref, v_ref, kc_ref, vc_ref, bias_ref, o_ref, s_ref, p_ref, inv_ref, kt_ref, kto_ref, *, rows):
    W = GRID_W
    band = NA_KH * W
    lane = lax.broadcasted_iota(jnp.int32, (W, 2 * HEAD_DIM), 1)
    G = NA_GROUP
    n_groups = rows // G
    M2 = 2 * W
    kt_ref[...] = k_ref[0].T
    kto_ref[...] = k_ref[0, W:W + kto_ref.shape[1], :].T
    kct = kc_ref[0].T

    def band_start(r):
        return min(max(r - NA_KH // 2, 0), rows - NA_KH)

    def stage_scores(g, slot):
        for i in range(G):
            r = g * G + i
            r0 = band_start(r)
            qs = _head_stack(q_ref[0, r * W:(r + 1) * W, :], lane)
            ktb, t0 = (kt_ref, r0 * W) if r0 % 2 == 0 else (kto_ref, (r0 - 1) * W)
            base = r0 - r + NA_KH - 1
            for jp in range(NA_KH // 2):
                cols = slice(jp * M2, (jp + 1) * M2)
                s_ref[slot, i * M2:(i + 1) * M2, cols] = (_dot(qs, ktb[:, t0 + jp * M2:t0 + (jp + 1) * M2])
                                                          + bias_ref[0, base + 2 * jp])
            s_ref[slot, i * M2:(i + 1) * M2, band:] = _dot(qs, kct)

    def stage_softmax(slot):
        for i in range(G):
            s = s_ref[slot, i * M2:(i + 1) * M2, :]
            p = jnp.exp2(s - jnp.max(s, axis=-1, keepdims=True))
            inv = 1.0 / jnp.sum(p, axis=-1, keepdims=True)
            p_ref[slot, i * M2:(i + 1) * M2, :] = p.astype(BF16)
            inv_ref[slot, i * M2:(i + 1) * M2, :] = jnp.broadcast_to(inv, (M2, M2))

    def stage_values(g, slot):
        for i in range(G):
            r = g * G + i
            r0 = band_start(r)
            vb = v_ref[0, r0 * W:r0 * W + band, :]
            p = p_ref[slot, i * M2:(i + 1) * M2, :]
            o2 = (_dot(p[:, :band], vb) + _dot(p[:, band:], vc_ref[0])) * inv_ref[slot, i * M2:(i + 1) * M2, :]
            o = jnp.where(lane < HEAD_DIM, o2[:W], o2[W:])
            o_ref[0, r * W:(r + 1) * W, :] = o.astype(o_ref.dtype)

    stage_scores(0, 0)
    stage_scores(1, 1)
    stage_softmax(0)

    for g in range(n_groups - 2):
        stage_scores(g + 2, g % 2)
        stage_softmax((g + 1) % 2)
        stage_values(g, g % 2)
    stage_softmax((n_groups - 1) % 2)
    stage_values(n_groups - 2, (n_groups - 2) % 2)
    stage_values(n_groups - 1, (n_groups - 1) % 2)


def na_attend(u, u_ctx, bias):
    B, L, _ = u.shape
    Lc = u_ctx.shape[1]
    HP = N_HEADS // 2
    lw = 2 * HEAD_DIM
    oq, ok, ov = U_Q // lw, U_K // lw, U_V // lw
    return pl.pallas_call(
        functools.partial(_na_kernel, rows=L // GRID_W),
        grid=(HP, B),
        in_specs=[pl.BlockSpec((1, L, lw), lambda h, b: (b, 0, oq + h)),
                  pl.BlockSpec((1, L, lw), lambda h, b: (b, 0, ok + h)),
                  pl.BlockSpec((1, L, lw), lambda h, b: (b, 0, ov + h)),
                  pl.BlockSpec((1, Lc, lw), lambda h, b: (b, 0, ok + h)),
                  pl.BlockSpec((1, Lc, lw), lambda h, b: (b, 0, ov + h)),
                  pl.BlockSpec((1, 2 * NA_KH - 2, 2 * GRID_W, 2 * GRID_W), lambda h, b: (h, 0, 0, 0))],
        out_specs=pl.BlockSpec((1, L, lw), lambda h, b: (b, 0, h)),
        out_shape=jax.ShapeDtypeStruct((B, L, D), BF16),
        scratch_shapes=[pltpu.VMEM((2, NA_GROUP * lw, NA_KH * GRID_W + Lc), F32),
                        pltpu.VMEM((2, NA_GROUP * lw, NA_KH * GRID_W + Lc), BF16),
                        pltpu.VMEM((2, NA_GROUP * lw, lw), F32),
                        pltpu.VMEM((lw, L), BF16), pltpu.VMEM((lw, L - 2 * GRID_W), BF16)],
        compiler_params=_cparams(("parallel", "parallel")),
        name="na_attend",
    )(u, u, u, u_ctx, u_ctx, bias)


def _ctxattn_kernel(q_ref, k_ref, v_ref, o_ref):
    Lc = q_ref.shape[1]
    lane = lax.broadcasted_iota(jnp.int32, (Lc, 2 * HEAD_DIM), 1)
    q2 = (q_ref[0].astype(F32) * (HEAD_DIM ** -0.5)).astype(BF16)
    s = _nt(_head_stack(q2, lane), k_ref[0])
    m = jnp.max(s, axis=-1, keepdims=True)
    p = jnp.exp(s - m)
    p = (p * (1.0 / jnp.sum(p, axis=-1, keepdims=True))).astype(BF16)
    o2 = _dot(p, v_ref[0])
    o_ref[0] = jnp.where(lane < HEAD_DIM, o2[:Lc], o2[Lc:]).astype(o_ref.dtype)


def ctx_attend(u_ctx):
    B, Lc, _ = u_ctx.shape
    HP = N_HEADS // 2
    lw = 2 * HEAD_DIM
    oq, ok, ov = U_Q // lw, U_K // lw, U_V // lw
    return pl.pallas_call(
        _ctxattn_kernel,
        grid=(HP, B),
        in_specs=[pl.BlockSpec((1, Lc, lw), lambda h, b: (b, 0, oq + h)),
                  pl.BlockSpec((1, Lc, lw), lambda h, b: (b, 0, ok + h)),
                  pl.BlockSpec((1, Lc, lw), lambda h, b: (b, 0, ov + h))],
        out_specs=pl.BlockSpec((1, Lc, lw), lambda h, b: (b, 0, h)),
        out_shape=jax.ShapeDtypeStruct((B, Lc, D), BF16),
        compiler_params=_cparams(("parallel", "parallel")),
        name="ctx_attend",
    )(u_ctx, u_ctx, u_ctx)


MERGE_CK = 256


def _merge_kernel(cv_ref, cvp_ref, cvn_ref, cw_ref, ys_ref, yn_ref, g_ref, h_ref, gt_ref, wc_ref, ws_ref, wn_ref, wo_ref,
                  o_ref, *, seq_len):
    tm = h_ref.shape[1]
    row = lax.broadcasted_iota(jnp.int32, (tm, MERGE_CK), 0)
    pos = (pl.program_id(1) * tm + row) & (seq_len - 1)
    first, last = pos == 0, pos == seq_len - 1
    top, bottom = row == 0, row == tm - 1
    yc_proj = None
    for c in range(0, D, MERGE_CK):
        ch = slice(c, c + MERGE_CK)
        chc, chx = slice(D + c, D + c + MERGE_CK), slice(2 * D + c, 2 * D + c + MERGE_CK)
        p = cv_ref[0, :, chc].astype(F32) * cv_ref[0, :, chx].astype(F32)
        p_before = (cvp_ref[0, :, chc].astype(F32) * cvp_ref[0, :, chx].astype(F32))[HALO - 1:HALO]
        p_after = (cvn_ref[0, :, chc].astype(F32) * cvn_ref[0, :, chx].astype(F32))[0:1]
        prev = jnp.where(first, 0.0, jnp.where(top, p_before, pltpu.roll(p, 1, axis=0)))
        nxt = jnp.where(last, 0.0, jnp.where(bottom, p_after, pltpu.roll(p, tm - 1, axis=0)))
        w = cw_ref[:, ch]
        yc = (cv_ref[0, :, ch].astype(F32) * (prev * w[0:1] + p * w[1:2] + nxt * w[2:3])).astype(BF16)
        t = _dot(yc, wc_ref[ch, :])
        yc_proj = t if yc_proj is None else yc_proj + t

    g = _sigmoid(g_ref[0].astype(F32))
    m = (g[:, 0:D] * yc_proj
         + g[:, D:2 * D] * _dot(ys_ref[0], ws_ref[...])
         + g[:, 2 * D:3 * D] * _dot(yn_ref[0], wn_ref[...]))
    o_ref[0] = h_ref[0] + gt_ref[0] * _dot(m.astype(BF16), wo_ref[...])


def merge(u, ys, yn, h, gt, cw, wc, ws, wn, wo, tm, seq_len):
    B, L, _ = h.shape
    assert seq_len & (seq_len - 1) == 0 and (tm % seq_len == 0 or seq_len % tm == 0)
    nb, hb = tm // HALO, L // HALO
    tok = lambda: pl.BlockSpec((1, tm, D), lambda b, i: (b, i, 0))
    cvw = 3 * D
    assert U_CC == U_CB + D and U_CX == U_CB + 2 * D and U_CB % cvw == 0
    wsp = lambda: pl.BlockSpec((D, D), lambda b, i: (0, 0))
    return pl.pallas_call(
        functools.partial(_merge_kernel, seq_len=seq_len),
        grid=(B, L // tm),
        in_specs=[pl.BlockSpec((1, tm, cvw), lambda b, i: (b, i, U_CB // cvw)),
                  pl.BlockSpec((1, HALO, cvw), lambda b, i: (b, jnp.maximum(i * nb - 1, 0), U_CB // cvw)),
                  pl.BlockSpec((1, HALO, cvw), lambda b, i: (b, jnp.minimum((i + 1) * nb, hb - 1), U_CB // cvw)),
                  pl.BlockSpec((3, D), lambda b, i: (0, 0)),
                  tok(), tok(),
                  pl.BlockSpec((1, tm, 3 * D), lambda b, i: (b, i, U_GATE // (3 * D))),
                  tok(),
                  pl.BlockSpec((1, 1, D), lambda b, i: (b, 0, 0)),
                  wsp(), wsp(), wsp(), wsp()],
        out_specs=tok(),
        out_shape=jax.ShapeDtypeStruct((B, L, D), F32),
        compiler_params=_cparams(("parallel", "parallel")),
        name="merge",
    )(u, u, u, cw, ys, yn, u, h, gt, wc, ws, wn, wo)


MLP_TF = 1024


def _mlp_kernel(h_ref, sh_ref, sc_ref, gt_ref, nw_ref, fw_ref, w1_ref, w2_ref, o_ref, *, final):
    h = h_ref[0]
    xn = _norm_mod(h, nw_ref[...], sc_ref[0], sh_ref[0]).astype(BF16)
    acc = None
    for k in range(0, D_FF, MLP_TF):
        a = jnp.square(jnp.maximum(_dot(xn, w1_ref[:, k:k + MLP_TF]), 0.0)).astype(BF16)
        t = _dot(a, w2_ref[k:k + MLP_TF, :])
        acc = t if acc is None else acc + t
    o = h + gt_ref[0] * acc
    if final:
        ms = jnp.mean(o * o, axis=-1, keepdims=True)
        o = o * lax.rsqrt(ms + EPS) * fw_ref[...]
    o_ref[0] = o


def mlp(h, sh, sc, gt, nw, fw, w1, w2, tm, final):
    B, L, _ = h.shape
    vec = lambda: pl.BlockSpec((1, 1, D), lambda b, i: (b, 0, 0))
    par = lambda: pl.BlockSpec((1, D), lambda b, i: (0, 0))
    resident = lambda shape: pl.BlockSpec(shape, lambda b, i: (0, 0), pipeline_mode=pl.Buffered(1))
    return pl.pallas_call(
        functools.partial(_mlp_kernel, final=final),
        grid=(B, L // tm),
        in_specs=[pl.BlockSpec((1, tm, D), lambda b, i: (b, i, 0)),
                  vec(), vec(), vec(), par(), par(),
                  resident((D, D_FF)), resident((D_FF, D))],
        out_specs=pl.BlockSpec((1, tm, D), lambda b, i: (b, i, 0)),
        out_shape=jax.ShapeDtypeStruct((B, L, D), F32),
        compiler_params=_cparams(("parallel", "parallel")),
        name="mlp",
    )(h, sh, sc, gt, nw, fw, w1, w2)


def _rope_tables(L):
    t = jnp.arange(L, dtype=jnp.int32)
    row = (t // GRID_W).astype(F32)
    col = (t % GRID_W).astype(F32)
    half = HEAD_DIM // 2
    inv = ROPE_BASE ** (-jnp.arange(0, half, 2, dtype=F32) / half)
    ang_r = row[:, None] * inv
    ang_c = col[:, None] * inv
    ang = jnp.concatenate([ang_r, ang_r, ang_c, ang_c], axis=-1)
    cos = jnp.tile(jnp.cos(ang), (1, 2))
    sin = jnp.tile(jnp.sin(ang), (1, 2))
    even = ((jnp.arange(2 * HEAD_DIM) // (half // 2)) % 2 == 0)[None, :]
    return cos, jnp.where(even, -sin, 0.0), jnp.where(even, 0.0, sin)


def _rpb_kernel(r_ref, oh_ref, ok_ref, o_ref):
    val = _dot(jnp.concatenate(_split3(r_ref[...]), axis=1), oh_ref[...])
    o_ref[...] = jnp.where(ok_ref[...] > 0.0, LOG2E * val, -jnp.inf)


def _na_bias_table(rpb):
    H, NR, NC = rpb.shape
    W = GRID_W
    col = np.arange(W)
    col_start = np.clip(col - NA_KW // 2, 0, W - NA_KW)
    col_ok = (col[None, :] >= col_start[:, None]) & (col[None, :] < col_start[:, None] + NA_KW)
    dc_idx = np.clip(col[None, :] - col[:, None], -(NA_KW - 1), NA_KW - 1) + NA_KW - 1
    rows = jnp.transpose(rpb.reshape(H // 2, 2, NR, NC), (0, 2, 1, 3)).reshape(H * NR, NC)
    rows = jnp.pad(rows, ((0, 0), (0, DT_PAD - NC)))
    krow = jnp.arange(3 * DT_PAD, dtype=jnp.int32)[:, None] % DT_PAD
    onehot = (krow == jnp.asarray(dc_idx.reshape(1, W * W), jnp.int32)).astype(BF16)
    ok = jnp.asarray(col_ok.reshape(1, W * W), F32)
    tn = 1024
    tab = pl.pallas_call(
        _rpb_kernel,
        grid=(W * W // tn,),
        in_specs=[pl.BlockSpec((H * NR, DT_PAD), lambda j: (0, 0)),
                  pl.BlockSpec((3 * DT_PAD, tn), lambda j: (0, j)),
                  pl.BlockSpec((1, tn), lambda j: (0, j))],
        out_specs=pl.BlockSpec((H * NR, tn), lambda j: (0, j)),
        out_shape=jax.ShapeDtypeStruct((H * NR, W * W), F32),
        compiler_params=_cparams(("arbitrary",)),
        name="rpb_table",
    )(rows, onehot, ok)
    tab = tab.reshape(H // 2, NR, 2 * W, W)
    return jnp.concatenate([tab[:, :NR - 1], tab[:, 1:]], axis=-1)


def _prep_w_in(w_in):
    order = [(R_Q, D), (R_K, D), (R_V, D), (R_GATE, 3 * D), (R_CB, D), (R_CC, D), (R_CX, D), (R_Z, D), (R_XBC, XBC)]
    w = jnp.concatenate([w_in[:, o:o + n] for o, n in order], axis=1).astype(BF16)
    wdt = w_in[:, R_DT:R_DT + 2 * N_HEADS].astype(BF16)
    return w, jnp.pad(wdt, ((0, 0), (0, DT_PAD - 2 * N_HEADS))), wdt.T


def kernel(x, c, ctx, c_ctx, w_ada, b_ada, norm1_w, w_in, conv_mix_w, ssd_conv_w, ssd_conv_b, ssd_a_log, ssd_dt_bias,
           ssd_d, ssd_norm_w, na_rpb, w_br_conv, w_br_ssd, w_br_na, w_out, norm2_w, w_ff1, w_ff2, final_norm_w):
    B, L, _ = x.shape
    Lc = ctx.shape[1]
    depth = w_in.shape[0]
    cos, sa, sb = _rope_tables(L)
    n_mod = B + 1
    pad = (-n_mod) % 8
    c_rows = jnp.concatenate([c, c_ctx[None, :], jnp.zeros((pad, D), F32)], axis=0)
    zero_state = jnp.zeros((B, 2, SSD_GROUPS, SSD_STATE, D // SSD_GROUPS), F32)
    fw = final_norm_w.reshape(1, D)
    Tc = B * Lc
    tmc = min(TM_PROJ, Tc)
    flat = lambda a: a.reshape(1, Tc, a.shape[-1])
    h, hc = x, flat(ctx)
    for l in range(depth):
        last = l == depth - 1
        mod = ada_mod(c_rows, w_ada[l], b_ada[l])
        m_lat = mod[:B].reshape(B, 1, 6, D)
        m_ctx = mod[B:B + 1].reshape(1, 1, 6, D)
        sh1, sc1, gt1, sh2, sc2, gt2 = (m_lat[:, :, i] for i in range(6))
        csh1, csc1, cgt1, csh2, csc2, cgt2 = (m_ctx[:, :, i] for i in range(6))
        w, wdt, wdtT = _prep_w_in(w_in[l])
        nw1 = norm1_w[l].reshape(1, D)
        conv_p = (ssd_conv_w[l], ssd_conv_b[l])
        u, dt, dtT = in_proj(h, sh1, sc1, nw1, w, wdt, wdtT, *conv_p, tm=TM_PROJ, seq_len=L, rope_tables=(cos, sa, sb))
        uc, dtc, dtTc = in_proj(hc, csh1, csc1, nw1, w, wdt, wdtT, *conv_p, tm=tmc, seq_len=Lc)
        uc = uc.reshape(B, Lc, U_COLS)
        dtc = dtc.reshape(B, Lc, DT_PAD)
        dtTc = jnp.transpose(dtTc.reshape(2 * N_HEADS, B, Lc), (1, 0, 2))
        d_skip_e = jnp.repeat(ssd_d[l], HEAD_DIM, axis=1)
        ssd_p = (ssd_a_log[l], ssd_dt_bias[l], d_skip_e, ssd_norm_w[l])
        y_ssd_c, ctx_states = ssd_mix(uc, dtc, dtTc, zero_state, *ssd_p)
        y_ssd, _ = ssd_mix(u, dt, dtT, ctx_states, *ssd_p)
        y_na = na_attend(u, uc, _na_bias_table(na_rpb[l]))
        wb = [t[l].astype(BF16) for t in (w_br_conv, w_br_ssd, w_br_na, w_out)]
        w1, w2 = w_ff1[l].astype(BF16), w_ff2[l].astype(BF16)
        nw2 = norm2_w[l].reshape(1, D)
        h = merge(u, y_ssd, y_na, h, gt1, conv_mix_w[l], *wb, tm=TM_MERGE, seq_len=L)
        h = mlp(h, sh2, sc2, gt2, nw2, fw, w1, w2, tm=TM_PROJ, final=last)
        if not last:
            y_na_c = ctx_attend(uc)
            hc = merge(flat(uc), flat(y_ssd_c), flat(y_na_c), hc, cgt1, conv_mix_w[l], *wb, tm=min(TM_MERGE, Tc), seq_len=Lc)
            hc = mlp(hc, csh2, csc2, cgt2, nw2, fw, w1, w2, tm=tmc, final=False)
    return h
```

```python
import functools
import math

import jax
import jax.numpy as jnp
import numpy as np
from jax import lax
from jax.experimental import pallas as pl
from jax.experimental.pallas import tpu as pltpu

F32 = jnp.float32
BF16 = jnp.bfloat16
HIGHEST = lax.Precision.HIGHEST
LOG2E = math.log2(math.e)

D = 1024
EPS = 1e-6
GRID_W = 64
N_HEADS = 16
HEAD_DIM = 64
SSD_GROUPS = 2
SSD_STATE = 128
CHUNK = 128
NA_KH = 8
NA_KW = 16
ROPE_BASE = 10000.0
D_FF = 4 * D
XBC = D + 2 * SSD_GROUPS * SSD_STATE
R_CB, R_CC, R_CX, R_Z, R_XBC, R_DT, R_Q, R_K, R_V, R_GATE = (
    0, 1024, 2048, 3072, 4096, 5632, 5664, 6688, 7712, 8736)

LANES = 128
HALO = 16
VMEM_LIMIT = 56 * 1024 * 1024

U_Q, U_V, U_Z, U_K, U_CB, U_CC, U_XS, U_CX, U_GATE, U_BC, U_COLS = (
    0, 1024, 2048, 3072, 4096, 5120, 6144, 7168, 8192, 11264, 11776)
U_TILE = 23 * LANES
U_SUB = 512
DT_PAD = LANES
TM_PROJ = 1024
TM_MERGE = 512
ADA_TN = 1536
SSD_SUB = 8


def _cparams(sem):
    return pltpu.CompilerParams(dimension_semantics=sem, vmem_limit_bytes=VMEM_LIMIT)


def _nt(a, b):
    return lax.dot_general(a, b, (((1,), (1,)), ((), ())), preferred_element_type=F32)


def _tn(a, b):
    return lax.dot_general(a, b, (((0,), (0,)), ((), ())), preferred_element_type=F32)


def _dot(a, b):
    return jnp.dot(a, b, preferred_element_type=F32)


def _dot_hi(a, b):
    return jnp.dot(a, b, preferred_element_type=F32, precision=HIGHEST)


def _sigmoid(x):
    return 0.5 * jnp.tanh(0.5 * x) + 0.5


def _silu(x):
    return x * _sigmoid(x)


def _norm_mod(x, nw, sc, sh):
    ms = jnp.mean(x * x, axis=-1, keepdims=True)
    y = x * lax.rsqrt(ms + EPS) * nw
    return y * (1.0 + sc) + sh


def _ada_kernel(c_ref, w_ref, b_ref, o_ref):
    o_ref[...] = _dot_hi(_silu(c_ref[...]), w_ref[...]) + b_ref[...]


def ada_mod(c_rows, w_ada, b_ada):
    rows = c_rows.shape[0]
    tn = ADA_TN
    return pl.pallas_call(
        _ada_kernel,
        grid=(6 * D // tn,),
        in_specs=[pl.BlockSpec((rows, D), lambda j: (0, 0)),
                  pl.BlockSpec((D, tn), lambda j: (0, j)),
                  pl.BlockSpec((1, tn), lambda j: (0, j))],
        out_specs=pl.BlockSpec((rows, tn), lambda j: (0, j)),
        out_shape=jax.ShapeDtypeStruct((rows, 6 * D), F32),
        compiler_params=_cparams(("arbitrary",)),
        name="ada_mod",
    )(c_rows, w_ada, b_ada.reshape(1, 6 * D))


def _rope_slab(x, cos, sa, sb):
    return x * cos + pltpu.roll(x, 2 * HEAD_DIM - 16, axis=1) * sa + pltpu.roll(x, 16, axis=1) * sb


def _inproj_kernel(x_ref, sh_ref, sc_ref, nw_ref, w_ref, wdt_ref, wdtT_ref, cw_ref, cb_ref, *rest, rope, seq_len):
    if rope:
        cos_ref, sa_ref, sb_ref, u_ref, dt_ref, dtT_ref, xn_ref = rest
    else:
        u_ref, dt_ref, dtT_ref, xn_ref = rest
    j, i = pl.program_id(1), pl.program_id(2)
    ni, tm = xn_ref.shape[0], xn_ref.shape[1]

    def normalise():
        xb = _norm_mod(x_ref[0], nw_ref[...], sc_ref[0], sh_ref[0]).astype(BF16)
        xn_ref[i] = xb
        dt_ref[0] = _dot(xb, wdt_ref[...])
        dtT_ref[0] = _nt(wdtT_ref[...], xb)
        return xb

    def rope_piece(r, wd, scale):
        lw = 2 * HEAD_DIM
        return jnp.concatenate([_rope_slab(r[:, s:s + lw], cos_ref[...], sa_ref[...], sb_ref[...]) * scale
                                for s in range(0, wd, lw)], axis=1)

    def conv_piece(r, ws, wd, c0):
        before = xn_ref[jnp.maximum(i - 1, 0), tm - HALO:tm, :]
        after = xn_ref[jnp.minimum(i + 1, ni - 1), 0:HALO, :]
        r_before = _dot(before, ws)[HALO - 1:HALO]
        r_after = _dot(after, ws)[0:1]
        row = lax.broadcasted_iota(jnp.int32, (tm, wd), 0)
        if seq_len >= tm:
            r_before = jnp.where((i * tm) & (seq_len - 1) == 0, 0.0, r_before)
            r_after = jnp.where(((i + 1) * tm) & (seq_len - 1) == 0, 0.0, r_after)
            prev = jnp.where(row == 0, r_before, pltpu.roll(r, 1, axis=0))
            nxt = jnp.where(row == tm - 1, r_after, pltpu.roll(r, tm - 1, axis=0))
        else:
            pos = (i * tm + row) & (seq_len - 1)
            prev = jnp.where(pos == 0, 0.0, jnp.where(row == 0, r_before, pltpu.roll(r, 1, axis=0)))
            nxt = jnp.where(pos == seq_len - 1, 0.0,
                            jnp.where(row == tm - 1, r_after, pltpu.roll(r, tm - 1, axis=0)))
        cw = cw_ref[:, c0:c0 + wd]
        h = prev * cw[0:1] + r * cw[1:2] + nxt * cw[2:3] + cb_ref[:, c0:c0 + wd]
        return h + h * jnp.tanh(h)

    def tile(jt, xb=None):
        for off, wd, kind, arg in _tile_pieces(jt):
            ws = w_ref[:, off:off + wd]
            r = _dot(xn_ref[i] if xb is None else xb, ws)
            if kind == "rope" and rope:
                r = rope_piece(r, wd, arg)
            elif kind == "conv":
                r = conv_piece(r, ws, wd, arg)
            u_ref[0, :, off:off + wd] = r.astype(u_ref.dtype)

    for jt in range(U_COLS // U_TILE):
        @pl.when(j == jt)
        def _(jt=jt):
            tile(jt, normalise() if jt == 0 else None)


def _tile_pieces(jt):
    special = [(U_Q, D, "rope", LOG2E * HEAD_DIM ** -0.5), (U_K, D, "rope", 1.0),
               (U_XS, D, "conv", 0), (U_BC, XBC - D, "conv", D)]
    lo, hi = jt * U_TILE, (jt + 1) * U_TILE
    cuts = sorted({lo, hi} | {c for s0, w0, _, _ in special for c in (s0, s0 + w0) if lo < c < hi})
    pieces = []
    for a, b in zip(cuts[:-1], cuts[1:]):
        kind, arg, seg0 = "plain", None, a
        for s0, w0, k0, a0 in special:
            if s0 <= a and b <= s0 + w0:
                kind, arg, seg0 = k0, a0, s0
        for off in range(a, b, U_SUB):
            wd = min(U_SUB, b - off)
            piece_arg = arg + (off - seg0) if kind == "conv" else arg
            pieces.append((off - lo, wd, kind, piece_arg))
    return pieces


def in_proj(x, sh, sc, nw, w, wdt, wdtT, conv_w, conv_b, tm, seq_len, rope_tables=None):
    B, L, _ = x.shape
    assert seq_len & (seq_len - 1) == 0 and (tm % seq_len == 0 or seq_len % tm == 0)
    tn = U_TILE
    ni = L // tm
    grid = (B, U_COLS // tn, ni)
    rope = rope_tables is not None

    def first(j, i):
        return jnp.where(j == 0, i, ni - 1)

    last_rope_tile = (U_K + D - 1) // U_TILE

    def roped(j, i):
        return jnp.where(j <= last_rope_tile, i, ni - 1)

    rope_specs = [pl.BlockSpec((tm, 2 * HEAD_DIM), lambda b, j, i: (roped(j, i), 0))] * 3 if rope else []
    return pl.pallas_call(
        functools.partial(_inproj_kernel, rope=rope, seq_len=seq_len),
        grid=grid,
        in_specs=[pl.BlockSpec((1, tm, D), lambda b, j, i: (b, first(j, i), 0)),
                  pl.BlockSpec((1, 1, D), lambda b, j, i: (b, 0, 0)),
                  pl.BlockSpec((1, 1, D), lambda b, j, i: (b, 0, 0)),
                  pl.BlockSpec((1, D), lambda b, j, i: (0, 0)),
                  pl.BlockSpec((D, tn), lambda b, j, i: (0, j)),
                  pl.BlockSpec((D, DT_PAD), lambda b, j, i: (0, 0)),
                  pl.BlockSpec((2 * N_HEADS, D), lambda b, j, i: (0, 0)),
                  pl.BlockSpec((3, XBC), lambda b, j, i: (0, 0)),
                  pl.BlockSpec((1, XBC), lambda b, j, i: (0, 0))] + rope_specs,
        out_specs=[pl.BlockSpec((1, tm, tn), lambda b, j, i: (b, i, j)),
                   pl.BlockSpec((1, tm, DT_PAD), lambda b, j, i: (b, first(j, i), 0)),
                   pl.BlockSpec((1, 2 * N_HEADS, tm), lambda b, j, i: (b, 0, first(j, i)))],
        out_shape=[jax.ShapeDtypeStruct((B, L, U_COLS), BF16),
                   jax.ShapeDtypeStruct((B, L, DT_PAD), F32),
                   jax.ShapeDtypeStruct((B, 2 * N_HEADS, L), F32)],
        scratch_shapes=[pltpu.VMEM((ni, tm, D), BF16)],
        compiler_params=_cparams(("parallel", "arbitrary", "arbitrary")),
        name="in_proj",
    )(x, sh, sc, nw, w, wdt, wdtT, 0.5 * conv_w, 0.5 * conv_b.reshape(1, XBC), *(rope_tables if rope else ()))


def _split3(x):
    hi = x.astype(BF16)
    r1 = x - hi.astype(F32)
    mid = r1.astype(BF16)
    lo = (r1 - mid.astype(F32)).astype(BF16)
    return hi, mid, lo


def _head_stack(x2, lane):
    zero = jnp.zeros_like(x2)
    return jnp.concatenate([jnp.where(lane < HEAD_DIM, x2, zero), jnp.where(lane >= HEAD_DIM, x2, zero)], axis=0)


def _ssd_chunk(d, sub, rows, xs_ref, bc_ref, z_ref, dt_ref, dtT_ref, alr_ref, alc_ref, dbr_ref, dbc_ref, dsk_ref,
               nw_ref, ex_ref, y_ref, yacc_ref, st_ref):
    H = N_HEADS
    blk = slice(sub * CHUNK, (sub + 1) * CHUNK)
    xsb = xs_ref[0, blk, :]
    xs = xsb.astype(F32)
    bc = bc_ref[0, blk, :]
    dt_c = jax.nn.softplus(dt_ref[0, blk, :] + dbr_ref[...])
    a_c = dt_c * (-LOG2E * jnp.exp(alr_ref[...]))
    dtT_c = jax.nn.softplus(dtT_ref[0, :, blk] + dbc_ref[...])
    aT_c = dtT_c * (-LOG2E * jnp.exp(alc_ref[...]))

    ri = lax.broadcasted_iota(jnp.int32, (CHUNK, CHUNK), 0)
    ci = lax.broadcasted_iota(jnp.int32, (CHUNK, CHUNK), 1)
    tri = (ri >= ci) if d == 0 else (ri <= ci)
    trib = tri.astype(BF16)
    tribT = ((ri <= ci) if d == 0 else (ri >= ci)).astype(BF16)
    cs = _dot(jnp.concatenate([trib] * 3, axis=1), jnp.concatenate(_split3(a_c), axis=0))
    csT = _dot(jnp.concatenate(_split3(aT_c), axis=1), jnp.concatenate([tribT] * 3, axis=0))
    last = cs[CHUNK - 1:CHUNK] if d == 0 else cs[0:1]
    w_c = dt_c * jnp.exp2(last - cs)
    ein_c = jnp.exp2(cs)
    ex = ex_ref[d]
    w_e = _dot(jnp.concatenate(_split3(w_c)[:2], axis=1), ex)
    ein_e = _dot(jnp.concatenate(_split3(ein_c)[:2], axis=1), ex)
    cd_e = ein_e[CHUNK - 1:CHUNK] if d == 0 else ein_e[0:1]
    Xdec = (xs * w_e).astype(BF16)
    rowT = csT - jnp.log2(dtT_c)
    lane = lax.broadcasted_iota(jnp.int32, (CHUNK, 2 * HEAD_DIM), 1)

    ys = []
    GW = D // SSD_GROUPS
    for g in range(SSD_GROUPS):
        Bg = bc[:, g * SSD_STATE:(g + 1) * SSD_STATE]
        Cg = bc[:, (SSD_GROUPS + g) * SSD_STATE:(SSD_GROUPS + g + 1) * SSD_STATE]
        S = _nt(Cg, Bg)
        st = st_ref[d, g]
        y_off = _dot(Cg, st.astype(BF16)) * ein_e[:, g * GW:(g + 1) * GW]
        y_diag = []
        for p in range(GW // (2 * HEAD_DIM)):
            hA = g * (H // SSD_GROUPS) + 2 * p
            Ms = []
            for h in (hA, hA + 1):
                col = cs[:, d * H + h:d * H + h + 1]
                rowv = rowT[d * H + h:d * H + h + 1, :]
                Ms.append((S * jnp.exp2(jnp.where(tri, col - rowv, -jnp.inf))).astype(BF16))
            Mcat = jnp.concatenate(Ms, axis=1)
            Xp = xsb[:, hA * HEAD_DIM:(hA + 2) * HEAD_DIM]
            y_diag.append(_dot(Mcat, _head_stack(Xp, lane)))
        ys.append(y_off + jnp.concatenate(y_diag, axis=1))
        st_ref[d, g] = st * cd_e[:, g * GW:(g + 1) * GW] + _tn(Bg, Xdec[:, g * GW:(g + 1) * GW])
    y = jnp.concatenate(ys, axis=1)

    if d == 0:
        yacc_ref[rows, :] = y
    else:
        ysum = yacc_ref[rows, :] + y + xs * (dsk_ref[0:1] + dsk_ref[1:2])
        yg = ysum * _silu(z_ref[0, blk, :].astype(F32))
        parts = []
        for g in range(SSD_GROUPS):
            v = yg[:, g * GW:(g + 1) * GW]
            ms = jnp.mean(v * v, axis=-1, keepdims=True)
            parts.append(v * lax.rsqrt(ms + EPS) * nw_ref[:, g * GW:(g + 1) * GW])
        y_ref[0, blk, :] = jnp.concatenate(parts, axis=1).astype(y_ref.dtype)


def _ssd_kernel(xs_ref, bc_ref, z_ref, dt_ref, dtT_ref, h0_ref, alr_ref, alc_ref, dbr_ref, dbc_ref, dsk_ref, nw_ref,
                ex_ref, y_ref, sto_ref, yacc_ref, st_ref, *, ns, nsub):
    t = pl.program_id(1)

    @pl.when(t == 0)
    def _():
        st_ref[...] = h0_ref[0]

    args = (xs_ref, bc_ref, z_ref, dt_ref, dtT_ref, alr_ref, alc_ref, dbr_ref, dbc_ref, dsk_ref, nw_ref,
            ex_ref, y_ref, yacc_ref, st_ref)

    def seq_rows(step, sub):
        return pl.ds(pl.multiple_of((step * nsub + sub) * CHUNK, CHUNK), CHUNK)

    @pl.when(t < ns)
    def _():
        for sub in range(nsub):
            _ssd_chunk(0, sub, seq_rows(t, sub), *args)

    @pl.when(t >= ns)
    def _():
        for sub in reversed(range(nsub)):
            _ssd_chunk(1, sub, seq_rows(2 * ns - 1 - t, sub), *args)

    @pl.when(t == 2 * ns - 1)
    def _():
        sto_ref[0] = st_ref[...]


def ssd_mix(u, dt, dtT, h0, a_log, dt_bias, d_skip_e, norm_w):
    B, L, _ = u.shape
    nsub = min(SSD_SUB, L // CHUNK)
    RB = nsub * CHUNK
    ns = L // RB
    H2 = 2 * N_HEADS

    def chunk(t):
        return jnp.where(t < ns, t, 2 * ns - 1 - t)

    def late(t):
        return jnp.where(t < ns, ns - 1, 2 * ns - 1 - t)

    st_shape = (2, SSD_GROUPS, SSD_STATE, D // SSD_GROUPS)
    st_spec = pl.BlockSpec((1,) + st_shape, lambda b, t: (b, 0, 0, 0, 0))
    small = lambda shape: pl.BlockSpec(shape, lambda b, t: (0,) * len(shape))
    krow = jnp.arange(2 * DT_PAD, dtype=jnp.int32)[:, None] % DT_PAD
    head = jnp.arange(D, dtype=jnp.int32)[None, :] // HEAD_DIM
    ex = jnp.stack([krow == d * N_HEADS + head for d in range(2)]).astype(BF16)
    lane_pad = lambda v: jnp.pad(v.reshape(1, H2), ((0, 0), (0, DT_PAD - H2)))
    return pl.pallas_call(
        functools.partial(_ssd_kernel, ns=ns, nsub=nsub),
        grid=(B, 2 * ns),
        in_specs=[pl.BlockSpec((1, RB, D), lambda b, t: (b, chunk(t), U_XS // D)),
                  pl.BlockSpec((1, RB, XBC - D), lambda b, t: (b, chunk(t), U_BC // (XBC - D))),
                  pl.BlockSpec((1, RB, D), lambda b, t: (b, late(t), U_Z // D)),
                  pl.BlockSpec((1, RB, DT_PAD), lambda b, t: (b, chunk(t), 0)),
                  pl.BlockSpec((1, H2, RB), lambda b, t: (b, 0, chunk(t))),
                  st_spec,
                  small((1, DT_PAD)), small((H2, 1)), small((1, DT_PAD)), small((H2, 1)),
                  small((2, D)), small((1, D)), small((2, 2 * DT_PAD, D))],
        out_specs=[pl.BlockSpec((1, RB, D), lambda b, t: (b, late(t), 0)),
                   st_spec],
        out_shape=[jax.ShapeDtypeStruct((B, L, D), BF16),
                   jax.ShapeDtypeStruct((B,) + st_shape, F32)],
        scratch_shapes=[pltpu.VMEM((L, D), F32), pltpu.VMEM(st_shape, F32)],
        compiler_params=_cparams(("parallel", "arbitrary")),
        name="ssd_mix",
    )(u, u, u, dt, dtT, h0,
      lane_pad(a_log), a_log.reshape(H2, 1), lane_pad(dt_bias), dt_bias.reshape(H2, 1),
      d_skip_e, norm_w.reshape(1, D), ex)


NA_GROUP = 1


def _na_kernel(q_ref, k_ref, v_ref, kc_ref, vc_ref, bias_ref, o_ref, s_ref, p_ref, inv_ref, kt_ref, kto_ref, *, rows):
    W = GRID_W
    band = NA_KH * W
    lane = lax.broadcasted_iota(jnp.int32, (W, 2 * HEAD_DIM), 1)
    G = NA_GROUP
    n_groups = rows // G
    M2 = 2 * W
    kt_ref[...] = k_ref[0].T
    kto_ref[...] = k_ref[0, W:W + kto_ref.shape[1], :].T
    kct = kc_ref[0].T

    def band_start(r):
        return min(max(r - NA_KH // 2, 0), rows - NA_KH)

    def stage_scores(g, slot):
        for i in range(G):
            r = g * G + i
            r0 = band_start(r)
            qs = _head_stack(q_ref[0, r * W:(r + 1) * W, :], lane)
            ktb, t0 = (kt_ref, r0 * W) if r0 % 2 == 0 else (kto_ref, (r0 - 1) * W)
            base = r0 - r + NA_KH - 1
            for jp in range(NA_KH // 2):
                cols = slice(jp * M2, (jp + 1) * M2)
                s_ref[slot, i * M2:(i + 1) * M2, cols] = (_dot(qs, ktb[:, t0 + jp * M2:t0 + (jp + 1) * M2])
                                                          + bias_ref[0, base + 2 * jp])
            s_ref[slot, i * M2:(i + 1) * M2, band:] = _dot(qs, kct)

    def stage_softmax(slot):
        for i in range(G):
            s = s_ref[slot, i * M2:(i + 1) * M2, :]
            p = jnp.exp2(s - jnp.max(s, axis=-1, keepdims=True))
            inv = 1.0 / jnp.sum(p, axis=-1, keepdims=True)
            p_ref[slot, i * M2:(i + 1) * M2, :] = p.astype(BF16)
            inv_ref[slot, i * M2:(i + 1) * M2, :] = jnp.broadcast_to(inv, (M2, M2))

    def stage_values(g, slot):
        for i in range(G):
            r = g * G + i
            r0 = band_start(r)
            vb = v_ref[0, r0 * W:r0 * W + band, :]
            p = p_ref[slot, i * M2:(i + 1) * M2, :]
            o2 = (_dot(p[:, :band], vb) + _dot(p[:, band:], vc_ref[0])) * inv_ref[slot, i * M2:(i + 1) * M2, :]
            o = jnp.where(lane < HEAD_DIM, o2[:W], o2[W:])
            o_ref[0, r * W:(r + 1) * W, :] = o.astype(o_ref.dtype)

    stage_scores(0, 0)
    stage_scores(1, 1)
    stage_softmax(0)

    for g in range(n_groups - 2):
        stage_scores(g + 2, g % 2)
        stage_softmax((g + 1) % 2)
        stage_values(g, g % 2)
    stage_softmax((n_groups - 1) % 2)
    stage_values(n_groups - 2, (n_groups - 2) % 2)
    stage_values(n_groups - 1, (n_groups - 1) % 2)


def na_attend(u, u_ctx, bias):
    B, L, _ = u.shape
    Lc = u_ctx.shape[1]
    HP = N_HEADS // 2
    lw = 2 * HEAD_DIM
    oq, ok, ov = U_Q // lw, U_K // lw, U_V // lw
    return pl.pallas_call(
        functools.partial(_na_kernel, rows=L // GRID_W),
        grid=(HP, B),
        in_specs=[pl.BlockSpec((1, L, lw), lambda h, b: (b, 0, oq + h)),
                  pl.BlockSpec((1, L, lw), lambda h, b: (b, 0, ok + h)),
                  pl.BlockSpec((1, L, lw), lambda h, b: (b, 0, ov + h)),
                  pl.BlockSpec((1, Lc, lw), lambda h, b: (b, 0, ok + h)),
                  pl.BlockSpec((1, Lc, lw), lambda h, b: (b, 0, ov + h)),
                  pl.BlockSpec((1, 2 * NA_KH - 2, 2 * GRID_W, 2 * GRID_W), lambda h, b: (h, 0, 0, 0))],
        out_specs=pl.BlockSpec((1, L, lw), lambda h, b: (b, 0, h)),
        out_shape=jax.ShapeDtypeStruct((B, L, D), BF16),
        scratch_shapes=[pltpu.VMEM((2, NA_GROUP * lw, NA_KH * GRID_W + Lc), F32),
                        pltpu.VMEM((2, NA_GROUP * lw, NA_KH * GRID_W + Lc), BF16),
                        pltpu.VMEM((2, NA_GROUP * lw, lw), F32),
                        pltpu.VMEM((lw, L), BF16), pltpu.VMEM((lw, L - 2 * GRID_W), BF16)],
        compiler_params=_cparams(("parallel", "parallel")),
        name="na_attend",
    )(u, u, u, u_ctx, u_ctx, bias)


def _ctxattn_kernel(q_ref, k_ref, v_ref, o_ref):
    Lc = q_ref.shape[1]
    lane = lax.broadcasted_iota(jnp.int32, (Lc, 2 * HEAD_DIM), 1)
    q2 = (q_ref[0].astype(F32) * (HEAD_DIM ** -0.5)).astype(BF16)
    s = _nt(_head_stack(q2, lane), k_ref[0])
    m = jnp.max(s, axis=-1, keepdims=True)
    p = jnp.exp(s - m)
    p = (p * (1.0 / jnp.sum(p, axis=-1, keepdims=True))).astype(BF16)
    o2 = _dot(p, v_ref[0])
    o_ref[0] = jnp.where(lane < HEAD_DIM, o2[:Lc], o2[Lc:]).astype(o_ref.dtype)


def ctx_attend(u_ctx):
    B, Lc, _ = u_ctx.shape
    HP = N_HEADS // 2
    lw = 2 * HEAD_DIM
    oq, ok, ov = U_Q // lw, U_K // lw, U_V // lw
    return pl.pallas_call(
        _ctxattn_kernel,
        grid=(HP, B),
        in_specs=[pl.BlockSpec((1, Lc, lw), lambda h, b: (b, 0, oq + h)),
                  pl.BlockSpec((1, Lc, lw), lambda h, b: (b, 0, ok + h)),
                  pl.BlockSpec((1, Lc, lw), lambda h, b: (b, 0, ov + h))],
        out_specs=pl.BlockSpec((1, Lc, lw), lambda h, b: (b, 0, h)),
        out_shape=jax.ShapeDtypeStruct((B, Lc, D), BF16),
        compiler_params=_cparams(("parallel", "parallel")),
        name="ctx_attend",
    )(u_ctx, u_ctx, u_ctx)


MERGE_CK = 256


def _merge_kernel(cb_ref, cc_ref, cx_ref, ccp_ref, cxp_ref, ccn_ref, cxn_ref, cw_ref, ys_ref, yn_ref, gc_ref, gs_ref,
                  gn_ref, h_ref, gt_ref, wc_ref, ws_ref, wn_ref, wo_ref, o_ref, *, seq_len):
    tm = h_ref.shape[1]
    row = lax.broadcasted_iota(jnp.int32, (tm, MERGE_CK), 0)
    pos = (pl.program_id(1) * tm + row) & (seq_len - 1)
    first, last = pos == 0, pos == seq_len - 1
    top, bottom = row == 0, row == tm - 1
    yc_proj = None
    for c in range(0, D, MERGE_CK):
        ch = slice(c, c + MERGE_CK)
        p = cc_ref[0, :, ch].astype(F32) * cx_ref[0, :, ch].astype(F32)
        p_before = (ccp_ref[0, :, ch].astype(F32) * cxp_ref[0, :, ch].astype(F32))[HALO - 1:HALO]
        p_after = (ccn_ref[0, :, ch].astype(F32) * cxn_ref[0, :, ch].astype(F32))[0:1]
        prev = jnp.where(first, 0.0, jnp.where(top, p_before, pltpu.roll(p, 1, axis=0)))
        nxt = jnp.where(last, 0.0, jnp.where(bottom, p_after, pltpu.roll(p, tm - 1, axis=0)))
        w = cw_ref[:, ch]
        yc = (cb_ref[0, :, ch].astype(F32) * (prev * w[0:1] + p * w[1:2] + nxt * w[2:3])).astype(BF16)
        t = _dot(yc, wc_ref[ch, :])
        yc_proj = t if yc_proj is None else yc_proj + t

    gate = lambda ref: _sigmoid(ref[0].astype(F32))
    m = (gate(gc_ref) * yc_proj
         + gate(gs_ref) * _dot(ys_ref[0], ws_ref[...])
         + gate(gn_ref) * _dot(yn_ref[0], wn_ref[...]))
    o_ref[0] = h_ref[0] + gt_ref[0] * _dot(m.astype(BF16), wo_ref[...])


def merge(u, ys, yn, h, gt, cw, wc, ws, wn, wo, tm, seq_len):
    B, L, _ = h.shape
    assert seq_len & (seq_len - 1) == 0 and (tm % seq_len == 0 or seq_len % tm == 0)
    nb, hb = tm // HALO, L // HALO
    tok = lambda: pl.BlockSpec((1, tm, D), lambda b, i: (b, i, 0))
    ucol = lambda c: pl.BlockSpec((1, tm, D), lambda b, i: (b, i, c // D))
    before = lambda c: pl.BlockSpec((1, HALO, D), lambda b, i: (b, jnp.maximum(i * nb - 1, 0), c // D))
    after = lambda c: pl.BlockSpec((1, HALO, D), lambda b, i: (b, jnp.minimum((i + 1) * nb, hb - 1), c // D))
    wsp = lambda: pl.BlockSpec((D, D), lambda b, i: (0, 0))
    return pl.pallas_call(
        functools.partial(_merge_kernel, seq_len=seq_len),
        grid=(B, L // tm),
        in_specs=[ucol(U_CB), ucol(U_CC), ucol(U_CX), before(U_CC), before(U_CX), after(U_CC), after(U_CX),
                  pl.BlockSpec((3, D), lambda b, i: (0, 0)),
                  tok(), tok(),
                  ucol(U_GATE), ucol(U_GATE + D), ucol(U_GATE + 2 * D),
                  tok(),
                  pl.BlockSpec((1, 1, D), lambda b, i: (b, 0, 0)),
                  wsp(), wsp(), wsp(), wsp()],
        out_specs=tok(),
        out_shape=jax.ShapeDtypeStruct((B, L, D), F32),
        compiler_params=_cparams(("parallel", "parallel")),
        name="merge",
    )(u, u, u, u, u, u, u, cw, ys, yn, u, u, u, h, gt, wc, ws, wn, wo)


MLP_TF = 1024


def _mlp_kernel(h_ref, sh_ref, sc_ref, gt_ref, nw_ref, fw_ref, w1_ref, w2_ref, o_ref, *, final):
    h = h_ref[0]
    xn = _norm_mod(h, nw_ref[...], sc_ref[0], sh_ref[0]).astype(BF16)
    acc = None
    for k in range(0, D_FF, MLP_TF):
        a = jnp.square(jnp.maximum(_dot(xn, w1_ref[:, k:k + MLP_TF]), 0.0)).astype(BF16)
        t = _dot(a, w2_ref[k:k + MLP_TF, :])
        acc = t if acc is None else acc + t
    o = h + gt_ref[0] * acc
    if final:
        ms = jnp.mean(o * o, axis=-1, keepdims=True)
        o = o * lax.rsqrt(ms + EPS) * fw_ref[...]
    o_ref[0] = o


def mlp(h, sh, sc, gt, nw, fw, w1, w2, tm, final):
    B, L, _ = h.shape
    vec = lambda: pl.BlockSpec((1, 1, D), lambda b, i: (b, 0, 0))
    par = lambda: pl.BlockSpec((1, D), lambda b, i: (0, 0))
    resident = lambda shape: pl.BlockSpec(shape, lambda b, i: (0, 0), pipeline_mode=pl.Buffered(1))
    return pl.pallas_call(
        functools.partial(_mlp_kernel, final=final),
        grid=(B, L // tm),
        in_specs=[pl.BlockSpec((1, tm, D), lambda b, i: (b, i, 0)),
                  vec(), vec(), vec(), par(), par(),
                  resident((D, D_FF)), resident((D_FF, D))],
        out_specs=pl.BlockSpec((1, tm, D), lambda b, i: (b, i, 0)),
        out_shape=jax.ShapeDtypeStruct((B, L, D), F32),
        compiler_params=_cparams(("parallel", "parallel")),
        name="mlp",
    )(h, sh, sc, gt, nw, fw, w1, w2)


def _rope_tables(L):
    t = jnp.arange(L, dtype=jnp.int32)
    row = (t // GRID_W).astype(F32)
    col = (t % GRID_W).astype(F32)
    half = HEAD_DIM // 2
    inv = ROPE_BASE ** (-jnp.arange(0, half, 2, dtype=F32) / half)
    ang_r = row[:, None] * inv
    ang_c = col[:, None] * inv
    ang = jnp.concatenate([ang_r, ang_r, ang_c, ang_c], axis=-1)
    cos = jnp.tile(jnp.cos(ang), (1, 2))
    sin = jnp.tile(jnp.sin(ang), (1, 2))
    even = ((jnp.arange(2 * HEAD_DIM) // (half // 2)) % 2 == 0)[None, :]
    return cos, jnp.where(even, -sin, 0.0), jnp.where(even, 0.0, sin)


def _rpb_kernel(r_ref, oh_ref, ok_ref, o_ref):
    val = _dot(jnp.concatenate(_split3(r_ref[...]), axis=1), oh_ref[...])
    o_ref[...] = jnp.where(ok_ref[...] > 0.0, LOG2E * val, -jnp.inf)


def _na_bias_table(rpb):
    H, NR, NC = rpb.shape
    W = GRID_W
    col = np.arange(W)
    col_start = np.clip(col - NA_KW // 2, 0, W - NA_KW)
    col_ok = (col[None, :] >= col_start[:, None]) & (col[None, :] < col_start[:, None] + NA_KW)
    dc_idx = np.clip(col[None, :] - col[:, None], -(NA_KW - 1), NA_KW - 1) + NA_KW - 1
    rows = jnp.transpose(rpb.reshape(H // 2, 2, NR, NC), (0, 2, 1, 3)).reshape(H * NR, NC)
    rows = jnp.pad(rows, ((0, 0), (0, DT_PAD - NC)))
    krow = jnp.arange(3 * DT_PAD, dtype=jnp.int32)[:, None] % DT_PAD
    onehot = (krow == jnp.asarray(dc_idx.reshape(1, W * W), jnp.int32)).astype(BF16)
    ok = jnp.asarray(col_ok.reshape(1, W * W), F32)
    tn = 1024
    tab = pl.pallas_call(
        _rpb_kernel,
        grid=(W * W // tn,),
        in_specs=[pl.BlockSpec((H * NR, DT_PAD), lambda j: (0, 0)),
                  pl.BlockSpec((3 * DT_PAD, tn), lambda j: (0, j)),
                  pl.BlockSpec((1, tn), lambda j: (0, j))],
        out_specs=pl.BlockSpec((H * NR, tn), lambda j: (0, j)),
        out_shape=jax.ShapeDtypeStruct((H * NR, W * W), F32),
        compiler_params=_cparams(("arbitrary",)),
        name="rpb_table",
    )(rows, onehot, ok)
    tab = tab.reshape(H // 2, NR, 2 * W, W)
    return jnp.concatenate([tab[:, :NR - 1], tab[:, 1:]], axis=-1)


def _prep_w_in(w_in):
    order = [(R_Q, D), (R_V, D), (R_Z, D), (R_K, D), (R_CB, D), (R_CC, D), (R_XBC, D), (R_CX, D), (R_GATE, 3 * D),
             (R_XBC + D, XBC - D)]
    w = jnp.concatenate([w_in[:, o:o + n] for o, n in order], axis=1).astype(BF16)
    wdt = w_in[:, R_DT:R_DT + 2 * N_HEADS].astype(BF16)
    return w, jnp.pad(wdt, ((0, 0), (0, DT_PAD - 2 * N_HEADS))), wdt.T


def kernel(x, c, ctx, c_ctx, w_ada, b_ada, norm1_w, w_in, conv_mix_w, ssd_conv_w, ssd_conv_b, ssd_a_log, ssd_dt_bias,
           ssd_d, ssd_norm_w, na_rpb, w_br_conv, w_br_ssd, w_br_na, w_out, norm2_w, w_ff1, w_ff2, final_norm_w):
    B, L, _ = x.shape
    Lc = ctx.shape[1]
    depth = w_in.shape[0]
    cos, sa, sb = _rope_tables(L)
    n_mod = B + 1
    pad = (-n_mod) % 8
    c_rows = jnp.concatenate([c, c_ctx[None, :], jnp.zeros((pad, D), F32)], axis=0)
    zero_state = jnp.zeros((B, 2, SSD_GROUPS, SSD_STATE, D // SSD_GROUPS), F32)
    fw = final_norm_w.reshape(1, D)
    Tc = B * Lc
    tmc = min(TM_PROJ, Tc)
    flat = lambda a: a.reshape(1, Tc, a.shape[-1])
    h, hc = x, flat(ctx)
    for l in range(depth):
        last = l == depth - 1
        mod = ada_mod(c_rows, w_ada[l], b_ada[l])
        m_lat = mod[:B].reshape(B, 1, 6, D)
        m_ctx = mod[B:B + 1].reshape(1, 1, 6, D)
        sh1, sc1, gt1, sh2, sc2, gt2 = (m_lat[:, :, i] for i in range(6))
        csh1, csc1, cgt1, csh2, csc2, cgt2 = (m_ctx[:, :, i] for i in range(6))
        w, wdt, wdtT = _prep_w_in(w_in[l])
        nw1 = norm1_w[l].reshape(1, D)
        conv_p = (ssd_conv_w[l], ssd_conv_b[l])
        u, dt, dtT = in_proj(h, sh1, sc1, nw1, w, wdt, wdtT, *conv_p, tm=TM_PROJ, seq_len=L, rope_tables=(cos, sa, sb))
        uc, dtc, dtTc = in_proj(hc, csh1, csc1, nw1, w, wdt, wdtT, *conv_p, tm=tmc, seq_len=Lc)
        uc = uc.reshape(B, Lc, U_COLS)
        dtc = dtc.reshape(B, Lc, DT_PAD)
        dtTc = jnp.transpose(dtTc.reshape(2 * N_HEADS, B, Lc), (1, 0, 2))
        d_skip_e = jnp.repeat(ssd_d[l], HEAD_DIM, axis=1)
        ssd_p = (ssd_a_log[l], ssd_dt_bias[l], d_skip_e, ssd_norm_w[l])
        y_ssd_c, ctx_states = ssd_mix(uc, dtc, dtTc, zero_state, *ssd_p)
        y_ssd, _ = ssd_mix(u, dt, dtT, ctx_states, *ssd_p)
        y_na = na_attend(u, uc, _na_bias_table(na_rpb[l]))
        wb = [t[l].astype(BF16) for t in (w_br_conv, w_br_ssd, w_br_na, w_out)]
        w1, w2 = w_ff1[l].astype(BF16), w_ff2[l].astype(BF16)
        nw2 = norm2_w[l].reshape(1, D)
        h = merge(u, y_ssd, y_na, h, gt1, conv_mix_w[l], *wb, tm=TM_MERGE, seq_len=L)
        h = mlp(h, sh2, sc2, gt2, nw2, fw, w1, w2, tm=TM_PROJ, final=last)
        if not last:
            y_na_c = ctx_attend(uc)
            hc = merge(flat(uc), flat(y_ssd_c), flat(y_na_c), hc, cgt1, conv_mix_w[l], *wb, tm=min(TM_MERGE, Tc), seq_len=Lc)
            hc = mlp(hc, csh2, csc2, cgt2, nw2, fw, w1, w2, tm=tmc, final=False)
    return h
```

```python
import functools
import math

import jax
import jax.numpy as jnp
import numpy as np
from jax import lax
from jax.experimental import pallas as pl
from jax.experimental.pallas import tpu as pltpu

F32 = jnp.float32
BF16 = jnp.bfloat16
HIGHEST = lax.Precision.HIGHEST
LOG2E = math.log2(math.e)

D = 1024
EPS = 1e-6
GRID_W = 64
N_HEADS = 16
HEAD_DIM = 64
SSD_GROUPS = 2
SSD_STATE = 128
CHUNK = 128
NA_KH = 8
NA_KW = 16
ROPE_BASE = 10000.0
D_FF = 4 * D
XBC = D + 2 * SSD_GROUPS * SSD_STATE
R_CB, R_CC, R_CX, R_Z, R_XBC, R_DT, R_Q, R_K, R_V, R_GATE = (
    0, 1024, 2048, 3072, 4096, 5632, 5664, 6688, 7712, 8736)

LANES = 128
HALO = 16
VMEM_LIMIT = 56 * 1024 * 1024

U_Q, U_V, U_Z, U_K, U_CB, U_CC, U_XS, U_CX, U_GATE, U_BC, U_COLS = (
    0, 1024, 2048, 3072, 4096, 5120, 6144, 7168, 8192, 11264, 11776)
U_TILE = 23 * LANES
U_SUB = 512
DT_PAD = LANES
TM_PROJ = 1024
TM_MERGE = 512
ADA_TN = 1536
SSD_SUB = 8
SSD_NB_CTX = 4


def _cparams(sem):
    return pltpu.CompilerParams(dimension_semantics=sem, vmem_limit_bytes=VMEM_LIMIT)


def _nt(a, b):
    return lax.dot_general(a, b, (((1,), (1,)), ((), ())), preferred_element_type=F32)


def _tn(a, b):
    return lax.dot_general(a, b, (((0,), (0,)), ((), ())), preferred_element_type=F32)


def _dot(a, b):
    return jnp.dot(a, b, preferred_element_type=F32)


def _dot_hi(a, b):
    return jnp.dot(a, b, preferred_element_type=F32, precision=HIGHEST)


def _sigmoid(x):
    return 0.5 * jnp.tanh(0.5 * x) + 0.5


def _silu(x):
    return x * _sigmoid(x)


def _norm_mod(x, nw, sc, sh):
    ms = jnp.mean(x * x, axis=-1, keepdims=True)
    y = x * lax.rsqrt(ms + EPS) * nw
    return y * (1.0 + sc) + sh


def _ada_kernel(c_ref, w_ref, b_ref, o_ref):
    o_ref[...] = _dot_hi(_silu(c_ref[...]), w_ref[...]) + b_ref[...]


def ada_mod(c_rows, w_ada, b_ada):
    rows = c_rows.shape[0]
    tn = ADA_TN
    return pl.pallas_call(
        _ada_kernel,
        grid=(6 * D // tn,),
        in_specs=[pl.BlockSpec((rows, D), lambda j: (0, 0)),
                  pl.BlockSpec((D, tn), lambda j: (0, j)),
                  pl.BlockSpec((1, tn), lambda j: (0, j))],
        out_specs=pl.BlockSpec((rows, tn), lambda j: (0, j)),
        out_shape=jax.ShapeDtypeStruct((rows, 6 * D), F32),
        compiler_params=_cparams(("arbitrary",)),
        name="ada_mod",
    )(c_rows, w_ada, b_ada.reshape(1, 6 * D))


def _rope_slab(x, cos, sa, sb):
    return x * cos + pltpu.roll(x, 2 * HEAD_DIM - 16, axis=1) * sa + pltpu.roll(x, 16, axis=1) * sb


def _inproj_kernel(x_ref, sh_ref, sc_ref, nw_ref, w_ref, wdt_ref, wdtT_ref, cw_ref, cb_ref, *rest, rope, seq_len):
    if rope:
        cos_ref, sa_ref, sb_ref, u_ref, dt_ref, dtT_ref, xn_ref = rest
    else:
        u_ref, dt_ref, dtT_ref, xn_ref = rest
    j, i = pl.program_id(1), pl.program_id(2)
    ni, tm = xn_ref.shape[0], xn_ref.shape[1]

    def normalise():
        xb = _norm_mod(x_ref[0], nw_ref[...], sc_ref[0], sh_ref[0]).astype(BF16)
        xn_ref[i] = xb
        dt_ref[0] = _dot(xb, wdt_ref[...])
        dtT_ref[0] = _nt(wdtT_ref[...], xb)
        return xb

    def rope_piece(r, wd, scale):
        lw = 2 * HEAD_DIM
        return jnp.concatenate([_rope_slab(r[:, s:s + lw], cos_ref[...], sa_ref[...], sb_ref[...]) * scale
                                for s in range(0, wd, lw)], axis=1)

    def conv_piece(ws, wd, c0):
        before = xn_ref[jnp.maximum(i - 1, 0), tm - HALO:tm, :]
        after = xn_ref[jnp.minimum(i + 1, ni - 1), 0:HALO, :]
        res = _dot(jnp.concatenate([before, xn_ref[i], after], axis=0), ws)
        r = res[HALO:HALO + tm]
        r_before = res[HALO - 1:HALO]
        r_after = res[HALO + tm:HALO + tm + 1]
        row = lax.broadcasted_iota(jnp.int32, (tm, wd), 0)
        if seq_len >= tm:
            r_before = jnp.where((i * tm) & (seq_len - 1) == 0, 0.0, r_before)
            r_after = jnp.where(((i + 1) * tm) & (seq_len - 1) == 0, 0.0, r_after)
            prev = jnp.where(row == 0, r_before, pltpu.roll(r, 1, axis=0))
            nxt = jnp.where(row == tm - 1, r_after, pltpu.roll(r, tm - 1, axis=0))
        else:
            pos = (i * tm + row) & (seq_len - 1)
            prev = jnp.where(pos == 0, 0.0, jnp.where(row == 0, r_before, pltpu.roll(r, 1, axis=0)))
            nxt = jnp.where(pos == seq_len - 1, 0.0,
                            jnp.where(row == tm - 1, r_after, pltpu.roll(r, tm - 1, axis=0)))
        cw = cw_ref[:, c0:c0 + wd]
        h = prev * cw[0:1] + r * cw[1:2] + nxt * cw[2:3] + cb_ref[:, c0:c0 + wd]
        return h + h * jnp.tanh(h)

    def tile(jt, xb=None):
        for off, wd, kind, arg in _tile_pieces(jt):
            ws = w_ref[:, off:off + wd]
            if kind == "conv":
                r = conv_piece(ws, wd, arg)
            else:
                r = _dot(xn_ref[i] if xb is None else xb, ws)
                if kind == "rope" and rope:
                    r = rope_piece(r, wd, arg)
            u_ref[0, :, off:off + wd] = r.astype(u_ref.dtype)

    for jt in range(U_COLS // U_TILE):
        @pl.when(j == jt)
        def _(jt=jt):
            tile(jt, normalise() if jt == 0 else None)


def _tile_pieces(jt):
    special = [(U_Q, D, "rope", LOG2E * HEAD_DIM ** -0.5), (U_K, D, "rope", 1.0),
               (U_XS, D, "conv", 0), (U_BC, XBC - D, "conv", D)]
    lo, hi = jt * U_TILE, (jt + 1) * U_TILE
    cuts = sorted({lo, hi} | {c for s0, w0, _, _ in special for c in (s0, s0 + w0) if lo < c < hi})
    pieces = []
    for a, b in zip(cuts[:-1], cuts[1:]):
        kind, arg, seg0 = "plain", None, a
        for s0, w0, k0, a0 in special:
            if s0 <= a and b <= s0 + w0:
                kind, arg, seg0 = k0, a0, s0
        for off in range(a, b, U_SUB):
            wd = min(U_SUB, b - off)
            piece_arg = arg + (off - seg0) if kind == "conv" else arg
            pieces.append((off - lo, wd, kind, piece_arg))
    return pieces


def in_proj(x, sh, sc, nw, w, wdt, wdtT, conv_w, conv_b, tm, seq_len, rope_tables=None):
    B, L, _ = x.shape
    assert seq_len & (seq_len - 1) == 0 and (tm % seq_len == 0 or seq_len % tm == 0)
    tn = U_TILE
    ni = L // tm
    grid = (B, U_COLS // tn, ni)
    rope = rope_tables is not None

    def first(j, i):
        return jnp.where(j == 0, i, ni - 1)

    last_rope_tile = (U_K + D - 1) // U_TILE

    def roped(j, i):
        return jnp.where(j <= last_rope_tile, i, ni - 1)

    rope_specs = [pl.BlockSpec((tm, 2 * HEAD_DIM), lambda b, j, i: (roped(j, i), 0))] * 3 if rope else []
    return pl.pallas_call(
        functools.partial(_inproj_kernel, rope=rope, seq_len=seq_len),
        grid=grid,
        in_specs=[pl.BlockSpec((1, tm, D), lambda b, j, i: (b, first(j, i), 0)),
                  pl.BlockSpec((1, 1, D), lambda b, j, i: (b, 0, 0)),
                  pl.BlockSpec((1, 1, D), lambda b, j, i: (b, 0, 0)),
                  pl.BlockSpec((1, D), lambda b, j, i: (0, 0)),
                  pl.BlockSpec((D, tn), lambda b, j, i: (0, j)),
                  pl.BlockSpec((D, DT_PAD), lambda b, j, i: (0, 0)),
                  pl.BlockSpec((2 * N_HEADS, D), lambda b, j, i: (0, 0)),
                  pl.BlockSpec((3, XBC), lambda b, j, i: (0, 0)),
                  pl.BlockSpec((1, XBC), lambda b, j, i: (0, 0))] + rope_specs,
        out_specs=[pl.BlockSpec((1, tm, tn), lambda b, j, i: (b, i, j)),
                   pl.BlockSpec((1, tm, DT_PAD), lambda b, j, i: (b, first(j, i), 0)),
                   pl.BlockSpec((1, 2 * N_HEADS, tm), lambda b, j, i: (b, 0, first(j, i)))],
        out_shape=[jax.ShapeDtypeStruct((B, L, U_COLS), BF16),
                   jax.ShapeDtypeStruct((B, L, DT_PAD), F32),
                   jax.ShapeDtypeStruct((B, 2 * N_HEADS, L), F32)],
        scratch_shapes=[pltpu.VMEM((ni, tm, D), BF16)],
        compiler_params=_cparams(("parallel", "arbitrary", "arbitrary")),
        name="in_proj",
    )(x, sh, sc, nw, w, wdt, wdtT, 0.5 * conv_w, 0.5 * conv_b.reshape(1, XBC), *(rope_tables if rope else ()))


def _split3(x):
    hi = x.astype(BF16)
    r1 = x - hi.astype(F32)
    mid = r1.astype(BF16)
    lo = (r1 - mid.astype(F32)).astype(BF16)
    return hi, mid, lo


def _head_stack(x2, lane):
    zero = jnp.zeros_like(x2)
    return jnp.concatenate([jnp.where(lane < HEAD_DIM, x2, zero), jnp.where(lane >= HEAD_DIM, x2, zero)], axis=0)


def _ssd_chunk(d, n, sub, rows, xs_ref, bc_ref, z_ref, dt_ref, dtT_ref, alr_ref, alc_ref, dbr_ref, dbc_ref, dsk_ref,
               nw_ref, ex_ref, y_ref, yacc_ref, st_ref):
    H = N_HEADS
    blk = slice(sub * CHUNK, (sub + 1) * CHUNK)
    xsb = xs_ref[n, blk, :]
    xs = xsb.astype(F32)
    bc = bc_ref[n, blk, :]
    dt_c = jax.nn.softplus(dt_ref[n, blk, :] + dbr_ref[...])
    a_c = dt_c * (-LOG2E * jnp.exp(alr_ref[...]))
    dtT_c = jax.nn.softplus(dtT_ref[n, :, blk] + dbc_ref[...])
    aT_c = dtT_c * (-LOG2E * jnp.exp(alc_ref[...]))

    ri = lax.broadcasted_iota(jnp.int32, (CHUNK, CHUNK), 0)
    ci = lax.broadcasted_iota(jnp.int32, (CHUNK, CHUNK), 1)
    tri = (ri >= ci) if d == 0 else (ri <= ci)
    trib = tri.astype(BF16)
    tribT = ((ri <= ci) if d == 0 else (ri >= ci)).astype(BF16)
    cs = _dot(jnp.concatenate([trib] * 3, axis=1), jnp.concatenate(_split3(a_c), axis=0))
    csT = _dot(jnp.concatenate(_split3(aT_c), axis=1), jnp.concatenate([tribT] * 3, axis=0))
    last = cs[CHUNK - 1:CHUNK] if d == 0 else cs[0:1]
    w_c = dt_c * jnp.exp2(last - cs)
    ein_c = jnp.exp2(cs)
    ex = ex_ref[d]
    w_e = _dot(jnp.concatenate(_split3(w_c)[:2], axis=1), ex)
    ein_e = _dot(jnp.concatenate(_split3(ein_c)[:2], axis=1), ex)
    cd_e = ein_e[CHUNK - 1:CHUNK] if d == 0 else ein_e[0:1]
    Xdec = (xs * w_e).astype(BF16)
    rowT = csT - jnp.log2(dtT_c)
    lane = lax.broadcasted_iota(jnp.int32, (CHUNK, 2 * HEAD_DIM), 1)

    ys = []
    GW = D // SSD_GROUPS
    for g in range(SSD_GROUPS):
        Bg = bc[:, g * SSD_STATE:(g + 1) * SSD_STATE]
        Cg = bc[:, (SSD_GROUPS + g) * SSD_STATE:(SSD_GROUPS + g + 1) * SSD_STATE]
        S = _nt(Cg, Bg)
        st = st_ref[n, d, g]
        y_off = _dot(Cg, st.astype(BF16)) * ein_e[:, g * GW:(g + 1) * GW]
        y_diag = []
        for p in range(GW // (2 * HEAD_DIM)):
            hA = g * (H // SSD_GROUPS) + 2 * p
            Ms = []
            for h in (hA, hA + 1):
                col = cs[:, d * H + h:d * H + h + 1]
                rowv = rowT[d * H + h:d * H + h + 1, :]
                Ms.append((S * jnp.exp2(jnp.where(tri, col - rowv, -jnp.inf))).astype(BF16))
            Mcat = jnp.concatenate(Ms, axis=1)
            Xp = xsb[:, hA * HEAD_DIM:(hA + 2) * HEAD_DIM]
            y_diag.append(_dot(Mcat, _head_stack(Xp, lane)))
        ys.append(y_off + jnp.concatenate(y_diag, axis=1))
        st_ref[n, d, g] = st * cd_e[:, g * GW:(g + 1) * GW] + _tn(Bg, Xdec[:, g * GW:(g + 1) * GW])
    y = jnp.concatenate(ys, axis=1)

    if d == 0:
        yacc_ref[n, rows, :] = y
    else:
        ysum = yacc_ref[n, rows, :] + y + xs * (dsk_ref[0:1] + dsk_ref[1:2])
        yg = ysum * _silu(z_ref[n, blk, :].astype(F32))
        parts = []
        for g in range(SSD_GROUPS):
            v = yg[:, g * GW:(g + 1) * GW]
            ms = jnp.mean(v * v, axis=-1, keepdims=True)
            parts.append(v * lax.rsqrt(ms + EPS) * nw_ref[:, g * GW:(g + 1) * GW])
        y_ref[n, blk, :] = jnp.concatenate(parts, axis=1).astype(y_ref.dtype)


def _ssd_kernel(xs_ref, bc_ref, z_ref, dt_ref, dtT_ref, h0_ref, alr_ref, alc_ref, dbr_ref, dbc_ref, dsk_ref, nw_ref,
                ex_ref, y_ref, sto_ref, yacc_ref, st_ref, *, ns, nsub):
    t = pl.program_id(1)
    nb = xs_ref.shape[0]

    @pl.when(t == 0)
    def _():
        st_ref[...] = h0_ref[...]

    args = (xs_ref, bc_ref, z_ref, dt_ref, dtT_ref, alr_ref, alc_ref, dbr_ref, dbc_ref, dsk_ref, nw_ref,
            ex_ref, y_ref, yacc_ref, st_ref)

    def seq_rows(step, sub):
        return pl.ds(pl.multiple_of((step * nsub + sub) * CHUNK, CHUNK), CHUNK)

    @pl.when(t < ns)
    def _():
        for sub in range(nsub):
            for n in range(nb):
                _ssd_chunk(0, n, sub, seq_rows(t, sub), *args)

    @pl.when(t >= ns)
    def _():
        for sub in reversed(range(nsub)):
            for n in range(nb):
                _ssd_chunk(1, n, sub, seq_rows(2 * ns - 1 - t, sub), *args)

    @pl.when(t == 2 * ns - 1)
    def _():
        sto_ref[...] = st_ref[...]


def ssd_mix(u, dt, dtT, h0, a_log, dt_bias, d_skip_e, norm_w, nb=1):
    B, L, _ = u.shape
    nsub = min(SSD_SUB, L // CHUNK)
    RB = nsub * CHUNK
    ns = L // RB
    NB = math.gcd(B, nb)
    H2 = 2 * N_HEADS

    def chunk(t):
        return jnp.where(t < ns, t, 2 * ns - 1 - t)

    def late(t):
        return jnp.where(t < ns, ns - 1, 2 * ns - 1 - t)

    st_shape = (2, SSD_GROUPS, SSD_STATE, D // SSD_GROUPS)
    st_spec = pl.BlockSpec((NB,) + st_shape, lambda b, t: (b, 0, 0, 0, 0))
    small = lambda shape: pl.BlockSpec(shape, lambda b, t: (0,) * len(shape))
    krow = jnp.arange(2 * DT_PAD, dtype=jnp.int32)[:, None] % DT_PAD
    head = jnp.arange(D, dtype=jnp.int32)[None, :] // HEAD_DIM
    ex = jnp.stack([krow == d * N_HEADS + head for d in range(2)]).astype(BF16)
    lane_pad = lambda v: jnp.pad(v.reshape(1, H2), ((0, 0), (0, DT_PAD - H2)))
    return pl.pallas_call(
        functools.partial(_ssd_kernel, ns=ns, nsub=nsub),
        grid=(B // NB, 2 * ns),
        in_specs=[pl.BlockSpec((NB, RB, D), lambda b, t: (b, chunk(t), U_XS // D)),
                  pl.BlockSpec((NB, RB, XBC - D), lambda b, t: (b, chunk(t), U_BC // (XBC - D))),
                  pl.BlockSpec((NB, RB, D), lambda b, t: (b, late(t), U_Z // D)),
                  pl.BlockSpec((NB, RB, DT_PAD), lambda b, t: (b, chunk(t), 0)),
                  pl.BlockSpec((NB, H2, RB), lambda b, t: (b, 0, chunk(t))),
                  st_spec,
                  small((1, DT_PAD)), small((H2, 1)), small((1, DT_PAD)), small((H2, 1)),
                  small((2, D)), small((1, D)), small((2, 2 * DT_PAD, D))],
        out_specs=[pl.BlockSpec((NB, RB, D), lambda b, t: (b, late(t), 0)),
                   st_spec],
        out_shape=[jax.ShapeDtypeStruct((B, L, D), BF16),
                   jax.ShapeDtypeStruct((B,) + st_shape, F32)],
        scratch_shapes=[pltpu.VMEM((NB, L, D), F32), pltpu.VMEM((NB,) + st_shape, F32)],
        compiler_params=_cparams(("parallel", "arbitrary")),
        name="ssd_mix",
    )(u, u, u, dt, dtT, h0,
      lane_pad(a_log), a_log.reshape(H2, 1), lane_pad(dt_bias), dt_bias.reshape(H2, 1),
      d_skip_e, norm_w.reshape(1, D), ex)


NA_GROUP = 1


def _na_kernel(q_ref, k_ref, v_ref, kc_ref, vc_ref, bias_ref, o_ref, s_ref, p_ref, inv_ref, kt_ref, kto_ref, *, rows):
    W = GRID_W
    band = NA_KH * W
    lane = lax.broadcasted_iota(jnp.int32, (W, 2 * HEAD_DIM), 1)
    G = NA_GROUP
    n_groups = rows // G
    M2 = 2 * W
    kt_ref[...] = k_ref[0].T
    kto_ref[...] = k_ref[0, W:W + kto_ref.shape[1], :].T
    kct = kc_ref[0].T

    def band_start(r):
        return min(max(r - NA_KH // 2, 0), rows - NA_KH)

    def stage_scores(g, slot):
        for i in range(G):
            r = g * G + i
            r0 = band_start(r)
            qs = _head_stack(q_ref[0, r * W:(r + 1) * W, :], lane)
            ktb, t0 = (kt_ref, r0 * W) if r0 % 2 == 0 else (kto_ref, (r0 - 1) * W)
            base = r0 - r + NA_KH - 1
            for jp in range(NA_KH // 2):
                cols = slice(jp * M2, (jp + 1) * M2)
                s_ref[slot, i * M2:(i + 1) * M2, cols] = (_dot(qs, ktb[:, t0 + jp * M2:t0 + (jp + 1) * M2])
                                                          + bias_ref[0, base + 2 * jp])
            s_ref[slot, i * M2:(i + 1) * M2, band:] = _dot(qs, kct)

    def stage_softmax(slot):
        for i in range(G):
            s = s_ref[slot, i * M2:(i + 1) * M2, :]
            p = jnp.exp2(s - jnp.max(s, axis=-1, keepdims=True))
            inv = 1.0 / jnp.sum(p, axis=-1, keepdims=True)
            p_ref[slot, i * M2:(i + 1) * M2, :] = p.astype(BF16)
            inv_ref[slot, i * M2:(i + 1) * M2, :] = jnp.broadcast_to(inv, (M2, M2))

    def stage_values(g, slot):
        for i in range(G):
            r = g * G + i
            r0 = band_start(r)
            vb = v_ref[0, r0 * W:r0 * W + band, :]
            p = p_ref[slot, i * M2:(i + 1) * M2, :]
            o2 = (_dot(p[:, :band], vb) + _dot(p[:, band:], vc_ref[0])) * inv_ref[slot, i * M2:(i + 1) * M2, :]
            o = jnp.where(lane < HEAD_DIM, o2[:W], o2[W:])
            o_ref[0, r * W:(r + 1) * W, :] = o.astype(o_ref.dtype)

    stage_scores(0, 0)
    stage_scores(1, 1)
    stage_softmax(0)

    for g in range(n_groups - 2):
        stage_scores(g + 2, g % 2)
        stage_softmax((g + 1) % 2)
        stage_values(g, g % 2)
    stage_softmax((n_groups - 1) % 2)
    stage_values(n_groups - 2, (n_groups - 2) % 2)
    stage_values(n_groups - 1, (n_groups - 1) % 2)


def na_attend(u, u_ctx, bias):
    B, L, _ = u.shape
    Lc = u_ctx.shape[1]
    HP = N_HEADS // 2
    lw = 2 * HEAD_DIM
    oq, ok, ov = U_Q // lw, U_K // lw, U_V // lw
    return pl.pallas_call(
        functools.partial(_na_kernel, rows=L // GRID_W),
        grid=(HP, B),
        in_specs=[pl.BlockSpec((1, L, lw), lambda h, b: (b, 0, oq + h)),
                  pl.BlockSpec((1, L, lw), lambda h, b: (b, 0, ok + h)),
                  pl.BlockSpec((1, L, lw), lambda h, b: (b, 0, ov + h)),
                  pl.BlockSpec((1, Lc, lw), lambda h, b: (b, 0, ok + h)),
                  pl.BlockSpec((1, Lc, lw), lambda h, b: (b, 0, ov + h)),
                  pl.BlockSpec((1, 2 * NA_KH - 2, 2 * GRID_W, 2 * GRID_W), lambda h, b: (h, 0, 0, 0))],
        out_specs=pl.BlockSpec((1, L, lw), lambda h, b: (b, 0, h)),
        out_shape=jax.ShapeDtypeStruct((B, L, D), BF16),
        scratch_shapes=[pltpu.VMEM((2, NA_GROUP * lw, NA_KH * GRID_W + Lc), F32),
                        pltpu.VMEM((2, NA_GROUP * lw, NA_KH * GRID_W + Lc), BF16),
                        pltpu.VMEM((2, NA_GROUP * lw, lw), F32),
                        pltpu.VMEM((lw, L), BF16), pltpu.VMEM((lw, L - 2 * GRID_W), BF16)],
        compiler_params=_cparams(("parallel", "parallel")),
        name="na_attend",
    )(u, u, u, u_ctx, u_ctx, bias)


CTX_NB = 4


def _ctxattn_kernel(q_ref, k_ref, v_ref, o_ref):
    Lc = q_ref.shape[1]
    lane = lax.broadcasted_iota(jnp.int32, (Lc, 2 * HEAD_DIM), 1)
    for n in range(q_ref.shape[0]):
        q2 = (q_ref[n].astype(F32) * (HEAD_DIM ** -0.5)).astype(BF16)
        s = _nt(_head_stack(q2, lane), k_ref[n])
        m = jnp.max(s, axis=-1, keepdims=True)
        p = jnp.exp(s - m)
        p = (p * (1.0 / jnp.sum(p, axis=-1, keepdims=True))).astype(BF16)
        o2 = _dot(p, v_ref[n])
        o_ref[n] = jnp.where(lane < HEAD_DIM, o2[:Lc], o2[Lc:]).astype(o_ref.dtype)


def ctx_attend(u_ctx):
    B, Lc, _ = u_ctx.shape
    HP = N_HEADS // 2
    lw = 2 * HEAD_DIM
    oq, ok, ov = U_Q // lw, U_K // lw, U_V // lw
    nb = math.gcd(B, CTX_NB)
    return pl.pallas_call(
        _ctxattn_kernel,
        grid=(HP, B // nb),
        in_specs=[pl.BlockSpec((nb, Lc, lw), lambda h, b: (b, 0, oq + h)),
                  pl.BlockSpec((nb, Lc, lw), lambda h, b: (b, 0, ok + h)),
                  pl.BlockSpec((nb, Lc, lw), lambda h, b: (b, 0, ov + h))],
        out_specs=pl.BlockSpec((nb, Lc, lw), lambda h, b: (b, 0, h)),
        out_shape=jax.ShapeDtypeStruct((B, Lc, D), BF16),
        compiler_params=_cparams(("parallel", "parallel")),
        name="ctx_attend",
    )(u_ctx, u_ctx, u_ctx)


MERGE_CK = 256


def _merge_kernel(cb_ref, cc_ref, cx_ref, ccp_ref, cxp_ref, ccn_ref, cxn_ref, cw_ref, ys_ref, yn_ref, gc_ref, gs_ref,
                  gn_ref, h_ref, gt_ref, wc_ref, ws_ref, wn_ref, wo_ref, o_ref, *, seq_len):
    tm = h_ref.shape[1]
    row = lax.broadcasted_iota(jnp.int32, (tm, MERGE_CK), 0)
    pos = (pl.program_id(1) * tm + row) & (seq_len - 1)
    first, last = pos == 0, pos == seq_len - 1
    top, bottom = row == 0, row == tm - 1
    yc_proj = None
    for c in range(0, D, MERGE_CK):
        ch = slice(c, c + MERGE_CK)
        p = cc_ref[0, :, ch].astype(F32) * cx_ref[0, :, ch].astype(F32)
        p_before = (ccp_ref[0, :, ch].astype(F32) * cxp_ref[0, :, ch].astype(F32))[HALO - 1:HALO]
        p_after = (ccn_ref[0, :, ch].astype(F32) * cxn_ref[0, :, ch].astype(F32))[0:1]
        prev = jnp.where(first, 0.0, jnp.where(top, p_before, pltpu.roll(p, 1, axis=0)))
        nxt = jnp.where(last, 0.0, jnp.where(bottom, p_after, pltpu.roll(p, tm - 1, axis=0)))
        w = cw_ref[:, ch]
        yc = (cb_ref[0, :, ch].astype(F32) * (prev * w[0:1] + p * w[1:2] + nxt * w[2:3])).astype(BF16)
        t = _dot(yc, wc_ref[ch, :])
        yc_proj = t if yc_proj is None else yc_proj + t

    gate = lambda ref: _sigmoid(ref[0].astype(F32))
    m = (gate(gc_ref) * yc_proj
         + gate(gs_ref) * _dot(ys_ref[0], ws_ref[...])
         + gate(gn_ref) * _dot(yn_ref[0], wn_ref[...]))
    o_ref[0] = h_ref[0] + gt_ref[0] * _dot(m.astype(BF16), wo_ref[...])


def merge(u, ys, yn, h, gt, cw, wc, ws, wn, wo, tm, seq_len):
    B, L, _ = h.shape
    assert seq_len & (seq_len - 1) == 0 and (tm % seq_len == 0 or seq_len % tm == 0)
    nb, hb = tm // HALO, L // HALO
    tok = lambda: pl.BlockSpec((1, tm, D), lambda b, i: (b, i, 0))
    ucol = lambda c: pl.BlockSpec((1, tm, D), lambda b, i: (b, i, c // D))
    before = lambda c: pl.BlockSpec((1, HALO, D), lambda b, i: (b, jnp.maximum(i * nb - 1, 0), c // D))
    after = lambda c: pl.BlockSpec((1, HALO, D), lambda b, i: (b, jnp.minimum((i + 1) * nb, hb - 1), c // D))
    wsp = lambda: pl.BlockSpec((D, D), lambda b, i: (0, 0))
    return pl.pallas_call(
        functools.partial(_merge_kernel, seq_len=seq_len),
        grid=(B, L // tm),
        in_specs=[ucol(U_CB), ucol(U_CC), ucol(U_CX), before(U_CC), before(U_CX), after(U_CC), after(U_CX),
                  pl.BlockSpec((3, D), lambda b, i: (0, 0)),
                  tok(), tok(),
                  ucol(U_GATE), ucol(U_GATE + D), ucol(U_GATE + 2 * D),
                  tok(),
                  pl.BlockSpec((1, 1, D), lambda b, i: (b, 0, 0)),
                  wsp(), wsp(), wsp(), wsp()],
        out_specs=tok(),
        out_shape=jax.ShapeDtypeStruct((B, L, D), F32),
        compiler_params=_cparams(("parallel", "parallel")),
        name="merge",
    )(u, u, u, u, u, u, u, cw, ys, yn, u, u, u, h, gt, wc, ws, wn, wo)


MLP_TF = 1024


def _mlp_kernel(h_ref, sh_ref, sc_ref, gt_ref, nw_ref, fw_ref, w1_ref, w2_ref, o_ref, *, final):
    h = h_ref[0]
    xn = _norm_mod(h, nw_ref[...], sc_ref[0], sh_ref[0]).astype(BF16)
    acc = None
    for k in range(0, D_FF, MLP_TF):
        a = jnp.square(jnp.maximum(_dot(xn, w1_ref[:, k:k + MLP_TF]), 0.0)).astype(BF16)
        t = _dot(a, w2_ref[k:k + MLP_TF, :])
        acc = t if acc is None else acc + t
    o = h + gt_ref[0] * acc
    if final:
        ms = jnp.mean(o * o, axis=-1, keepdims=True)
        o = o * lax.rsqrt(ms + EPS) * fw_ref[...]
    o_ref[0] = o


def mlp(h, sh, sc, gt, nw, fw, w1, w2, tm, final):
    B, L, _ = h.shape
    vec = lambda: pl.BlockSpec((1, 1, D), lambda b, i: (b, 0, 0))
    par = lambda: pl.BlockSpec((1, D), lambda b, i: (0, 0))
    resident = lambda shape: pl.BlockSpec(shape, lambda b, i: (0, 0), pipeline_mode=pl.Buffered(1))
    return pl.pallas_call(
        functools.partial(_mlp_kernel, final=final),
        grid=(B, L // tm),
        in_specs=[pl.BlockSpec((1, tm, D), lambda b, i: (b, i, 0)),
                  vec(), vec(), vec(), par(), par(),
                  resident((D, D_FF)), resident((D_FF, D))],
        out_specs=pl.BlockSpec((1, tm, D), lambda b, i: (b, i, 0)),
        out_shape=jax.ShapeDtypeStruct((B, L, D), F32),
        compiler_params=_cparams(("parallel", "parallel")),
        name="mlp",
    )(h, sh, sc, gt, nw, fw, w1, w2)


def _rope_tables(L):
    t = jnp.arange(L, dtype=jnp.int32)
    row = (t // GRID_W).astype(F32)
    col = (t % GRID_W).astype(F32)
    half = HEAD_DIM // 2
    inv = ROPE_BASE ** (-jnp.arange(0, half, 2, dtype=F32) / half)
    ang_r = row[:, None] * inv
    ang_c = col[:, None] * inv
    ang = jnp.concatenate([ang_r, ang_r, ang_c, ang_c], axis=-1)
    cos = jnp.tile(jnp.cos(ang), (1, 2))
    sin = jnp.tile(jnp.sin(ang), (1, 2))
    even = ((jnp.arange(2 * HEAD_DIM) // (half // 2)) % 2 == 0)[None, :]
    return cos, jnp.where(even, -sin, 0.0), jnp.where(even, 0.0, sin)


def _rpb_kernel(r_ref, oh_ref, ok_ref, o_ref):
    val = _dot(jnp.concatenate(_split3(r_ref[...]), axis=1), oh_ref[...])
    o_ref[...] = jnp.where(ok_ref[...] > 0.0, LOG2E * val, -jnp.inf)


def _na_bias_table(rpb):
    H, NR, NC = rpb.shape
    W = GRID_W
    col = np.arange(W)
    col_start = np.clip(col - NA_KW // 2, 0, W - NA_KW)
    col_ok = (col[None, :] >= col_start[:, None]) & (col[None, :] < col_start[:, None] + NA_KW)
    dc_idx = np.clip(col[None, :] - col[:, None], -(NA_KW - 1), NA_KW - 1) + NA_KW - 1
    rows = jnp.transpose(rpb.reshape(H // 2, 2, NR, NC), (0, 2, 1, 3)).reshape(H * NR, NC)
    rows = jnp.pad(rows, ((0, 0), (0, DT_PAD - NC)))
    krow = jnp.arange(3 * DT_PAD, dtype=jnp.int32)[:, None] % DT_PAD
    onehot = (krow == jnp.asarray(dc_idx.reshape(1, W * W), jnp.int32)).astype(BF16)
    ok = jnp.asarray(col_ok.reshape(1, W * W), F32)
    tn = 1024
    tab = pl.pallas_call(
        _rpb_kernel,
        grid=(W * W // tn,),
        in_specs=[pl.BlockSpec((H * NR, DT_PAD), lambda j: (0, 0)),
                  pl.BlockSpec((3 * DT_PAD, tn), lambda j: (0, j)),
                  pl.BlockSpec((1, tn), lambda j: (0, j))],
        out_specs=pl.BlockSpec((H * NR, tn), lambda j: (0, j)),
        out_shape=jax.ShapeDtypeStruct((H * NR, W * W), F32),
        compiler_params=_cparams(("arbitrary",)),
        name="rpb_table",
    )(rows, onehot, ok)
    tab = tab.reshape(H // 2, NR, 2 * W, W)
    return jnp.concatenate([tab[:, :NR - 1], tab[:, 1:]], axis=-1)


def _prep_w_in(w_in):
    order = [(R_Q, D), (R_V, D), (R_Z, D), (R_K, D), (R_CB, D), (R_CC, D), (R_XBC, D), (R_CX, D), (R_GATE, 3 * D),
             (R_XBC + D, XBC - D)]
    w = jnp.concatenate([w_in[:, o:o + n] for o, n in order], axis=1).astype(BF16)
    wdt = w_in[:, R_DT:R_DT + 2 * N_HEADS].astype(BF16)
    return w, jnp.pad(wdt, ((0, 0), (0, DT_PAD - 2 * N_HEADS))), wdt.T


def kernel(x, c, ctx, c_ctx, w_ada, b_ada, norm1_w, w_in, conv_mix_w, ssd_conv_w, ssd_conv_b, ssd_a_log, ssd_dt_bias,
           ssd_d, ssd_norm_w, na_rpb, w_br_conv, w_br_ssd, w_br_na, w_out, norm2_w, w_ff1, w_ff2, final_norm_w):
    B, L, _ = x.shape
    Lc = ctx.shape[1]
    depth = w_in.shape[0]
    cos, sa, sb = _rope_tables(L)
    n_mod = B + 1
    pad = (-n_mod) % 8
    c_rows = jnp.concatenate([c, c_ctx[None, :], jnp.zeros((pad, D), F32)], axis=0)
    zero_state = jnp.zeros((B, 2, SSD_GROUPS, SSD_STATE, D // SSD_GROUPS), F32)
    fw = final_norm_w.reshape(1, D)
    Tc = B * Lc
    tmc = min(TM_PROJ, Tc)
    flat = lambda a: a.reshape(1, Tc, a.shape[-1])
    h, hc = x, flat(ctx)
    for l in range(depth):
        last = l == depth - 1
        mod = ada_mod(c_rows, w_ada[l], b_ada[l])
        m_lat = mod[:B].reshape(B, 1, 6, D)
        m_ctx = mod[B:B + 1].reshape(1, 1, 6, D)
        sh1, sc1, gt1, sh2, sc2, gt2 = (m_lat[:, :, i] for i in range(6))
        csh1, csc1, cgt1, csh2, csc2, cgt2 = (m_ctx[:, :, i] for i in range(6))
        w, wdt, wdtT = _prep_w_in(w_in[l])
        nw1 = norm1_w[l].reshape(1, D)
        conv_p = (ssd_conv_w[l], ssd_conv_b[l])
        u, dt, dtT = in_proj(h, sh1, sc1, nw1, w, wdt, wdtT, *conv_p, tm=TM_PROJ, seq_len=L, rope_tables=(cos, sa, sb))
        uc, dtc, dtTc = in_proj(hc, csh1, csc1, nw1, w, wdt, wdtT, *conv_p, tm=tmc, seq_len=Lc)
        uc = uc.reshape(B, Lc, U_COLS)
        dtc = dtc.reshape(B, Lc, DT_PAD)
        dtTc = jnp.transpose(dtTc.reshape(2 * N_HEADS, B, Lc), (1, 0, 2))
        d_skip_e = jnp.repeat(ssd_d[l], HEAD_DIM, axis=1)
        ssd_p = (ssd_a_log[l], ssd_dt_bias[l], d_skip_e, ssd_norm_w[l])
        y_ssd_c, ctx_states = ssd_mix(uc, dtc, dtTc, zero_state, *ssd_p, nb=SSD_NB_CTX)
        y_ssd, _ = ssd_mix(u, dt, dtT, ctx_states, *ssd_p)
        y_na = na_attend(u, uc, _na_bias_table(na_rpb[l]))
        wb = [t[l].astype(BF16) for t in (w_br_conv, w_br_ssd, w_br_na, w_out)]
        w1, w2 = w_ff1[l].astype(BF16), w_ff2[l].astype(BF16)
        nw2 = norm2_w[l].reshape(1, D)
        h = merge(u, y_ssd, y_na, h, gt1, conv_mix_w[l], *wb, tm=TM_MERGE, seq_len=L)
        h = mlp(h, sh2, sc2, gt2, nw2, fw, w1, w2, tm=TM_PROJ, final=last)
        if not last:
            y_na_c = ctx_attend(uc)
            hc = merge(flat(uc), flat(y_ssd_c), flat(y_na_c), hc, cgt1, conv_mix_w[l], *wb, tm=min(TM_MERGE, Tc), seq_len=Lc)
            hc = mlp(hc, csh2, csc2, cgt2, nw2, fw, w1, w2, tm=tmc, final=False)
    return h
```

```python
import functools
import math

import jax
import jax.numpy as jnp
import numpy as np
from jax import lax
from jax.experimental import pallas as pl
from jax.experimental.pallas import tpu as pltpu

F32 = jnp.float32
BF16 = jnp.bfloat16
HIGHEST = lax.Precision.HIGHEST
LOG2E = math.log2(math.e)

D = 1024
EPS = 1e-6
GRID_W = 64
N_HEADS = 16
HEAD_DIM = 64
SSD_GROUPS = 2
SSD_STATE = 128
CHUNK = 128
NA_KH = 8
NA_KW = 16
ROPE_BASE = 10000.0
D_FF = 4 * D
XBC = D + 2 * SSD_GROUPS * SSD_STATE
R_CB, R_CC, R_CX, R_Z, R_XBC, R_DT, R_Q, R_K, R_V, R_GATE = (
    0, 1024, 2048, 3072, 4096, 5632, 5664, 6688, 7712, 8736)

LANES = 128
HALO = 16
VMEM_LIMIT = 56 * 1024 * 1024

U_Q, U_V, U_Z, U_K, U_CB, U_CC, U_XS, U_CX, U_GATE, U_BC, U_COLS = (
    0, 1024, 2048, 3072, 4096, 5120, 6144, 7168, 8192, 11264, 11776)
U_TILE = 23 * LANES
U_SUB = 512
DT_PAD = LANES
TM_PROJ = 1024
TM_MERGE = 512
ADA_TN = 1536
SSD_SUB = 8
SSD_NB_CTX = 4


def _cparams(sem):
    return pltpu.CompilerParams(dimension_semantics=sem, vmem_limit_bytes=VMEM_LIMIT)


def _nt(a, b):
    return lax.dot_general(a, b, (((1,), (1,)), ((), ())), preferred_element_type=F32)


def _tn(a, b):
    return lax.dot_general(a, b, (((0,), (0,)), ((), ())), preferred_element_type=F32)


def _dot(a, b):
    return jnp.dot(a, b, preferred_element_type=F32)


def _dot_hi(a, b):
    return jnp.dot(a, b, preferred_element_type=F32, precision=HIGHEST)


def _sigmoid(x):
    return 0.5 * jnp.tanh(0.5 * x) + 0.5


def _silu(x):
    return x * _sigmoid(x)


def _norm_mod(x, nw, sc, sh):
    ms = jnp.mean(x * x, axis=-1, keepdims=True)
    y = x * lax.rsqrt(ms + EPS) * nw
    return y * (1.0 + sc) + sh


def _ada_kernel(c_ref, w_ref, b_ref, o_ref):
    o_ref[...] = _dot_hi(_silu(c_ref[...]), w_ref[...]) + b_ref[...]


def ada_mod(c_rows, w_ada, b_ada):
    rows = c_rows.shape[0]
    tn = ADA_TN
    return pl.pallas_call(
        _ada_kernel,
        grid=(6 * D // tn,),
        in_specs=[pl.BlockSpec((rows, D), lambda j: (0, 0)),
                  pl.BlockSpec((D, tn), lambda j: (0, j)),
                  pl.BlockSpec((1, tn), lambda j: (0, j))],
        out_specs=pl.BlockSpec((rows, tn), lambda j: (0, j)),
        out_shape=jax.ShapeDtypeStruct((rows, 6 * D), F32),
        compiler_params=_cparams(("arbitrary",)),
        name="ada_mod",
    )(c_rows, w_ada, b_ada.reshape(1, 6 * D))


def _rope_slab(x, cos, sa, sb):
    return x * cos + pltpu.roll(x, 2 * HEAD_DIM - 16, axis=1) * sa + pltpu.roll(x, 16, axis=1) * sb


def _inproj_kernel(x_ref, sh_ref, sc_ref, nw_ref, w_ref, wdt_ref, wdtT_ref, cw_ref, cb_ref, *rest, rope, seq_len):
    if rope:
        cos_ref, sa_ref, sb_ref, u_ref, dt_ref, dtT_ref, xn_ref = rest
    else:
        u_ref, dt_ref, dtT_ref, xn_ref = rest
    j, i = pl.program_id(1), pl.program_id(2)
    ni, tm = xn_ref.shape[0], xn_ref.shape[1]

    def normalise():
        xb = _norm_mod(x_ref[0], nw_ref[...], sc_ref[0], sh_ref[0]).astype(BF16)
        xn_ref[i] = xb
        dt_ref[0] = _dot(xb, wdt_ref[...])
        dtT_ref[0] = _nt(wdtT_ref[...], xb)
        return xb

    def rope_piece(r, wd, scale):
        lw = 2 * HEAD_DIM
        return jnp.concatenate([_rope_slab(r[:, s:s + lw], cos_ref[...], sa_ref[...], sb_ref[...]) * scale
                                for s in range(0, wd, lw)], axis=1)

    def conv_piece(ws, wd, c0):
        before = xn_ref[jnp.maximum(i - 1, 0), tm - HALO:tm, :]
        after = xn_ref[jnp.minimum(i + 1, ni - 1), 0:HALO, :]
        res = _dot(jnp.concatenate([before, xn_ref[i], after], axis=0), ws)
        r = res[HALO:HALO + tm]
        r_before = res[HALO - 1:HALO]
        r_after = res[HALO + tm:HALO + tm + 1]
        row = lax.broadcasted_iota(jnp.int32, (tm, wd), 0)
        if seq_len >= tm:
            r_before = jnp.where((i * tm) & (seq_len - 1) == 0, 0.0, r_before)
            r_after = jnp.where(((i + 1) * tm) & (seq_len - 1) == 0, 0.0, r_after)
            prev = jnp.where(row == 0, r_before, pltpu.roll(r, 1, axis=0))
            nxt = jnp.where(row == tm - 1, r_after, pltpu.roll(r, tm - 1, axis=0))
        else:
            pos = (i * tm + row) & (seq_len - 1)
            prev = jnp.where(pos == 0, 0.0, jnp.where(row == 0, r_before, pltpu.roll(r, 1, axis=0)))
            nxt = jnp.where(pos == seq_len - 1, 0.0,
                            jnp.where(row == tm - 1, r_after, pltpu.roll(r, tm - 1, axis=0)))
        cw = cw_ref[:, c0:c0 + wd]
        h = prev * cw[0:1] + r * cw[1:2] + nxt * cw[2:3] + cb_ref[:, c0:c0 + wd]
        return h + h * jnp.tanh(h)

    def tile(jt, xb=None):
        for off, wd, kind, arg in _tile_pieces(jt):
            ws = w_ref[:, off:off + wd]
            if kind == "conv":
                r = conv_piece(ws, wd, arg)
            else:
                r = _dot(xn_ref[i] if xb is None else xb, ws)
                if kind == "rope" and rope:
                    r = rope_piece(r, wd, arg)
            u_ref[0, :, off:off + wd] = r.astype(u_ref.dtype)

    for jt in range(U_COLS // U_TILE):
        @pl.when(j == jt)
        def _(jt=jt):
            tile(jt, normalise() if jt == 0 else None)


def _tile_pieces(jt):
    special = [(U_Q, D, "rope", LOG2E * HEAD_DIM ** -0.5), (U_K, D, "rope", 1.0),
               (U_XS, D, "conv", 0), (U_BC, XBC - D, "conv", D)]
    lo, hi = jt * U_TILE, (jt + 1) * U_TILE
    cuts = sorted({lo, hi} | {c for s0, w0, _, _ in special for c in (s0, s0 + w0) if lo < c < hi})
    pieces = []
    for a, b in zip(cuts[:-1], cuts[1:]):
        kind, arg, seg0 = "plain", None, a
        for s0, w0, k0, a0 in special:
            if s0 <= a and b <= s0 + w0:
                kind, arg, seg0 = k0, a0, s0
        for off in range(a, b, U_SUB):
            wd = min(U_SUB, b - off)
            piece_arg = arg + (off - seg0) if kind == "conv" else arg
            pieces.append((off - lo, wd, kind, piece_arg))
    return pieces


def in_proj(x, sh, sc, nw, w, wdt, wdtT, conv_w, conv_b, tm, seq_len, rope_tables=None):
    B, L, _ = x.shape
    assert seq_len & (seq_len - 1) == 0 and (tm % seq_len == 0 or seq_len % tm == 0)
    tn = U_TILE
    ni = L // tm
    grid = (B, U_COLS // tn, ni)
    rope = rope_tables is not None

    def first(j, i):
        return jnp.where(j == 0, i, ni - 1)

    last_rope_tile = (U_K + D - 1) // U_TILE

    def roped(j, i):
        return jnp.where(j <= last_rope_tile, i, ni - 1)

    rope_specs = [pl.BlockSpec((tm, 2 * HEAD_DIM), lambda b, j, i: (roped(j, i), 0))] * 3 if rope else []
    return pl.pallas_call(
        functools.partial(_inproj_kernel, rope=rope, seq_len=seq_len),
        grid=grid,
        in_specs=[pl.BlockSpec((1, tm, D), lambda b, j, i: (b, first(j, i), 0)),
                  pl.BlockSpec((1, 1, D), lambda b, j, i: (b, 0, 0)),
                  pl.BlockSpec((1, 1, D), lambda b, j, i: (b, 0, 0)),
                  pl.BlockSpec((1, D), lambda b, j, i: (0, 0)),
                  pl.BlockSpec((D, tn), lambda b, j, i: (0, j)),
                  pl.BlockSpec((D, DT_PAD), lambda b, j, i: (0, 0)),
                  pl.BlockSpec((2 * N_HEADS, D), lambda b, j, i: (0, 0)),
                  pl.BlockSpec((3, XBC), lambda b, j, i: (0, 0)),
                  pl.BlockSpec((1, XBC), lambda b, j, i: (0, 0))] + rope_specs,
        out_specs=[pl.BlockSpec((1, tm, tn), lambda b, j, i: (b, i, j)),
                   pl.BlockSpec((1, tm, DT_PAD), lambda b, j, i: (b, first(j, i), 0)),
                   pl.BlockSpec((1, 2 * N_HEADS, tm), lambda b, j, i: (b, 0, first(j, i)))],
        out_shape=[jax.ShapeDtypeStruct((B, L, U_COLS), BF16),
                   jax.ShapeDtypeStruct((B, L, DT_PAD), F32),
                   jax.ShapeDtypeStruct((B, 2 * N_HEADS, L), F32)],
        scratch_shapes=[pltpu.VMEM((ni, tm, D), BF16)],
        compiler_params=_cparams(("parallel", "arbitrary", "arbitrary")),
        name="in_proj",
    )(x, sh, sc, nw, w, wdt, wdtT, 0.5 * conv_w, 0.5 * conv_b.reshape(1, XBC), *(rope_tables if rope else ()))


def _split3(x):
    hi = x.astype(BF16)
    r1 = x - hi.astype(F32)
    mid = r1.astype(BF16)
    lo = (r1 - mid.astype(F32)).astype(BF16)
    return hi, mid, lo


def _head_stack(x2, lane):
    zero = jnp.zeros_like(x2)
    return jnp.concatenate([jnp.where(lane < HEAD_DIM, x2, zero), jnp.where(lane >= HEAD_DIM, x2, zero)], axis=0)


def _ssd_chunk(d, n, sub, rows, xs_ref, bc_ref, z_ref, dt_ref, dtT_ref, alr_ref, alc_ref, dbr_ref, dbc_ref, dsk_ref,
               nw_ref, ex_ref, y_ref, yacc_ref, st_ref):
    H = N_HEADS
    blk = slice(sub * CHUNK, (sub + 1) * CHUNK)
    xsb = xs_ref[n, blk, :]
    xs = xsb.astype(F32)
    bc = bc_ref[n, blk, :]
    dt_c = jax.nn.softplus(dt_ref[n, blk, :] + dbr_ref[...])
    a_c = dt_c * (-LOG2E * jnp.exp(alr_ref[...]))
    dtT_c = jax.nn.softplus(dtT_ref[n, :, blk] + dbc_ref[...])
    aT_c = dtT_c * (-LOG2E * jnp.exp(alc_ref[...]))

    ri = lax.broadcasted_iota(jnp.int32, (CHUNK, CHUNK), 0)
    ci = lax.broadcasted_iota(jnp.int32, (CHUNK, CHUNK), 1)
    tri = (ri >= ci) if d == 0 else (ri <= ci)
    trib = tri.astype(BF16)
    tribT = ((ri <= ci) if d == 0 else (ri >= ci)).astype(BF16)
    cs = _dot(jnp.concatenate([trib] * 3, axis=1), jnp.concatenate(_split3(a_c), axis=0))
    csT = _dot(jnp.concatenate(_split3(aT_c), axis=1), jnp.concatenate([tribT] * 3, axis=0))
    last = cs[CHUNK - 1:CHUNK] if d == 0 else cs[0:1]
    w_c = dt_c * jnp.exp2(last - cs)
    ein_c = jnp.exp2(cs)
    ex = ex_ref[d]
    both_e = _dot(jnp.concatenate([jnp.concatenate(_split3(w_c)[:2], axis=1),
                                   jnp.concatenate(_split3(ein_c)[:2], axis=1)], axis=0), ex)
    w_e, ein_e = both_e[:CHUNK], both_e[CHUNK:]
    cd_e = ein_e[CHUNK - 1:CHUNK] if d == 0 else ein_e[0:1]
    Xdec = (xs * w_e).astype(BF16)
    rowT = csT - jnp.log2(dtT_c)
    lane = lax.broadcasted_iota(jnp.int32, (CHUNK, 2 * HEAD_DIM), 1)

    ys = []
    GW = D // SSD_GROUPS
    for g in range(SSD_GROUPS):
        Bg = bc[:, g * SSD_STATE:(g + 1) * SSD_STATE]
        Cg = bc[:, (SSD_GROUPS + g) * SSD_STATE:(SSD_GROUPS + g + 1) * SSD_STATE]
        S = _nt(Cg, Bg)
        st = st_ref[n, d, g]
        y_off = _dot(Cg, st.astype(BF16)) * ein_e[:, g * GW:(g + 1) * GW]
        y_diag = []
        for p in range(GW // (2 * HEAD_DIM)):
            hA = g * (H // SSD_GROUPS) + 2 * p
            Ms = []
            for h in (hA, hA + 1):
                col = cs[:, d * H + h:d * H + h + 1]
                rowv = rowT[d * H + h:d * H + h + 1, :]
                Ms.append((S * jnp.exp2(jnp.where(tri, col - rowv, -jnp.inf))).astype(BF16))
            Mcat = jnp.concatenate(Ms, axis=1)
            Xp = xsb[:, hA * HEAD_DIM:(hA + 2) * HEAD_DIM]
            y_diag.append(_dot(Mcat, _head_stack(Xp, lane)))
        ys.append(y_off + jnp.concatenate(y_diag, axis=1))
        st_ref[n, d, g] = st * cd_e[:, g * GW:(g + 1) * GW] + _tn(Bg, Xdec[:, g * GW:(g + 1) * GW])
    y = jnp.concatenate(ys, axis=1)

    if d == 0:
        yacc_ref[n, rows, :] = y
    else:
        ysum = yacc_ref[n, rows, :] + y + xs * (dsk_ref[0:1] + dsk_ref[1:2])
        yg = ysum * _silu(z_ref[n, blk, :].astype(F32))
        parts = []
        for g in range(SSD_GROUPS):
            v = yg[:, g * GW:(g + 1) * GW]
            ms = jnp.mean(v * v, axis=-1, keepdims=True)
            parts.append(v * lax.rsqrt(ms + EPS) * nw_ref[:, g * GW:(g + 1) * GW])
        y_ref[n, blk, :] = jnp.concatenate(parts, axis=1).astype(y_ref.dtype)


def _ssd_kernel(xs_ref, bc_ref, z_ref, dt_ref, dtT_ref, h0_ref, alr_ref, alc_ref, dbr_ref, dbc_ref, dsk_ref, nw_ref,
                ex_ref, y_ref, sto_ref, yacc_ref, st_ref, *, ns, nsub):
    t = pl.program_id(1)
    nb = xs_ref.shape[0]

    @pl.when(t == 0)
    def _():
        st_ref[...] = h0_ref[...]

    args = (xs_ref, bc_ref, z_ref, dt_ref, dtT_ref, alr_ref, alc_ref, dbr_ref, dbc_ref, dsk_ref, nw_ref,
            ex_ref, y_ref, yacc_ref, st_ref)

    def seq_rows(step, sub):
        return pl.ds(pl.multiple_of((step * nsub + sub) * CHUNK, CHUNK), CHUNK)

    @pl.when(t < ns)
    def _():
        for sub in range(nsub):
            for n in range(nb):
                _ssd_chunk(0, n, sub, seq_rows(t, sub), *args)

    @pl.when(t >= ns)
    def _():
        for sub in reversed(range(nsub)):
            for n in range(nb):
                _ssd_chunk(1, n, sub, seq_rows(2 * ns - 1 - t, sub), *args)

    @pl.when(t == 2 * ns - 1)
    def _():
        sto_ref[...] = st_ref[...]


def ssd_mix(u, dt, dtT, h0, a_log, dt_bias, d_skip_e, norm_w, nb=1):
    B, L, _ = u.shape
    nsub = min(SSD_SUB, L // CHUNK)
    RB = nsub * CHUNK
    ns = L // RB
    NB = math.gcd(B, nb)
    H2 = 2 * N_HEADS

    def chunk(t):
        return jnp.where(t < ns, t, 2 * ns - 1 - t)

    def late(t):
        return jnp.where(t < ns, ns - 1, 2 * ns - 1 - t)

    st_shape = (2, SSD_GROUPS, SSD_STATE, D // SSD_GROUPS)
    st_spec = pl.BlockSpec((NB,) + st_shape, lambda b, t: (b, 0, 0, 0, 0))
    small = lambda shape: pl.BlockSpec(shape, lambda b, t: (0,) * len(shape))
    krow = jnp.arange(2 * DT_PAD, dtype=jnp.int32)[:, None] % DT_PAD
    head = jnp.arange(D, dtype=jnp.int32)[None, :] // HEAD_DIM
    ex = jnp.stack([krow == d * N_HEADS + head for d in range(2)]).astype(BF16)
    lane_pad = lambda v: jnp.pad(v.reshape(1, H2), ((0, 0), (0, DT_PAD - H2)))
    return pl.pallas_call(
        functools.partial(_ssd_kernel, ns=ns, nsub=nsub),
        grid=(B // NB, 2 * ns),
        in_specs=[pl.BlockSpec((NB, RB, D), lambda b, t: (b, chunk(t), U_XS // D)),
                  pl.BlockSpec((NB, RB, XBC - D), lambda b, t: (b, chunk(t), U_BC // (XBC - D))),
                  pl.BlockSpec((NB, RB, D), lambda b, t: (b, late(t), U_Z // D)),
                  pl.BlockSpec((NB, RB, DT_PAD), lambda b, t: (b, chunk(t), 0)),
                  pl.BlockSpec((NB, H2, RB), lambda b, t: (b, 0, chunk(t))),
                  st_spec,
                  small((1, DT_PAD)), small((H2, 1)), small((1, DT_PAD)), small((H2, 1)),
                  small((2, D)), small((1, D)), small((2, 2 * DT_PAD, D))],
        out_specs=[pl.BlockSpec((NB, RB, D), lambda b, t: (b, late(t), 0)),
                   st_spec],
        out_shape=[jax.ShapeDtypeStruct((B, L, D), BF16),
                   jax.ShapeDtypeStruct((B,) + st_shape, F32)],
        scratch_shapes=[pltpu.VMEM((NB, L, D), F32), pltpu.VMEM((NB,) + st_shape, F32)],
        compiler_params=_cparams(("parallel", "arbitrary")),
        name="ssd_mix",
    )(u, u, u, dt, dtT, h0,
      lane_pad(a_log), a_log.reshape(H2, 1), lane_pad(dt_bias), dt_bias.reshape(H2, 1),
      d_skip_e, norm_w.reshape(1, D), ex)


NA_GROUP = 1


def _na_kernel(q_ref, k_ref, v_ref, kc_ref, vc_ref, bias_ref, o_ref, s_ref, p_ref, inv_ref, kt_ref, kto_ref, sc_ref, *,
               rows):
    W = GRID_W
    band = NA_KH * W
    lane = lax.broadcasted_iota(jnp.int32, (W, 2 * HEAD_DIM), 1)
    G = NA_GROUP
    n_groups = rows // G
    M2 = 2 * W
    kt_ref[...] = k_ref[0].T
    kto_ref[...] = k_ref[0, W:W + kto_ref.shape[1], :].T
    kct = kc_ref[0].T
    lane_all = lax.broadcasted_iota(jnp.int32, q_ref.shape[1:], 1)
    q_all = q_ref[0]
    zero_all = jnp.zeros_like(q_all)
    sc_ref[0] = _dot(jnp.where(lane_all < HEAD_DIM, q_all, zero_all), kct)
    sc_ref[1] = _dot(jnp.where(lane_all >= HEAD_DIM, q_all, zero_all), kct)

    def band_start(r):
        return min(max(r - NA_KH // 2, 0), rows - NA_KH)

    def stage_scores(g, slot):
        for i in range(G):
            r = g * G + i
            r0 = band_start(r)
            qs = _head_stack(q_ref[0, r * W:(r + 1) * W, :], lane)
            ktb, t0 = (kt_ref, r0 * W) if r0 % 2 == 0 else (kto_ref, (r0 - 1) * W)
            base = r0 - r + NA_KH - 1
            for jp in range(NA_KH // 2):
                cols = slice(jp * M2, (jp + 1) * M2)
                s_ref[slot, i * M2:(i + 1) * M2, cols] = (_dot(qs, ktb[:, t0 + jp * M2:t0 + (jp + 1) * M2])
                                                          + bias_ref[0, base + 2 * jp])

    def stage_softmax(g, slot):
        for i in range(G):
            r = g * G + i
            s_ctx = jnp.concatenate([sc_ref[0, r * W:(r + 1) * W, :], sc_ref[1, r * W:(r + 1) * W, :]], axis=0)
            s = jnp.concatenate([s_ref[slot, i * M2:(i + 1) * M2, :], s_ctx], axis=1)
            p = jnp.exp2(s - jnp.max(s, axis=-1, keepdims=True))
            inv = 1.0 / jnp.sum(p, axis=-1, keepdims=True)
            p_ref[slot, i * M2:(i + 1) * M2, :] = p.astype(BF16)
            inv_ref[slot, i * M2:(i + 1) * M2, :] = jnp.broadcast_to(inv, (M2, M2))

    def stage_values(g, slot):
        for i in range(G):
            r = g * G + i
            r0 = band_start(r)
            vb = v_ref[0, r0 * W:r0 * W + band, :]
            p = p_ref[slot, i * M2:(i + 1) * M2, :]
            o2 = (_dot(p[:, :band], vb) + _dot(p[:, band:], vc_ref[0])) * inv_ref[slot, i * M2:(i + 1) * M2, :]
            o = jnp.where(lane < HEAD_DIM, o2[:W], o2[W:])
            o_ref[0, r * W:(r + 1) * W, :] = o.astype(o_ref.dtype)

    stage_scores(0, 0)
    stage_scores(1, 1)
    stage_softmax(0, 0)

    for g in range(n_groups - 2):
        stage_scores(g + 2, g % 2)
        stage_softmax(g + 1, (g + 1) % 2)
        stage_values(g, g % 2)
    stage_softmax(n_groups - 1, (n_groups - 1) % 2)
    stage_values(n_groups - 2, (n_groups - 2) % 2)
    stage_values(n_groups - 1, (n_groups - 1) % 2)


def na_attend(u, u_ctx, bias):
    B, L, _ = u.shape
    Lc = u_ctx.shape[1]
    HP = N_HEADS // 2
    lw = 2 * HEAD_DIM
    oq, ok, ov = U_Q // lw, U_K // lw, U_V // lw
    return pl.pallas_call(
        functools.partial(_na_kernel, rows=L // GRID_W),
        grid=(HP, B),
        in_specs=[pl.BlockSpec((1, L, lw), lambda h, b: (b, 0, oq + h)),
                  pl.BlockSpec((1, L, lw), lambda h, b: (b, 0, ok + h)),
                  pl.BlockSpec((1, L, lw), lambda h, b: (b, 0, ov + h)),
                  pl.BlockSpec((1, Lc, lw), lambda h, b: (b, 0, ok + h)),
                  pl.BlockSpec((1, Lc, lw), lambda h, b: (b, 0, ov + h)),
                  pl.BlockSpec((1, 2 * NA_KH - 2, 2 * GRID_W, 2 * GRID_W), lambda h, b: (h, 0, 0, 0))],
        out_specs=pl.BlockSpec((1, L, lw), lambda h, b: (b, 0, h)),
        out_shape=jax.ShapeDtypeStruct((B, L, D), BF16),
        scratch_shapes=[pltpu.VMEM((2, NA_GROUP * lw, NA_KH * GRID_W), F32),
                        pltpu.VMEM((2, NA_GROUP * lw, NA_KH * GRID_W + Lc), BF16),
                        pltpu.VMEM((2, NA_GROUP * lw, lw), F32),
                        pltpu.VMEM((lw, L), BF16), pltpu.VMEM((lw, L - 2 * GRID_W), BF16),
                        pltpu.VMEM((2, L, Lc), F32)],
        compiler_params=_cparams(("parallel", "parallel")),
        name="na_attend",
    )(u, u, u, u_ctx, u_ctx, bias)


CTX_NB = 4


def _ctxattn_kernel(q_ref, k_ref, v_ref, o_ref):
    Lc = q_ref.shape[1]
    lane = lax.broadcasted_iota(jnp.int32, (Lc, 2 * HEAD_DIM), 1)
    for n in range(q_ref.shape[0]):
        q2 = (q_ref[n].astype(F32) * (HEAD_DIM ** -0.5)).astype(BF16)
        s = _nt(_head_stack(q2, lane), k_ref[n])
        m = jnp.max(s, axis=-1, keepdims=True)
        p = jnp.exp(s - m)
        p = (p * (1.0 / jnp.sum(p, axis=-1, keepdims=True))).astype(BF16)
        o2 = _dot(p, v_ref[n])
        o_ref[n] = jnp.where(lane < HEAD_DIM, o2[:Lc], o2[Lc:]).astype(o_ref.dtype)


def ctx_attend(u_ctx):
    B, Lc, _ = u_ctx.shape
    HP = N_HEADS // 2
    lw = 2 * HEAD_DIM
    oq, ok, ov = U_Q // lw, U_K // lw, U_V // lw
    nb = math.gcd(B, CTX_NB)
    return pl.pallas_call(
        _ctxattn_kernel,
        grid=(HP, B // nb),
        in_specs=[pl.BlockSpec((nb, Lc, lw), lambda h, b: (b, 0, oq + h)),
                  pl.BlockSpec((nb, Lc, lw), lambda h, b: (b, 0, ok + h)),
                  pl.BlockSpec((nb, Lc, lw), lambda h, b: (b, 0, ov + h))],
        out_specs=pl.BlockSpec((nb, Lc, lw), lambda h, b: (b, 0, h)),
        out_shape=jax.ShapeDtypeStruct((B, Lc, D), BF16),
        compiler_params=_cparams(("parallel", "parallel")),
        name="ctx_attend",
    )(u_ctx, u_ctx, u_ctx)


MERGE_CK = 256


def _merge_kernel(cb_ref, cc_ref, cx_ref, ccp_ref, cxp_ref, ccn_ref, cxn_ref, cw_ref, ys_ref, yn_ref, gc_ref, gs_ref,
                  gn_ref, h_ref, gt_ref, wc_ref, ws_ref, wn_ref, wo_ref, o_ref, *, seq_len):
    tm = h_ref.shape[1]
    row = lax.broadcasted_iota(jnp.int32, (tm, MERGE_CK), 0)
    pos = (pl.program_id(1) * tm + row) & (seq_len - 1)
    first, last = pos == 0, pos == seq_len - 1
    top, bottom = row == 0, row == tm - 1
    yc_proj = None
    for c in range(0, D, MERGE_CK):
        ch = slice(c, c + MERGE_CK)
        p = cc_ref[0, :, ch].astype(F32) * cx_ref[0, :, ch].astype(F32)
        p_before = (ccp_ref[0, :, ch].astype(F32) * cxp_ref[0, :, ch].astype(F32))[HALO - 1:HALO]
        p_after = (ccn_ref[0, :, ch].astype(F32) * cxn_ref[0, :, ch].astype(F32))[0:1]
        prev = jnp.where(first, 0.0, jnp.where(top, p_before, pltpu.roll(p, 1, axis=0)))
        nxt = jnp.where(last, 0.0, jnp.where(bottom, p_after, pltpu.roll(p, tm - 1, axis=0)))
        w = cw_ref[:, ch]
        yc = (cb_ref[0, :, ch].astype(F32) * (prev * w[0:1] + p * w[1:2] + nxt * w[2:3])).astype(BF16)
        t = _dot(yc, wc_ref[ch, :])
        yc_proj = t if yc_proj is None else yc_proj + t

    gate = lambda ref: _sigmoid(ref[0].astype(F32))
    m = (gate(gc_ref) * yc_proj
         + gate(gs_ref) * _dot(ys_ref[0], ws_ref[...])
         + gate(gn_ref) * _dot(yn_ref[0], wn_ref[...]))
    o_ref[0] = h_ref[0] + gt_ref[0] * _dot(m.astype(BF16), wo_ref[...])


def merge(u, ys, yn, h, gt, cw, wc, ws, wn, wo, tm, seq_len):
    B, L, _ = h.shape
    assert seq_len & (seq_len - 1) == 0 and (tm % seq_len == 0 or seq_len % tm == 0)
    nb, hb = tm // HALO, L // HALO
    tok = lambda: pl.BlockSpec((1, tm, D), lambda b, i: (b, i, 0))
    ucol = lambda c: pl.BlockSpec((1, tm, D), lambda b, i: (b, i, c // D))
    before = lambda c: pl.BlockSpec((1, HALO, D), lambda b, i: (b, jnp.maximum(i * nb - 1, 0), c // D))
    after = lambda c: pl.BlockSpec((1, HALO, D), lambda b, i: (b, jnp.minimum((i + 1) * nb, hb - 1), c // D))
    wsp = lambda: pl.BlockSpec((D, D), lambda b, i: (0, 0))
    return pl.pallas_call(
        functools.partial(_merge_kernel, seq_len=seq_len),
        grid=(B, L // tm),
        in_specs=[ucol(U_CB), ucol(U_CC), ucol(U_CX), before(U_CC), before(U_CX), after(U_CC), after(U_CX),
                  pl.BlockSpec((3, D), lambda b, i: (0, 0)),
                  tok(), tok(),
                  ucol(U_GATE), ucol(U_GATE + D), ucol(U_GATE + 2 * D),
                  tok(),
                  pl.BlockSpec((1, 1, D), lambda b, i: (b, 0, 0)),
                  wsp(), wsp(), wsp(), wsp()],
        out_specs=tok(),
        out_shape=jax.ShapeDtypeStruct((B, L, D), F32),
        compiler_params=_cparams(("parallel", "parallel")),
        name="merge",
    )(u, u, u, u, u, u, u, cw, ys, yn, u, u, u, h, gt, wc, ws, wn, wo)


MLP_TF = 1024


def _mlp_kernel(h_ref, sh_ref, sc_ref, gt_ref, nw_ref, fw_ref, w1_ref, w2_ref, o_ref, *, final):
    h = h_ref[0]
    xn = _norm_mod(h, nw_ref[...], sc_ref[0], sh_ref[0]).astype(BF16)
    acc = None
    for k in range(0, D_FF, MLP_TF):
        a = jnp.square(jnp.maximum(_dot(xn, w1_ref[:, k:k + MLP_TF]), 0.0)).astype(BF16)
        t = _dot(a, w2_ref[k:k + MLP_TF, :])
        acc = t if acc is None else acc + t
    o = h + gt_ref[0] * acc
    if final:
        ms = jnp.mean(o * o, axis=-1, keepdims=True)
        o = o * lax.rsqrt(ms + EPS) * fw_ref[...]
    o_ref[0] = o


def mlp(h, sh, sc, gt, nw, fw, w1, w2, tm, final):
    B, L, _ = h.shape
    vec = lambda: pl.BlockSpec((1, 1, D), lambda b, i: (b, 0, 0))
    par = lambda: pl.BlockSpec((1, D), lambda b, i: (0, 0))
    resident = lambda shape: pl.BlockSpec(shape, lambda b, i: (0, 0), pipeline_mode=pl.Buffered(1))
    return pl.pallas_call(
        functools.partial(_mlp_kernel, final=final),
        grid=(B, L // tm),
        in_specs=[pl.BlockSpec((1, tm, D), lambda b, i: (b, i, 0)),
                  vec(), vec(), vec(), par(), par(),
                  resident((D, D_FF)), resident((D_FF, D))],
        out_specs=pl.BlockSpec((1, tm, D), lambda b, i: (b, i, 0)),
        out_shape=jax.ShapeDtypeStruct((B, L, D), F32),
        compiler_params=_cparams(("parallel", "parallel")),
        name="mlp",
    )(h, sh, sc, gt, nw, fw, w1, w2)


def _rope_tables(L):
    t = jnp.arange(L, dtype=jnp.int32)
    row = (t // GRID_W).astype(F32)
    col = (t % GRID_W).astype(F32)
    half = HEAD_DIM // 2
    inv = ROPE_BASE ** (-jnp.arange(0, half, 2, dtype=F32) / half)
    ang_r = row[:, None] * inv
    ang_c = col[:, None] * inv
    ang = jnp.concatenate([ang_r, ang_r, ang_c, ang_c], axis=-1)
    cos = jnp.tile(jnp.cos(ang), (1, 2))
    sin = jnp.tile(jnp.sin(ang), (1, 2))
    even = ((jnp.arange(2 * HEAD_DIM) // (half // 2)) % 2 == 0)[None, :]
    return cos, jnp.where(even, -sin, 0.0), jnp.where(even, 0.0, sin)


def _rpb_kernel(r_ref, oh_ref, ok_ref, o_ref):
    val = _dot(jnp.concatenate(_split3(r_ref[...]), axis=1), oh_ref[...])
    o_ref[...] = jnp.where(ok_ref[...] > 0.0, LOG2E * val, -jnp.inf)


def _na_bias_table(rpb):
    H, NR, NC = rpb.shape
    W = GRID_W
    col = np.arange(W)
    col_start = np.clip(col - NA_KW // 2, 0, W - NA_KW)
    col_ok = (col[None, :] >= col_start[:, None]) & (col[None, :] < col_start[:, None] + NA_KW)
    dc_idx = np.clip(col[None, :] - col[:, None], -(NA_KW - 1), NA_KW - 1) + NA_KW - 1
    rows = jnp.transpose(rpb.reshape(H // 2, 2, NR, NC), (0, 2, 1, 3)).reshape(H * NR, NC)
    rows = jnp.pad(rows, ((0, 0), (0, DT_PAD - NC)))
    krow = jnp.arange(3 * DT_PAD, dtype=jnp.int32)[:, None] % DT_PAD
    onehot = (krow == jnp.asarray(dc_idx.reshape(1, W * W), jnp.int32)).astype(BF16)
    ok = jnp.asarray(col_ok.reshape(1, W * W), F32)
    tn = 1024
    tab = pl.pallas_call(
        _rpb_kernel,
        grid=(W * W // tn,),
        in_specs=[pl.BlockSpec((H * NR, DT_PAD), lambda j: (0, 0)),
                  pl.BlockSpec((3 * DT_PAD, tn), lambda j: (0, j)),
                  pl.BlockSpec((1, tn), lambda j: (0, j))],
        out_specs=pl.BlockSpec((H * NR, tn), lambda j: (0, j)),
        out_shape=jax.ShapeDtypeStruct((H * NR, W * W), F32),
        compiler_params=_cparams(("arbitrary",)),
        name="rpb_table",
    )(rows, onehot, ok)
    tab = tab.reshape(H // 2, NR, 2 * W, W)
    return jnp.concatenate([tab[:, :NR - 1], tab[:, 1:]], axis=-1)


def _prep_w_in(w_in):
    order = [(R_Q, D), (R_V, D), (R_Z, D), (R_K, D), (R_CB, D), (R_CC, D), (R_XBC, D), (R_CX, D), (R_GATE, 3 * D),
             (R_XBC + D, XBC - D)]
    w = jnp.concatenate([w_in[:, o:o + n] for o, n in order], axis=1).astype(BF16)
    wdt = w_in[:, R_DT:R_DT + 2 * N_HEADS].astype(BF16)
    return w, jnp.pad(wdt, ((0, 0), (0, DT_PAD - 2 * N_HEADS))), wdt.T


def kernel(x, c, ctx, c_ctx, w_ada, b_ada, norm1_w, w_in, conv_mix_w, ssd_conv_w, ssd_conv_b, ssd_a_log, ssd_dt_bias,
           ssd_d, ssd_norm_w, na_rpb, w_br_conv, w_br_ssd, w_br_na, w_out, norm2_w, w_ff1, w_ff2, final_norm_w):
    B, L, _ = x.shape
    Lc = ctx.shape[1]
    depth = w_in.shape[0]
    cos, sa, sb = _rope_tables(L)
    n_mod = B + 1
    pad = (-n_mod) % 8
    c_rows = jnp.concatenate([c, c_ctx[None, :], jnp.zeros((pad, D), F32)], axis=0)
    zero_state = jnp.zeros((B, 2, SSD_GROUPS, SSD_STATE, D // SSD_GROUPS), F32)
    fw = final_norm_w.reshape(1, D)
    Tc = B * Lc
    tmc = min(TM_PROJ, Tc)
    flat = lambda a: a.reshape(1, Tc, a.shape[-1])
    h, hc = x, flat(ctx)
    for l in range(depth):
        last = l == depth - 1
        mod = ada_mod(c_rows, w_ada[l], b_ada[l])
        m_lat = mod[:B].reshape(B, 1, 6, D)
        m_ctx = mod[B:B + 1].reshape(1, 1, 6, D)
        sh1, sc1, gt1, sh2, sc2, gt2 = (m_lat[:, :, i] for i in range(6))
        csh1, csc1, cgt1, csh2, csc2, cgt2 = (m_ctx[:, :, i] for i in range(6))
        w, wdt, wdtT = _prep_w_in(w_in[l])
        nw1 = norm1_w[l].reshape(1, D)
        conv_p = (ssd_conv_w[l], ssd_conv_b[l])
        u, dt, dtT = in_proj(h, sh1, sc1, nw1, w, wdt, wdtT, *conv_p, tm=TM_PROJ, seq_len=L, rope_tables=(cos, sa, sb))
        uc, dtc, dtTc = in_proj(hc, csh1, csc1, nw1, w, wdt, wdtT, *conv_p, tm=tmc, seq_len=Lc)
        uc = uc.reshape(B, Lc, U_COLS)
        dtc = dtc.reshape(B, Lc, DT_PAD)
        dtTc = jnp.transpose(dtTc.reshape(2 * N_HEADS, B, Lc), (1, 0, 2))
        d_skip_e = jnp.repeat(ssd_d[l], HEAD_DIM, axis=1)
        ssd_p = (ssd_a_log[l], ssd_dt_bias[l], d_skip_e, ssd_norm_w[l])
        y_ssd_c, ctx_states = ssd_mix(uc, dtc, dtTc, zero_state, *ssd_p, nb=SSD_NB_CTX)
        y_ssd, _ = ssd_mix(u, dt, dtT, ctx_states, *ssd_p)
        y_na = na_attend(u, uc, _na_bias_table(na_rpb[l]))
        wb = [t[l].astype(BF16) for t in (w_br_conv, w_br_ssd, w_br_na, w_out)]
        w1, w2 = w_ff1[l].astype(BF16), w_ff2[l].astype(BF16)
        nw2 = norm2_w[l].reshape(1, D)
        h = merge(u, y_ssd, y_na, h, gt1, conv_mix_w[l], *wb, tm=TM_MERGE, seq_len=L)
        h = mlp(h, sh2, sc2, gt2, nw2, fw, w1, w2, tm=TM_PROJ, final=last)
        if not last:
            y_na_c = ctx_attend(uc)
            hc = merge(flat(uc), flat(y_ssd_c), flat(y_na_c), hc, cgt1, conv_mix_w[l], *wb, tm=min(TM_MERGE, Tc), seq_len=Lc)
            hc = mlp(hc, csh2, csc2, cgt2, nw2, fw, w1, w2, tm=tmc, final=False)
    return h
```

```python
import functools
import math

import jax
import jax.numpy as jnp
import numpy as np
from jax import lax
from jax.experimental import pallas as pl
from jax.experimental.pallas import tpu as pltpu

F32 = jnp.float32
BF16 = jnp.bfloat16
HIGHEST = lax.Precision.HIGHEST
LOG2E = math.log2(math.e)

D = 1024
EPS = 1e-6
GRID_W = 64
N_HEADS = 16
HEAD_DIM = 64
SSD_GROUPS = 2
SSD_STATE = 128
CHUNK = 128
NA_KH = 8
NA_KW = 16
ROPE_BASE = 10000.0
D_FF = 4 * D
XBC = D + 2 * SSD_GROUPS * SSD_STATE
R_CB, R_CC, R_CX, R_Z, R_XBC, R_DT, R_Q, R_K, R_V, R_GATE = (
    0, 1024, 2048, 3072, 4096, 5632, 5664, 6688, 7712, 8736)

LANES = 128
HALO = 16
VMEM_LIMIT = 56 * 1024 * 1024

U_Q, U_V, U_Z, U_K, U_CB, U_CC, U_XS, U_CX, U_GATE, U_BC, U_COLS = (
    0, 1024, 2048, 3072, 4096, 5120, 6144, 7168, 8192, 11264, 11776)
U_TILE = 23 * LANES
U_SUB = 512
DT_PAD = LANES
TM_PROJ = 1024
TM_MERGE = 512
ADA_TN = 1536
SSD_SUB = 8
SSD_NB_CTX = 4


def _cparams(sem):
    return pltpu.CompilerParams(dimension_semantics=sem, vmem_limit_bytes=VMEM_LIMIT)


def _nt(a, b):
    return lax.dot_general(a, b, (((1,), (1,)), ((), ())), preferred_element_type=F32)


def _tn(a, b):
    return lax.dot_general(a, b, (((0,), (0,)), ((), ())), preferred_element_type=F32)


def _dot(a, b):
    return jnp.dot(a, b, preferred_element_type=F32)


def _dot_hi(a, b):
    return jnp.dot(a, b, preferred_element_type=F32, precision=HIGHEST)


def _sigmoid(x):
    return 0.5 * jnp.tanh(0.5 * x) + 0.5


def _silu(x):
    return x * _sigmoid(x)


def _norm_mod(x, nw, sc, sh):
    ms = jnp.mean(x * x, axis=-1, keepdims=True)
    y = x * lax.rsqrt(ms + EPS) * nw
    return y * (1.0 + sc) + sh


def _ada_kernel(c_ref, w_ref, b_ref, o_ref):
    o_ref[...] = _dot_hi(_silu(c_ref[...]), w_ref[...]) + b_ref[...]


def ada_mod(c_rows, w_ada, b_ada):
    rows = c_rows.shape[0]
    tn = ADA_TN
    return pl.pallas_call(
        _ada_kernel,
        grid=(6 * D // tn,),
        in_specs=[pl.BlockSpec((rows, D), lambda j: (0, 0)),
                  pl.BlockSpec((D, tn), lambda j: (0, j)),
                  pl.BlockSpec((1, tn), lambda j: (0, j))],
        out_specs=pl.BlockSpec((rows, tn), lambda j: (0, j)),
        out_shape=jax.ShapeDtypeStruct((rows, 6 * D), F32),
        compiler_params=_cparams(("arbitrary",)),
        name="ada_mod",
    )(c_rows, w_ada, b_ada.reshape(1, 6 * D))


def _rope_slab(x, cos, sa, sb):
    return x * cos + pltpu.roll(x, 2 * HEAD_DIM - 16, axis=1) * sa + pltpu.roll(x, 16, axis=1) * sb


def _inproj_kernel(x_ref, sh_ref, sc_ref, nw_ref, w_ref, wdt_ref, cw_ref, cb_ref, *rest, rope, seq_len):
    if rope:
        cos_ref, sa_ref, sb_ref, u_ref, dt_ref, xn_ref = rest
    else:
        u_ref, dt_ref, xn_ref = rest
    j, i = pl.program_id(1), pl.program_id(2)
    ni, tm = xn_ref.shape[0], xn_ref.shape[1]

    def normalise():
        xb = _norm_mod(x_ref[0], nw_ref[...], sc_ref[0], sh_ref[0]).astype(BF16)
        xn_ref[i] = xb
        dt_ref[0] = _dot(xb, wdt_ref[...])
        return xb

    def rope_piece(r, wd, scale):
        lw = 2 * HEAD_DIM
        return jnp.concatenate([_rope_slab(r[:, s:s + lw], cos_ref[...], sa_ref[...], sb_ref[...]) * scale
                                for s in range(0, wd, lw)], axis=1)

    def conv_piece(ws, wd, c0):
        before = xn_ref[jnp.maximum(i - 1, 0), tm - HALO:tm, :]
        after = xn_ref[jnp.minimum(i + 1, ni - 1), 0:HALO, :]
        res = _dot(jnp.concatenate([before, xn_ref[i], after], axis=0), ws)
        r = res[HALO:HALO + tm]
        r_before = res[HALO - 1:HALO]
        r_after = res[HALO + tm:HALO + tm + 1]
        row = lax.broadcasted_iota(jnp.int32, (tm, wd), 0)
        if seq_len >= tm:
            r_before = jnp.where((i * tm) & (seq_len - 1) == 0, 0.0, r_before)
            r_after = jnp.where(((i + 1) * tm) & (seq_len - 1) == 0, 0.0, r_after)
            prev = jnp.where(row == 0, r_before, pltpu.roll(r, 1, axis=0))
            nxt = jnp.where(row == tm - 1, r_after, pltpu.roll(r, tm - 1, axis=0))
        else:
            pos = (i * tm + row) & (seq_len - 1)
            prev = jnp.where(pos == 0, 0.0, jnp.where(row == 0, r_before, pltpu.roll(r, 1, axis=0)))
            nxt = jnp.where(pos == seq_len - 1, 0.0,
                            jnp.where(row == tm - 1, r_after, pltpu.roll(r, tm - 1, axis=0)))
        cw = cw_ref[:, c0:c0 + wd]
        h = prev * cw[0:1] + r * cw[1:2] + nxt * cw[2:3] + cb_ref[:, c0:c0 + wd]
        return h + h * jnp.tanh(h)

    def tile(jt, xb=None):
        for off, wd, kind, arg in _tile_pieces(jt):
            ws = w_ref[:, off:off + wd]
            if kind == "conv":
                r = conv_piece(ws, wd, arg)
            else:
                r = _dot(xn_ref[i] if xb is None else xb, ws)
                if kind == "rope" and rope:
                    r = rope_piece(r, wd, arg)
            u_ref[0, :, off:off + wd] = r.astype(u_ref.dtype)

    for jt in range(U_COLS // U_TILE):
        @pl.when(j == jt)
        def _(jt=jt):
            tile(jt, normalise() if jt == 0 else None)


def _tile_pieces(jt):
    special = [(U_Q, D, "rope", LOG2E * HEAD_DIM ** -0.5), (U_K, D, "rope", 1.0),
               (U_XS, D, "conv", 0), (U_BC, XBC - D, "conv", D)]
    lo, hi = jt * U_TILE, (jt + 1) * U_TILE
    cuts = sorted({lo, hi} | {c for s0, w0, _, _ in special for c in (s0, s0 + w0) if lo < c < hi})
    pieces = []
    for a, b in zip(cuts[:-1], cuts[1:]):
        kind, arg, seg0 = "plain", None, a
        for s0, w0, k0, a0 in special:
            if s0 <= a and b <= s0 + w0:
                kind, arg, seg0 = k0, a0, s0
        for off in range(a, b, U_SUB):
            wd = min(U_SUB, b - off)
            piece_arg = arg + (off - seg0) if kind == "conv" else arg
            pieces.append((off - lo, wd, kind, piece_arg))
    return pieces


def in_proj(x, sh, sc, nw, w, wdt, conv_w, conv_b, tm, seq_len, rope_tables=None):
    B, L, _ = x.shape
    assert seq_len & (seq_len - 1) == 0 and (tm % seq_len == 0 or seq_len % tm == 0)
    tn = U_TILE
    ni = L // tm
    grid = (B, U_COLS // tn, ni)
    rope = rope_tables is not None

    def first(j, i):
        return jnp.where(j == 0, i, ni - 1)

    last_rope_tile = (U_K + D - 1) // U_TILE

    def roped(j, i):
        return jnp.where(j <= last_rope_tile, i, ni - 1)

    rope_specs = [pl.BlockSpec((tm, 2 * HEAD_DIM), lambda b, j, i: (roped(j, i), 0))] * 3 if rope else []
    return pl.pallas_call(
        functools.partial(_inproj_kernel, rope=rope, seq_len=seq_len),
        grid=grid,
        in_specs=[pl.BlockSpec((1, tm, D), lambda b, j, i: (b, first(j, i), 0)),
                  pl.BlockSpec((1, 1, D), lambda b, j, i: (b, 0, 0)),
                  pl.BlockSpec((1, 1, D), lambda b, j, i: (b, 0, 0)),
                  pl.BlockSpec((1, D), lambda b, j, i: (0, 0)),
                  pl.BlockSpec((D, tn), lambda b, j, i: (0, j)),
                  pl.BlockSpec((D, DT_PAD), lambda b, j, i: (0, 0)),
                  pl.BlockSpec((3, XBC), lambda b, j, i: (0, 0)),
                  pl.BlockSpec((1, XBC), lambda b, j, i: (0, 0))] + rope_specs,
        out_specs=[pl.BlockSpec((1, tm, tn), lambda b, j, i: (b, i, j)),
                   pl.BlockSpec((1, tm, DT_PAD), lambda b, j, i: (b, first(j, i), 0))],
        out_shape=[jax.ShapeDtypeStruct((B, L, U_COLS), BF16),
                   jax.ShapeDtypeStruct((B, L, DT_PAD), F32)],
        scratch_shapes=[pltpu.VMEM((ni, tm, D), BF16)],
        compiler_params=_cparams(("parallel", "arbitrary", "arbitrary")),
        name="in_proj",
    )(x, sh, sc, nw, w, wdt, 0.5 * conv_w, 0.5 * conv_b.reshape(1, XBC), *(rope_tables if rope else ()))


def _split3(x):
    hi = x.astype(BF16)
    r1 = x - hi.astype(F32)
    mid = r1.astype(BF16)
    lo = (r1 - mid.astype(F32)).astype(BF16)
    return hi, mid, lo


def _head_stack(x2, lane):
    zero = jnp.zeros_like(x2)
    return jnp.concatenate([jnp.where(lane < HEAD_DIM, x2, zero), jnp.where(lane >= HEAD_DIM, x2, zero)], axis=0)


def _ssd_chunk(d, n, sub, rows, xs_ref, bc_ref, z_ref, dt_ref, alr_ref, alc_ref, dbr_ref, dbc_ref, dsk_ref,
               nw_ref, ex_ref, y_ref, yacc_ref, st_ref):
    H = N_HEADS
    blk = slice(sub * CHUNK, (sub + 1) * CHUNK)
    xsb = xs_ref[n, blk, :]
    xs = xsb.astype(F32)
    bc = bc_ref[n, blk, :]
    dt_c = jax.nn.softplus(dt_ref[n, blk, :] + dbr_ref[...])
    a_c = dt_c * (-LOG2E * jnp.exp(alr_ref[...]))
    dtT_c = jax.nn.softplus(dt_ref[n, blk, :].T[:2 * H] + dbc_ref[...])
    aT_c = dtT_c * (-LOG2E * jnp.exp(alc_ref[...]))

    ri = lax.broadcasted_iota(jnp.int32, (CHUNK, CHUNK), 0)
    ci = lax.broadcasted_iota(jnp.int32, (CHUNK, CHUNK), 1)
    tri = (ri >= ci) if d == 0 else (ri <= ci)
    trib = tri.astype(BF16)
    tribT = ((ri <= ci) if d == 0 else (ri >= ci)).astype(BF16)
    cs = _dot(jnp.concatenate([trib] * 3, axis=1), jnp.concatenate(_split3(a_c), axis=0))
    csT = _dot(jnp.concatenate(_split3(aT_c), axis=1), jnp.concatenate([tribT] * 3, axis=0))
    last = cs[CHUNK - 1:CHUNK] if d == 0 else cs[0:1]
    w_c = dt_c * jnp.exp2(last - cs)
    ein_c = jnp.exp2(cs)
    ex = ex_ref[d]
    w_e = _dot(jnp.concatenate(_split3(w_c)[:2], axis=1), ex)
    ein_e = _dot(jnp.concatenate(_split3(ein_c)[:2], axis=1), ex)
    cd_e = ein_e[CHUNK - 1:CHUNK] if d == 0 else ein_e[0:1]
    Xdec = (xs * w_e).astype(BF16)
    rowT = csT - jnp.log2(dtT_c)
    lane = lax.broadcasted_iota(jnp.int32, (CHUNK, 2 * HEAD_DIM), 1)

    ys = []
    GW = D // SSD_GROUPS
    for g in range(SSD_GROUPS):
        Bg = bc[:, g * SSD_STATE:(g + 1) * SSD_STATE]
        Cg = bc[:, (SSD_GROUPS + g) * SSD_STATE:(SSD_GROUPS + g + 1) * SSD_STATE]
        S = _nt(Cg, Bg)
        st = st_ref[n, d, g]
        y_off = _dot(Cg, st.astype(BF16)) * ein_e[:, g * GW:(g + 1) * GW]
        y_diag = []
        for p in range(GW // (2 * HEAD_DIM)):
            hA = g * (H // SSD_GROUPS) + 2 * p
            Ms = []
            for h in (hA, hA + 1):
                col = cs[:, d * H + h:d * H + h + 1]
                rowv = rowT[d * H + h:d * H + h + 1, :]
                Ms.append((S * jnp.exp2(jnp.where(tri, col - rowv, -jnp.inf))).astype(BF16))
            Mcat = jnp.concatenate(Ms, axis=1)
            Xp = xsb[:, hA * HEAD_DIM:(hA + 2) * HEAD_DIM]
            y_diag.append(_dot(Mcat, _head_stack(Xp, lane)))
        ys.append(y_off + jnp.concatenate(y_diag, axis=1))
        st_ref[n, d, g] = st * cd_e[:, g * GW:(g + 1) * GW] + _tn(Bg, Xdec[:, g * GW:(g + 1) * GW])
    y = jnp.concatenate(ys, axis=1)

    if d == 0:
        yacc_ref[n, rows, :] = y
    else:
        ysum = yacc_ref[n, rows, :] + y + xs * (dsk_ref[0:1] + dsk_ref[1:2])
        yg = ysum * _silu(z_ref[n, blk, :].astype(F32))
        parts = []
        for g in range(SSD_GROUPS):
            v = yg[:, g * GW:(g + 1) * GW]
            ms = jnp.mean(v * v, axis=-1, keepdims=True)
            parts.append(v * lax.rsqrt(ms + EPS) * nw_ref[:, g * GW:(g + 1) * GW])
        y_ref[n, blk, :] = jnp.concatenate(parts, axis=1).astype(y_ref.dtype)


def _ssd_kernel(xs_ref, bc_ref, z_ref, dt_ref, h0_ref, alr_ref, alc_ref, dbr_ref, dbc_ref, dsk_ref, nw_ref,
                ex_ref, y_ref, sto_ref, yacc_ref, st_ref, *, ns, nsub):
    t = pl.program_id(1)
    nb = xs_ref.shape[0]

    @pl.when(t == 0)
    def _():
        st_ref[...] = h0_ref[...]

    args = (xs_ref, bc_ref, z_ref, dt_ref, alr_ref, alc_ref, dbr_ref, dbc_ref, dsk_ref, nw_ref,
            ex_ref, y_ref, yacc_ref, st_ref)

    def seq_rows(step, sub):
        return pl.ds(pl.multiple_of((step * nsub + sub) * CHUNK, CHUNK), CHUNK)

    @pl.when(t < ns)
    def _():
        for sub in range(nsub):
            for n in range(nb):
                _ssd_chunk(0, n, sub, seq_rows(t, sub), *args)

    @pl.when(t >= ns)
    def _():
        for sub in reversed(range(nsub)):
            for n in range(nb):
                _ssd_chunk(1, n, sub, seq_rows(2 * ns - 1 - t, sub), *args)

    @pl.when(t == 2 * ns - 1)
    def _():
        sto_ref[...] = st_ref[...]


def ssd_mix(u, dt, h0, a_log, dt_bias, d_skip_e, norm_w, nb=1):
    B, L, _ = u.shape
    nsub = min(SSD_SUB, L // CHUNK)
    RB = nsub * CHUNK
    ns = L // RB
    NB = math.gcd(B, nb)
    H2 = 2 * N_HEADS

    def chunk(t):
        return jnp.where(t < ns, t, 2 * ns - 1 - t)

    def late(t):
        return jnp.where(t < ns, ns - 1, 2 * ns - 1 - t)

    st_shape = (2, SSD_GROUPS, SSD_STATE, D // SSD_GROUPS)
    st_spec = pl.BlockSpec((NB,) + st_shape, lambda b, t: (b, 0, 0, 0, 0))
    small = lambda shape: pl.BlockSpec(shape, lambda b, t: (0,) * len(shape))
    krow = jnp.arange(2 * DT_PAD, dtype=jnp.int32)[:, None] % DT_PAD
    head = jnp.arange(D, dtype=jnp.int32)[None, :] // HEAD_DIM
    ex = jnp.stack([krow == d * N_HEADS + head for d in range(2)]).astype(BF16)
    lane_pad = lambda v: jnp.pad(v.reshape(1, H2), ((0, 0), (0, DT_PAD - H2)))
    return pl.pallas_call(
        functools.partial(_ssd_kernel, ns=ns, nsub=nsub),
        grid=(B // NB, 2 * ns),
        in_specs=[pl.BlockSpec((NB, RB, D), lambda b, t: (b, chunk(t), U_XS // D)),
                  pl.BlockSpec((NB, RB, XBC - D), lambda b, t: (b, chunk(t), U_BC // (XBC - D))),
                  pl.BlockSpec((NB, RB, D), lambda b, t: (b, late(t), U_Z // D)),
                  pl.BlockSpec((NB, RB, DT_PAD), lambda b, t: (b, chunk(t), 0)),
                  st_spec,
                  small((1, DT_PAD)), small((H2, 1)), small((1, DT_PAD)), small((H2, 1)),
                  small((2, D)), small((1, D)), small((2, 2 * DT_PAD, D))],
        out_specs=[pl.BlockSpec((NB, RB, D), lambda b, t: (b, late(t), 0)),
                   st_spec],
        out_shape=[jax.ShapeDtypeStruct((B, L, D), BF16),
                   jax.ShapeDtypeStruct((B,) + st_shape, F32)],
        scratch_shapes=[pltpu.VMEM((NB, L, D), F32), pltpu.VMEM((NB,) + st_shape, F32)],
        compiler_params=_cparams(("parallel", "arbitrary")),
        name="ssd_mix",
    )(u, u, u, dt, h0,
      lane_pad(a_log), a_log.reshape(H2, 1), lane_pad(dt_bias), dt_bias.reshape(H2, 1),
      d_skip_e, norm_w.reshape(1, D), ex)


NA_GROUP = 1


def _na_kernel(q_ref, k_ref, v_ref, kc_ref, vc_ref, bias_ref, o_ref, s_ref, p_ref, inv_ref, kt_ref, kto_ref, *, rows):
    W = GRID_W
    band = NA_KH * W
    lane = lax.broadcasted_iota(jnp.int32, (W, 2 * HEAD_DIM), 1)
    G = NA_GROUP
    n_groups = rows // G
    M2 = 2 * W
    kt_ref[...] = k_ref[0].T
    kto_ref[...] = k_ref[0, W:W + kto_ref.shape[1], :].T
    kct = kc_ref[0].T

    def band_start(r):
        return min(max(r - NA_KH // 2, 0), rows - NA_KH)

    def stage_scores(g, slot):
        for i in range(G):
            r = g * G + i
            r0 = band_start(r)
            qs = _head_stack(q_ref[0, r * W:(r + 1) * W, :], lane)
            ktb, t0 = (kt_ref, r0 * W) if r0 % 2 == 0 else (kto_ref, (r0 - 1) * W)
            base = r0 - r + NA_KH - 1
            for jp in range(NA_KH // 2):
                cols = slice(jp * M2, (jp + 1) * M2)
                s_ref[slot, i * M2:(i + 1) * M2, cols] = (_dot(qs, ktb[:, t0 + jp * M2:t0 + (jp + 1) * M2])
                                                          + bias_ref[0, base + 2 * jp])
            s_ref[slot, i * M2:(i + 1) * M2, band:] = _dot(qs, kct)

    def stage_softmax(slot):
        for i in range(G):
            s = s_ref[slot, i * M2:(i + 1) * M2, :]
            p = jnp.exp2(s - jnp.max(s, axis=-1, keepdims=True))
            inv = 1.0 / jnp.sum(p, axis=-1, keepdims=True)
            p_ref[slot, i * M2:(i + 1) * M2, :] = p.astype(BF16)
            inv_ref[slot, i * M2:(i + 1) * M2, :] = jnp.broadcast_to(inv, (M2, M2))

    def stage_values(g, slot):
        for i in range(G):
            r = g * G + i
            r0 = band_start(r)
            vb = v_ref[0, r0 * W:r0 * W + band, :]
            p = p_ref[slot, i * M2:(i + 1) * M2, :]
            o2 = (_dot(p[:, :band], vb) + _dot(p[:, band:], vc_ref[0])) * inv_ref[slot, i * M2:(i + 1) * M2, :]
            o = jnp.where(lane < HEAD_DIM, o2[:W], o2[W:])
            o_ref[0, r * W:(r + 1) * W, :] = o.astype(o_ref.dtype)

    stage_scores(0, 0)
    stage_scores(1, 1)
    stage_softmax(0)

    for g in range(n_groups - 2):
        stage_scores(g + 2, g % 2)
        stage_softmax((g + 1) % 2)
        stage_values(g, g % 2)
    stage_softmax((n_groups - 1) % 2)
    stage_values(n_groups - 2, (n_groups - 2) % 2)
    stage_values(n_groups - 1, (n_groups - 1) % 2)


def na_attend(u, u_ctx, bias):
    B, L, _ = u.shape
    Lc = u_ctx.shape[1]
    HP = N_HEADS // 2
    lw = 2 * HEAD_DIM
    oq, ok, ov = U_Q // lw, U_K // lw, U_V // lw
    return pl.pallas_call(
        functools.partial(_na_kernel, rows=L // GRID_W),
        grid=(HP, B),
        in_specs=[pl.BlockSpec((1, L, lw), lambda h, b: (b, 0, oq + h)),
                  pl.BlockSpec((1, L, lw), lambda h, b: (b, 0, ok + h)),
                  pl.BlockSpec((1, L, lw), lambda h, b: (b, 0, ov + h)),
                  pl.BlockSpec((1, Lc, lw), lambda h, b: (b, 0, ok + h)),
                  pl.BlockSpec((1, Lc, lw), lambda h, b: (b, 0, ov + h)),
                  pl.BlockSpec((1, 2 * NA_KH - 2, 2 * GRID_W, 2 * GRID_W), lambda h, b: (h, 0, 0, 0))],
        out_specs=pl.BlockSpec((1, L, lw), lambda h, b: (b, 0, h)),
        out_shape=jax.ShapeDtypeStruct((B, L, D), BF16),
        scratch_shapes=[pltpu.VMEM((2, NA_GROUP * lw, NA_KH * GRID_W + Lc), F32),
                        pltpu.VMEM((2, NA_GROUP * lw, NA_KH * GRID_W + Lc), BF16),
                        pltpu.VMEM((2, NA_GROUP * lw, lw), F32),
                        pltpu.VMEM((lw, L), BF16), pltpu.VMEM((lw, L - 2 * GRID_W), BF16)],
        compiler_params=_cparams(("parallel", "parallel")),
        name="na_attend",
    )(u, u, u, u_ctx, u_ctx, bias)


CTX_NB = 4


def _ctxattn_kernel(q_ref, k_ref, v_ref, o_ref):
    Lc = q_ref.shape[1]
    lane = lax.broadcasted_iota(jnp.int32, (Lc, 2 * HEAD_DIM), 1)
    for n in range(q_ref.shape[0]):
        q2 = (q_ref[n].astype(F32) * (HEAD_DIM ** -0.5)).astype(BF16)
        s = _nt(_head_stack(q2, lane), k_ref[n])
        m = jnp.max(s, axis=-1, keepdims=True)
        p = jnp.exp(s - m)
        p = (p * (1.0 / jnp.sum(p, axis=-1, keepdims=True))).astype(BF16)
        o2 = _dot(p, v_ref[n])
        o_ref[n] = jnp.where(lane < HEAD_DIM, o2[:Lc], o2[Lc:]).astype(o_ref.dtype)


def ctx_attend(u_ctx):
    B, Lc, _ = u_ctx.shape
    HP = N_HEADS // 2
    lw = 2 * HEAD_DIM
    oq, ok, ov = U_Q // lw, U_K // lw, U_V // lw
    nb = math.gcd(B, CTX_NB)
    return pl.pallas_call(
        _ctxattn_kernel,
        grid=(HP, B // nb),
        in_specs=[pl.BlockSpec((nb, Lc, lw), lambda h, b: (b, 0, oq + h)),
                  pl.BlockSpec((nb, Lc, lw), lambda h, b: (b, 0, ok + h)),
                  pl.BlockSpec((nb, Lc, lw), lambda h, b: (b, 0, ov + h))],
        out_specs=pl.BlockSpec((nb, Lc, lw), lambda h, b: (b, 0, h)),
        out_shape=jax.ShapeDtypeStruct((B, Lc, D), BF16),
        compiler_params=_cparams(("parallel", "parallel")),
        name="ctx_attend",
    )(u_ctx, u_ctx, u_ctx)


MERGE_CK = 256


def _merge_kernel(cb_ref, cc_ref, cx_ref, ccp_ref, cxp_ref, ccn_ref, cxn_ref, cw_ref, ys_ref, yn_ref, gc_ref, gs_ref,
                  gn_ref, h_ref, gt_ref, wc_ref, ws_ref, wn_ref, wo_ref, o_ref, *, seq_len):
    tm = h_ref.shape[1]
    row = lax.broadcasted_iota(jnp.int32, (tm, MERGE_CK), 0)
    pos = (pl.program_id(1) * tm + row) & (seq_len - 1)
    first, last = pos == 0, pos == seq_len - 1
    top, bottom = row == 0, row == tm - 1
    yc_proj = None
    for c in range(0, D, MERGE_CK):
        ch = slice(c, c + MERGE_CK)
        p = cc_ref[0, :, ch].astype(F32) * cx_ref[0, :, ch].astype(F32)
        p_before = (ccp_ref[0, :, ch].astype(F32) * cxp_ref[0, :, ch].astype(F32))[HALO - 1:HALO]
        p_after = (ccn_ref[0, :, ch].astype(F32) * cxn_ref[0, :, ch].astype(F32))[0:1]
        prev = jnp.where(first, 0.0, jnp.where(top, p_before, pltpu.roll(p, 1, axis=0)))
        nxt = jnp.where(last, 0.0, jnp.where(bottom, p_after, pltpu.roll(p, tm - 1, axis=0)))
        w = cw_ref[:, ch]
        yc = (cb_ref[0, :, ch].astype(F32) * (prev * w[0:1] + p * w[1:2] + nxt * w[2:3])).astype(BF16)
        t = _dot(yc, wc_ref[ch, :])
        yc_proj = t if yc_proj is None else yc_proj + t

    gate = lambda ref: _sigmoid(ref[0].astype(F32))
    m = (gate(gc_ref) * yc_proj
         + gate(gs_ref) * _dot(ys_ref[0], ws_ref[...])
         + gate(gn_ref) * _dot(yn_ref[0], wn_ref[...]))
    o_ref[0] = h_ref[0] + gt_ref[0] * _dot(m.astype(BF16), wo_ref[...])


def merge(u, ys, yn, h, gt, cw, wc, ws, wn, wo, tm, seq_len):
    B, L, _ = h.shape
    assert seq_len & (seq_len - 1) == 0 and (tm % seq_len == 0 or seq_len % tm == 0)
    nb, hb = tm // HALO, L // HALO
    tok = lambda: pl.BlockSpec((1, tm, D), lambda b, i: (b, i, 0))
    ucol = lambda c: pl.BlockSpec((1, tm, D), lambda b, i: (b, i, c // D))
    before = lambda c: pl.BlockSpec((1, HALO, D), lambda b, i: (b, jnp.maximum(i * nb - 1, 0), c // D))
    after = lambda c: pl.BlockSpec((1, HALO, D), lambda b, i: (b, jnp.minimum((i + 1) * nb, hb - 1), c // D))
    wsp = lambda: pl.BlockSpec((D, D), lambda b, i: (0, 0))
    return pl.pallas_call(
        functools.partial(_merge_kernel, seq_len=seq_len),
        grid=(B, L // tm),
        in_specs=[ucol(U_CB), ucol(U_CC), ucol(U_CX), before(U_CC), before(U_CX), after(U_CC), after(U_CX),
                  pl.BlockSpec((3, D), lambda b, i: (0, 0)),
                  tok(), tok(),
                  ucol(U_GATE), ucol(U_GATE + D), ucol(U_GATE + 2 * D),
                  tok(),
                  pl.BlockSpec((1, 1, D), lambda b, i: (b, 0, 0)),
                  wsp(), wsp(), wsp(), wsp()],
        out_specs=tok(),
        out_shape=jax.ShapeDtypeStruct((B, L, D), F32),
        compiler_params=_cparams(("parallel", "parallel")),
        name="merge",
    )(u, u, u, u, u, u, u, cw, ys, yn, u, u, u, h, gt, wc, ws, wn, wo)


MLP_TF = 1024


def _mlp_kernel(h_ref, sh_ref, sc_ref, gt_ref, nw_ref, fw_ref, w1_ref, w2_ref, o_ref, *, final):
    h = h_ref[0]
    xn = _norm_mod(h, nw_ref[...], sc_ref[0], sh_ref[0]).astype(BF16)
    acc = None
    for k in range(0, D_FF, MLP_TF):
        a = jnp.square(jnp.maximum(_dot(xn, w1_ref[:, k:k + MLP_TF]), 0.0)).astype(BF16)
        t = _dot(a, w2_ref[k:k + MLP_TF, :])
        acc = t if acc is None else acc + t
    o = h + gt_ref[0] * acc
    if final:
        ms = jnp.mean(o * o, axis=-1, keepdims=True)
        o = o * lax.rsqrt(ms + EPS) * fw_ref[...]
    o_ref[0] = o


def mlp(h, sh, sc, gt, nw, fw, w1, w2, tm, final):
    B, L, _ = h.shape
    vec = lambda: pl.BlockSpec((1, 1, D), lambda b, i: (b, 0, 0))
    par = lambda: pl.BlockSpec((1, D), lambda b, i: (0, 0))
    resident = lambda shape: pl.BlockSpec(shape, lambda b, i: (0, 0), pipeline_mode=pl.Buffered(1))
    return pl.pallas_call(
        functools.partial(_mlp_kernel, final=final),
        grid=(B, L // tm),
        in_specs=[pl.BlockSpec((1, tm, D), lambda b, i: (b, i, 0)),
                  vec(), vec(), vec(), par(), par(),
                  resident((D, D_FF)), resident((D_FF, D))],
        out_specs=pl.BlockSpec((1, tm, D), lambda b, i: (b, i, 0)),
        out_shape=jax.ShapeDtypeStruct((B, L, D), F32),
        compiler_params=_cparams(("parallel", "parallel")),
        name="mlp",
    )(h, sh, sc, gt, nw, fw, w1, w2)


def _rope_tables(L):
    t = jnp.arange(L, dtype=jnp.int32)
    row = (t // GRID_W).astype(F32)
    col = (t % GRID_W).astype(F32)
    half = HEAD_DIM // 2
    inv = ROPE_BASE ** (-jnp.arange(0, half, 2, dtype=F32) / half)
    ang_r = row[:, None] * inv
    ang_c = col[:, None] * inv
    ang = jnp.concatenate([ang_r, ang_r, ang_c, ang_c], axis=-1)
    cos = jnp.tile(jnp.cos(ang), (1, 2))
    sin = jnp.tile(jnp.sin(ang), (1, 2))
    even = ((jnp.arange(2 * HEAD_DIM) // (half // 2)) % 2 == 0)[None, :]
    return cos, jnp.where(even, -sin, 0.0), jnp.where(even, 0.0, sin)


def _rpb_kernel(r_ref, oh_ref, ok_ref, o_ref):
    val = _dot(jnp.concatenate(_split3(r_ref[...]), axis=1), oh_ref[...])
    o_ref[...] = jnp.where(ok_ref[...] > 0.0, LOG2E * val, -jnp.inf)


def _na_bias_table(rpb):
    H, NR, NC = rpb.shape
    W = GRID_W
    col = np.arange(W)
    col_start = np.clip(col - NA_KW // 2, 0, W - NA_KW)
    col_ok = (col[None, :] >= col_start[:, None]) & (col[None, :] < col_start[:, None] + NA_KW)
    dc_idx = np.clip(col[None, :] - col[:, None], -(NA_KW - 1), NA_KW - 1) + NA_KW - 1
    rows = jnp.transpose(rpb.reshape(H // 2, 2, NR, NC), (0, 2, 1, 3)).reshape(H * NR, NC)
    rows = jnp.pad(rows, ((0, 0), (0, DT_PAD - NC)))
    krow = jnp.arange(3 * DT_PAD, dtype=jnp.int32)[:, None] % DT_PAD
    onehot = (krow == jnp.asarray(dc_idx.reshape(1, W * W), jnp.int32)).astype(BF16)
    ok = jnp.asarray(col_ok.reshape(1, W * W), F32)
    tn = 1024
    tab = pl.pallas_call(
        _rpb_kernel,
        grid=(W * W // tn,),
        in_specs=[pl.BlockSpec((H * NR, DT_PAD), lambda j: (0, 0)),
                  pl.BlockSpec((3 * DT_PAD, tn), lambda j: (0, j)),
                  pl.BlockSpec((1, tn), lambda j: (0, j))],
        out_specs=pl.BlockSpec((H * NR, tn), lambda j: (0, j)),
        out_shape=jax.ShapeDtypeStruct((H * NR, W * W), F32),
        compiler_params=_cparams(("arbitrary",)),
        name="rpb_table",
    )(rows, onehot, ok)
    tab = tab.reshape(H // 2, NR, 2 * W, W)
    return jnp.concatenate([tab[:, :NR - 1], tab[:, 1:]], axis=-1)


def _prep_w_in(w_in):
    order = [(R_Q, D), (R_V, D), (R_Z, D), (R_K, D), (R_CB, D), (R_CC, D), (R_XBC, D), (R_CX, D), (R_GATE, 3 * D),
             (R_XBC + D, XBC - D)]
    w = jnp.concatenate([w_in[:, o:o + n] for o, n in order], axis=1).astype(BF16)
    wdt = w_in[:, R_DT:R_DT + 2 * N_HEADS].astype(BF16)
    return w, jnp.pad(wdt, ((0, 0), (0, DT_PAD - 2 * N_HEADS)))


def kernel(x, c, ctx, c_ctx, w_ada, b_ada, norm1_w, w_in, conv_mix_w, ssd_conv_w, ssd_conv_b, ssd_a_log, ssd_dt_bias,
           ssd_d, ssd_norm_w, na_rpb, w_br_conv, w_br_ssd, w_br_na, w_out, norm2_w, w_ff1, w_ff2, final_norm_w):
    B, L, _ = x.shape
    Lc = ctx.shape[1]
    depth = w_in.shape[0]
    cos, sa, sb = _rope_tables(L)
    n_mod = B + 1
    pad = (-n_mod) % 8
    c_rows = jnp.concatenate([c, c_ctx[None, :], jnp.zeros((pad, D), F32)], axis=0)
    zero_state = jnp.zeros((B, 2, SSD_GROUPS, SSD_STATE, D // SSD_GROUPS), F32)
    fw = final_norm_w.reshape(1, D)
    Tc = B * Lc
    tmc = min(TM_PROJ, Tc)
    flat = lambda a: a.reshape(1, Tc, a.shape[-1])
    h, hc = x, flat(ctx)
    for l in range(depth):
        last = l == depth - 1
        mod = ada_mod(c_rows, w_ada[l], b_ada[l])
        m_lat = mod[:B].reshape(B, 1, 6, D)
        m_ctx = mod[B:B + 1].reshape(1, 1, 6, D)
        sh1, sc1, gt1, sh2, sc2, gt2 = (m_lat[:, :, i] for i in range(6))
        csh1, csc1, cgt1, csh2, csc2, cgt2 = (m_ctx[:, :, i] for i in range(6))
        w, wdt = _prep_w_in(w_in[l])
        nw1 = norm1_w[l].reshape(1, D)
        conv_p = (ssd_conv_w[l], ssd_conv_b[l])
        u, dt = in_proj(h, sh1, sc1, nw1, w, wdt, *conv_p, tm=TM_PROJ, seq_len=L, rope_tables=(cos, sa, sb))
        uc, dtc = in_proj(hc, csh1, csc1, nw1, w, wdt, *conv_p, tm=tmc, seq_len=Lc)
        uc = uc.reshape(B, Lc, U_COLS)
        dtc = dtc.reshape(B, Lc, DT_PAD)
        d_skip_e = jnp.repeat(ssd_d[l], HEAD_DIM, axis=1)
        ssd_p = (ssd_a_log[l], ssd_dt_bias[l], d_skip_e, ssd_norm_w[l])
        y_ssd_c, ctx_states = ssd_mix(uc, dtc, zero_state, *ssd_p, nb=SSD_NB_CTX)
        y_ssd, _ = ssd_mix(u, dt, ctx_states, *ssd_p)
        y_na = na_attend(u, uc, _na_bias_table(na_rpb[l]))
        wb = [t[l].astype(BF16) for t in (w_br_conv, w_br_ssd, w_br_na, w_out)]
        w1, w2 = w_ff1[l].astype(BF16), w_ff2[l].astype(BF16)
        nw2 = norm2_w[l].reshape(1, D)
        h = merge(u, y_ssd, y_na, h, gt1, conv_mix_w[l], *wb, tm=TM_MERGE, seq_len=L)
        h = mlp(h, sh2, sc2, gt2, nw2, fw, w1, w2, tm=TM_PROJ, final=last)
        if not last:
            y_na_c = ctx_attend(uc)
            hc = merge(flat(uc), flat(y_ssd_c), flat(y_na_c), hc, cgt1, conv_mix_w[l], *wb, tm=min(TM_MERGE, Tc), seq_len=Lc)
            hc = mlp(hc, csh2, csc2, cgt2, nw2, fw, w1, w2, tm=tmc, final=False)
    return h
```

```python
import functools
import math

import jax
import jax.numpy as jnp
import numpy as np
from jax import lax
from jax.experimental import pallas as pl
from jax.experimental.pallas import tpu as pltpu

F32 = jnp.float32
BF16 = jnp.bfloat16
HIGHEST = lax.Precision.HIGHEST
LOG2E = math.log2(math.e)

D = 1024
EPS = 1e-6
GRID_W = 64
N_HEADS = 16
HEAD_DIM = 64
SSD_GROUPS = 2
SSD_STATE = 128
CHUNK = 128
NA_KH = 8
NA_KW = 16
ROPE_BASE = 10000.0
D_FF = 4 * D
XBC = D + 2 * SSD_GROUPS * SSD_STATE
R_CB, R_CC, R_CX, R_Z, R_XBC, R_DT, R_Q, R_K, R_V, R_GATE = (
    0, 1024, 2048, 3072, 4096, 5632, 5664, 6688, 7712, 8736)

LANES = 128
HALO = 16
VMEM_LIMIT = 56 * 1024 * 1024

U_Q, U_V, U_Z, U_K, U_CB, U_CC, U_XS, U_CX, U_GATE, U_BC, U_COLS = (
    0, 1024, 2048, 3072, 4096, 5120, 6144, 7168, 8192, 11264, 11776)
U_TILE = 23 * LANES
U_SUB = 512
DT_PAD = LANES
TM_PROJ = 1024
TM_MERGE = 512
ADA_TN = 1536
SSD_SUB = 8
SSD_NB_CTX = 4


def _cparams(sem):
    return pltpu.CompilerParams(dimension_semantics=sem, vmem_limit_bytes=VMEM_LIMIT)


def _nt(a, b):
    return lax.dot_general(a, b, (((1,), (1,)), ((), ())), preferred_element_type=F32)


def _tn(a, b):
    return lax.dot_general(a, b, (((0,), (0,)), ((), ())), preferred_element_type=F32)


def _dot(a, b):
    return jnp.dot(a, b, preferred_element_type=F32)


def _dot_hi(a, b):
    return jnp.dot(a, b, preferred_element_type=F32, precision=HIGHEST)


def _sigmoid(x):
    return 0.5 * jnp.tanh(0.5 * x) + 0.5


def _silu(x):
    return x * _sigmoid(x)


def _norm_mod(x, nw, sc, sh):
    ms = jnp.mean(x * x, axis=-1, keepdims=True)
    y = x * lax.rsqrt(ms + EPS) * nw
    return y * (1.0 + sc) + sh


def _ada_kernel(c_ref, w_ref, b_ref, o_ref):
    o_ref[...] = _dot_hi(_silu(c_ref[...]), w_ref[...]) + b_ref[...]


def ada_mod(c_rows, w_ada, b_ada):
    rows = c_rows.shape[0]
    tn = ADA_TN
    return pl.pallas_call(
        _ada_kernel,
        grid=(6 * D // tn,),
        in_specs=[pl.BlockSpec((rows, D), lambda j: (0, 0)),
                  pl.BlockSpec((D, tn), lambda j: (0, j)),
                  pl.BlockSpec((1, tn), lambda j: (0, j))],
        out_specs=pl.BlockSpec((rows, tn), lambda j: (0, j)),
        out_shape=jax.ShapeDtypeStruct((rows, 6 * D), F32),
        compiler_params=_cparams(("arbitrary",)),
        name="ada_mod",
    )(c_rows, w_ada, b_ada.reshape(1, 6 * D))


def _rope_slab(x, cos, sa, sb):
    return x * cos + pltpu.roll(x, 2 * HEAD_DIM - 16, axis=1) * sa + pltpu.roll(x, 16, axis=1) * sb


def _inproj_kernel(x_ref, sh_ref, sc_ref, nw_ref, w_ref, wdt_ref, cw_ref, cb_ref, *rest, rope, seq_len):
    if rope:
        cos_ref, sa_ref, sb_ref, u_ref, dt_ref, xn_ref = rest
    else:
        u_ref, dt_ref, xn_ref = rest
    j, i = pl.program_id(1), pl.program_id(2)
    ni, tm = xn_ref.shape[0], xn_ref.shape[1]

    def normalise():
        xb = _norm_mod(x_ref[0], nw_ref[...], sc_ref[0], sh_ref[0]).astype(BF16)
        xn_ref[i] = xb
        dt_ref[0] = _dot(xb, wdt_ref[...])
        return xb

    def rope_piece(r, wd, scale):
        lw = 2 * HEAD_DIM
        return jnp.concatenate([_rope_slab(r[:, s:s + lw], cos_ref[...], sa_ref[...], sb_ref[...]) * scale
                                for s in range(0, wd, lw)], axis=1)

    def conv_piece(ws, wd, c0):
        before = xn_ref[jnp.maximum(i - 1, 0), tm - HALO:tm, :]
        after = xn_ref[jnp.minimum(i + 1, ni - 1), 0:HALO, :]
        res = _dot(jnp.concatenate([before, xn_ref[i], after], axis=0), ws)
        r = res[HALO:HALO + tm]
        r_before = res[HALO - 1:HALO]
        r_after = res[HALO + tm:HALO + tm + 1]
        row = lax.broadcasted_iota(jnp.int32, (tm, wd), 0)
        if seq_len >= tm:
            r_before = jnp.where((i * tm) & (seq_len - 1) == 0, 0.0, r_before)
            r_after = jnp.where(((i + 1) * tm) & (seq_len - 1) == 0, 0.0, r_after)
            prev = jnp.where(row == 0, r_before, pltpu.roll(r, 1, axis=0))
            nxt = jnp.where(row == tm - 1, r_after, pltpu.roll(r, tm - 1, axis=0))
        else:
            pos = (i * tm + row) & (seq_len - 1)
            prev = jnp.where(pos == 0, 0.0, jnp.where(row == 0, r_before, pltpu.roll(r, 1, axis=0)))
            nxt = jnp.where(pos == seq_len - 1, 0.0,
                            jnp.where(row == tm - 1, r_after, pltpu.roll(r, tm - 1, axis=0)))
        cw = cw_ref[:, c0:c0 + wd]
        h = prev * cw[0:1] + r * cw[1:2] + nxt * cw[2:3] + cb_ref[:, c0:c0 + wd]
        return h + h * jnp.tanh(h)

    def tile(jt, xb=None):
        for off, wd, kind, arg in _tile_pieces(jt):
            ws = w_ref[:, off:off + wd]
            if kind == "conv":
                r = conv_piece(ws, wd, arg)
            else:
                r = _dot(xn_ref[i] if xb is None else xb, ws)
                if kind == "rope" and rope:
                    r = rope_piece(r, wd, arg)
            u_ref[0, :, off:off + wd] = r.astype(u_ref.dtype)

    for jt in range(U_COLS // U_TILE):
        @pl.when(j == jt)
        def _(jt=jt):
            tile(jt, normalise() if jt == 0 else None)


def _tile_pieces(jt):
    special = [(U_Q, D, "rope", LOG2E * HEAD_DIM ** -0.5), (U_K, D, "rope", 1.0),
               (U_XS, D, "conv", 0), (U_BC, XBC - D, "conv", D)]
    lo, hi = jt * U_TILE, (jt + 1) * U_TILE
    cuts = sorted({lo, hi} | {c for s0, w0, _, _ in special for c in (s0, s0 + w0) if lo < c < hi})
    pieces = []
    for a, b in zip(cuts[:-1], cuts[1:]):
        kind, arg, seg0 = "plain", None, a
        for s0, w0, k0, a0 in special:
            if s0 <= a and b <= s0 + w0:
                kind, arg, seg0 = k0, a0, s0
        for off in range(a, b, U_SUB):
            wd = min(U_SUB, b - off)
            piece_arg = arg + (off - seg0) if kind == "conv" else arg
            pieces.append((off - lo, wd, kind, piece_arg))
    return pieces


def in_proj(x, sh, sc, nw, w, wdt, conv_w, conv_b, tm, seq_len, rope_tables=None):
    B, L, _ = x.shape
    assert seq_len & (seq_len - 1) == 0 and (tm % seq_len == 0 or seq_len % tm == 0)
    tn = U_TILE
    ni = L // tm
    grid = (B, U_COLS // tn, ni)
    rope = rope_tables is not None

    def first(j, i):
        return jnp.where(j == 0, i, ni - 1)

    last_rope_tile = (U_K + D - 1) // U_TILE

    def roped(j, i):
        return jnp.where(j <= last_rope_tile, i, ni - 1)

    rope_specs = [pl.BlockSpec((tm, 2 * HEAD_DIM), lambda b, j, i: (roped(j, i), 0))] * 3 if rope else []
    return pl.pallas_call(
        functools.partial(_inproj_kernel, rope=rope, seq_len=seq_len),
        grid=grid,
        in_specs=[pl.BlockSpec((1, tm, D), lambda b, j, i: (b, first(j, i), 0)),
                  pl.BlockSpec((1, 1, D), lambda b, j, i: (b, 0, 0)),
                  pl.BlockSpec((1, 1, D), lambda b, j, i: (b, 0, 0)),
                  pl.BlockSpec((1, D), lambda b, j, i: (0, 0)),
                  pl.BlockSpec((D, tn), lambda b, j, i: (0, j)),
                  pl.BlockSpec((D, DT_PAD), lambda b, j, i: (0, 0)),
                  pl.BlockSpec((3, XBC), lambda b, j, i: (0, 0)),
                  pl.BlockSpec((1, XBC), lambda b, j, i: (0, 0))] + rope_specs,
        out_specs=[pl.BlockSpec((1, tm, tn), lambda b, j, i: (b, i, j)),
                   pl.BlockSpec((1, tm, DT_PAD), lambda b, j, i: (b, first(j, i), 0))],
        out_shape=[jax.ShapeDtypeStruct((B, L, U_COLS), BF16),
                   jax.ShapeDtypeStruct((B, L, DT_PAD), F32)],
        scratch_shapes=[pltpu.VMEM((ni, tm, D), BF16)],
        compiler_params=_cparams(("parallel", "arbitrary", "arbitrary")),
        name="in_proj",
    )(x, sh, sc, nw, w, wdt, 0.5 * conv_w, 0.5 * conv_b.reshape(1, XBC), *(rope_tables if rope else ()))


def _split3(x):
    hi = x.astype(BF16)
    r1 = x - hi.astype(F32)
    mid = r1.astype(BF16)
    lo = (r1 - mid.astype(F32)).astype(BF16)
    return hi, mid, lo


def _head_stack(x2, lane):
    zero = jnp.zeros_like(x2)
    return jnp.concatenate([jnp.where(lane < HEAD_DIM, x2, zero), jnp.where(lane >= HEAD_DIM, x2, zero)], axis=0)


def _ssd_chunk(d, n, sub, rows, xs_ref, bc_ref, z_ref, dt_ref, alr_ref, alc_ref, dbr_ref, dbc_ref, dsk_ref,
               nw_ref, ex_ref, y_ref, yacc_ref, st_ref):
    H = N_HEADS
    blk = slice(sub * CHUNK, (sub + 1) * CHUNK)
    xsb = xs_ref[n, blk, :]
    xs = xsb.astype(F32)
    bc = bc_ref[n, blk, :]
    dt_c = jax.nn.softplus(dt_ref[n, blk, :] + dbr_ref[...])
    a_c = dt_c * (-LOG2E * jnp.exp(alr_ref[...]))
    dtT_c = jax.nn.softplus(dt_ref[n, blk, :].T[:2 * H] + dbc_ref[...])
    aT_c = dtT_c * (-LOG2E * jnp.exp(alc_ref[...]))

    ri = lax.broadcasted_iota(jnp.int32, (CHUNK, CHUNK), 0)
    ci = lax.broadcasted_iota(jnp.int32, (CHUNK, CHUNK), 1)
    tri = (ri >= ci) if d == 0 else (ri <= ci)
    trib = tri.astype(BF16)
    tribT = ((ri <= ci) if d == 0 else (ri >= ci)).astype(BF16)
    cs = _dot(jnp.concatenate([trib] * 3, axis=1), jnp.concatenate(_split3(a_c), axis=0))
    csT = _dot(jnp.concatenate(_split3(aT_c), axis=1), jnp.concatenate([tribT] * 3, axis=0))
    last = cs[CHUNK - 1:CHUNK] if d == 0 else cs[0:1]
    w_c = dt_c * jnp.exp2(last - cs)
    ein_c = jnp.exp2(cs)
    ex = ex_ref[d]
    w_e = _dot(jnp.concatenate(_split3(w_c)[:2], axis=1), ex)
    ein_e = _dot(jnp.concatenate(_split3(ein_c)[:2], axis=1), ex)
    cd_e = ein_e[CHUNK - 1:CHUNK] if d == 0 else ein_e[0:1]
    Xdec = (xs * w_e).astype(BF16)
    rowT = csT - jnp.log2(dtT_c)
    lane = lax.broadcasted_iota(jnp.int32, (CHUNK, 2 * HEAD_DIM), 1)

    ys = []
    GW = D // SSD_GROUPS
    for g in range(SSD_GROUPS):
        Bg = bc[:, g * SSD_STATE:(g + 1) * SSD_STATE]
        Cg = bc[:, (SSD_GROUPS + g) * SSD_STATE:(SSD_GROUPS + g + 1) * SSD_STATE]
        S = _nt(Cg, Bg)
        st = st_ref[n, d, g]
        y_off = _dot(Cg, st.astype(BF16)) * ein_e[:, g * GW:(g + 1) * GW]
        y_diag = []
        for p in range(GW // (2 * HEAD_DIM)):
            hA = g * (H // SSD_GROUPS) + 2 * p
            Ms = []
            for h in (hA, hA + 1):
                col = cs[:, d * H + h:d * H + h + 1]
                rowv = rowT[d * H + h:d * H + h + 1, :]
                Ms.append((S * jnp.exp2(jnp.where(tri, col - rowv, -jnp.inf))).astype(BF16))
            Mcat = jnp.concatenate(Ms, axis=1)
            Xp = xsb[:, hA * HEAD_DIM:(hA + 2) * HEAD_DIM]
            y_diag.append(_dot(Mcat, _head_stack(Xp, lane)))
        ys.append(y_off + jnp.concatenate(y_diag, axis=1))
        st_ref[n, d, g] = st * cd_e[:, g * GW:(g + 1) * GW] + _tn(Bg, Xdec[:, g * GW:(g + 1) * GW])
    y = jnp.concatenate(ys, axis=1)

    if d == 0:
        yacc_ref[n, rows, :] = y
    else:
        ysum = yacc_ref[n, rows, :] + y + xs * (dsk_ref[0:1] + dsk_ref[1:2])
        yg = ysum * _silu(z_ref[n, blk, :].astype(F32))
        parts = []
        for g in range(SSD_GROUPS):
            v = yg[:, g * GW:(g + 1) * GW]
            ms = jnp.mean(v * v, axis=-1, keepdims=True)
            parts.append(v * lax.rsqrt(ms + EPS) * nw_ref[:, g * GW:(g + 1) * GW])
        y_ref[n, blk, :] = jnp.concatenate(parts, axis=1).astype(y_ref.dtype)


def _ssd_kernel(xs_ref, bc_ref, z_ref, dt_ref, h0_ref, alr_ref, alc_ref, dbr_ref, dbc_ref, dsk_ref, nw_ref,
                ex_ref, y_ref, sto_ref, yacc_ref, st_ref, *, ns, nsub):
    t = pl.program_id(1)
    nb = xs_ref.shape[0]

    @pl.when(t == 0)
    def _():
        st_ref[...] = h0_ref[...]

    args = (xs_ref, bc_ref, z_ref, dt_ref, alr_ref, alc_ref, dbr_ref, dbc_ref, dsk_ref, nw_ref,
            ex_ref, y_ref, yacc_ref, st_ref)

    def seq_rows(step, sub):
        return pl.ds(pl.multiple_of((step * nsub + sub) * CHUNK, CHUNK), CHUNK)

    @pl.when(t < ns)
    def _():
        for sub in range(nsub):
            for n in range(nb):
                _ssd_chunk(0, n, sub, seq_rows(t, sub), *args)

    @pl.when(t >= ns)
    def _():
        for sub in reversed(range(nsub)):
            for n in range(nb):
                _ssd_chunk(1, n, sub, seq_rows(2 * ns - 1 - t, sub), *args)

    @pl.when(t == 2 * ns - 1)
    def _():
        sto_ref[...] = st_ref[...]


def ssd_mix(u, dt, h0, a_log, dt_bias, d_skip_e, norm_w, nb=1):
    B, L, _ = u.shape
    nsub = min(SSD_SUB, L // CHUNK)
    RB = nsub * CHUNK
    ns = L // RB
    NB = math.gcd(B, nb)
    H2 = 2 * N_HEADS

    def chunk(t):
        return jnp.where(t < ns, t, 2 * ns - 1 - t)

    def late(t):
        return jnp.where(t < ns, ns - 1, 2 * ns - 1 - t)

    st_shape = (2, SSD_GROUPS, SSD_STATE, D // SSD_GROUPS)
    st_spec = pl.BlockSpec((NB,) + st_shape, lambda b, t: (b, 0, 0, 0, 0))
    small = lambda shape: pl.BlockSpec(shape, lambda b, t: (0,) * len(shape))
    krow = jnp.arange(2 * DT_PAD, dtype=jnp.int32)[:, None] % DT_PAD
    head = jnp.arange(D, dtype=jnp.int32)[None, :] // HEAD_DIM
    ex = jnp.stack([krow == d * N_HEADS + head for d in range(2)]).astype(BF16)
    lane_pad = lambda v: jnp.pad(v.reshape(1, H2), ((0, 0), (0, DT_PAD - H2)))
    return pl.pallas_call(
        functools.partial(_ssd_kernel, ns=ns, nsub=nsub),
        grid=(B // NB, 2 * ns),
        in_specs=[pl.BlockSpec((NB, RB, D), lambda b, t: (b, chunk(t), U_XS // D)),
                  pl.BlockSpec((NB, RB, XBC - D), lambda b, t: (b, chunk(t), U_BC // (XBC - D))),
                  pl.BlockSpec((NB, RB, D), lambda b, t: (b, late(t), U_Z // D)),
                  pl.BlockSpec((NB, RB, DT_PAD), lambda b, t: (b, chunk(t), 0)),
                  st_spec,
                  small((1, DT_PAD)), small((H2, 1)), small((1, DT_PAD)), small((H2, 1)),
                  small((2, D)), small((1, D)), small((2, 2 * DT_PAD, D))],
        out_specs=[pl.BlockSpec((NB, RB, D), lambda b, t: (b, late(t), 0)),
                   st_spec],
        out_shape=[jax.ShapeDtypeStruct((B, L, D), BF16),
                   jax.ShapeDtypeStruct((B,) + st_shape, F32)],
        scratch_shapes=[pltpu.VMEM((NB, L, D), F32), pltpu.VMEM((NB,) + st_shape, F32)],
        compiler_params=_cparams(("parallel", "arbitrary")),
        name="ssd_mix",
    )(u, u, u, dt, h0,
      lane_pad(a_log), a_log.reshape(H2, 1), lane_pad(dt_bias), dt_bias.reshape(H2, 1),
      d_skip_e, norm_w.reshape(1, D), ex)


NA_GROUP = 1


def _na_kernel(q_ref, k_ref, v_ref, kc_ref, vc_ref, bias_ref, o_ref, s_ref, p_ref, inv_ref, kt_ref, kto_ref, *, rows):
    W = GRID_W
    band = NA_KH * W
    lane = lax.broadcasted_iota(jnp.int32, (W, 2 * HEAD_DIM), 1)
    G = NA_GROUP
    n_groups = rows // G
    M2 = 2 * W
    kt_ref[...] = k_ref[0].T
    kto_ref[...] = k_ref[0, W:W + kto_ref.shape[1], :].T
    kct = kc_ref[0].T

    def band_start(r):
        return min(max(r - NA_KH // 2, 0), rows - NA_KH)

    def stage_scores(g, slot):
        for i in range(G):
            r = g * G + i
            r0 = band_start(r)
            qs = _head_stack(q_ref[0, r * W:(r + 1) * W, :], lane)
            ktb, t0 = (kt_ref, r0 * W) if r0 % 2 == 0 else (kto_ref, (r0 - 1) * W)
            base = r0 - r + NA_KH - 1
            for jp in range(NA_KH // 2):
                cols = slice(jp * M2, (jp + 1) * M2)
                s_ref[slot, i * M2:(i + 1) * M2, cols] = (_dot(qs, ktb[:, t0 + jp * M2:t0 + (jp + 1) * M2])
                                                          + bias_ref[0, base + 2 * jp])
            s_ref[slot, i * M2:(i + 1) * M2, band:] = _dot(qs, kct)

    def stage_softmax(slot):
        for i in range(G):
            s = s_ref[slot, i * M2:(i + 1) * M2, :]
            p = jnp.exp2(s - jnp.max(s, axis=-1, keepdims=True))
            inv = 1.0 / jnp.sum(p, axis=-1, keepdims=True)
            p_ref[slot, i * M2:(i + 1) * M2, :] = p.astype(BF16)
            inv_ref[slot, i * M2:(i + 1) * M2, :] = jnp.broadcast_to(inv, (M2, M2))

    def stage_values(g, slot):
        for i in range(G):
            r = g * G + i
            r0 = band_start(r)
            vb = v_ref[0, r0 * W:r0 * W + band, :]
            p = p_ref[slot, i * M2:(i + 1) * M2, :]
            o2 = (_dot(p[:, :band], vb) + _dot(p[:, band:], vc_ref[0])) * inv_ref[slot, i * M2:(i + 1) * M2, :]
            o = jnp.where(lane < HEAD_DIM, o2[:W], o2[W:])
            o_ref[0, r * W:(r + 1) * W, :] = o.astype(o_ref.dtype)

    stage_scores(0, 0)
    stage_scores(1, 1)
    stage_softmax(0)

    for g in range(n_groups - 2):
        stage_scores(g + 2, g % 2)
        stage_softmax((g + 1) % 2)
        stage_values(g, g % 2)
    stage_softmax((n_groups - 1) % 2)
    stage_values(n_groups - 2, (n_groups - 2) % 2)
    stage_values(n_groups - 1, (n_groups - 1) % 2)


def na_attend(u, u_ctx, bias):
    B, L, _ = u.shape
    Lc = u_ctx.shape[1]
    HP = N_HEADS // 2
    lw = 2 * HEAD_DIM
    oq, ok, ov = U_Q // lw, U_K // lw, U_V // lw
    return pl.pallas_call(
        functools.partial(_na_kernel, rows=L // GRID_W),
        grid=(HP, B),
        in_specs=[pl.BlockSpec((1, L, lw), lambda h, b: (b, 0, oq + h)),
                  pl.BlockSpec((1, L, lw), lambda h, b: (b, 0, ok + h)),
                  pl.BlockSpec((1, L, lw), lambda h, b: (b, 0, ov + h)),
                  pl.BlockSpec((1, Lc, lw), lambda h, b: (b, 0, ok + h)),
                  pl.BlockSpec((1, Lc, lw), lambda h, b: (b, 0, ov + h)),
                  pl.BlockSpec((1, 2 * NA_KH - 2, 2 * GRID_W, 2 * GRID_W), lambda h, b: (h, 0, 0, 0))],
        out_specs=pl.BlockSpec((1, L, lw), lambda h, b: (b, 0, h)),
        out_shape=jax.ShapeDtypeStruct((B, L, D), BF16),
        scratch_shapes=[pltpu.VMEM((2, NA_GROUP * lw, NA_KH * GRID_W + Lc), F32),
                        pltpu.VMEM((2, NA_GROUP * lw, NA_KH * GRID_W + Lc), BF16),
                        pltpu.VMEM((2, NA_GROUP * lw, lw), F32),
                        pltpu.VMEM((lw, L), BF16), pltpu.VMEM((lw, L - 2 * GRID_W), BF16)],
        compiler_params=_cparams(("parallel", "parallel")),
        name="na_attend",
    )(u, u, u, u_ctx, u_ctx, bias)


CTX_NB = 4


def _ctxattn_kernel(q_ref, k_ref, v_ref, o_ref):
    Lc = q_ref.shape[1]
    lane = lax.broadcasted_iota(jnp.int32, (Lc, 2 * HEAD_DIM), 1)
    for n in range(q_ref.shape[0]):
        q2 = (q_ref[n].astype(F32) * (HEAD_DIM ** -0.5)).astype(BF16)
        s = _nt(_head_stack(q2, lane), k_ref[n])
        m = jnp.max(s, axis=-1, keepdims=True)
        p = jnp.exp(s - m)
        p = (p * (1.0 / jnp.sum(p, axis=-1, keepdims=True))).astype(BF16)
        o2 = _dot(p, v_ref[n])
        o_ref[n] = jnp.where(lane < HEAD_DIM, o2[:Lc], o2[Lc:]).astype(o_ref.dtype)


def ctx_attend(u_ctx):
    B, Lc, _ = u_ctx.shape
    HP = N_HEADS // 2
    lw = 2 * HEAD_DIM
    oq, ok, ov = U_Q // lw, U_K // lw, U_V // lw
    nb = math.gcd(B, CTX_NB)
    return pl.pallas_call(
        _ctxattn_kernel,
        grid=(HP, B // nb),
        in_specs=[pl.BlockSpec((nb, Lc, lw), lambda h, b: (b, 0, oq + h)),
                  pl.BlockSpec((nb, Lc, lw), lambda h, b: (b, 0, ok + h)),
                  pl.BlockSpec((nb, Lc, lw), lambda h, b: (b, 0, ov + h))],
        out_specs=pl.BlockSpec((nb, Lc, lw), lambda h, b: (b, 0, h)),
        out_shape=jax.ShapeDtypeStruct((B, Lc, D), BF16),
        compiler_params=_cparams(("parallel", "parallel")),
        name="ctx_attend",
    )(u_ctx, u_ctx, u_ctx)


MERGE_CK = 256


def _merge_kernel(cb_ref, cc_ref, cx_ref, ccp_ref, cxp_ref, ccn_ref, cxn_ref, cw_ref, ys_ref, yn_ref, gc_ref, gs_ref,
                  gn_ref, h_ref, gt_ref, wc_ref, ws_ref, wn_ref, wo_ref, sh2_ref, sc2_ref, gt2_ref, nw2_ref, fw_ref,
                  w1_ref, w2_ref, o_ref, *, seq_len, final):
    tm = h_ref.shape[1]
    row = lax.broadcasted_iota(jnp.int32, (tm, MERGE_CK), 0)
    pos = (pl.program_id(1) * tm + row) & (seq_len - 1)
    first, last = pos == 0, pos == seq_len - 1
    top, bottom = row == 0, row == tm - 1
    yc_proj = None
    for c in range(0, D, MERGE_CK):
        ch = slice(c, c + MERGE_CK)
        p = cc_ref[0, :, ch].astype(F32) * cx_ref[0, :, ch].astype(F32)
        p_before = (ccp_ref[0, :, ch].astype(F32) * cxp_ref[0, :, ch].astype(F32))[HALO - 1:HALO]
        p_after = (ccn_ref[0, :, ch].astype(F32) * cxn_ref[0, :, ch].astype(F32))[0:1]
        prev = jnp.where(first, 0.0, jnp.where(top, p_before, pltpu.roll(p, 1, axis=0)))
        nxt = jnp.where(last, 0.0, jnp.where(bottom, p_after, pltpu.roll(p, tm - 1, axis=0)))
        w = cw_ref[:, ch]
        yc = (cb_ref[0, :, ch].astype(F32) * (prev * w[0:1] + p * w[1:2] + nxt * w[2:3])).astype(BF16)
        t = _dot(yc, wc_ref[ch, :])
        yc_proj = t if yc_proj is None else yc_proj + t

    gate = lambda ref: _sigmoid(ref[0].astype(F32))
    m = (gate(gc_ref) * yc_proj
         + gate(gs_ref) * _dot(ys_ref[0], ws_ref[...])
         + gate(gn_ref) * _dot(yn_ref[0], wn_ref[...]))
    h_mid = h_ref[0] + gt_ref[0] * _dot(m.astype(BF16), wo_ref[...])
    o_ref[0] = _mlp_body(h_mid, sh2_ref, sc2_ref, gt2_ref, nw2_ref, fw_ref, w1_ref, w2_ref, final)


def merge_mlp(u, ys, yn, h, gt, cw, wc, ws, wn, wo, sh2, sc2, gt2, nw2, fw, w1, w2, tm, seq_len, final):
    B, L, _ = h.shape
    assert seq_len & (seq_len - 1) == 0 and (tm % seq_len == 0 or seq_len % tm == 0)
    nb, hb = tm // HALO, L // HALO
    tok = lambda: pl.BlockSpec((1, tm, D), lambda b, i: (b, i, 0))
    ucol = lambda c: pl.BlockSpec((1, tm, D), lambda b, i: (b, i, c // D))
    before = lambda c: pl.BlockSpec((1, HALO, D), lambda b, i: (b, jnp.maximum(i * nb - 1, 0), c // D))
    after = lambda c: pl.BlockSpec((1, HALO, D), lambda b, i: (b, jnp.minimum((i + 1) * nb, hb - 1), c // D))
    resident = lambda shape: pl.BlockSpec(shape, lambda b, i: (0, 0), pipeline_mode=pl.Buffered(1))
    wsp = lambda: resident((D, D))
    vec = lambda: pl.BlockSpec((1, 1, D), lambda b, i: (b, 0, 0))
    par = lambda: pl.BlockSpec((1, D), lambda b, i: (0, 0))
    return pl.pallas_call(
        functools.partial(_merge_kernel, seq_len=seq_len, final=final),
        grid=(B, L // tm),
        in_specs=[ucol(U_CB), ucol(U_CC), ucol(U_CX), before(U_CC), before(U_CX), after(U_CC), after(U_CX),
                  pl.BlockSpec((3, D), lambda b, i: (0, 0)),
                  tok(), tok(),
                  ucol(U_GATE), ucol(U_GATE + D), ucol(U_GATE + 2 * D),
                  tok(),
                  pl.BlockSpec((1, 1, D), lambda b, i: (b, 0, 0)),
                  wsp(), wsp(), wsp(), wsp(),
                  vec(), vec(), vec(), par(), par(), resident((D, D_FF)), resident((D_FF, D))],
        out_specs=tok(),
        out_shape=jax.ShapeDtypeStruct((B, L, D), F32),
        compiler_params=_cparams(("parallel", "parallel")),
        name="merge_mlp",
    )(u, u, u, u, u, u, u, cw, ys, yn, u, u, u, h, gt, wc, ws, wn, wo, sh2, sc2, gt2, nw2, fw, w1, w2)


MLP_TF = 1024


def _mlp_body(h, sh_ref, sc_ref, gt_ref, nw_ref, fw_ref, w1_ref, w2_ref, final):
    xn = _norm_mod(h, nw_ref[...], sc_ref[0], sh_ref[0]).astype(BF16)
    acc = None
    for k in range(0, D_FF, MLP_TF):
        a = jnp.square(jnp.maximum(_dot(xn, w1_ref[:, k:k + MLP_TF]), 0.0)).astype(BF16)
        t = _dot(a, w2_ref[k:k + MLP_TF, :])
        acc = t if acc is None else acc + t
    o = h + gt_ref[0] * acc
    if final:
        ms = jnp.mean(o * o, axis=-1, keepdims=True)
        o = o * lax.rsqrt(ms + EPS) * fw_ref[...]
    return o


def _rope_tables(L):
    t = jnp.arange(L, dtype=jnp.int32)
    row = (t // GRID_W).astype(F32)
    col = (t % GRID_W).astype(F32)
    half = HEAD_DIM // 2
    inv = ROPE_BASE ** (-jnp.arange(0, half, 2, dtype=F32) / half)
    ang_r = row[:, None] * inv
    ang_c = col[:, None] * inv
    ang = jnp.concatenate([ang_r, ang_r, ang_c, ang_c], axis=-1)
    cos = jnp.tile(jnp.cos(ang), (1, 2))
    sin = jnp.tile(jnp.sin(ang), (1, 2))
    even = ((jnp.arange(2 * HEAD_DIM) // (half // 2)) % 2 == 0)[None, :]
    return cos, jnp.where(even, -sin, 0.0), jnp.where(even, 0.0, sin)


def _rpb_kernel(r_ref, oh_ref, ok_ref, o_ref):
    val = _dot(jnp.concatenate(_split3(r_ref[...]), axis=1), oh_ref[...])
    o_ref[...] = jnp.where(ok_ref[...] > 0.0, LOG2E * val, -jnp.inf)


def _na_bias_table(rpb):
    H, NR, NC = rpb.shape
    W = GRID_W
    col = np.arange(W)
    col_start = np.clip(col - NA_KW // 2, 0, W - NA_KW)
    col_ok = (col[None, :] >= col_start[:, None]) & (col[None, :] < col_start[:, None] + NA_KW)
    dc_idx = np.clip(col[None, :] - col[:, None], -(NA_KW - 1), NA_KW - 1) + NA_KW - 1
    rows = jnp.transpose(rpb.reshape(H // 2, 2, NR, NC), (0, 2, 1, 3)).reshape(H * NR, NC)
    rows = jnp.pad(rows, ((0, 0), (0, DT_PAD - NC)))
    krow = jnp.arange(3 * DT_PAD, dtype=jnp.int32)[:, None] % DT_PAD
    onehot = (krow == jnp.asarray(dc_idx.reshape(1, W * W), jnp.int32)).astype(BF16)
    ok = jnp.asarray(col_ok.reshape(1, W * W), F32)
    tn = 1024
    tab = pl.pallas_call(
        _rpb_kernel,
        grid=(W * W // tn,),
        in_specs=[pl.BlockSpec((H * NR, DT_PAD), lambda j: (0, 0)),
                  pl.BlockSpec((3 * DT_PAD, tn), lambda j: (0, j)),
                  pl.BlockSpec((1, tn), lambda j: (0, j))],
        out_specs=pl.BlockSpec((H * NR, tn), lambda j: (0, j)),
        out_shape=jax.ShapeDtypeStruct((H * NR, W * W), F32),
        compiler_params=_cparams(("arbitrary",)),
        name="rpb_table",
    )(rows, onehot, ok)
    tab = tab.reshape(H // 2, NR, 2 * W, W)
    return jnp.concatenate([tab[:, :NR - 1], tab[:, 1:]], axis=-1)


def _prep_w_in(w_in):
    order = [(R_Q, D), (R_V, D), (R_Z, D), (R_K, D), (R_CB, D), (R_CC, D), (R_XBC, D), (R_CX, D), (R_GATE, 3 * D),
             (R_XBC + D, XBC - D)]
    w = jnp.concatenate([w_in[:, o:o + n] for o, n in order], axis=1).astype(BF16)
    wdt = w_in[:, R_DT:R_DT + 2 * N_HEADS].astype(BF16)
    return w, jnp.pad(wdt, ((0, 0), (0, DT_PAD - 2 * N_HEADS)))


def kernel(x, c, ctx, c_ctx, w_ada, b_ada, norm1_w, w_in, conv_mix_w, ssd_conv_w, ssd_conv_b, ssd_a_log, ssd_dt_bias,
           ssd_d, ssd_norm_w, na_rpb, w_br_conv, w_br_ssd, w_br_na, w_out, norm2_w, w_ff1, w_ff2, final_norm_w):
    B, L, _ = x.shape
    Lc = ctx.shape[1]
    depth = w_in.shape[0]
    cos, sa, sb = _rope_tables(L)
    n_mod = B + 1
    pad = (-n_mod) % 8
    c_rows = jnp.concatenate([c, c_ctx[None, :], jnp.zeros((pad, D), F32)], axis=0)
    zero_state = jnp.zeros((B, 2, SSD_GROUPS, SSD_STATE, D // SSD_GROUPS), F32)
    fw = final_norm_w.reshape(1, D)
    Tc = B * Lc
    tmc = min(TM_PROJ, Tc)
    flat = lambda a: a.reshape(1, Tc, a.shape[-1])
    h, hc = x, flat(ctx)
    for l in range(depth):
        last = l == depth - 1
        mod = ada_mod(c_rows, w_ada[l], b_ada[l])
        m_lat = mod[:B].reshape(B, 1, 6, D)
        m_ctx = mod[B:B + 1].reshape(1, 1, 6, D)
        sh1, sc1, gt1, sh2, sc2, gt2 = (m_lat[:, :, i] for i in range(6))
        csh1, csc1, cgt1, csh2, csc2, cgt2 = (m_ctx[:, :, i] for i in range(6))
        w, wdt = _prep_w_in(w_in[l])
        nw1 = norm1_w[l].reshape(1, D)
        conv_p = (ssd_conv_w[l], ssd_conv_b[l])
        u, dt = in_proj(h, sh1, sc1, nw1, w, wdt, *conv_p, tm=TM_PROJ, seq_len=L, rope_tables=(cos, sa, sb))
        uc, dtc = in_proj(hc, csh1, csc1, nw1, w, wdt, *conv_p, tm=tmc, seq_len=Lc)
        uc = uc.reshape(B, Lc, U_COLS)
        dtc = dtc.reshape(B, Lc, DT_PAD)
        d_skip_e = jnp.repeat(ssd_d[l], HEAD_DIM, axis=1)
        ssd_p = (ssd_a_log[l], ssd_dt_bias[l], d_skip_e, ssd_norm_w[l])
        y_ssd_c, ctx_states = ssd_mix(uc, dtc, zero_state, *ssd_p, nb=SSD_NB_CTX)
        y_ssd, _ = ssd_mix(u, dt, ctx_states, *ssd_p)
        y_na = na_attend(u, uc, _na_bias_table(na_rpb[l]))
        wb = [t[l].astype(BF16) for t in (w_br_conv, w_br_ssd, w_br_na, w_out)]
        w1, w2 = w_ff1[l].astype(BF16), w_ff2[l].astype(BF16)
        nw2 = norm2_w[l].reshape(1, D)
        h = merge_mlp(u, y_ssd, y_na, h, gt1, conv_mix_w[l], *wb, sh2, sc2, gt2, nw2, fw, w1, w2,
                      tm=TM_MERGE, seq_len=L, final=last)
        if not last:
            y_na_c = ctx_attend(uc)
            hc = merge_mlp(flat(uc), flat(y_ssd_c), flat(y_na_c), hc, cgt1, conv_mix_w[l], *wb, csh2, csc2, cgt2, nw2, fw,
                           w1, w2, tm=min(TM_MERGE, Tc), seq_len=Lc, final=False)
    return h
```

```python
import functools
import math

import jax
import jax.numpy as jnp
import numpy as np
from jax import lax
from jax.experimental import pallas as pl
from jax.experimental.pallas import tpu as pltpu

F32 = jnp.float32
BF16 = jnp.bfloat16
HIGHEST = lax.Precision.HIGHEST
LOG2E = math.log2(math.e)

D = 1024
EPS = 1e-6
GRID_W = 64
N_HEADS = 16
HEAD_DIM = 64
SSD_GROUPS = 2
SSD_STATE = 128
CHUNK = 128
NA_KH = 8
NA_KW = 16
ROPE_BASE = 10000.0
D_FF = 4 * D
XBC = D + 2 * SSD_GROUPS * SSD_STATE
R_CB, R_CC, R_CX, R_Z, R_XBC, R_DT, R_Q, R_K, R_V, R_GATE = (
    0, 1024, 2048, 3072, 4096, 5632, 5664, 6688, 7712, 8736)

LANES = 128
HALO = 16
VMEM_LIMIT = 56 * 1024 * 1024

U_Q, U_V, U_Z, U_K, U_CB, U_CC, U_XS, U_CX, U_GATE, U_BC, U_COLS = (
    0, 1024, 2048, 3072, 4096, 5120, 6144, 7168, 8192, 11264, 11776)
U_TILE = 23 * LANES
U_SUB = 512
DT_PAD = LANES
TM_PROJ = 1024
TM_MERGE = 512
ADA_TN = 1536
SSD_SUB = 8
SSD_NB_CTX = 4


def _cparams(sem):
    return pltpu.CompilerParams(dimension_semantics=sem, vmem_limit_bytes=VMEM_LIMIT)


def _nt(a, b):
    return lax.dot_general(a, b, (((1,), (1,)), ((), ())), preferred_element_type=F32)


def _tn(a, b):
    return lax.dot_general(a, b, (((0,), (0,)), ((), ())), preferred_element_type=F32)


def _dot(a, b):
    return jnp.dot(a, b, preferred_element_type=F32)


def _dot_hi(a, b):
    return jnp.dot(a, b, preferred_element_type=F32, precision=HIGHEST)


def _sigmoid(x):
    return 0.5 * jnp.tanh(0.5 * x) + 0.5


def _silu(x):
    return x * _sigmoid(x)


def _norm_mod(x, nw, sc, sh):
    ms = jnp.mean(x * x, axis=-1, keepdims=True)
    y = x * lax.rsqrt(ms + EPS) * nw
    return y * (1.0 + sc) + sh


def _ada_kernel(c_ref, w_ref, b_ref, o_ref):
    o_ref[0] = _dot_hi(_silu(c_ref[...]), w_ref[0]) + b_ref[0]


def ada_mod(c_rows, w_ada, b_ada):
    rows = c_rows.shape[0]
    depth = w_ada.shape[0]
    tn = ADA_TN
    return pl.pallas_call(
        _ada_kernel,
        grid=(depth, 6 * D // tn),
        in_specs=[pl.BlockSpec((rows, D), lambda l, j: (0, 0)),
                  pl.BlockSpec((1, D, tn), lambda l, j: (l, 0, j)),
                  pl.BlockSpec((1, 1, tn), lambda l, j: (l, 0, j))],
        out_specs=pl.BlockSpec((1, rows, tn), lambda l, j: (l, 0, j)),
        out_shape=jax.ShapeDtypeStruct((depth, rows, 6 * D), F32),
        compiler_params=_cparams(("arbitrary", "arbitrary")),
        name="ada_mod",
    )(c_rows, w_ada, b_ada.reshape(depth, 1, 6 * D))


def _rope_slab(x, cos, sa, sb):
    return x * cos + pltpu.roll(x, 2 * HEAD_DIM - 16, axis=1) * sa + pltpu.roll(x, 16, axis=1) * sb


def _inproj_kernel(x_ref, sh_ref, sc_ref, nw_ref, w_ref, wdt_ref, cw_ref, cb_ref, *rest, rope, seq_len):
    if rope:
        cos_ref, sa_ref, sb_ref, u_ref, dt_ref, xn_ref = rest
    else:
        u_ref, dt_ref, xn_ref = rest
    j, i = pl.program_id(1), pl.program_id(2)
    ni, tm = xn_ref.shape[0], xn_ref.shape[1]

    def normalise():
        xb = _norm_mod(x_ref[0], nw_ref[...], sc_ref[0], sh_ref[0]).astype(BF16)
        xn_ref[i] = xb
        dt_ref[0] = _dot(xb, wdt_ref[...])
        return xb

    def rope_piece(r, wd, scale):
        lw = 2 * HEAD_DIM
        return jnp.concatenate([_rope_slab(r[:, s:s + lw], cos_ref[...], sa_ref[...], sb_ref[...]) * scale
                                for s in range(0, wd, lw)], axis=1)

    def conv_piece(ws, wd, c0):
        before = xn_ref[jnp.maximum(i - 1, 0), tm - HALO:tm, :]
        after = xn_ref[jnp.minimum(i + 1, ni - 1), 0:HALO, :]
        res = _dot(jnp.concatenate([before, xn_ref[i], after], axis=0), ws)
        r = res[HALO:HALO + tm]
        r_before = res[HALO - 1:HALO]
        r_after = res[HALO + tm:HALO + tm + 1]
        row = lax.broadcasted_iota(jnp.int32, (tm, wd), 0)
        if seq_len >= tm:
            r_before = jnp.where((i * tm) & (seq_len - 1) == 0, 0.0, r_before)
            r_after = jnp.where(((i + 1) * tm) & (seq_len - 1) == 0, 0.0, r_after)
            prev = jnp.where(row == 0, r_before, pltpu.roll(r, 1, axis=0))
            nxt = jnp.where(row == tm - 1, r_after, pltpu.roll(r, tm - 1, axis=0))
        else:
            pos = (i * tm + row) & (seq_len - 1)
            prev = jnp.where(pos == 0, 0.0, jnp.where(row == 0, r_before, pltpu.roll(r, 1, axis=0)))
            nxt = jnp.where(pos == seq_len - 1, 0.0,
                            jnp.where(row == tm - 1, r_after, pltpu.roll(r, tm - 1, axis=0)))
        cw = cw_ref[:, c0:c0 + wd]
        h = prev * cw[0:1] + r * cw[1:2] + nxt * cw[2:3] + cb_ref[:, c0:c0 + wd]
        return h + h * jnp.tanh(h)

    def tile(jt, xb=None):
        for off, wd, kind, arg in _tile_pieces(jt):
            ws = w_ref[:, off:off + wd]
            if kind == "conv":
                r = conv_piece(ws, wd, arg)
            else:
                r = _dot(xn_ref[i] if xb is None else xb, ws)
                if kind == "rope" and rope:
                    r = rope_piece(r, wd, arg)
            u_ref[0, :, off:off + wd] = r.astype(u_ref.dtype)

    for jt in range(U_COLS // U_TILE):
        @pl.when(j == jt)
        def _(jt=jt):
            tile(jt, normalise() if jt == 0 else None)


def _tile_pieces(jt):
    special = [(U_Q, D, "rope", LOG2E * HEAD_DIM ** -0.5), (U_K, D, "rope", 1.0),
               (U_XS, D, "conv", 0), (U_BC, XBC - D, "conv", D)]
    lo, hi = jt * U_TILE, (jt + 1) * U_TILE
    cuts = sorted({lo, hi} | {c for s0, w0, _, _ in special for c in (s0, s0 + w0) if lo < c < hi})
    pieces = []
    for a, b in zip(cuts[:-1], cuts[1:]):
        kind, arg, seg0 = "plain", None, a
        for s0, w0, k0, a0 in special:
            if s0 <= a and b <= s0 + w0:
                kind, arg, seg0 = k0, a0, s0
        for off in range(a, b, U_SUB):
            wd = min(U_SUB, b - off)
            piece_arg = arg + (off - seg0) if kind == "conv" else arg
            pieces.append((off - lo, wd, kind, piece_arg))
    return pieces


def in_proj(x, sh, sc, nw, w, wdt, conv_w, conv_b, tm, seq_len, rope_tables=None):
    B, L, _ = x.shape
    assert seq_len & (seq_len - 1) == 0 and (tm % seq_len == 0 or seq_len % tm == 0)
    tn = U_TILE
    ni = L // tm
    grid = (B, U_COLS // tn, ni)
    rope = rope_tables is not None

    def first(j, i):
        return jnp.where(j == 0, i, ni - 1)

    last_rope_tile = (U_K + D - 1) // U_TILE

    def roped(j, i):
        return jnp.where(j <= last_rope_tile, i, ni - 1)

    rope_specs = [pl.BlockSpec((tm, 2 * HEAD_DIM), lambda b, j, i: (roped(j, i), 0))] * 3 if rope else []
    return pl.pallas_call(
        functools.partial(_inproj_kernel, rope=rope, seq_len=seq_len),
        grid=grid,
        in_specs=[pl.BlockSpec((1, tm, D), lambda b, j, i: (b, first(j, i), 0)),
                  pl.BlockSpec((1, 1, D), lambda b, j, i: (b, 0, 0)),
                  pl.BlockSpec((1, 1, D), lambda b, j, i: (b, 0, 0)),
                  pl.BlockSpec((1, D), lambda b, j, i: (0, 0)),
                  pl.BlockSpec((D, tn), lambda b, j, i: (0, j)),
                  pl.BlockSpec((D, DT_PAD), lambda b, j, i: (0, 0)),
                  pl.BlockSpec((3, XBC), lambda b, j, i: (0, 0)),
                  pl.BlockSpec((1, XBC), lambda b, j, i: (0, 0))] + rope_specs,
        out_specs=[pl.BlockSpec((1, tm, tn), lambda b, j, i: (b, i, j)),
                   pl.BlockSpec((1, tm, DT_PAD), lambda b, j, i: (b, first(j, i), 0))],
        out_shape=[jax.ShapeDtypeStruct((B, L, U_COLS), BF16),
                   jax.ShapeDtypeStruct((B, L, DT_PAD), F32)],
        scratch_shapes=[pltpu.VMEM((ni, tm, D), BF16)],
        compiler_params=_cparams(("parallel", "arbitrary", "arbitrary")),
        name="in_proj",
    )(x, sh, sc, nw, w, wdt, 0.5 * conv_w, 0.5 * conv_b.reshape(1, XBC), *(rope_tables if rope else ()))


def _split3(x):
    hi = x.astype(BF16)
    r1 = x - hi.astype(F32)
    mid = r1.astype(BF16)
    lo = (r1 - mid.astype(F32)).astype(BF16)
    return hi, mid, lo


def _head_stack(x2, lane):
    zero = jnp.zeros_like(x2)
    return jnp.concatenate([jnp.where(lane < HEAD_DIM, x2, zero), jnp.where(lane >= HEAD_DIM, x2, zero)], axis=0)


def _ssd_chunk(d, n, sub, rows, xs_ref, bc_ref, z_ref, dt_ref, alr_ref, alc_ref, dbr_ref, dbc_ref, dsk_ref,
               nw_ref, ex_ref, y_ref, yacc_ref, st_ref):
    H = N_HEADS
    blk = slice(sub * CHUNK, (sub + 1) * CHUNK)
    xsb = xs_ref[n, blk, :]
    xs = xsb.astype(F32)
    bc = bc_ref[n, blk, :]
    dt_c = jax.nn.softplus(dt_ref[n, blk, :] + dbr_ref[...])
    a_c = dt_c * (-LOG2E * jnp.exp(alr_ref[...]))
    dtT_c = jax.nn.softplus(dt_ref[n, blk, :].T[:2 * H] + dbc_ref[...])
    aT_c = dtT_c * (-LOG2E * jnp.exp(alc_ref[...]))

    ri = lax.broadcasted_iota(jnp.int32, (CHUNK, CHUNK), 0)
    ci = lax.broadcasted_iota(jnp.int32, (CHUNK, CHUNK), 1)
    tri = (ri >= ci) if d == 0 else (ri <= ci)
    trib = tri.astype(BF16)
    tribT = ((ri <= ci) if d == 0 else (ri >= ci)).astype(BF16)
    cs = _dot(jnp.concatenate([trib] * 3, axis=1), jnp.concatenate(_split3(a_c), axis=0))
    csT = _dot(jnp.concatenate(_split3(aT_c), axis=1), jnp.concatenate([tribT] * 3, axis=0))
    last = cs[CHUNK - 1:CHUNK] if d == 0 else cs[0:1]
    w_c = dt_c * jnp.exp2(last - cs)
    ein_c = jnp.exp2(cs)
    ex = ex_ref[d]
    w_e = _dot(jnp.concatenate(_split3(w_c)[:2], axis=1), ex)
    ein_e = _dot(jnp.concatenate(_split3(ein_c)[:2], axis=1), ex)
    cd_e = ein_e[CHUNK - 1:CHUNK] if d == 0 else ein_e[0:1]
    Xdec = (xs * w_e).astype(BF16)
    rowT = csT - jnp.log2(dtT_c)
    lane = lax.broadcasted_iota(jnp.int32, (CHUNK, 2 * HEAD_DIM), 1)

    ys = []
    GW = D // SSD_GROUPS
    for g in range(SSD_GROUPS):
        Bg = bc[:, g * SSD_STATE:(g + 1) * SSD_STATE]
        Cg = bc[:, (SSD_GROUPS + g) * SSD_STATE:(SSD_GROUPS + g + 1) * SSD_STATE]
        S = _nt(Cg, Bg)
        st = st_ref[n, d, g]
        y_off = _dot(Cg, st.astype(BF16)) * ein_e[:, g * GW:(g + 1) * GW]
        y_diag = []
        for p in range(GW // (2 * HEAD_DIM)):
            hA = g * (H // SSD_GROUPS) + 2 * p
            Ms = []
            for h in (hA, hA + 1):
                col = cs[:, d * H + h:d * H + h + 1]
                rowv = rowT[d * H + h:d * H + h + 1, :]
                Ms.append((S * jnp.exp2(jnp.where(tri, col - rowv, -jnp.inf))).astype(BF16))
            Mcat = jnp.concatenate(Ms, axis=1)
            Xp = xsb[:, hA * HEAD_DIM:(hA + 2) * HEAD_DIM]
            y_diag.append(_dot(Mcat, _head_stack(Xp, lane)))
        ys.append(y_off + jnp.concatenate(y_diag, axis=1))
        st_ref[n, d, g] = st * cd_e[:, g * GW:(g + 1) * GW] + _tn(Bg, Xdec[:, g * GW:(g + 1) * GW])
    y = jnp.concatenate(ys, axis=1)

    if d == 0:
        yacc_ref[n, rows, :] = y
    else:
        ysum = yacc_ref[n, rows, :] + y + xs * (dsk_ref[0:1] + dsk_ref[1:2])
        yg = ysum * _silu(z_ref[n, blk, :].astype(F32))
        parts = []
        for g in range(SSD_GROUPS):
            v = yg[:, g * GW:(g + 1) * GW]
            ms = jnp.mean(v * v, axis=-1, keepdims=True)
            parts.append(v * lax.rsqrt(ms + EPS) * nw_ref[:, g * GW:(g + 1) * GW])
        y_ref[n, blk, :] = jnp.concatenate(parts, axis=1).astype(y_ref.dtype)


def _ssd_kernel(xs_ref, bc_ref, z_ref, dt_ref, h0_ref, alr_ref, alc_ref, dbr_ref, dbc_ref, dsk_ref, nw_ref,
                ex_ref, y_ref, sto_ref, yacc_ref, st_ref, *, ns, nsub):
    t = pl.program_id(1)
    nb = xs_ref.shape[0]

    @pl.when(t == 0)
    def _():
        st_ref[...] = h0_ref[...]

    args = (xs_ref, bc_ref, z_ref, dt_ref, alr_ref, alc_ref, dbr_ref, dbc_ref, dsk_ref, nw_ref,
            ex_ref, y_ref, yacc_ref, st_ref)

    def seq_rows(step, sub):
        return pl.ds(pl.multiple_of((step * nsub + sub) * CHUNK, CHUNK), CHUNK)

    @pl.when(t < ns)
    def _():
        for sub in range(nsub):
            for n in range(nb):
                _ssd_chunk(0, n, sub, seq_rows(t, sub), *args)

    @pl.when(t >= ns)
    def _():
        for sub in reversed(range(nsub)):
            for n in range(nb):
                _ssd_chunk(1, n, sub, seq_rows(2 * ns - 1 - t, sub), *args)

    @pl.when(t == 2 * ns - 1)
    def _():
        sto_ref[...] = st_ref[...]


def ssd_mix(u, dt, h0, a_log, dt_bias, d_skip_e, norm_w, nb=1):
    B, L, _ = u.shape
    nsub = min(SSD_SUB, L // CHUNK)
    RB = nsub * CHUNK
    ns = L // RB
    NB = math.gcd(B, nb)
    H2 = 2 * N_HEADS

    def chunk(t):
        return jnp.where(t < ns, t, 2 * ns - 1 - t)

    def late(t):
        return jnp.where(t < ns, ns - 1, 2 * ns - 1 - t)

    st_shape = (2, SSD_GROUPS, SSD_STATE, D // SSD_GROUPS)
    st_spec = pl.BlockSpec((NB,) + st_shape, lambda b, t: (b, 0, 0, 0, 0))
    small = lambda shape: pl.BlockSpec(shape, lambda b, t: (0,) * len(shape))
    krow = jnp.arange(2 * DT_PAD, dtype=jnp.int32)[:, None] % DT_PAD
    head = jnp.arange(D, dtype=jnp.int32)[None, :] // HEAD_DIM
    ex = jnp.stack([krow == d * N_HEADS + head for d in range(2)]).astype(BF16)
    lane_pad = lambda v: jnp.pad(v.reshape(1, H2), ((0, 0), (0, DT_PAD - H2)))
    return pl.pallas_call(
        functools.partial(_ssd_kernel, ns=ns, nsub=nsub),
        grid=(B // NB, 2 * ns),
        in_specs=[pl.BlockSpec((NB, RB, D), lambda b, t: (b, chunk(t), U_XS // D)),
                  pl.BlockSpec((NB, RB, XBC - D), lambda b, t: (b, chunk(t), U_BC // (XBC - D))),
                  pl.BlockSpec((NB, RB, D), lambda b, t: (b, late(t), U_Z // D)),
                  pl.BlockSpec((NB, RB, DT_PAD), lambda b, t: (b, chunk(t), 0)),
                  st_spec,
                  small((1, DT_PAD)), small((H2, 1)), small((1, DT_PAD)), small((H2, 1)),
                  small((2, D)), small((1, D)), small((2, 2 * DT_PAD, D))],
        out_specs=[pl.BlockSpec((NB, RB, D), lambda b, t: (b, late(t), 0)),
                   st_spec],
        out_shape=[jax.ShapeDtypeStruct((B, L, D), BF16),
                   jax.ShapeDtypeStruct((B,) + st_shape, F32)],
        scratch_shapes=[pltpu.VMEM((NB, L, D), F32), pltpu.VMEM((NB,) + st_shape, F32)],
        compiler_params=_cparams(("parallel", "arbitrary")),
        name="ssd_mix",
    )(u, u, u, dt, h0,
      lane_pad(a_log), a_log.reshape(H2, 1), lane_pad(dt_bias), dt_bias.reshape(H2, 1),
      d_skip_e, norm_w.reshape(1, D), ex)


NA_GROUP = 1


def _na_kernel(q_ref, k_ref, v_ref, kc_ref, vc_ref, bias_ref, o_ref, s_ref, p_ref, inv_ref, kt_ref, kto_ref, *, rows):
    W = GRID_W
    band = NA_KH * W
    lane = lax.broadcasted_iota(jnp.int32, (W, 2 * HEAD_DIM), 1)
    G = NA_GROUP
    n_groups = rows // G
    M2 = 2 * W
    kt_ref[...] = k_ref[0].T
    kto_ref[...] = k_ref[0, W:W + kto_ref.shape[1], :].T
    kct = kc_ref[0].T

    def band_start(r):
        return min(max(r - NA_KH // 2, 0), rows - NA_KH)

    def stage_scores(g, slot):
        for i in range(G):
            r = g * G + i
            r0 = band_start(r)
            qs = _head_stack(q_ref[0, r * W:(r + 1) * W, :], lane)
            ktb, t0 = (kt_ref, r0 * W) if r0 % 2 == 0 else (kto_ref, (r0 - 1) * W)
            base = r0 - r + NA_KH - 1
            for jp in range(NA_KH // 2):
                cols = slice(jp * M2, (jp + 1) * M2)
                s_ref[slot, i * M2:(i + 1) * M2, cols] = (_dot(qs, ktb[:, t0 + jp * M2:t0 + (jp + 1) * M2])
                                                          + bias_ref[0, base + 2 * jp])
            s_ref[slot, i * M2:(i + 1) * M2, band:] = _dot(qs, kct)

    def stage_softmax(slot):
        for i in range(G):
            s = s_ref[slot, i * M2:(i + 1) * M2, :]
            p = jnp.exp2(s - jnp.max(s, axis=-1, keepdims=True))
            inv = 1.0 / jnp.sum(p, axis=-1, keepdims=True)
            p_ref[slot, i * M2:(i + 1) * M2, :] = p.astype(BF16)
            inv_ref[slot, i * M2:(i + 1) * M2, :] = jnp.broadcast_to(inv, (M2, M2))

    def stage_values(g, slot):
        for i in range(G):
            r = g * G + i
            r0 = band_start(r)
            vb = v_ref[0, r0 * W:r0 * W + band, :]
            p = p_ref[slot, i * M2:(i + 1) * M2, :]
            o2 = (_dot(p[:, :band], vb) + _dot(p[:, band:], vc_ref[0])) * inv_ref[slot, i * M2:(i + 1) * M2, :]
            o = jnp.where(lane < HEAD_DIM, o2[:W], o2[W:])
            o_ref[0, r * W:(r + 1) * W, :] = o.astype(o_ref.dtype)

    stage_scores(0, 0)
    stage_scores(1, 1)
    stage_softmax(0)

    for g in range(n_groups - 2):
        stage_scores(g + 2, g % 2)
        stage_softmax((g + 1) % 2)
        stage_values(g, g % 2)
    stage_softmax((n_groups - 1) % 2)
    stage_values(n_groups - 2, (n_groups - 2) % 2)
    stage_values(n_groups - 1, (n_groups - 1) % 2)


def na_attend(u, u_ctx, bias):
    B, L, _ = u.shape
    Lc = u_ctx.shape[1]
    HP = N_HEADS // 2
    lw = 2 * HEAD_DIM
    oq, ok, ov = U_Q // lw, U_K // lw, U_V // lw
    return pl.pallas_call(
        functools.partial(_na_kernel, rows=L // GRID_W),
        grid=(HP, B),
        in_specs=[pl.BlockSpec((1, L, lw), lambda h, b: (b, 0, oq + h)),
                  pl.BlockSpec((1, L, lw), lambda h, b: (b, 0, ok + h)),
                  pl.BlockSpec((1, L, lw), lambda h, b: (b, 0, ov + h)),
                  pl.BlockSpec((1, Lc, lw), lambda h, b: (b, 0, ok + h)),
                  pl.BlockSpec((1, Lc, lw), lambda h, b: (b, 0, ov + h)),
                  pl.BlockSpec((1, 2 * NA_KH - 2, 2 * GRID_W, 2 * GRID_W), lambda h, b: (h, 0, 0, 0))],
        out_specs=pl.BlockSpec((1, L, lw), lambda h, b: (b, 0, h)),
        out_shape=jax.ShapeDtypeStruct((B, L, D), BF16),
        scratch_shapes=[pltpu.VMEM((2, NA_GROUP * lw, NA_KH * GRID_W + Lc), F32),
                        pltpu.VMEM((2, NA_GROUP * lw, NA_KH * GRID_W + Lc), BF16),
                        pltpu.VMEM((2, NA_GROUP * lw, lw), F32),
                        pltpu.VMEM((lw, L), BF16), pltpu.VMEM((lw, L - 2 * GRID_W), BF16)],
        compiler_params=_cparams(("parallel", "parallel")),
        name="na_attend",
    )(u, u, u, u_ctx, u_ctx, bias)


CTX_NB = 4


def _ctxattn_kernel(q_ref, k_ref, v_ref, o_ref):
    Lc = q_ref.shape[1]
    lane = lax.broadcasted_iota(jnp.int32, (Lc, 2 * HEAD_DIM), 1)
    for n in range(q_ref.shape[0]):
        q2 = (q_ref[n].astype(F32) * (HEAD_DIM ** -0.5)).astype(BF16)
        s = _nt(_head_stack(q2, lane), k_ref[n])
        m = jnp.max(s, axis=-1, keepdims=True)
        p = jnp.exp(s - m)
        p = (p * (1.0 / jnp.sum(p, axis=-1, keepdims=True))).astype(BF16)
        o2 = _dot(p, v_ref[n])
        o_ref[n] = jnp.where(lane < HEAD_DIM, o2[:Lc], o2[Lc:]).astype(o_ref.dtype)


def ctx_attend(u_ctx):
    B, Lc, _ = u_ctx.shape
    HP = N_HEADS // 2
    lw = 2 * HEAD_DIM
    oq, ok, ov = U_Q // lw, U_K // lw, U_V // lw
    nb = math.gcd(B, CTX_NB)
    return pl.pallas_call(
        _ctxattn_kernel,
        grid=(HP, B // nb),
        in_specs=[pl.BlockSpec((nb, Lc, lw), lambda h, b: (b, 0, oq + h)),
                  pl.BlockSpec((nb, Lc, lw), lambda h, b: (b, 0, ok + h)),
                  pl.BlockSpec((nb, Lc, lw), lambda h, b: (b, 0, ov + h))],
        out_specs=pl.BlockSpec((nb, Lc, lw), lambda h, b: (b, 0, h)),
        out_shape=jax.ShapeDtypeStruct((B, Lc, D), BF16),
        compiler_params=_cparams(("parallel", "parallel")),
        name="ctx_attend",
    )(u_ctx, u_ctx, u_ctx)


MERGE_CK = 256


def _merge_kernel(cb_ref, cc_ref, cx_ref, ccp_ref, cxp_ref, ccn_ref, cxn_ref, cw_ref, ys_ref, yn_ref, gc_ref, gs_ref,
                  gn_ref, h_ref, gt_ref, wc_ref, ws_ref, wn_ref, wo_ref, o_ref, *, seq_len):
    tm = h_ref.shape[1]
    row = lax.broadcasted_iota(jnp.int32, (tm, MERGE_CK), 0)
    pos = (pl.program_id(1) * tm + row) & (seq_len - 1)
    first, last = pos == 0, pos == seq_len - 1
    top, bottom = row == 0, row == tm - 1
    yc_proj = None
    for c in range(0, D, MERGE_CK):
        ch = slice(c, c + MERGE_CK)
        p = cc_ref[0, :, ch].astype(F32) * cx_ref[0, :, ch].astype(F32)
        p_before = (ccp_ref[0, :, ch].astype(F32) * cxp_ref[0, :, ch].astype(F32))[HALO - 1:HALO]
        p_after = (ccn_ref[0, :, ch].astype(F32) * cxn_ref[0, :, ch].astype(F32))[0:1]
        prev = jnp.where(first, 0.0, jnp.where(top, p_before, pltpu.roll(p, 1, axis=0)))
        nxt = jnp.where(last, 0.0, jnp.where(bottom, p_after, pltpu.roll(p, tm - 1, axis=0)))
        w = cw_ref[:, ch]
        yc = (cb_ref[0, :, ch].astype(F32) * (prev * w[0:1] + p * w[1:2] + nxt * w[2:3])).astype(BF16)
        t = _dot(yc, wc_ref[ch, :])
        yc_proj = t if yc_proj is None else yc_proj + t

    gate = lambda ref: _sigmoid(ref[0].astype(F32))
    m = (gate(gc_ref) * yc_proj
         + gate(gs_ref) * _dot(ys_ref[0], ws_ref[...])
         + gate(gn_ref) * _dot(yn_ref[0], wn_ref[...]))
    o_ref[0] = h_ref[0] + gt_ref[0] * _dot(m.astype(BF16), wo_ref[...])


def merge(u, ys, yn, h, gt, cw, wc, ws, wn, wo, tm, seq_len):
    B, L, _ = h.shape
    assert seq_len & (seq_len - 1) == 0 and (tm % seq_len == 0 or seq_len % tm == 0)
    nb, hb = tm // HALO, L // HALO
    tok = lambda: pl.BlockSpec((1, tm, D), lambda b, i: (b, i, 0))
    ucol = lambda c: pl.BlockSpec((1, tm, D), lambda b, i: (b, i, c // D))
    before = lambda c: pl.BlockSpec((1, HALO, D), lambda b, i: (b, jnp.maximum(i * nb - 1, 0), c // D))
    after = lambda c: pl.BlockSpec((1, HALO, D), lambda b, i: (b, jnp.minimum((i + 1) * nb, hb - 1), c // D))
    wsp = lambda: pl.BlockSpec((D, D), lambda b, i: (0, 0))
    return pl.pallas_call(
        functools.partial(_merge_kernel, seq_len=seq_len),
        grid=(B, L // tm),
        in_specs=[ucol(U_CB), ucol(U_CC), ucol(U_CX), before(U_CC), before(U_CX), after(U_CC), after(U_CX),
                  pl.BlockSpec((3, D), lambda b, i: (0, 0)),
                  tok(), tok(),
                  ucol(U_GATE), ucol(U_GATE + D), ucol(U_GATE + 2 * D),
                  tok(),
                  pl.BlockSpec((1, 1, D), lambda b, i: (b, 0, 0)),
                  wsp(), wsp(), wsp(), wsp()],
        out_specs=tok(),
        out_shape=jax.ShapeDtypeStruct((B, L, D), F32),
        compiler_params=_cparams(("parallel", "parallel")),
        name="merge",
    )(u, u, u, u, u, u, u, cw, ys, yn, u, u, u, h, gt, wc, ws, wn, wo)


MLP_TF = 1024


def _mlp_kernel(h_ref, sh_ref, sc_ref, gt_ref, nw_ref, fw_ref, w1_ref, w2_ref, o_ref, *, final):
    h = h_ref[0]
    xn = _norm_mod(h, nw_ref[...], sc_ref[0], sh_ref[0]).astype(BF16)
    acc = None
    for k in range(0, D_FF, MLP_TF):
        a = jnp.square(jnp.maximum(_dot(xn, w1_ref[:, k:k + MLP_TF]), 0.0)).astype(BF16)
        t = _dot(a, w2_ref[k:k + MLP_TF, :])
        acc = t if acc is None else acc + t
    o = h + gt_ref[0] * acc
    if final:
        ms = jnp.mean(o * o, axis=-1, keepdims=True)
        o = o * lax.rsqrt(ms + EPS) * fw_ref[...]
    o_ref[0] = o


def mlp(h, sh, sc, gt, nw, fw, w1, w2, tm, final):
    B, L, _ = h.shape
    vec = lambda: pl.BlockSpec((1, 1, D), lambda b, i: (b, 0, 0))
    par = lambda: pl.BlockSpec((1, D), lambda b, i: (0, 0))
    resident = lambda shape: pl.BlockSpec(shape, lambda b, i: (0, 0), pipeline_mode=pl.Buffered(1))
    return pl.pallas_call(
        functools.partial(_mlp_kernel, final=final),
        grid=(B, L // tm),
        in_specs=[pl.BlockSpec((1, tm, D), lambda b, i: (b, i, 0)),
                  vec(), vec(), vec(), par(), par(),
                  resident((D, D_FF)), resident((D_FF, D))],
        out_specs=pl.BlockSpec((1, tm, D), lambda b, i: (b, i, 0)),
        out_shape=jax.ShapeDtypeStruct((B, L, D), F32),
        compiler_params=_cparams(("parallel", "parallel")),
        name="mlp",
    )(h, sh, sc, gt, nw, fw, w1, w2)


def _rope_tables(L):
    t = jnp.arange(L, dtype=jnp.int32)
    row = (t // GRID_W).astype(F32)
    col = (t % GRID_W).astype(F32)
    half = HEAD_DIM // 2
    inv = ROPE_BASE ** (-jnp.arange(0, half, 2, dtype=F32) / half)
    ang_r = row[:, None] * inv
    ang_c = col[:, None] * inv
    ang = jnp.concatenate([ang_r, ang_r, ang_c, ang_c], axis=-1)
    cos = jnp.tile(jnp.cos(ang), (1, 2))
    sin = jnp.tile(jnp.sin(ang), (1, 2))
    even = ((jnp.arange(2 * HEAD_DIM) // (half // 2)) % 2 == 0)[None, :]
    return cos, jnp.where(even, -sin, 0.0), jnp.where(even, 0.0, sin)


def _rpb_kernel(r_ref, oh_ref, ok_ref, o_ref):
    val = _dot(jnp.concatenate(_split3(r_ref[...]), axis=1), oh_ref[...])
    o_ref[...] = jnp.where(ok_ref[...] > 0.0, LOG2E * val, -jnp.inf)


def _na_bias_table(rpb):
    H, NR, NC = rpb.shape
    W = GRID_W
    col = np.arange(W)
    col_start = np.clip(col - NA_KW // 2, 0, W - NA_KW)
    col_ok = (col[None, :] >= col_start[:, None]) & (col[None, :] < col_start[:, None] + NA_KW)
    dc_idx = np.clip(col[None, :] - col[:, None], -(NA_KW - 1), NA_KW - 1) + NA_KW - 1
    rows = jnp.transpose(rpb.reshape(H // 2, 2, NR, NC), (0, 2, 1, 3)).reshape(H * NR, NC)
    rows = jnp.pad(rows, ((0, 0), (0, DT_PAD - NC)))
    krow = jnp.arange(3 * DT_PAD, dtype=jnp.int32)[:, None] % DT_PAD
    onehot = (krow == jnp.asarray(dc_idx.reshape(1, W * W), jnp.int32)).astype(BF16)
    ok = jnp.asarray(col_ok.reshape(1, W * W), F32)
    tn = 1024
    tab = pl.pallas_call(
        _rpb_kernel,
        grid=(W * W // tn,),
        in_specs=[pl.BlockSpec((H * NR, DT_PAD), lambda j: (0, 0)),
                  pl.BlockSpec((3 * DT_PAD, tn), lambda j: (0, j)),
                  pl.BlockSpec((1, tn), lambda j: (0, j))],
        out_specs=pl.BlockSpec((H * NR, tn), lambda j: (0, j)),
        out_shape=jax.ShapeDtypeStruct((H * NR, W * W), F32),
        compiler_params=_cparams(("arbitrary",)),
        name="rpb_table",
    )(rows, onehot, ok)
    tab = tab.reshape(H // 2, NR, 2 * W, W)
    return jnp.concatenate([tab[:, :NR - 1], tab[:, 1:]], axis=-1)


def _prep_w_in(w_in):
    order = [(R_Q, D), (R_V, D), (R_Z, D), (R_K, D), (R_CB, D), (R_CC, D), (R_XBC, D), (R_CX, D), (R_GATE, 3 * D),
             (R_XBC + D, XBC - D)]
    w = jnp.concatenate([w_in[:, o:o + n] for o, n in order], axis=1).astype(BF16)
    wdt = w_in[:, R_DT:R_DT + 2 * N_HEADS].astype(BF16)
    return w, jnp.pad(wdt, ((0, 0), (0, DT_PAD - 2 * N_HEADS)))


def kernel(x, c, ctx, c_ctx, w_ada, b_ada, norm1_w, w_in, conv_mix_w, ssd_conv_w, ssd_conv_b, ssd_a_log, ssd_dt_bias,
           ssd_d, ssd_norm_w, na_rpb, w_br_conv, w_br_ssd, w_br_na, w_out, norm2_w, w_ff1, w_ff2, final_norm_w):
    B, L, _ = x.shape
    Lc = ctx.shape[1]
    depth = w_in.shape[0]
    cos, sa, sb = _rope_tables(L)
    n_mod = B + 1
    pad = (-n_mod) % 8
    c_rows = jnp.concatenate([c, c_ctx[None, :], jnp.zeros((pad, D), F32)], axis=0)
    zero_state = jnp.zeros((B, 2, SSD_GROUPS, SSD_STATE, D // SSD_GROUPS), F32)
    fw = final_norm_w.reshape(1, D)
    Tc = B * Lc
    tmc = min(TM_PROJ, Tc)
    flat = lambda a: a.reshape(1, Tc, a.shape[-1])
    h, hc = x, flat(ctx)
    mods = ada_mod(c_rows, w_ada, b_ada)
    for l in range(depth):
        last = l == depth - 1
        mod = mods[l]
        m_lat = mod[:B].reshape(B, 1, 6, D)
        m_ctx = mod[B:B + 1].reshape(1, 1, 6, D)
        sh1, sc1, gt1, sh2, sc2, gt2 = (m_lat[:, :, i] for i in range(6))
        csh1, csc1, cgt1, csh2, csc2, cgt2 = (m_ctx[:, :, i] for i in range(6))
        w, wdt = _prep_w_in(w_in[l])
        nw1 = norm1_w[l].reshape(1, D)
        conv_p = (ssd_conv_w[l], ssd_conv_b[l])
        u, dt = in_proj(h, sh1, sc1, nw1, w, wdt, *conv_p, tm=TM_PROJ, seq_len=L, rope_tables=(cos, sa, sb))
        uc, dtc = in_proj(hc, csh1, csc1, nw1, w, wdt, *conv_p, tm=tmc, seq_len=Lc)
        uc = uc.reshape(B, Lc, U_COLS)
        dtc = dtc.reshape(B, Lc, DT_PAD)
        d_skip_e = jnp.repeat(ssd_d[l], HEAD_DIM, axis=1)
        ssd_p = (ssd_a_log[l], ssd_dt_bias[l], d_skip_e, ssd_norm_w[l])
        y_ssd_c, ctx_states = ssd_mix(uc, dtc, zero_state, *ssd_p, nb=SSD_NB_CTX)
        y_ssd, _ = ssd_mix(u, dt, ctx_states, *ssd_p)
        y_na = na_attend(u, uc, _na_bias_table(na_rpb[l]))
        wb = [t[l].astype(BF16) for t in (w_br_conv, w_br_ssd, w_br_na, w_out)]
        w1, w2 = w_ff1[l].astype(BF16), w_ff2[l].astype(BF16)
        nw2 = norm2_w[l].reshape(1, D)
        h = merge(u, y_ssd, y_na, h, gt1, conv_mix_w[l], *wb, tm=TM_MERGE, seq_len=L)
        h = mlp(h, sh2, sc2, gt2, nw2, fw, w1, w2, tm=TM_PROJ, final=last)
        if not last:
            y_na_c = ctx_attend(uc)
            hc = merge(flat(uc), flat(y_ssd_c), flat(y_na_c), hc, cgt1, conv_mix_w[l], *wb, tm=min(TM_MERGE, Tc), seq_len=Lc)
            hc = mlp(hc, csh2, csc2, cgt2, nw2, fw, w1, w2, tm=tmc, final=False)
    return h
```
